```python
import math
import jax, jax.numpy as jnp
from jax import lax
import numpy as np

D_MODEL = 1024
BATCH = 1
SEQ = 16384
DEPTH = 2

GRID_W = 64
CTX_LEN = 256
HEAD_DIM = 64
MIX_WIDTH = D_MODEL
NA_WIDTH = MIX_WIDTH // 4
NA_HEADS = NA_WIDTH // HEAD_DIM
NA_WIN_ROWS = 8
NA_WIN_COLS = 16
SSD_WIDTH = MIX_WIDTH // 2
SSD_HEADDIM = 64
SSD_HEADS = SSD_WIDTH // SSD_HEADDIM
SSD_GROUPS = 2
SSD_HEADS_PER_GROUP = SSD_HEADS // SSD_GROUPS
SSD_STATE = 128
SSD_CONV = 5
SSD_CHUNK = 128
SSD_CONV_CH = SSD_WIDTH + 2 * SSD_GROUPS * SSD_STATE
GQA_WIDTH = MIX_WIDTH - NA_WIDTH - SSD_WIDTH
GQA_Q_HEADS = GQA_WIDTH // HEAD_DIM
GQA_KV_HEADS = GQA_Q_HEADS // 2
GQA_REP = GQA_Q_HEADS // GQA_KV_HEADS
Q_BLOCK = 128
ROPE_THETA = 10000.0
ROPE_PAIRS = HEAD_DIM // 4
NA_IN = 3 * NA_WIDTH
SSD_IN = SSD_WIDTH + SSD_CONV_CH + 2 * SSD_HEADS
GQA_IN = GQA_WIDTH + 2 * GQA_KV_HEADS * HEAD_DIM
IN_WIDTH = NA_IN + SSD_IN + GQA_IN
FFN_HIDDEN = -(-(8 * D_MODEL) // (3 * 256)) * 256
EPS = 1e-6

kernel_name = 'hybrid_na_ssd_gqa_dit_block'


def rms_norm(x, w):
    xf = x.astype(jnp.float32)
    y = xf * lax.rsqrt(jnp.mean(xf * xf, axis=-1, keepdims=True) + EPS)
    return y.astype(x.dtype) * w


def axial_rope(x, rows, cols):
    freqs = ROPE_THETA ** (-jnp.arange(ROPE_PAIRS, dtype=jnp.float32) / ROPE_PAIRS)

    def rotate(xh, pos):
        ang = pos[:, None] * freqs[None, :]
        ang = jnp.concatenate([ang, ang], axis=-1)[None, :, None, :]
        cos = jnp.cos(ang).astype(xh.dtype)
        sin = jnp.sin(ang).astype(xh.dtype)
        x1, x2 = jnp.split(xh, 2, axis=-1)
        return xh * cos + jnp.concatenate([-x2, x1], axis=-1) * sin

    half = HEAD_DIM // 2
    return jnp.concatenate([rotate(x[..., :half], rows), rotate(x[..., half:], cols)], axis=-1)


def gqa_softmax_attention(q, k, v):
    s = jnp.einsum('bqgrd,bkgd->bgrqk', q, k) * (HEAD_DIM ** -0.5)
    p = jax.nn.softmax(s.astype(jnp.float32), axis=-1).astype(v.dtype)
    return jnp.einsum('bgrqk,bkgd->bqgrd', p, v)


def neighbourhood_attention(u, u_c, rpb, ctx_out):
    b, l, _ = u.shape
    lc = u_c.shape[1]
    rows = l // GRID_W
    wh = min(NA_WIN_ROWS, rows)
    q, k, v = [t.reshape(b, rows, GRID_W, NA_HEADS, HEAD_DIM) for t in jnp.split(u, 3, axis=-1)]
    qc, kc, vc = [t.reshape(b, lc, NA_HEADS, HEAD_DIM) for t in jnp.split(u_c, 3, axis=-1)]
    scale = HEAD_DIM ** -0.5
    r = jnp.arange(rows)
    col = jnp.arange(GRID_W)
    r_start = jnp.clip(r - wh // 2, 0, rows - wh)
    rows_idx = r_start[:, None] + jnp.arange(wh)[None, :]
    c_start = jnp.clip(col - NA_WIN_COLS // 2, 0, GRID_W - NA_WIN_COLS)
    in_win = (col[None, :] >= c_start[:, None]) & (col[None, :] < c_start[:, None] + NA_WIN_COLS)
    dr = rows_idx - r[:, None] + NA_WIN_ROWS - 1
    dc = jnp.clip(col[None, :] - col[:, None] + NA_WIN_COLS - 1, 0, 2 * NA_WIN_COLS - 2)
    bias = rpb[:, dr[:, None, :, None], dc[None, :, None, :]].transpose(1, 0, 2, 3, 4)
    kb = k[:, rows_idx]
    vb = v[:, rows_idx]
    s_win = jnp.einsum('brqhd,brswhd->brhqsw', q, kb) * scale + bias[None]
    s_win = jnp.where(in_win[:, None, :], s_win, -jnp.inf)
    s_ctx = jnp.einsum('brqhd,bkhd->brhqk', q, kc) * scale
    n_win = wh * GRID_W
    s = jnp.concatenate([s_win.reshape(b, rows, NA_HEADS, GRID_W, n_win), s_ctx], axis=-1)
    p = jax.nn.softmax(s.astype(jnp.float32), axis=-1).astype(v.dtype)
    p_win = p[..., :n_win].reshape(b, rows, NA_HEADS, GRID_W, wh, GRID_W)
    p_ctx = p[..., n_win:]
    y = jnp.einsum('brhqsw,brswhd->brqhd', p_win, vb) + jnp.einsum('brhqk,bkhd->brqhd', p_ctx, vc)
    y = y.reshape(b, l, NA_WIDTH)
    y_c = None
    if ctx_out:
        y_c = gqa_softmax_attention(qc[:, :, :, None, :], kc, vc).reshape(b, lc, NA_WIDTH)
    return y, y_c


def depthwise_conv(x, w, bias):
    y = lax.conv_general_dilated(x, w[:, None, :], window_strides=(1,),
                                 padding=[(SSD_CONV // 2, SSD_CONV // 2)],
                                 dimension_numbers=('NWC', 'WIO', 'NWC'),
                                 feature_group_count=x.shape[-1])
    return y + bias


def ssd_scan(xs, dt, a, bm, cm, h0):
    b, l, h, p = xs.shape
    n = bm.shape[-1]
    nc = l // SSD_CHUNK
    xr = (xs * dt[..., None]).reshape(b, nc, SSD_CHUNK, h, p)
    br = bm.reshape(b, nc, SSD_CHUNK, h, n)
    cr = cm.reshape(b, nc, SSD_CHUNK, h, n)
    a_cum = jnp.cumsum((dt * a).reshape(b, nc, SSD_CHUNK, h), axis=2)
    seg = a_cum[:, :, :, None, :] - a_cum[:, :, None, :, :]
    causal = jnp.tril(jnp.ones((SSD_CHUNK, SSD_CHUNK), dtype=bool))[None, None, :, :, None]
    decay = jnp.exp(jnp.where(causal, seg, -jnp.inf))
    g = jnp.einsum('bcthn,bcshn->bctsh', cr, br) * decay
    y_diag = jnp.einsum('bctsh,bcshp->bcthp', g, xr)
    to_end = jnp.exp(a_cum[:, :, -1:, :] - a_cum)
    chunk_states = jnp.einsum('bcsh,bcshn,bcshp->bchpn', to_end, br, xr)
    chunk_decay = jnp.exp(a_cum[:, :, -1, :])

    def step(h_prev, inp):
        dec, st = inp
        return dec[:, :, None, None] * h_prev + st, h_prev

    h_final, h_enter = lax.scan(step, h0, (chunk_decay.transpose(1, 0, 2),
                                           chunk_states.transpose(1, 0, 2, 3, 4)))
    h_enter = h_enter.transpose(1, 0, 2, 3, 4)
    y_off = jnp.einsum('bcthn,bcth,bchpn->bcthp', cr, jnp.exp(a_cum), h_enter)
    return (y_diag + y_off).reshape(b, l, h, p), h_final


def ssd_mixer(u, u_c, conv_w, conv_b, dt_bias, a_log, d_skip, norm_w, ctx_out):
    f32 = jnp.float32
    a = -jnp.exp(a_log.astype(f32))

    def prep(v):
        b, l, _ = v.shape
        z, xbc, dt_raw = jnp.split(v, [SSD_WIDTH, SSD_WIDTH + SSD_CONV_CH], axis=-1)
        xbc = jax.nn.silu(depthwise_conv(xbc, conv_w, conv_b))
        xs, bm, cm = jnp.split(xbc, [SSD_WIDTH, SSD_WIDTH + SSD_GROUPS * SSD_STATE], axis=-1)
        xs = xs.reshape(b, l, SSD_HEADS, SSD_HEADDIM).astype(f32)
        bm = jnp.repeat(bm.reshape(b, l, SSD_GROUPS, SSD_STATE), SSD_HEADS_PER_GROUP, axis=2).astype(f32)
        cm = jnp.repeat(cm.reshape(b, l, SSD_GROUPS, SSD_STATE), SSD_HEADS_PER_GROUP, axis=2).astype(f32)
        dt = jax.nn.softplus(dt_raw.reshape(b, l, 2, SSD_HEADS).astype(f32) + dt_bias.astype(f32))
        return z, xs, bm, cm, dt

    def flip(t):
        return jnp.flip(t, axis=1)

    z, xs, bm, cm, dt = prep(u)
    zc, xsc, bmc, cmc, dtc = prep(u_c)
    h0 = jnp.zeros((u.shape[0], SSD_HEADS, SSD_HEADDIM, SSD_STATE), f32)
    yc_f, hc_f = ssd_scan(xsc, dtc[:, :, 0], a[0], bmc, cmc, h0)
    yc_b, hc_b = ssd_scan(flip(xsc), flip(dtc[:, :, 1]), a[1], flip(bmc), flip(cmc), h0)
    y_f, _ = ssd_scan(xs, dt[:, :, 0], a[0], bm, cm, hc_f)
    y_b, _ = ssd_scan(flip(xs), flip(dt[:, :, 1]), a[1], flip(bm), flip(cm), hc_b)

    def finish(yf, yb_flipped, xsd, zz):
        y = yf + flip(yb_flipped) + d_skip.astype(f32)[:, None] * xsd
        y = y.reshape(zz.shape).astype(zz.dtype) * jax.nn.silu(zz)
        return rms_norm(y, norm_w)

    y = finish(y_f, y_b, xs, z)
    y_c = finish(yc_f, yc_b, xsc, zc) if ctx_out else None
    return y, y_c


def gqa_attention(u, u_c, q_norm_w, k_norm_w, ctx_out):
    b, l, _ = u.shape
    lc = u_c.shape[1]
    kv_w = GQA_KV_HEADS * HEAD_DIM

    def prep(v, n):
        q, k, vv = jnp.split(v, [GQA_WIDTH, GQA_WIDTH + kv_w], axis=-1)
        q = rms_norm(q.reshape(b, n, GQA_Q_HEADS, HEAD_DIM), q_norm_w)
        k = rms_norm(k.reshape(b, n, GQA_KV_HEADS, HEAD_DIM), k_norm_w)
        return q, k, vv.reshape(b, n, GQA_KV_HEADS, HEAD_DIM)

    q, k, v = prep(u, l)
    qc, kc, vc = prep(u_c, lc)
    t = jnp.arange(l)
    rows = (t // GRID_W).astype(jnp.float32)
    cols = (t % GRID_W).astype(jnp.float32)
    q = axial_rope(q, rows, cols)
    k = axial_rope(k, rows, cols)
    k_all = jnp.concatenate([k, kc], axis=1)
    v_all = jnp.concatenate([v, vc], axis=1)
    nb = l // Q_BLOCK
    qb = q.reshape(b, nb, Q_BLOCK, GQA_KV_HEADS, GQA_REP, HEAD_DIM).transpose(1, 0, 2, 3, 4, 5)
    o = lax.map(lambda qi: gqa_softmax_attention(qi, k_all, v_all), qb)
    y = o.transpose(1, 0, 2, 3, 4, 5).reshape(b, l, GQA_WIDTH)
    y_c = None
    if ctx_out:
        qcg = qc.reshape(b, lc, GQA_KV_HEADS, GQA_REP, HEAD_DIM)
        y_c = gqa_softmax_attention(qcg, kc, vc).reshape(b, lc, GQA_WIDTH)
    return y, y_c


def swiglu(h, w_gate, w_up, w_down):
    return (jax.nn.silu(h @ w_gate) * (h @ w_up)) @ w_down


def setup_inputs(seed: int = 0) -> dict:
    key = jax.random.key(seed)
    ks = jax.random.split(key, 26)
    f32 = jnp.float32
    L = DEPTH

    def nrm(k, shape, scale):
        return jax.random.normal(k, shape, f32) * scale

    dt0 = jnp.exp(jax.random.uniform(ks[12], (L, 2, SSD_HEADS), f32, math.log(1e-3), math.log(1e-1)))
    return {
        'x': nrm(ks[0], (BATCH, SEQ, D_MODEL), 1.0),
        'c': nrm(ks[1], (BATCH, D_MODEL), 1.0),
        'ctx': nrm(ks[2], (BATCH, CTX_LEN, D_MODEL), 1.0),
        'c_ctx': nrm(ks[3], (D_MODEL,), 1.0),
        'mod_w': nrm(ks[4], (L, D_MODEL, 6 * D_MODEL), 0.5 * D_MODEL ** -0.5),
        'mod_b': nrm(ks[5], (L, 6 * D_MODEL), 0.01),
        'norm_attn_w': 1.0 + nrm(ks[6], (L, D_MODEL), 0.05),
        'norm_ffn_w': 1.0 + nrm(ks[7], (L, D_MODEL), 0.05),
        'w_in': nrm(ks[8], (L, D_MODEL, IN_WIDTH), D_MODEL ** -0.5),
        'na_rpb': nrm(ks[9], (L, NA_HEADS, 2 * NA_WIN_ROWS - 1, 2 * NA_WIN_COLS - 1), 0.1),
        'ssd_conv_w': nrm(ks[10], (L, SSD_CONV, SSD_CONV_CH), SSD_CONV ** -0.5),
        'ssd_conv_b': nrm(ks[11], (L, SSD_CONV_CH), 0.01),
        'ssd_dt_bias': dt0 + jnp.log(-jnp.expm1(-dt0)),
        'ssd_a_log': jnp.log(jax.random.uniform(ks[13], (L, 2, SSD_HEADS), f32, 1.0, 16.0)),
        'ssd_d': 1.0 + nrm(ks[14], (L, SSD_HEADS), 0.05),
        'ssd_norm_w': 1.0 + nrm(ks[15], (L, SSD_WIDTH), 0.05),
        'q_norm_w': 1.0 + nrm(ks[16], (L, HEAD_DIM), 0.05),
        'k_norm_w': 1.0 + nrm(ks[17], (L, HEAD_DIM), 0.05),
        'w_out': nrm(ks[18], (L, MIX_WIDTH, D_MODEL), MIX_WIDTH ** -0.5),
        'ffn_w_gate': nrm(ks[19], (L, D_MODEL, FFN_HIDDEN), D_MODEL ** -0.5),
        'ffn_w_up': nrm(ks[20], (L, D_MODEL, FFN_HIDDEN), D_MODEL ** -0.5),
        'ffn_w_down': nrm(ks[21], (L, FFN_HIDDEN, D_MODEL), FFN_HIDDEN ** -0.5),
        'final_norm_w': 1.0 + nrm(ks[22], (D_MODEL,), 0.05),
    }


def reference(x, c, ctx, c_ctx, mod_w, mod_b, norm_attn_w, norm_ffn_w, w_in, na_rpb,
              ssd_conv_w, ssd_conv_b, ssd_dt_bias, ssd_a_log, ssd_d, ssd_norm_w,
              q_norm_w, k_norm_w, w_out, ffn_w_gate, ffn_w_up, ffn_w_down, final_norm_w):
    cx = ctx
    splits = [NA_IN, NA_IN + SSD_IN]
    for i in range(DEPTH):
        ctx_out = i < DEPTH - 1
        mod = (jax.nn.silu(c) @ mod_w[i] + mod_b[i])[:, None, :]
        mod_c = (jax.nn.silu(c_ctx) @ mod_w[i] + mod_b[i])[None, None, :]
        sh_m, sc_m, g_m, sh_f, sc_f, g_f = jnp.split(mod, 6, axis=-1)
        csh_m, csc_m, cg_m, csh_f, csc_f, cg_f = jnp.split(mod_c, 6, axis=-1)
        h = rms_norm(x, norm_attn_w[i]) * (1.0 + sc_m) + sh_m
        hc = rms_norm(cx, norm_attn_w[i]) * (1.0 + csc_m) + csh_m
        ua, ub, ug = jnp.split(h @ w_in[i], splits, axis=-1)
        uca, ucb, ucg = jnp.split(hc @ w_in[i], splits, axis=-1)
        ya, yca = neighbourhood_attention(ua, uca, na_rpb[i], ctx_out)
        yb, ycb = ssd_mixer(ub, ucb, ssd_conv_w[i], ssd_conv_b[i], ssd_dt_bias[i], ssd_a_log[i],
                            ssd_d[i], ssd_norm_w[i], ctx_out)
        yg, ycg = gqa_attention(ug, ucg, q_norm_w[i], k_norm_w[i], ctx_out)
        x = x + g_m * (jnp.concatenate([ya, yb, yg], axis=-1) @ w_out[i])
        hf = rms_norm(x, norm_ffn_w[i]) * (1.0 + sc_f) + sh_f
        x = x + g_f * swiglu(hf, ffn_w_gate[i], ffn_w_up[i], ffn_w_down[i])
        if ctx_out:
            cx = cx + cg_m * (jnp.concatenate([yca, ycb, ycg], axis=-1) @ w_out[i])
            hcf = rms_norm(cx, norm_ffn_w[i]) * (1.0 + csc_f) + csh_f
            cx = cx + cg_f * swiglu(hcf, ffn_w_gate[i], ffn_w_up[i], ffn_w_down[i])
    return rms_norm(x, final_norm_w)
```

```python
import functools
import math

import jax
import jax.numpy as jnp
from jax import lax
from jax.experimental import pallas as pl
from jax.experimental.pallas import tpu as pltpu

F32 = jnp.float32
BF16 = jnp.bfloat16

D_MODEL = 1024
GRID_W = 64
HEAD_DIM = 64
NA_WIDTH = 256
NA_HEADS = 4
NA_WIN_ROWS = 8
NA_WIN_COLS = 16
SSD_WIDTH = 512
SSD_HEADS = 8
SSD_GROUPS = 2
SSD_STATE = 128
SSD_CONV = 5
SSD_CHUNK = 128
SSD_CONV_CH = SSD_WIDTH + 2 * SSD_GROUPS * SSD_STATE
GQA_WIDTH = 256
GQA_Q_HEADS = 4
GQA_KV_HEADS = 2
ROPE_THETA = 10000.0
ROPE_PAIRS = HEAD_DIM // 4
FFN_HIDDEN = 2816
EPS = 1e-6
ATTN_SCALE = HEAD_DIM ** -0.5

LANES = 128
SUBLANES = 8
VMEM_LIMIT_BYTES = 56 * 1024 * 1024

ROW_TILE = 256
NA_ROWS_PER_STEP = 4
GQA_Q_TILE = 256
GQA_K_TILE = 512

C_NA = 0
C_Z = C_NA + 3 * NA_WIDTH
C_XBC = C_Z + SSD_WIDTH
C_DT = C_XBC + SSD_CONV_CH
C_GQ = C_DT + LANES
C_GK = C_GQ + GQA_WIDTH
C_GV = C_GK + 2 * LANES
C_END = C_GV + 2 * LANES


def _silu(v):
    return v * (1.0 / (1.0 + jnp.exp(-v)))


def _softplus(v):
    return jnp.maximum(v, 0.0) + jnp.log(1.0 + jnp.exp(-jnp.abs(v)))


def _split3(v):
    hi = v.astype(BF16)
    r1 = v - hi.astype(F32)
    mid = r1.astype(BF16)
    lo = (r1 - mid.astype(F32)).astype(BF16)
    return hi, mid, lo


def _dot(a, b):
    return jnp.dot(a, b, preferred_element_type=F32)


def _dot_nt(a, b):
    return lax.dot_general(a, b, (((1,), (1,)), ((), ())), preferred_element_type=F32)


def _dot_tn(a, b):
    return lax.dot_general(a, b, (((0,), (0,)), ((), ())), preferred_element_type=F32)


def _exact_dot(v, sel):
    hi, mid, lo = _split3(v)
    return _dot(hi, sel) + _dot(mid, sel) + _dot(lo, sel)


def _exact_dot_lhs(sel, v):
    hi, mid, lo = _split3(v)
    return _dot(sel, hi) + _dot(sel, mid) + _dot(sel, lo)


def _params(*sem):
    return pltpu.CompilerParams(dimension_semantics=sem, vmem_limit_bytes=VMEM_LIMIT_BYTES)


def _resident(shape, index_map):
    return pl.BlockSpec(shape, index_map, pipeline_mode=pl.Buffered(1))


def _mod_kernel(cc_ref, w_ref, b_ref, o_ref):
    a = _silu(cc_ref[...])
    o_ref[0] = jnp.dot(a, w_ref[0], preferred_element_type=F32) + b_ref[0]


def _modulation(cc, mod_w, mod_b):
    depth = mod_w.shape[0]
    ncol = mod_w.shape[2] // D_MODEL
    return pl.pallas_call(
        _mod_kernel,
        grid=(depth, ncol),
        in_specs=[
            pl.BlockSpec((SUBLANES, D_MODEL), lambda l, j: (0, 0)),
            pl.BlockSpec((1, D_MODEL, D_MODEL), lambda l, j: (l, 0, j)),
            pl.BlockSpec((1, 1, D_MODEL), lambda l, j: (l, 0, j)),
        ],
        out_specs=pl.BlockSpec((1, SUBLANES, D_MODEL), lambda l, j: (l, 0, j)),
        out_shape=jax.ShapeDtypeStruct((depth, SUBLANES, ncol * D_MODEL), F32),
        compiler_params=_params("arbitrary", "arbitrary"),
        name="modulation",
    )(cc, mod_w, mod_b.reshape(depth, 1, -1))


def _inproj_kernel(x_ref, mod_ref, nw_ref, w_ref, dtb_ref, qkw_ref, cos_ref, sin_ref, ones_ref,
                   na_ref, z_ref, xbc_ref, dt_ref, gq_ref, gk_ref, gv_ref):
    x = x_ref[...]
    ms = jnp.mean(x * x, axis=-1, keepdims=True)
    xn = x * lax.rsqrt(ms + EPS) * nw_ref[...]
    sh = mod_ref[:, 0:D_MODEL]
    sc = mod_ref[:, D_MODEL:2 * D_MODEL]
    h = (xn * (1.0 + sc) + sh).astype(BF16)
    u = _dot(h, w_ref[...])
    na_ref[...] = u[:, C_NA:C_Z].astype(BF16)
    z_ref[...] = u[:, C_Z:C_XBC]
    xbc_ref[...] = u[:, C_XBC:C_DT]
    dt_ref[...] = _softplus(u[:, C_DT:C_GQ] + dtb_ref[...])
    gv_ref[...] = u[:, C_GV:C_END].astype(BF16)
    g = u[:, C_GQ:C_GV]
    gsq = g * g
    hi = gsq.astype(BF16)
    lo = (gsq - hi.astype(F32)).astype(BF16)
    ss = _dot(hi, ones_ref[...]) + _dot(lo, ones_ref[...])
    gn = g * lax.rsqrt(ss * (1.0 / HEAD_DIM) + EPS) * qkw_ref[...]
    width = gn.shape[-1]
    lane = lax.broadcasted_iota(jnp.int32, gn.shape, 1)
    first = (lane % (2 * ROPE_PAIRS)) < ROPE_PAIRS
    partner = jnp.where(first, pltpu.roll(gn, width - ROPE_PAIRS, 1), pltpu.roll(gn, ROPE_PAIRS, 1))
    cos = jnp.concatenate([cos_ref[...]] * (width // LANES), axis=-1)
    sin = jnp.concatenate([sin_ref[...]] * (width // LANES), axis=-1)
    gr = gn * cos + partner * sin
    gq_ref[...] = gr[:, :GQA_WIDTH].astype(BF16)
    gk_ref[...] = gr[:, GQA_WIDTH:].astype(BF16)


def _inproj(x_all, mod4, layer, norm_w, w_cat, dt_bias_pad, qk_w, cos_t, sin_t, ones_bd):
    t = x_all.shape[0]
    n = t // ROW_TILE
    row = lambda i: (i, 0)
    const = lambda i: (0, 0)
    outs = [
        (3 * NA_WIDTH, BF16), (SSD_WIDTH, F32), (SSD_CONV_CH, F32), (LANES, F32),
        (GQA_WIDTH, BF16), (2 * LANES, BF16), (2 * LANES, BF16),
    ]
    return pl.pallas_call(
        _inproj_kernel,
        grid=(n,),
        in_specs=[
            pl.BlockSpec((ROW_TILE, D_MODEL), row),
            pl.BlockSpec((None, None, 1, 6 * D_MODEL), lambda i: (layer, jnp.minimum(i, 1), 0, 0)),
            pl.BlockSpec((1, D_MODEL), const),
            _resident((D_MODEL, C_END), const),
            pl.BlockSpec((1, LANES), const),
            pl.BlockSpec((1, 2 * GQA_WIDTH), const),
            pl.BlockSpec((ROW_TILE, LANES), row),
            pl.BlockSpec((ROW_TILE, LANES), row),
            _resident((2 * GQA_WIDTH, 2 * GQA_WIDTH), const),
        ],
        out_specs=[pl.BlockSpec((ROW_TILE, w), row) for w, _ in outs],
        out_shape=[jax.ShapeDtypeStruct((t, w), d) for w, d in outs],
        compiler_params=_params("parallel"),
        name="inproj",
    )(x_all, mod4, norm_w, w_cat, dt_bias_pad, qk_w, cos_t, sin_t, ones_bd)


def _conv_kernel(x_ref, prev_ref, next_ref, w_ref, b_ref, o_ref, ext_ref):
    i = pl.program_id(0)
    n = pl.num_programs(0)
    rows = x_ref.shape[0]
    has_prev = i >= 2
    has_next = jnp.logical_and(i >= 1, i < n - 1)
    ext_ref[0:SUBLANES, :] = jnp.where(has_prev, prev_ref[...], 0.0)
    ext_ref[SUBLANES:SUBLANES + rows, :] = x_ref[...]
    ext_ref[SUBLANES + rows:2 * SUBLANES + rows, :] = jnp.where(has_next, next_ref[...], 0.0)
    acc = jnp.broadcast_to(b_ref[...], x_ref.shape)
    half = SSD_CONV // 2
    for j in range(SSD_CONV):
        acc = acc + w_ref[j:j + 1, :] * ext_ref[pl.ds(SUBLANES - half + j, rows), :]
    o_ref[...] = _silu(acc)


def _ssd_conv(xbc, conv_w_pad, conv_b):
    t, ch = xbc.shape
    n = t // ROW_TILE
    per = ROW_TILE // SUBLANES
    last = t // SUBLANES - 1
    return pl.pallas_call(
        _conv_kernel,
        grid=(n,),
        in_specs=[
            pl.BlockSpec((ROW_TILE, ch), lambda i: (i, 0)),
            pl.BlockSpec((SUBLANES, ch), lambda i: (jnp.maximum(i * per - 1, 0), 0)),
            pl.BlockSpec((SUBLANES, ch), lambda i: (jnp.minimum((i + 1) * per, last), 0)),
            pl.BlockSpec((SUBLANES, ch), lambda i: (0, 0)),
            pl.BlockSpec((1, ch), lambda i: (0, 0)),
        ],
        out_specs=pl.BlockSpec((ROW_TILE, ch), lambda i: (i, 0)),
        out_shape=jax.ShapeDtypeStruct((t, ch), F32),
        scratch_shapes=[pltpu.VMEM((ROW_TILE + 2 * SUBLANES, ch), F32)],
        compiler_params=_params("parallel"),
        name="ssd_conv",
    )(xbc, xbc, xbc, conv_w_pad, conv_b)


def _ssd_direction(xbc, dt, a_row, tri, expand, s_ref, lane0, backward):
    q = SSD_CHUNK
    x = xbc[:, :SSD_WIDTH]
    nb = SSD_GROUPS * SSD_STATE
    bmat = xbc[:, SSD_WIDTH:SSD_WIDTH + nb].astype(BF16)
    cmat = xbc[:, SSD_WIDTH + nb:].astype(BF16)
    cum = _exact_dot_lhs(tri, dt * a_row)
    last = 0 if backward else q - 1
    total = cum[last:last + 1, :]
    stacked = jnp.concatenate([dt, jnp.exp(cum), jnp.exp(total - cum)], axis=0)
    full = _exact_dot(stacked, expand)
    dt_full, ea_full, te_full = full[0:q], full[q:2 * q], full[2 * q:3 * q]
    xr = x * dt_full
    xr_b = xr.astype(BF16)
    xt_b = (xr * te_full).astype(BF16)
    cum_t = cum.T
    ti = lax.broadcasted_iota(jnp.int32, (q, q), 0)
    si = lax.broadcasted_iota(jnp.int32, (q, q), 1)
    keep = (si >= ti) if backward else (si <= ti)
    lane = lax.broadcasted_iota(jnp.int32, (q, LANES), 1)
    heads_per_group = SSD_HEADS // SSD_GROUPS
    width_g = heads_per_group * HEAD_DIM
    pieces = []
    decay_total = ea_full[last:last + 1, :]
    for g in range(SSD_GROUPS):
        bg = bmat[:, g * SSD_STATE:(g + 1) * SSD_STATE]
        cg = cmat[:, g * SSD_STATE:(g + 1) * SSD_STATE]
        gmat = _dot_nt(cg, bg)
        s_old = s_ref[:, g * width_g:(g + 1) * width_g]
        y_off = _dot(cg, s_old.astype(BF16)) * ea_full[:, g * width_g:(g + 1) * width_g]
        s_new = _dot_tn(bg, xt_b[:, g * width_g:(g + 1) * width_g])
        s_ref[:, g * width_g:(g + 1) * width_g] = decay_total[:, g * width_g:(g + 1) * width_g] * s_old + s_new
        for pair in range(heads_per_group // 2):
            col = g * width_g + pair * LANES
            xr_pair = xr_b[:, col:col + LANES]
            ys = []
            for r in range(2):
                hl = lane0 + g * heads_per_group + 2 * pair + r
                seg = jnp.broadcast_to(cum[:, hl:hl + 1], (q, q)) - jnp.broadcast_to(cum_t[hl:hl + 1, :], (q, q))
                dec = jnp.exp(jnp.where(keep, seg, -jnp.inf))
                ys.append(_dot((gmat * dec).astype(BF16), xr_pair))
            y_diag = jnp.where(lane < HEAD_DIM, ys[0], ys[1])
            pieces.append(y_diag + y_off[:, pair * LANES:(pair + 1) * LANES])
    return jnp.concatenate(pieces, axis=-1)


def _ssd_kernel(xf_ref, dtf_ref, xb_ref, dtb_ref, a_ref, tril_ref, triu_ref, ef_ref, eb_ref,
                yf_ref, yb_ref, sf_ref, sb_ref):
    @pl.when(pl.program_id(0) == 0)
    def _():
        sf_ref[...] = jnp.zeros_like(sf_ref)
        sb_ref[...] = jnp.zeros_like(sb_ref)

    a_row = a_ref[...]
    yf_ref[...] = _ssd_direction(xf_ref[...], dtf_ref[...], a_row, tril_ref[...], ef_ref[...], sf_ref, 0, False)
    yb_ref[...] = _ssd_direction(xb_ref[...], dtb_ref[...], a_row, triu_ref[...], eb_ref[...], sb_ref,
                                 SSD_HEADS, True)


def _ssd_scan(xconv, dt, a_row, n_ctx_chunks):
    t = xconv.shape[0]
    n = t // SSD_CHUNK
    q = SSD_CHUNK
    r = jnp.arange(q)
    tril = (r[None, :] <= r[:, None]).astype(BF16)
    triu = (r[None, :] >= r[:, None]).astype(BF16)
    lanes = jnp.arange(LANES)[:, None]
    head_of_col = (jnp.arange(SSD_WIDTH) // HEAD_DIM)[None, :]
    expand_f = (lanes == head_of_col).astype(BF16)
    expand_b = (lanes == head_of_col + SSD_HEADS).astype(BF16)

    def fwd(i):
        return (i, 0)

    def bwd(i):
        return (jnp.where(i < n_ctx_chunks, n_ctx_chunks - 1 - i, n - 1 - (i - n_ctx_chunks)), 0)

    const = lambda i: (0, 0)
    return pl.pallas_call(
        _ssd_kernel,
        grid=(n,),
        in_specs=[
            pl.BlockSpec((q, SSD_CONV_CH), fwd),
            pl.BlockSpec((q, LANES), fwd),
            pl.BlockSpec((q, SSD_CONV_CH), bwd),
            pl.BlockSpec((q, LANES), bwd),
            pl.BlockSpec((1, LANES), const),
            pl.BlockSpec((q, q), const),
            pl.BlockSpec((q, q), const),
            pl.BlockSpec((LANES, SSD_WIDTH), const),
            pl.BlockSpec((LANES, SSD_WIDTH), const),
        ],
        out_specs=[pl.BlockSpec((q, SSD_WIDTH), fwd), pl.BlockSpec((q, SSD_WIDTH), bwd)],
        out_shape=[jax.ShapeDtypeStruct((t, SSD_WIDTH), F32)] * 2,
        scratch_shapes=[pltpu.VMEM((SSD_STATE, SSD_WIDTH), F32)] * 2,
        compiler_params=_params("arbitrary"),
        name="ssd_scan",
    )(xconv, dt, xconv, dt, a_row, tril, triu, expand_f, expand_b)


def _na_kernel(q_ref, k_ref, v_ref, bias_ref, o_ref, *, n_ctx, n_rows):
    i = pl.program_id(0)
    is_ctx = i == 0
    kc = k_ref[0:n_ctx, :]
    vc = v_ref[0:n_ctx, :]
    lane = lax.broadcasted_iota(jnp.int32, (GRID_W, NA_WIDTH), 1)
    win = NA_WIN_ROWS * GRID_W
    for j in range(NA_ROWS_PER_STEP):
        r = jnp.maximum((i - 1) * NA_ROWS_PER_STEP + j, 0)
        r_start = jnp.clip(r - NA_WIN_ROWS // 2, 0, n_rows - NA_WIN_ROWS)
        variant = jnp.where(is_ctx, NA_WIN_ROWS, r_start - r + NA_WIN_ROWS - 1)
        start = pl.multiple_of(n_ctx + r_start * GRID_W, GRID_W)
        kw = k_ref[pl.ds(start, win), :]
        vw = v_ref[pl.ds(start, win), :]
        qj = q_ref[j * GRID_W:(j + 1) * GRID_W, :]
        out = jnp.zeros((GRID_W, NA_WIDTH), F32)
        for h in range(NA_HEADS):
            mine = (lane >= h * HEAD_DIM) & (lane < (h + 1) * HEAD_DIM)
            qm = jnp.where(mine, qj, jnp.zeros_like(qj))
            s_w = _dot_nt(qm, kw) + bias_ref[variant, h]
            s_c = _dot_nt(qm, kc)
            m = jnp.maximum(jnp.max(s_w, axis=-1, keepdims=True), jnp.max(s_c, axis=-1, keepdims=True))
            p_w = jnp.exp(s_w - m)
            p_c = jnp.exp(s_c - m)
            l = jnp.sum(p_w, axis=-1, keepdims=True) + jnp.sum(p_c, axis=-1, keepdims=True)
            y = (_dot(p_w.astype(BF16), vw) + _dot(p_c.astype(BF16), vc)) / l
            out = jnp.where(mine, y, out)
        o_ref[j * GRID_W:(j + 1) * GRID_W, :] = out.astype(o_ref.dtype)


def _na_bias_table(rpb):
    col = jnp.arange(GRID_W)
    c_start = jnp.clip(col - NA_WIN_COLS // 2, 0, GRID_W - NA_WIN_COLS)
    in_win = (col[None, :] >= c_start[:, None]) & (col[None, :] < c_start[:, None] + NA_WIN_COLS)
    dc = jnp.clip(col[None, :] - col[:, None] + NA_WIN_COLS - 1, 0, 2 * NA_WIN_COLS - 2)
    off = jnp.arange(NA_WIN_ROWS)
    dr = off[:, None] + jnp.arange(NA_WIN_ROWS)[None, :]
    tab = rpb[:, dr[:, None, :, None], dc[None, :, None, :]]
    tab = jnp.where(in_win[None, None, :, None, :], tab, -jnp.inf)
    tab = tab.transpose(1, 0, 2, 3, 4).reshape(NA_WIN_ROWS, NA_HEADS, GRID_W, NA_WIN_ROWS * GRID_W)
    masked = jnp.full((1,) + tab.shape[1:], -jnp.inf, F32)
    return jnp.concatenate([tab, masked], axis=0)


def _neighbourhood_attention(na, bias_tab, n_ctx):
    t = na.shape[0]
    n_rows = (t - n_ctx) // GRID_W
    step_rows = NA_ROWS_PER_STEP * GRID_W
    n = t // step_rows
    return pl.pallas_call(
        functools.partial(_na_kernel, n_ctx=n_ctx, n_rows=n_rows),
        grid=(n,),
        in_specs=[
            pl.BlockSpec((step_rows, NA_WIDTH), lambda i: (i, 0)),
            _resident((t, NA_WIDTH), lambda i: (0, 1)),
            _resident((t, NA_WIDTH), lambda i: (0, 2)),
            _resident(bias_tab.shape, lambda i: (0, 0, 0, 0)),
        ],
        out_specs=pl.BlockSpec((step_rows, NA_WIDTH), lambda i: (i, 0)),
        out_shape=jax.ShapeDtypeStruct((t, NA_WIDTH), BF16),
        compiler_params=_params("parallel"),
        name="neighbourhood_attention",
    )(na, na, na, bias_tab)


def _gqa_kernel(q_ref, k_ref, v_ref, o_ref, m_ref, l_ref, acc_ref, *, n_ctx, n_lat_blocks):
    i = pl.program_id(1)
    tq = q_ref.shape[0]
    q = q_ref[...]
    lane = lax.broadcasted_iota(jnp.int32, q.shape, 1)
    low = lane < HEAD_DIM
    zero = jnp.zeros_like(q)
    q2 = jnp.concatenate([jnp.where(low, q, zero), jnp.where(low, zero, q)], axis=0)
    m_ref[...] = jnp.full_like(m_ref, -jnp.inf)
    l_ref[...] = jnp.zeros_like(l_ref)
    acc_ref[...] = jnp.zeros_like(acc_ref)

    def block(start, size):
        k = k_ref[pl.ds(start, size), :]
        v = v_ref[pl.ds(start, size), :]
        s = _dot_nt(q2, k)
        m_old = m_ref[...]
        m_new = jnp.maximum(m_old, jnp.max(s, axis=-1, keepdims=True))
        alpha = jnp.exp(m_old - m_new)
        p = jnp.exp(s - m_new)
        l_ref[...] = alpha * l_ref[...] + jnp.sum(p, axis=-1, keepdims=True)
        acc_ref[...] = alpha * acc_ref[...] + _dot(p.astype(BF16), v)
        m_ref[...] = m_new

    block(0, n_ctx)

    def body(kb, carry):
        block(pl.multiple_of(n_ctx + kb * GQA_K_TILE, GQA_K_TILE // 2), GQA_K_TILE)
        return carry

    lax.fori_loop(0, jnp.where(i == 0, 0, n_lat_blocks), body, 0)
    o = acc_ref[...] / l_ref[...]
    o_ref[...] = jnp.where(low, o[:tq], o[tq:]).astype(o_ref.dtype)


def _gqa_attention(gq, gk, gv, n_ctx):
    t = gq.shape[0]
    n = t // GQA_Q_TILE
    n_lat_blocks = (t - n_ctx) // GQA_K_TILE
    return pl.pallas_call(
        functools.partial(_gqa_kernel, n_ctx=n_ctx, n_lat_blocks=n_lat_blocks),
        grid=(GQA_KV_HEADS, n),
        in_specs=[
            pl.BlockSpec((GQA_Q_TILE, LANES), lambda g, i: (i, g)),
            pl.BlockSpec((t, LANES), lambda g, i: (0, g)),
            pl.BlockSpec((t, LANES), lambda g, i: (0, g)),
        ],
        out_specs=pl.BlockSpec((GQA_Q_TILE, LANES), lambda g, i: (i, g)),
        out_shape=jax.ShapeDtypeStruct((t, GQA_WIDTH), BF16),
        scratch_shapes=[
            pltpu.VMEM((2 * GQA_Q_TILE, 1), F32),
            pltpu.VMEM((2 * GQA_Q_TILE, 1), F32),
            pltpu.VMEM((2 * GQA_Q_TILE, LANES), F32),
        ],
        compiler_params=_params("arbitrary", "arbitrary"),
        name="gqa_attention",
    )(gq, gk, gv)


def _out_ffn_kernel(ya_ref, yf_ref, yb_ref, xs_ref, z_ref, yg_ref, x_ref, mod_ref,
                    dskip_ref, snw_ref, wo_ref, fnw_ref, wg_ref, wu_ref, wd_ref, final_ref,
                    o_ref, *, final):
    y = yf_ref[...] + yb_ref[...] + dskip_ref[...] * xs_ref[...]
    y = y * _silu(z_ref[...])
    ms = jnp.mean(y * y, axis=-1, keepdims=True)
    y = y * lax.rsqrt(ms + EPS) * snw_ref[...]
    mix = jnp.concatenate([ya_ref[...], y.astype(BF16), yg_ref[...]], axis=-1)
    g_m = mod_ref[:, 2 * D_MODEL:3 * D_MODEL]
    sh_f = mod_ref[:, 3 * D_MODEL:4 * D_MODEL]
    sc_f = mod_ref[:, 4 * D_MODEL:5 * D_MODEL]
    g_f = mod_ref[:, 5 * D_MODEL:6 * D_MODEL]
    x1 = x_ref[...] + g_m * _dot(mix, wo_ref[...])
    ms1 = jnp.mean(x1 * x1, axis=-1, keepdims=True)
    hf = (x1 * lax.rsqrt(ms1 + EPS) * fnw_ref[...] * (1.0 + sc_f) + sh_f).astype(BF16)
    act = (_silu(_dot(hf, wg_ref[...])) * _dot(hf, wu_ref[...])).astype(BF16)
    x2 = x1 + g_f * _dot(act, wd_ref[...])
    if final:
        ms2 = jnp.mean(x2 * x2, axis=-1, keepdims=True)
        x2 = x2 * lax.rsqrt(ms2 + EPS) * final_ref[...]
    o_ref[...] = x2


def _out_ffn(ya, yf, yb, xconv, z, yg, x_all, mod4, layer, d_full, ssd_nw, w_out, ffn_nw,
             w_gate, w_up, w_down, final_nw, final):
    t = x_all.shape[0]
    skip = 1 if final else 0
    n = t // ROW_TILE - skip
    row = lambda i: (i + skip, 0)
    const = lambda i: (0, 0)
    return pl.pallas_call(
        functools.partial(_out_ffn_kernel, final=final),
        grid=(n,),
        in_specs=[
            pl.BlockSpec((ROW_TILE, NA_WIDTH), row),
            pl.BlockSpec((ROW_TILE, SSD_WIDTH), row),
            pl.BlockSpec((ROW_TILE, SSD_WIDTH), row),
            pl.BlockSpec((ROW_TILE, SSD_WIDTH), row),
            pl.BlockSpec((ROW_TILE, SSD_WIDTH), row),
            pl.BlockSpec((ROW_TILE, GQA_WIDTH), row),
            pl.BlockSpec((ROW_TILE, D_MODEL), row),
            pl.BlockSpec((None, None, 1, 6 * D_MODEL), lambda i: (layer, jnp.minimum(i + skip, 1), 0, 0)),
            pl.BlockSpec((1, SSD_WIDTH), const),
            pl.BlockSpec((1, SSD_WIDTH), const),
            _resident((D_MODEL, D_MODEL), const),
            pl.BlockSpec((1, D_MODEL), const),
            _resident((D_MODEL, FFN_HIDDEN), const),
            _resident((D_MODEL, FFN_HIDDEN), const),
            _resident((FFN_HIDDEN, D_MODEL), const),
            pl.BlockSpec((1, D_MODEL), const),
        ],
        out_specs=pl.BlockSpec((ROW_TILE, D_MODEL), lambda i: (i, 0)),
        out_shape=jax.ShapeDtypeStruct((n * ROW_TILE, D_MODEL), F32),
        compiler_params=_params("parallel"),
        name="out_ffn",
    )(ya, yf, yb, xconv, z, yg, x_all, mod4, d_full, ssd_nw, w_out, ffn_nw, w_gate, w_up, w_down, final_nw)


def _rearranged_w_in(w):
    na_in = 3 * NA_WIDTH
    o_z = na_in
    o_xbc = o_z + SSD_WIDTH
    o_dt = o_xbc + SSD_CONV_CH
    o_gq = o_dt + 2 * SSD_HEADS
    o_gk = o_gq + GQA_WIDTH
    o_gv = o_gk + GQA_KV_HEADS * HEAD_DIM
    na = jnp.concatenate([w[:, :NA_WIDTH] * ATTN_SCALE, w[:, NA_WIDTH:na_in]], axis=1)
    dt = jnp.pad(w[:, o_dt:o_gq], ((0, 0), (0, LANES - 2 * SSD_HEADS)))

    def dup(seg):
        heads = [seg[:, h * HEAD_DIM:(h + 1) * HEAD_DIM] for h in range(GQA_KV_HEADS)]
        return jnp.concatenate([p for h in heads for p in (h, h)], axis=1)

    return jnp.concatenate(
        [na, w[:, o_z:o_xbc], w[:, o_xbc:o_dt], dt, w[:, o_gq:o_gk], dup(w[:, o_gk:o_gv]), dup(w[:, o_gv:])],
        axis=1).astype(BF16)


def _rope_tables(n_ctx, n_lat):
    freqs = ROPE_THETA ** (-jnp.arange(ROPE_PAIRS, dtype=F32) / ROPE_PAIRS)
    tok = jnp.arange(n_lat)
    rows = (tok // GRID_W).astype(F32)
    cols = (tok % GRID_W).astype(F32)

    def ang(pos):
        a = pos[:, None] * freqs[None, :]
        return jnp.concatenate([a, a], axis=-1)

    a = jnp.concatenate([ang(rows), ang(cols)], axis=-1)
    sign = jnp.where((jnp.arange(HEAD_DIM) % (2 * ROPE_PAIRS)) < ROPE_PAIRS, -1.0, 1.0).astype(F32)
    cos = jnp.concatenate([jnp.ones((n_ctx, HEAD_DIM), F32), jnp.cos(a)], axis=0)
    sin = jnp.concatenate([jnp.zeros((n_ctx, HEAD_DIM), F32), jnp.sin(a) * sign[None, :]], axis=0)
    return jnp.tile(cos, (1, LANES // HEAD_DIM)), jnp.tile(sin, (1, LANES // HEAD_DIM))


def kernel(x, c, ctx, c_ctx, mod_w, mod_b, norm_attn_w, norm_ffn_w, w_in, na_rpb, ssd_conv_w, ssd_conv_b,
           ssd_dt_bias, ssd_a_log, ssd_d, ssd_norm_w, q_norm_w, k_norm_w, w_out, ffn_w_gate, ffn_w_up,
           ffn_w_down, final_norm_w):
    depth = mod_w.shape[0]
    batch, n_lat, _ = x.shape
    n_ctx = ctx.shape[1]
    assert batch == 1 and n_ctx == ROW_TILE and n_lat % (NA_ROWS_PER_STEP * GRID_W) == 0
    assert n_lat % GQA_K_TILE == 0 and n_lat // GRID_W >= NA_WIN_ROWS

    x_all = jnp.concatenate([ctx[0], x[0]], axis=0)
    cc = jnp.zeros((SUBLANES, D_MODEL), F32).at[0].set(c_ctx).at[1].set(c[0])
    mod = _modulation(cc, mod_w, mod_b)
    mod4 = mod[:, :2].reshape(depth, 2, 1, 6 * D_MODEL)

    cos_t, sin_t = _rope_tables(n_ctx, n_lat)
    blk = jnp.arange(2 * GQA_WIDTH) // HEAD_DIM
    ones_bd = (blk[:, None] == blk[None, :]).astype(BF16)
    head_cols = jnp.arange(SSD_WIDTH) // HEAD_DIM

    for i in range(depth):
        final = i == depth - 1
        w_cat = _rearranged_w_in(w_in[i])
        dt_bias_pad = jnp.pad(ssd_dt_bias[i].reshape(1, -1), ((0, 0), (0, LANES - 2 * SSD_HEADS)))
        qk_w = jnp.concatenate([jnp.tile(q_norm_w[i] * ATTN_SCALE, GQA_Q_HEADS),
                                jnp.tile(k_norm_w[i], 2 * GQA_KV_HEADS)]).reshape(1, -1)
        na, z, xbc, dt, gq, gk, gv = _inproj(x_all, mod4, i, norm_attn_w[i].reshape(1, -1), w_cat,
                                             dt_bias_pad, qk_w, cos_t, sin_t, ones_bd)
        conv_w_pad = jnp.pad(ssd_conv_w[i], ((0, SUBLANES - SSD_CONV), (0, 0)))
        xconv = _ssd_conv(xbc, conv_w_pad, ssd_conv_b[i].reshape(1, -1))
        a_row = jnp.pad(-jnp.exp(ssd_a_log[i].astype(F32)).reshape(1, -1), ((0, 0), (0, LANES - 2 * SSD_HEADS)))
        yf, yb = _ssd_scan(xconv, dt, a_row, n_ctx // SSD_CHUNK)
        ya = _neighbourhood_attention(na, _na_bias_table(na_rpb[i]), n_ctx)
        yg = _gqa_attention(gq, gk, gv, n_ctx)
        d_full = ssd_d[i].astype(F32)[head_cols].reshape(1, -1)
        x_all = _out_ffn(ya, yf, yb, xconv, z, yg, x_all, mod4, i, d_full, ssd_norm_w[i].reshape(1, -1),
                         w_out[i].astype(BF16), norm_ffn_w[i].reshape(1, -1), ffn_w_gate[i].astype(BF16),
                         ffn_w_up[i].astype(BF16), ffn_w_down[i].astype(BF16),
                         final_norm_w.reshape(1, -1), final)
    return x_all[None]
```

```python
import functools
import math

import jax
import jax.numpy as jnp
from jax import lax
from jax.experimental import pallas as pl
from jax.experimental.pallas import tpu as pltpu

F32 = jnp.float32
BF16 = jnp.bfloat16

D_MODEL = 1024
GRID_W = 64
HEAD_DIM = 64
NA_WIDTH = 256
NA_HEADS = 4
NA_WIN_ROWS = 8
NA_WIN_COLS = 16
SSD_WIDTH = 512
SSD_HEADS = 8
SSD_GROUPS = 2
SSD_STATE = 128
SSD_CONV = 5
SSD_CHUNK = 128
SSD_CONV_CH = SSD_WIDTH + 2 * SSD_GROUPS * SSD_STATE
GQA_WIDTH = 256
GQA_Q_HEADS = 4
GQA_KV_HEADS = 2
ROPE_THETA = 10000.0
ROPE_PAIRS = HEAD_DIM // 4
FFN_HIDDEN = 2816
EPS = 1e-6
ATTN_SCALE = HEAD_DIM ** -0.5
LOG2E = math.log2(math.e)

LANES = 128
SUBLANES = 8
VMEM_LIMIT_BYTES = 56 * 1024 * 1024

ROW_TILE = 256
NA_ROWS_PER_STEP = 4
GQA_Q_TILE = 256
GQA_K_TILE = 1024

C_NA = 0
C_Z = C_NA + 3 * NA_WIDTH
C_XBC = C_Z + SSD_WIDTH
C_DT = C_XBC + SSD_CONV_CH
C_GQ = C_DT + LANES
C_GK = C_GQ + GQA_WIDTH
C_GV = C_GK + 2 * LANES
C_END = C_GV + 2 * LANES


def _silu(v):
    return v * (1.0 / (1.0 + jnp.exp(-v)))


def _softplus(v):
    return jnp.maximum(v, 0.0) + jnp.log(1.0 + jnp.exp(-jnp.abs(v)))


def _split3(v):
    hi = v.astype(BF16)
    r1 = v - hi.astype(F32)
    mid = r1.astype(BF16)
    lo = (r1 - mid.astype(F32)).astype(BF16)
    return hi, mid, lo


def _dot(a, b):
    return jnp.dot(a, b, preferred_element_type=F32)


def _dot_nt(a, b):
    return lax.dot_general(a, b, (((1,), (1,)), ((), ())), preferred_element_type=F32)


def _dot_tn(a, b):
    return lax.dot_general(a, b, (((0,), (0,)), ((), ())), preferred_element_type=F32)


def _exact_dot(v, sel):
    hi, mid, lo = _split3(v)
    return _dot(hi, sel) + _dot(mid, sel) + _dot(lo, sel)


def _exact_dot_lhs(sel, v):
    hi, mid, lo = _split3(v)
    return _dot(sel, hi) + _dot(sel, mid) + _dot(sel, lo)


def _params(*sem):
    return pltpu.CompilerParams(dimension_semantics=sem, vmem_limit_bytes=VMEM_LIMIT_BYTES)


def _resident(shape, index_map):
    return pl.BlockSpec(shape, index_map, pipeline_mode=pl.Buffered(1))


def _mod_kernel(cc_ref, w_ref, b_ref, o_ref):
    a = _silu(cc_ref[...])
    o_ref[0] = jnp.dot(a, w_ref[0], preferred_element_type=F32) + b_ref[0]


def _modulation(cc, mod_w, mod_b):
    depth = mod_w.shape[0]
    ncol = mod_w.shape[2] // D_MODEL
    return pl.pallas_call(
        _mod_kernel,
        grid=(depth, ncol),
        in_specs=[
            pl.BlockSpec((SUBLANES, D_MODEL), lambda l, j: (0, 0)),
            pl.BlockSpec((1, D_MODEL, D_MODEL), lambda l, j: (l, 0, j)),
            pl.BlockSpec((1, 1, D_MODEL), lambda l, j: (l, 0, j)),
        ],
        out_specs=pl.BlockSpec((1, SUBLANES, D_MODEL), lambda l, j: (l, 0, j)),
        out_shape=jax.ShapeDtypeStruct((depth, SUBLANES, ncol * D_MODEL), F32),
        compiler_params=_params("arbitrary", "arbitrary"),
        name="modulation",
    )(cc, mod_w, mod_b.reshape(depth, 1, -1))


def _inproj_kernel(x_ref, mod_ref, nw_ref, w_ref, dtb_ref, qkw_ref, cos_ref, sin_ref, ones_ref,
                   na_ref, z_ref, xbc_ref, dt_ref, gq_ref, gk_ref, gv_ref):
    x = x_ref[...]
    ms = jnp.mean(x * x, axis=-1, keepdims=True)
    xn = x * lax.rsqrt(ms + EPS) * nw_ref[...]
    sh = mod_ref[:, 0:D_MODEL]
    sc = mod_ref[:, D_MODEL:2 * D_MODEL]
    h = (xn * (1.0 + sc) + sh).astype(BF16)
    u = _dot(h, w_ref[...])
    na_ref[...] = u[:, C_NA:C_Z].astype(BF16)
    z_ref[...] = u[:, C_Z:C_XBC]
    xbc_ref[...] = u[:, C_XBC:C_DT]
    dt_ref[...] = _softplus(u[:, C_DT:C_GQ] + dtb_ref[...])
    vlane = lax.broadcasted_iota(jnp.int32, (x.shape[0], C_END - C_GV), 1)
    gv_ref[...] = jnp.where(vlane % LANES < HEAD_DIM, u[:, C_GV:C_END], 1.0).astype(BF16)
    g = u[:, C_GQ:C_GV]
    gsq = g * g
    hi = gsq.astype(BF16)
    lo = (gsq - hi.astype(F32)).astype(BF16)
    ss = _dot(hi, ones_ref[...]) + _dot(lo, ones_ref[...])
    gn = g * lax.rsqrt(ss * (1.0 / HEAD_DIM) + EPS) * qkw_ref[...]
    width = gn.shape[-1]
    lane = lax.broadcasted_iota(jnp.int32, gn.shape, 1)
    first = (lane % (2 * ROPE_PAIRS)) < ROPE_PAIRS
    partner = jnp.where(first, pltpu.roll(gn, width - ROPE_PAIRS, 1), pltpu.roll(gn, ROPE_PAIRS, 1))
    cos = jnp.concatenate([cos_ref[...]] * (width // LANES), axis=-1)
    sin = jnp.concatenate([sin_ref[...]] * (width // LANES), axis=-1)
    gr = gn * cos + partner * sin
    gq_ref[...] = gr[:, :GQA_WIDTH].astype(BF16)
    gk_ref[...] = gr[:, GQA_WIDTH:].astype(BF16)


def _inproj(x_all, mod4, layer, norm_w, w_cat, dt_bias_pad, qk_w, cos_t, sin_t, ones_bd):
    t = x_all.shape[0]
    n = t // ROW_TILE
    row = lambda i: (i, 0)
    const = lambda i: (0, 0)
    outs = [
        (3 * NA_WIDTH, BF16), (SSD_WIDTH, F32), (SSD_CONV_CH, F32), (LANES, F32),
        (GQA_WIDTH, BF16), (2 * LANES, BF16), (2 * LANES, BF16),
    ]
    return pl.pallas_call(
        _inproj_kernel,
        grid=(n,),
        in_specs=[
            pl.BlockSpec((ROW_TILE, D_MODEL), row),
            pl.BlockSpec((None, None, 1, 6 * D_MODEL), lambda i: (layer, jnp.minimum(i, 1), 0, 0)),
            pl.BlockSpec((1, D_MODEL), const),
            _resident((D_MODEL, C_END), const),
            pl.BlockSpec((1, LANES), const),
            pl.BlockSpec((1, 2 * GQA_WIDTH), const),
            pl.BlockSpec((ROW_TILE, LANES), row),
            pl.BlockSpec((ROW_TILE, LANES), row),
            _resident((2 * GQA_WIDTH, 2 * GQA_WIDTH), const),
        ],
        out_specs=[pl.BlockSpec((ROW_TILE, w), row) for w, _ in outs],
        out_shape=[jax.ShapeDtypeStruct((t, w), d) for w, d in outs],
        compiler_params=_params("parallel"),
        name="inproj",
    )(x_all, mod4, norm_w, w_cat, dt_bias_pad, qk_w, cos_t, sin_t, ones_bd)


def _conv_kernel(x_ref, prev_ref, next_ref, w_ref, b_ref, o_ref, ext_ref):
    i = pl.program_id(0)
    n = pl.num_programs(0)
    rows = x_ref.shape[0]
    has_prev = i >= 2
    has_next = jnp.logical_and(i >= 1, i < n - 1)
    ext_ref[0:SUBLANES, :] = jnp.where(has_prev, prev_ref[...], 0.0)
    ext_ref[SUBLANES:SUBLANES + rows, :] = x_ref[...]
    ext_ref[SUBLANES + rows:2 * SUBLANES + rows, :] = jnp.where(has_next, next_ref[...], 0.0)
    acc = jnp.broadcast_to(b_ref[...], x_ref.shape)
    half = SSD_CONV // 2
    for j in range(SSD_CONV):
        acc = acc + w_ref[j:j + 1, :] * ext_ref[pl.ds(SUBLANES - half + j, rows), :]
    o_ref[...] = _silu(acc)


def _ssd_conv(xbc, conv_w_pad, conv_b):
    t, ch = xbc.shape
    n = t // ROW_TILE
    per = ROW_TILE // SUBLANES
    last = t // SUBLANES - 1
    return pl.pallas_call(
        _conv_kernel,
        grid=(n,),
        in_specs=[
            pl.BlockSpec((ROW_TILE, ch), lambda i: (i, 0)),
            pl.BlockSpec((SUBLANES, ch), lambda i: (jnp.maximum(i * per - 1, 0), 0)),
            pl.BlockSpec((SUBLANES, ch), lambda i: (jnp.minimum((i + 1) * per, last), 0)),
            pl.BlockSpec((SUBLANES, ch), lambda i: (0, 0)),
            pl.BlockSpec((1, ch), lambda i: (0, 0)),
        ],
        out_specs=pl.BlockSpec((ROW_TILE, ch), lambda i: (i, 0)),
        out_shape=jax.ShapeDtypeStruct((t, ch), F32),
        scratch_shapes=[pltpu.VMEM((ROW_TILE + 2 * SUBLANES, ch), F32)],
        compiler_params=_params("parallel"),
        name="ssd_conv",
    )(xbc, xbc, xbc, conv_w_pad, conv_b)


def _ssd_direction(xbc, dt, a_row, tri, expand, s_ref, lane0, backward):
    q = SSD_CHUNK
    x = xbc[:, :SSD_WIDTH]
    nb = SSD_GROUPS * SSD_STATE
    bmat = xbc[:, SSD_WIDTH:SSD_WIDTH + nb].astype(BF16)
    cmat = xbc[:, SSD_WIDTH + nb:].astype(BF16)
    cum = _exact_dot_lhs(tri, dt * a_row)
    last = 0 if backward else q - 1
    total = cum[last:last + 1, :]
    stacked = jnp.concatenate([dt, jnp.exp(cum), jnp.exp(total - cum)], axis=0)
    full = _exact_dot(stacked, expand)
    dt_full, ea_full, te_full = full[0:q], full[q:2 * q], full[2 * q:3 * q]
    xr = x * dt_full
    xr_b = xr.astype(BF16)
    xt_b = (xr * te_full).astype(BF16)
    cum_t = cum.T
    ti = lax.broadcasted_iota(jnp.int32, (q, q), 0)
    si = lax.broadcasted_iota(jnp.int32, (q, q), 1)
    keep = (si >= ti) if backward else (si <= ti)
    lane = lax.broadcasted_iota(jnp.int32, (q, LANES), 1)
    heads_per_group = SSD_HEADS // SSD_GROUPS
    width_g = heads_per_group * HEAD_DIM
    pieces = []
    decay_total = ea_full[last:last + 1, :]
    for g in range(SSD_GROUPS):
        bg = bmat[:, g * SSD_STATE:(g + 1) * SSD_STATE]
        cg = cmat[:, g * SSD_STATE:(g + 1) * SSD_STATE]
        gmat = _dot_nt(cg, bg)
        s_old = s_ref[:, g * width_g:(g + 1) * width_g]
        y_off = _dot(cg, s_old.astype(BF16)) * ea_full[:, g * width_g:(g + 1) * width_g]
        s_new = _dot_tn(bg, xt_b[:, g * width_g:(g + 1) * width_g])
        s_ref[:, g * width_g:(g + 1) * width_g] = decay_total[:, g * width_g:(g + 1) * width_g] * s_old + s_new
        for pair in range(heads_per_group // 2):
            col = g * width_g + pair * LANES
            xr_pair = xr_b[:, col:col + LANES]
            ys = []
            for r in range(2):
                hl = lane0 + g * heads_per_group + 2 * pair + r
                seg = jnp.broadcast_to(cum[:, hl:hl + 1], (q, q)) - jnp.broadcast_to(cum_t[hl:hl + 1, :], (q, q))
                dec = jnp.exp(jnp.where(keep, seg, -jnp.inf))
                ys.append(_dot((gmat * dec).astype(BF16), xr_pair))
            y_diag = jnp.where(lane < HEAD_DIM, ys[0], ys[1])
            pieces.append(y_diag + y_off[:, pair * LANES:(pair + 1) * LANES])
    return jnp.concatenate(pieces, axis=-1)


def _ssd_kernel(xf_ref, dtf_ref, xb_ref, dtb_ref, a_ref, tril_ref, triu_ref, ef_ref, eb_ref,
                yf_ref, yb_ref, sf_ref, sb_ref):
    @pl.when(pl.program_id(0) == 0)
    def _():
        sf_ref[...] = jnp.zeros_like(sf_ref)
        sb_ref[...] = jnp.zeros_like(sb_ref)

    a_row = a_ref[...]
    yf_ref[...] = _ssd_direction(xf_ref[...], dtf_ref[...], a_row, tril_ref[...], ef_ref[...], sf_ref, 0, False)
    yb_ref[...] = _ssd_direction(xb_ref[...], dtb_ref[...], a_row, triu_ref[...], eb_ref[...], sb_ref,
                                 SSD_HEADS, True)


def _ssd_scan(xconv, dt, a_row, n_ctx_chunks):
    t = xconv.shape[0]
    n = t // SSD_CHUNK
    q = SSD_CHUNK
    r = jnp.arange(q)
    tril = (r[None, :] <= r[:, None]).astype(BF16)
    triu = (r[None, :] >= r[:, None]).astype(BF16)
    lanes = jnp.arange(LANES)[:, None]
    head_of_col = (jnp.arange(SSD_WIDTH) // HEAD_DIM)[None, :]
    expand_f = (lanes == head_of_col).astype(BF16)
    expand_b = (lanes == head_of_col + SSD_HEADS).astype(BF16)

    def fwd(i):
        return (i, 0)

    def bwd(i):
        return (jnp.where(i < n_ctx_chunks, n_ctx_chunks - 1 - i, n - 1 - (i - n_ctx_chunks)), 0)

    const = lambda i: (0, 0)
    return pl.pallas_call(
        _ssd_kernel,
        grid=(n,),
        in_specs=[
            pl.BlockSpec((q, SSD_CONV_CH), fwd),
            pl.BlockSpec((q, LANES), fwd),
            pl.BlockSpec((q, SSD_CONV_CH), bwd),
            pl.BlockSpec((q, LANES), bwd),
            pl.BlockSpec((1, LANES), const),
            pl.BlockSpec((q, q), const),
            pl.BlockSpec((q, q), const),
            pl.BlockSpec((LANES, SSD_WIDTH), const),
            pl.BlockSpec((LANES, SSD_WIDTH), const),
        ],
        out_specs=[pl.BlockSpec((q, SSD_WIDTH), fwd), pl.BlockSpec((q, SSD_WIDTH), bwd)],
        out_shape=[jax.ShapeDtypeStruct((t, SSD_WIDTH), F32)] * 2,
        scratch_shapes=[pltpu.VMEM((SSD_STATE, SSD_WIDTH), F32)] * 2,
        compiler_params=_params("arbitrary"),
        name="ssd_scan",
    )(xconv, dt, xconv, dt, a_row, tril, triu, expand_f, expand_b)


def _na_kernel(q_ref, k_ref, v_ref, bias_ref, o_ref, *, n_ctx, n_rows):
    i = pl.program_id(0)
    is_ctx = i == 0
    kc = k_ref[0:n_ctx, :]
    vc = v_ref[0:n_ctx, :]
    lane = lax.broadcasted_iota(jnp.int32, (GRID_W, NA_WIDTH), 1)
    mine = [(lane >= h * HEAD_DIM) & (lane < (h + 1) * HEAD_DIM) for h in range(NA_HEADS)]
    win = NA_WIN_ROWS * GRID_W

    def lane_tiles(a):
        return [a[:, c * LANES:(c + 1) * LANES] for c in range(a.shape[-1] // LANES)]

    for j in range(NA_ROWS_PER_STEP):
        r = jnp.maximum((i - 1) * NA_ROWS_PER_STEP + j, 0)
        r_start = jnp.clip(r - NA_WIN_ROWS // 2, 0, n_rows - NA_WIN_ROWS)
        variant = jnp.where(is_ctx, NA_WIN_ROWS, r_start - r + NA_WIN_ROWS - 1)
        start = pl.multiple_of(n_ctx + r_start * GRID_W, GRID_W)
        kw = k_ref[pl.ds(start, win), :]
        vw = v_ref[pl.ds(start, win), :]
        qj = q_ref[j * GRID_W:(j + 1) * GRID_W, :]
        qm = jnp.concatenate([jnp.where(mine[h], qj, jnp.zeros_like(qj)) for h in range(NA_HEADS)], axis=0)
        s_w = _dot_nt(qm, kw) + bias_ref[variant]
        s_c = _dot_nt(qm, kc)
        m = functools.reduce(jnp.maximum, lane_tiles(s_w) + lane_tiles(s_c))
        m = jnp.broadcast_to(jnp.max(m, axis=-1, keepdims=True), m.shape)
        p_w = jnp.exp(s_w - jnp.concatenate([m] * (s_w.shape[-1] // LANES), axis=-1))
        p_c = jnp.exp(s_c - jnp.concatenate([m] * (s_c.shape[-1] // LANES), axis=-1))
        l = jnp.sum(functools.reduce(jnp.add, lane_tiles(p_w) + lane_tiles(p_c)), axis=-1, keepdims=True)
        y = (_dot(p_w.astype(BF16), vw) + _dot(p_c.astype(BF16), vc)) * (1.0 / l)
        out = y[(NA_HEADS - 1) * GRID_W:]
        for h in range(NA_HEADS - 2, -1, -1):
            out = jnp.where(mine[h], y[h * GRID_W:(h + 1) * GRID_W], out)
        o_ref[j * GRID_W:(j + 1) * GRID_W, :] = out.astype(o_ref.dtype)


def _na_bias_table(rpb):
    col = jnp.arange(GRID_W)
    c_start = jnp.clip(col - NA_WIN_COLS // 2, 0, GRID_W - NA_WIN_COLS)
    in_win = (col[None, :] >= c_start[:, None]) & (col[None, :] < c_start[:, None] + NA_WIN_COLS)
    dc = jnp.clip(col[None, :] - col[:, None] + NA_WIN_COLS - 1, 0, 2 * NA_WIN_COLS - 2)
    n_dc = 2 * NA_WIN_COLS - 1
    n_dr = 2 * NA_WIN_ROWS - 1
    onehot = (dc[None, :, :] == jnp.arange(n_dc)[:, None, None]).astype(F32).reshape(n_dc, GRID_W * GRID_W)
    t2 = jnp.dot(rpb.reshape(NA_HEADS * n_dr, n_dc).astype(F32), onehot, precision=lax.Precision.HIGHEST)
    t2 = jnp.where(in_win[None, None], t2.reshape(NA_HEADS, n_dr, GRID_W, GRID_W), -jnp.inf)
    tab = jnp.stack([t2[:, v:v + NA_WIN_ROWS] for v in range(NA_WIN_ROWS)])
    tab = tab.transpose(0, 1, 3, 2, 4).reshape(NA_WIN_ROWS, NA_HEADS * GRID_W, NA_WIN_ROWS * GRID_W)
    masked = jnp.full((1,) + tab.shape[1:], -jnp.inf, F32)
    return jnp.concatenate([tab, masked], axis=0)


def _neighbourhood_attention(na, bias_tab, n_ctx):
    t = na.shape[0]
    n_rows = (t - n_ctx) // GRID_W
    step_rows = NA_ROWS_PER_STEP * GRID_W
    n = t // step_rows
    return pl.pallas_call(
        functools.partial(_na_kernel, n_ctx=n_ctx, n_rows=n_rows),
        grid=(n,),
        in_specs=[
            pl.BlockSpec((step_rows, NA_WIDTH), lambda i: (i, 0)),
            _resident((t, NA_WIDTH), lambda i: (0, 1)),
            _resident((t, NA_WIDTH), lambda i: (0, 2)),
            _resident(bias_tab.shape, lambda i: (0, 0, 0)),
        ],
        out_specs=pl.BlockSpec((step_rows, NA_WIDTH), lambda i: (i, 0)),
        out_shape=jax.ShapeDtypeStruct((t, NA_WIDTH), BF16),
        compiler_params=_params("parallel"),
        name="neighbourhood_attention",
    )(na, na, na, bias_tab)


def _gqa_kernel(q_ref, kt_ref, v_ref, o_ref, m_ref, acc_ref, *, n_ctx, n_lat_blocks):
    i = pl.program_id(1)
    tq = q_ref.shape[0]
    q = q_ref[...]
    lane = lax.broadcasted_iota(jnp.int32, q.shape, 1)
    low = lane < HEAD_DIM
    zero = jnp.zeros_like(q)
    q2 = jnp.concatenate([jnp.where(low, q, zero), jnp.where(low, zero, q)], axis=0)
    m_ref[...] = jnp.full_like(m_ref, -jnp.inf)
    acc_ref[...] = jnp.zeros_like(acc_ref)

    def block(start, size):
        kt = kt_ref[:, pl.ds(start, size)]
        v = v_ref[pl.ds(start, size), :]
        s = _dot(q2, kt)
        nb = size // LANES
        mx = s[:, 0:LANES]
        for c in range(1, nb):
            mx = jnp.maximum(mx, s[:, c * LANES:(c + 1) * LANES])
        m_old = m_ref[...]
        m_new = jnp.maximum(m_old, jnp.max(mx, axis=-1, keepdims=True))
        alpha = jnp.exp2(m_old - m_new)
        p = jnp.exp2(s - jnp.concatenate([m_new] * nb, axis=-1))
        acc_ref[...] = alpha * acc_ref[...] + _dot(p.astype(BF16), v)
        m_ref[...] = m_new

    block(0, n_ctx)

    def body(kb, carry):
        block(pl.multiple_of(n_ctx + kb * GQA_K_TILE, LANES), GQA_K_TILE)
        return carry

    lax.fori_loop(0, jnp.where(i == 0, 0, n_lat_blocks), body, 0)
    acc = acc_ref[...]
    o = acc / pltpu.roll(acc, HEAD_DIM, 1)
    o_ref[...] = jnp.where(low, o[:tq], pltpu.roll(o[tq:], HEAD_DIM, 1)).astype(o_ref.dtype)


def _gqa_attention(gq, gkt, gv, n_ctx):
    t = gq.shape[0]
    n = t // GQA_Q_TILE
    n_lat_blocks = (t - n_ctx) // GQA_K_TILE
    return pl.pallas_call(
        functools.partial(_gqa_kernel, n_ctx=n_ctx, n_lat_blocks=n_lat_blocks),
        grid=(GQA_KV_HEADS, n),
        in_specs=[
            pl.BlockSpec((GQA_Q_TILE, LANES), lambda g, i: (i, g)),
            pl.BlockSpec((LANES, t), lambda g, i: (g, 0)),
            pl.BlockSpec((t, LANES), lambda g, i: (0, g)),
        ],
        out_specs=pl.BlockSpec((GQA_Q_TILE, LANES), lambda g, i: (i, g)),
        out_shape=jax.ShapeDtypeStruct((t, GQA_WIDTH), BF16),
        scratch_shapes=[
            pltpu.VMEM((2 * GQA_Q_TILE, LANES), F32),
            pltpu.VMEM((2 * GQA_Q_TILE, LANES), F32),
        ],
        compiler_params=_params("arbitrary", "arbitrary"),
        name="gqa_attention",
    )(gq, gkt, gv)


def _out_ffn_kernel(ya_ref, yf_ref, yb_ref, xs_ref, z_ref, yg_ref, x_ref, mod_ref,
                    dskip_ref, snw_ref, wo_ref, fnw_ref, wg_ref, wu_ref, wd_ref, final_ref,
                    o_ref, *, final):
    y = yf_ref[...] + yb_ref[...] + dskip_ref[...] * xs_ref[...]
    y = y * _silu(z_ref[...])
    ms = jnp.mean(y * y, axis=-1, keepdims=True)
    y = y * lax.rsqrt(ms + EPS) * snw_ref[...]
    mix = jnp.concatenate([ya_ref[...], y.astype(BF16), yg_ref[...]], axis=-1)
    g_m = mod_ref[:, 2 * D_MODEL:3 * D_MODEL]
    sh_f = mod_ref[:, 3 * D_MODEL:4 * D_MODEL]
    sc_f = mod_ref[:, 4 * D_MODEL:5 * D_MODEL]
    g_f = mod_ref[:, 5 * D_MODEL:6 * D_MODEL]
    x1 = x_ref[...] + g_m * _dot(mix, wo_ref[...])
    ms1 = jnp.mean(x1 * x1, axis=-1, keepdims=True)
    hf = (x1 * lax.rsqrt(ms1 + EPS) * fnw_ref[...] * (1.0 + sc_f) + sh_f).astype(BF16)
    act = (_silu(_dot(hf, wg_ref[...])) * _dot(hf, wu_ref[...])).astype(BF16)
    x2 = x1 + g_f * _dot(act, wd_ref[...])
    if final:
        ms2 = jnp.mean(x2 * x2, axis=-1, keepdims=True)
        x2 = x2 * lax.rsqrt(ms2 + EPS) * final_ref[...]
    o_ref[...] = x2


def _out_ffn(ya, yf, yb, xconv, z, yg, x_all, mod4, layer, d_full, ssd_nw, w_out, ffn_nw,
             w_gate, w_up, w_down, final_nw, final):
    t = x_all.shape[0]
    skip = 1 if final else 0
    n = t // ROW_TILE - skip
    row = lambda i: (i + skip, 0)
    const = lambda i: (0, 0)
    return pl.pallas_call(
        functools.partial(_out_ffn_kernel, final=final),
        grid=(n,),
        in_specs=[
            pl.BlockSpec((ROW_TILE, NA_WIDTH), row),
            pl.BlockSpec((ROW_TILE, SSD_WIDTH), row),
            pl.BlockSpec((ROW_TILE, SSD_WIDTH), row),
            pl.BlockSpec((ROW_TILE, SSD_WIDTH), row),
            pl.BlockSpec((ROW_TILE, SSD_WIDTH), row),
            pl.BlockSpec((ROW_TILE, GQA_WIDTH), row),
            pl.BlockSpec((ROW_TILE, D_MODEL), row),
            pl.BlockSpec((None, None, 1, 6 * D_MODEL), lambda i: (layer, jnp.minimum(i + skip, 1), 0, 0)),
            pl.BlockSpec((1, SSD_WIDTH), const),
            pl.BlockSpec((1, SSD_WIDTH), const),
            _resident((D_MODEL, D_MODEL), const),
            pl.BlockSpec((1, D_MODEL), const),
            _resident((D_MODEL, FFN_HIDDEN), const),
            _resident((D_MODEL, FFN_HIDDEN), const),
            _resident((FFN_HIDDEN, D_MODEL), const),
            pl.BlockSpec((1, D_MODEL), const),
        ],
        out_specs=pl.BlockSpec((ROW_TILE, D_MODEL), lambda i: (i, 0)),
        out_shape=jax.ShapeDtypeStruct((n * ROW_TILE, D_MODEL), F32),
        compiler_params=_params("parallel"),
        name="out_ffn",
    )(ya, yf, yb, xconv, z, yg, x_all, mod4, d_full, ssd_nw, w_out, ffn_nw, w_gate, w_up, w_down, final_nw)


def _rearranged_w_in(w):
    na_in = 3 * NA_WIDTH
    o_z = na_in
    o_xbc = o_z + SSD_WIDTH
    o_dt = o_xbc + SSD_CONV_CH
    o_gq = o_dt + 2 * SSD_HEADS
    o_gk = o_gq + GQA_WIDTH
    o_gv = o_gk + GQA_KV_HEADS * HEAD_DIM
    na = jnp.concatenate([w[:, :NA_WIDTH] * ATTN_SCALE, w[:, NA_WIDTH:na_in]], axis=1)
    dt = jnp.pad(w[:, o_dt:o_gq], ((0, 0), (0, LANES - 2 * SSD_HEADS)))

    def spread(seg, second_copy):
        heads = [seg[:, h * HEAD_DIM:(h + 1) * HEAD_DIM] for h in range(GQA_KV_HEADS)]
        return jnp.concatenate([p for h in heads for p in (h, h if second_copy else jnp.zeros_like(h))], axis=1)

    return jnp.concatenate(
        [na, w[:, o_z:o_xbc], w[:, o_xbc:o_dt], dt, w[:, o_gq:o_gk], spread(w[:, o_gk:o_gv], True),
         spread(w[:, o_gv:], False)], axis=1).astype(BF16)


def _rope_tables(n_ctx, n_lat):
    freqs = ROPE_THETA ** (-jnp.arange(ROPE_PAIRS, dtype=F32) / ROPE_PAIRS)
    tok = jnp.arange(n_lat)
    rows = (tok // GRID_W).astype(F32)
    cols = (tok % GRID_W).astype(F32)

    def ang(pos):
        a = pos[:, None] * freqs[None, :]
        return jnp.concatenate([a, a], axis=-1)

    a = jnp.concatenate([ang(rows), ang(cols)], axis=-1)
    sign = jnp.where((jnp.arange(HEAD_DIM) % (2 * ROPE_PAIRS)) < ROPE_PAIRS, -1.0, 1.0).astype(F32)
    cos = jnp.concatenate([jnp.ones((n_ctx, HEAD_DIM), F32), jnp.cos(a)], axis=0)
    sin = jnp.concatenate([jnp.zeros((n_ctx, HEAD_DIM), F32), jnp.sin(a) * sign[None, :]], axis=0)
    return jnp.tile(cos, (1, LANES // HEAD_DIM)), jnp.tile(sin, (1, LANES // HEAD_DIM))


def kernel(x, c, ctx, c_ctx, mod_w, mod_b, norm_attn_w, norm_ffn_w, w_in, na_rpb, ssd_conv_w, ssd_conv_b,
           ssd_dt_bias, ssd_a_log, ssd_d, ssd_norm_w, q_norm_w, k_norm_w, w_out, ffn_w_gate, ffn_w_up,
           ffn_w_down, final_norm_w):
    depth = mod_w.shape[0]
    batch, n_lat, _ = x.shape
    n_ctx = ctx.shape[1]
    assert batch == 1 and n_ctx == ROW_TILE and n_lat % (NA_ROWS_PER_STEP * GRID_W) == 0
    assert n_lat % GQA_K_TILE == 0 and n_lat // GRID_W >= NA_WIN_ROWS

    x_all = jnp.concatenate([ctx[0], x[0]], axis=0)
    cc = jnp.zeros((SUBLANES, D_MODEL), F32).at[0].set(c_ctx).at[1].set(c[0])
    mod = _modulation(cc, mod_w, mod_b)
    mod4 = mod[:, :2].reshape(depth, 2, 1, 6 * D_MODEL)

    cos_t, sin_t = _rope_tables(n_ctx, n_lat)
    blk = jnp.arange(2 * GQA_WIDTH) // HEAD_DIM
    ones_bd = (blk[:, None] == blk[None, :]).astype(BF16)
    head_cols = jnp.arange(SSD_WIDTH) // HEAD_DIM

    for i in range(depth):
        final = i == depth - 1
        w_cat = _rearranged_w_in(w_in[i])
        dt_bias_pad = jnp.pad(ssd_dt_bias[i].reshape(1, -1), ((0, 0), (0, LANES - 2 * SSD_HEADS)))
        qk_w = jnp.concatenate([jnp.tile(q_norm_w[i] * (ATTN_SCALE * LOG2E), GQA_Q_HEADS),
                                jnp.tile(k_norm_w[i], 2 * GQA_KV_HEADS)]).reshape(1, -1)
        na, z, xbc, dt, gq, gk, gv = _inproj(x_all, mod4, i, norm_attn_w[i].reshape(1, -1), w_cat,
                                             dt_bias_pad, qk_w, cos_t, sin_t, ones_bd)
        conv_w_pad = jnp.pad(ssd_conv_w[i], ((0, SUBLANES - SSD_CONV), (0, 0)))
        xconv = _ssd_conv(xbc, conv_w_pad, ssd_conv_b[i].reshape(1, -1))
        a_row = jnp.pad(-jnp.exp(ssd_a_log[i].astype(F32)).reshape(1, -1), ((0, 0), (0, LANES - 2 * SSD_HEADS)))
        yf, yb = _ssd_scan(xconv, dt, a_row, n_ctx // SSD_CHUNK)
        ya = _neighbourhood_attention(na, _na_bias_table(na_rpb[i]), n_ctx)
        yg = _gqa_attention(gq, gk.T, gv, n_ctx)
        d_full = ssd_d[i].astype(F32)[head_cols].reshape(1, -1)
        x_all = _out_ffn(ya, yf, yb, xconv, z, yg, x_all, mod4, i, d_full, ssd_norm_w[i].reshape(1, -1),
                         w_out[i].astype(BF16), norm_ffn_w[i].reshape(1, -1), ffn_w_gate[i].astype(BF16),
                         ffn_w_up[i].astype(BF16), ffn_w_down[i].astype(BF16),
                         final_norm_w.reshape(1, -1), final)
    return x_all[None]
```

```python
import functools
import math

import jax
import jax.numpy as jnp
from jax import lax
from jax.experimental import pallas as pl
from jax.experimental.pallas import tpu as pltpu

F32 = jnp.float32
BF16 = jnp.bfloat16

D_MODEL = 1024
GRID_W = 64
HEAD_DIM = 64
NA_WIDTH = 256
NA_HEADS = 4
NA_WIN_ROWS = 8
NA_WIN_COLS = 16
SSD_WIDTH = 512
SSD_HEADS = 8
SSD_GROUPS = 2
SSD_STATE = 128
SSD_CONV = 5
SSD_CHUNK = 128
SSD_CONV_CH = SSD_WIDTH + 2 * SSD_GROUPS * SSD_STATE
GQA_WIDTH = 256
GQA_Q_HEADS = 4
GQA_KV_HEADS = 2
ROPE_THETA = 10000.0
ROPE_PAIRS = HEAD_DIM // 4
FFN_HIDDEN = 2816
EPS = 1e-6
ATTN_SCALE = HEAD_DIM ** -0.5
LOG2E = math.log2(math.e)

LANES = 128
SUBLANES = 8
VMEM_LIMIT_BYTES = 56 * 1024 * 1024

ROW_TILE = 256
NA_ROWS_PER_STEP = 4
GQA_Q_TILE = 256
GQA_K_SUB = 256
GQA_STEPS_PER_TRIP = 8

C_NA = 0
C_Z = C_NA + 3 * NA_WIDTH
C_XBC = C_Z + SSD_WIDTH
C_DT = C_XBC + SSD_CONV_CH
C_GQ = C_DT + LANES
C_GK = C_GQ + GQA_WIDTH
C_GV = C_GK + 2 * LANES
C_END = C_GV + 2 * LANES


def _silu(v):
    return v * (1.0 / (1.0 + jnp.exp(-v)))


def _softplus(v):
    return jnp.maximum(v, 0.0) + jnp.log(1.0 + jnp.exp(-jnp.abs(v)))


def _split3(v):
    hi = v.astype(BF16)
    r1 = v - hi.astype(F32)
    mid = r1.astype(BF16)
    lo = (r1 - mid.astype(F32)).astype(BF16)
    return hi, mid, lo


def _dot(a, b):
    return jnp.dot(a, b, preferred_element_type=F32)


def _dot_nt(a, b):
    return lax.dot_general(a, b, (((1,), (1,)), ((), ())), preferred_element_type=F32)


def _dot_tn(a, b):
    return lax.dot_general(a, b, (((0,), (0,)), ((), ())), preferred_element_type=F32)


def _exact_dot(v, sel):
    hi, mid, lo = _split3(v)
    return _dot(hi, sel) + _dot(mid, sel) + _dot(lo, sel)


def _exact_dot_lhs(sel, v):
    hi, mid, lo = _split3(v)
    return _dot(sel, hi) + _dot(sel, mid) + _dot(sel, lo)


def _params(*sem):
    return pltpu.CompilerParams(dimension_semantics=sem, vmem_limit_bytes=VMEM_LIMIT_BYTES)


def _resident(shape, index_map):
    return pl.BlockSpec(shape, index_map, pipeline_mode=pl.Buffered(1))


def _mod_kernel(cc_ref, w_ref, b_ref, o_ref):
    a = _silu(cc_ref[...])
    o_ref[0] = jnp.dot(a, w_ref[0], preferred_element_type=F32) + b_ref[0]


def _modulation(cc, mod_w, mod_b):
    depth = mod_w.shape[0]
    ncol = mod_w.shape[2] // D_MODEL
    return pl.pallas_call(
        _mod_kernel,
        grid=(depth, ncol),
        in_specs=[
            pl.BlockSpec((SUBLANES, D_MODEL), lambda l, j: (0, 0)),
            pl.BlockSpec((1, D_MODEL, D_MODEL), lambda l, j: (l, 0, j)),
            pl.BlockSpec((1, 1, D_MODEL), lambda l, j: (l, 0, j)),
        ],
        out_specs=pl.BlockSpec((1, SUBLANES, D_MODEL), lambda l, j: (l, 0, j)),
        out_shape=jax.ShapeDtypeStruct((depth, SUBLANES, ncol * D_MODEL), F32),
        compiler_params=_params("arbitrary", "arbitrary"),
        name="modulation",
    )(cc, mod_w, mod_b.reshape(depth, 1, -1))


def _inproj_kernel(x_ref, mod_ref, nw_ref, w_ref, dtb_ref, qkw_ref, cos_ref, sin_ref, ones_ref,
                   na_ref, z_ref, xbc_ref, dt_ref, gq_ref, gk_ref, gv_ref):
    x = x_ref[...]
    ms = jnp.mean(x * x, axis=-1, keepdims=True)
    xn = x * lax.rsqrt(ms + EPS) * nw_ref[...]
    sh = mod_ref[:, 0:D_MODEL]
    sc = mod_ref[:, D_MODEL:2 * D_MODEL]
    h = (xn * (1.0 + sc) + sh).astype(BF16)
    u = _dot(h, w_ref[...])
    na_ref[...] = u[:, C_NA:C_Z].astype(BF16)
    z_ref[...] = u[:, C_Z:C_XBC]
    xbc_ref[...] = u[:, C_XBC:C_DT]
    dt_ref[...] = _softplus(u[:, C_DT:C_GQ] + dtb_ref[...])
    vlane = lax.broadcasted_iota(jnp.int32, (x.shape[0], C_END - C_GV), 1)
    gv_ref[...] = jnp.where(vlane % LANES < HEAD_DIM, u[:, C_GV:C_END], 1.0).astype(BF16)
    g = u[:, C_GQ:C_GV]
    gsq = g * g
    hi = gsq.astype(BF16)
    lo = (gsq - hi.astype(F32)).astype(BF16)
    ss = _dot(hi, ones_ref[...]) + _dot(lo, ones_ref[...])
    gn = g * lax.rsqrt(ss * (1.0 / HEAD_DIM) + EPS) * qkw_ref[...]
    width = gn.shape[-1]
    lane = lax.broadcasted_iota(jnp.int32, gn.shape, 1)
    first = (lane % (2 * ROPE_PAIRS)) < ROPE_PAIRS
    partner = jnp.where(first, pltpu.roll(gn, width - ROPE_PAIRS, 1), pltpu.roll(gn, ROPE_PAIRS, 1))
    cos = jnp.concatenate([cos_ref[...]] * (width // LANES), axis=-1)
    sin = jnp.concatenate([sin_ref[...]] * (width // LANES), axis=-1)
    gr = gn * cos + partner * sin
    gq_ref[...] = gr[:, :GQA_WIDTH].astype(BF16)
    gk_ref[...] = gr[:, GQA_WIDTH:].astype(BF16)


def _inproj(x_all, mod4, layer, norm_w, w_cat, dt_bias_pad, qk_w, cos_t, sin_t, ones_bd):
    t = x_all.shape[0]
    n = t // ROW_TILE
    row = lambda i: (i, 0)
    const = lambda i: (0, 0)
    outs = [
        (3 * NA_WIDTH, BF16), (SSD_WIDTH, F32), (SSD_CONV_CH, F32), (LANES, F32),
        (GQA_WIDTH, BF16), (2 * LANES, BF16), (2 * LANES, BF16),
    ]
    return pl.pallas_call(
        _inproj_kernel,
        grid=(n,),
        in_specs=[
            pl.BlockSpec((ROW_TILE, D_MODEL), row),
            pl.BlockSpec((None, None, 1, 6 * D_MODEL), lambda i: (layer, jnp.minimum(i, 1), 0, 0)),
            pl.BlockSpec((1, D_MODEL), const),
            _resident((D_MODEL, C_END), const),
            pl.BlockSpec((1, LANES), const),
            pl.BlockSpec((1, 2 * GQA_WIDTH), const),
            pl.BlockSpec((ROW_TILE, LANES), row),
            pl.BlockSpec((ROW_TILE, LANES), row),
            _resident((2 * GQA_WIDTH, 2 * GQA_WIDTH), const),
        ],
        out_specs=[pl.BlockSpec((ROW_TILE, w), row) for w, _ in outs],
        out_shape=[jax.ShapeDtypeStruct((t, w), d) for w, d in outs],
        compiler_params=_params("parallel"),
        name="inproj",
    )(x_all, mod4, norm_w, w_cat, dt_bias_pad, qk_w, cos_t, sin_t, ones_bd)


def _conv_kernel(x_ref, prev_ref, next_ref, w_ref, b_ref, o_ref, ext_ref):
    i = pl.program_id(0)
    n = pl.num_programs(0)
    rows = x_ref.shape[0]
    has_prev = i >= 2
    has_next = jnp.logical_and(i >= 1, i < n - 1)
    ext_ref[0:SUBLANES, :] = jnp.where(has_prev, prev_ref[...], 0.0)
    ext_ref[SUBLANES:SUBLANES + rows, :] = x_ref[...]
    ext_ref[SUBLANES + rows:2 * SUBLANES + rows, :] = jnp.where(has_next, next_ref[...], 0.0)
    acc = jnp.broadcast_to(b_ref[...], x_ref.shape)
    half = SSD_CONV // 2
    for j in range(SSD_CONV):
        acc = acc + w_ref[j:j + 1, :] * ext_ref[pl.ds(SUBLANES - half + j, rows), :]
    o_ref[...] = _silu(acc)


def _ssd_conv(xbc, conv_w_pad, conv_b):
    t, ch = xbc.shape
    n = t // ROW_TILE
    per = ROW_TILE // SUBLANES
    last = t // SUBLANES - 1
    return pl.pallas_call(
        _conv_kernel,
        grid=(n,),
        in_specs=[
            pl.BlockSpec((ROW_TILE, ch), lambda i: (i, 0)),
            pl.BlockSpec((SUBLANES, ch), lambda i: (jnp.maximum(i * per - 1, 0), 0)),
            pl.BlockSpec((SUBLANES, ch), lambda i: (jnp.minimum((i + 1) * per, last), 0)),
            pl.BlockSpec((SUBLANES, ch), lambda i: (0, 0)),
            pl.BlockSpec((1, ch), lambda i: (0, 0)),
        ],
        out_specs=pl.BlockSpec((ROW_TILE, ch), lambda i: (i, 0)),
        out_shape=jax.ShapeDtypeStruct((t, ch), F32),
        scratch_shapes=[pltpu.VMEM((ROW_TILE + 2 * SUBLANES, ch), F32)],
        compiler_params=_params("parallel"),
        name="ssd_conv",
    )(xbc, xbc, xbc, conv_w_pad, conv_b)


def _ssd_direction(xbc, dt, a_row, tri, expand, s_ref, lane0, backward):
    q = SSD_CHUNK
    x = xbc[:, :SSD_WIDTH]
    nb = SSD_GROUPS * SSD_STATE
    bmat = xbc[:, SSD_WIDTH:SSD_WIDTH + nb].astype(BF16)
    cmat = xbc[:, SSD_WIDTH + nb:].astype(BF16)
    cum = _exact_dot_lhs(tri, dt * a_row)
    yield
    last = 0 if backward else q - 1
    dt_hi = dt.astype(BF16)
    dt_lo = (dt - dt_hi.astype(F32)).astype(BF16)
    dt_full = _dot(dt_hi, expand) + _dot(dt_lo, expand)
    cum_full = _exact_dot(cum, expand)
    yield
    ea_full = jnp.exp(cum_full)
    te_full = jnp.exp(cum_full[last:last + 1, :] - cum_full)
    xr = x * dt_full
    xr_b = xr.astype(BF16)
    xt_b = (xr * te_full).astype(BF16)
    cum_t = cum.T
    ti = lax.broadcasted_iota(jnp.int32, (q, q), 0)
    si = lax.broadcasted_iota(jnp.int32, (q, q), 1)
    keep = (si >= ti) if backward else (si <= ti)
    lane = lax.broadcasted_iota(jnp.int32, (q, LANES), 1)
    heads_per_group = SSD_HEADS // SSD_GROUPS
    width_g = heads_per_group * HEAD_DIM
    pieces = []
    decay_total = ea_full[last:last + 1, :]
    for g in range(SSD_GROUPS):
        bg = bmat[:, g * SSD_STATE:(g + 1) * SSD_STATE]
        cg = cmat[:, g * SSD_STATE:(g + 1) * SSD_STATE]
        gmat = _dot_nt(cg, bg)
        s_old = s_ref[:, g * width_g:(g + 1) * width_g]
        y_off = _dot(cg, s_old.astype(BF16)) * ea_full[:, g * width_g:(g + 1) * width_g]
        s_new = _dot_tn(bg, xt_b[:, g * width_g:(g + 1) * width_g])
        s_ref[:, g * width_g:(g + 1) * width_g] = decay_total[:, g * width_g:(g + 1) * width_g] * s_old + s_new
        yield
        for pair in range(heads_per_group // 2):
            col = g * width_g + pair * LANES
            xr_pair = xr_b[:, col:col + LANES]
            ys = []
            for r in range(2):
                hl = lane0 + g * heads_per_group + 2 * pair + r
                seg = jnp.broadcast_to(cum[:, hl:hl + 1], (q, q)) - jnp.broadcast_to(cum_t[hl:hl + 1, :], (q, q))
                dec = jnp.exp(jnp.where(keep, seg, -jnp.inf))
                ys.append(_dot((gmat * dec).astype(BF16), xr_pair))
            y_diag = jnp.where(lane < HEAD_DIM, ys[0], ys[1])
            pieces.append(y_diag + y_off[:, pair * LANES:(pair + 1) * LANES])
            yield
    return jnp.concatenate(pieces, axis=-1)


def _run_interleaved(*stage_generators):
    results = [None] * len(stage_generators)
    live = list(range(len(stage_generators)))
    while live:
        for idx in list(live):
            try:
                next(stage_generators[idx])
            except StopIteration as done:
                results[idx] = done.value
                live.remove(idx)
    return results


def _ssd_kernel(xf_ref, dtf_ref, xb_ref, dtb_ref, a_ref, tril_ref, triu_ref, ef_ref, eb_ref,
                yf_ref, yb_ref, sf_ref, sb_ref):
    @pl.when(pl.program_id(0) == 0)
    def _():
        sf_ref[...] = jnp.zeros_like(sf_ref)
        sb_ref[...] = jnp.zeros_like(sb_ref)

    a_row = a_ref[...]
    yf, yb = _run_interleaved(
        _ssd_direction(xf_ref[...], dtf_ref[...], a_row, tril_ref[...], ef_ref[...], sf_ref, 0, False),
        _ssd_direction(xb_ref[...], dtb_ref[...], a_row, triu_ref[...], eb_ref[...], sb_ref, SSD_HEADS, True))
    yf_ref[...] = yf
    yb_ref[...] = yb


def _ssd_scan(xconv, dt, a_row, n_ctx_chunks):
    t = xconv.shape[0]
    n = t // SSD_CHUNK
    q = SSD_CHUNK
    r = jnp.arange(q)
    tril = (r[None, :] <= r[:, None]).astype(BF16)
    triu = (r[None, :] >= r[:, None]).astype(BF16)
    lanes = jnp.arange(LANES)[:, None]
    head_of_col = (jnp.arange(SSD_WIDTH) // HEAD_DIM)[None, :]
    expand_f = (lanes == head_of_col).astype(BF16)
    expand_b = (lanes == head_of_col + SSD_HEADS).astype(BF16)

    def fwd(i):
        return (i, 0)

    def bwd(i):
        return (jnp.where(i < n_ctx_chunks, n_ctx_chunks - 1 - i, n - 1 - (i - n_ctx_chunks)), 0)

    const = lambda i: (0, 0)
    return pl.pallas_call(
        _ssd_kernel,
        grid=(n,),
        in_specs=[
            pl.BlockSpec((q, SSD_CONV_CH), fwd),
            pl.BlockSpec((q, LANES), fwd),
            pl.BlockSpec((q, SSD_CONV_CH), bwd),
            pl.BlockSpec((q, LANES), bwd),
            pl.BlockSpec((1, LANES), const),
            pl.BlockSpec((q, q), const),
            pl.BlockSpec((q, q), const),
            pl.BlockSpec((LANES, SSD_WIDTH), const),
            pl.BlockSpec((LANES, SSD_WIDTH), const),
        ],
        out_specs=[pl.BlockSpec((q, SSD_WIDTH), fwd), pl.BlockSpec((q, SSD_WIDTH), bwd)],
        out_shape=[jax.ShapeDtypeStruct((t, SSD_WIDTH), F32)] * 2,
        scratch_shapes=[pltpu.VMEM((SSD_STATE, SSD_WIDTH), F32)] * 2,
        compiler_params=_params("arbitrary"),
        name="ssd_scan",
    )(xconv, dt, xconv, dt, a_row, tril, triu, expand_f, expand_b)


def _na_kernel(q_ref, k_ref, v_ref, bias_ref, o_ref, *, n_ctx, n_rows):
    i = pl.program_id(0)
    is_ctx = i == 0
    kc = k_ref[0:n_ctx, :]
    vc = v_ref[0:n_ctx, :]
    lane = lax.broadcasted_iota(jnp.int32, (GRID_W, NA_WIDTH), 1)
    mine = [(lane >= h * HEAD_DIM) & (lane < (h + 1) * HEAD_DIM) for h in range(NA_HEADS)]
    win = NA_WIN_ROWS * GRID_W

    def lane_tiles(a):
        return [a[:, c * LANES:(c + 1) * LANES] for c in range(a.shape[-1] // LANES)]

    for j in range(NA_ROWS_PER_STEP):
        r = jnp.maximum((i - 1) * NA_ROWS_PER_STEP + j, 0)
        r_start = jnp.clip(r - NA_WIN_ROWS // 2, 0, n_rows - NA_WIN_ROWS)
        variant = jnp.where(is_ctx, NA_WIN_ROWS, r_start - r + NA_WIN_ROWS - 1)
        start = pl.multiple_of(n_ctx + r_start * GRID_W, GRID_W)
        kw = k_ref[pl.ds(start, win), :]
        vw = v_ref[pl.ds(start, win), :]
        qj = q_ref[j * GRID_W:(j + 1) * GRID_W, :]
        qm = jnp.concatenate([jnp.where(mine[h], qj, jnp.zeros_like(qj)) for h in range(NA_HEADS)], axis=0)
        s_w = _dot_nt(qm, kw) + bias_ref[variant]
        s_c = _dot_nt(qm, kc)
        m = functools.reduce(jnp.maximum, lane_tiles(s_w) + lane_tiles(s_c))
        m = jnp.broadcast_to(jnp.max(m, axis=-1, keepdims=True), m.shape)
        p_w = jnp.exp(s_w - jnp.concatenate([m] * (s_w.shape[-1] // LANES), axis=-1))
        p_c = jnp.exp(s_c - jnp.concatenate([m] * (s_c.shape[-1] // LANES), axis=-1))
        l = jnp.sum(functools.reduce(jnp.add, lane_tiles(p_w) + lane_tiles(p_c)), axis=-1, keepdims=True)
        y = (_dot(p_w.astype(BF16), vw) + _dot(p_c.astype(BF16), vc)) * (1.0 / l)
        out = y[(NA_HEADS - 1) * GRID_W:]
        for h in range(NA_HEADS - 2, -1, -1):
            out = jnp.where(mine[h], y[h * GRID_W:(h + 1) * GRID_W], out)
        o_ref[j * GRID_W:(j + 1) * GRID_W, :] = out.astype(o_ref.dtype)


def _na_bias_table(rpb):
    col = jnp.arange(GRID_W)
    c_start = jnp.clip(col - NA_WIN_COLS // 2, 0, GRID_W - NA_WIN_COLS)
    in_win = (col[None, :] >= c_start[:, None]) & (col[None, :] < c_start[:, None] + NA_WIN_COLS)
    dc = jnp.clip(col[None, :] - col[:, None] + NA_WIN_COLS - 1, 0, 2 * NA_WIN_COLS - 2)
    n_dc = 2 * NA_WIN_COLS - 1
    n_dr = 2 * NA_WIN_ROWS - 1
    onehot = (dc[None, :, :] == jnp.arange(n_dc)[:, None, None]).astype(F32).reshape(n_dc, GRID_W * GRID_W)
    t2 = jnp.dot(rpb.reshape(NA_HEADS * n_dr, n_dc).astype(F32), onehot, precision=lax.Precision.HIGHEST)
    t2 = jnp.where(in_win[None, None], t2.reshape(NA_HEADS, n_dr, GRID_W, GRID_W), -jnp.inf)
    tab = jnp.stack([t2[:, v:v + NA_WIN_ROWS] for v in range(NA_WIN_ROWS)])
    tab = tab.transpose(0, 1, 3, 2, 4).reshape(NA_WIN_ROWS, NA_HEADS * GRID_W, NA_WIN_ROWS * GRID_W)
    masked = jnp.full((1,) + tab.shape[1:], -jnp.inf, F32)
    return jnp.concatenate([tab, masked], axis=0)


def _neighbourhood_attention(na, bias_tab, n_ctx):
    t = na.shape[0]
    n_rows = (t - n_ctx) // GRID_W
    step_rows = NA_ROWS_PER_STEP * GRID_W
    n = t // step_rows
    return pl.pallas_call(
        functools.partial(_na_kernel, n_ctx=n_ctx, n_rows=n_rows),
        grid=(n,),
        in_specs=[
            pl.BlockSpec((step_rows, NA_WIDTH), lambda i: (i, 0)),
            _resident((t, NA_WIDTH), lambda i: (0, 1)),
            _resident((t, NA_WIDTH), lambda i: (0, 2)),
            _resident(bias_tab.shape, lambda i: (0, 0, 0)),
        ],
        out_specs=pl.BlockSpec((step_rows, NA_WIDTH), lambda i: (i, 0)),
        out_shape=jax.ShapeDtypeStruct((t, NA_WIDTH), BF16),
        compiler_params=_params("parallel"),
        name="neighbourhood_attention",
    )(na, na, na, bias_tab)


def _gqa_kernel(qt_ref, k_ref, vt_ref, o_ref, acc_ref, s_ref, p_ref, *, n_sub):
    tq = qt_ref.shape[1]
    qt = qt_ref[...]
    row = lax.broadcasted_iota(jnp.int32, qt.shape, 0)
    top = row < HEAD_DIM
    zero = jnp.zeros_like(qt)
    qt2 = jnp.concatenate([jnp.where(top, qt, zero), jnp.where(top, zero, qt)], axis=1)
    acc_ref[...] = jnp.zeros_like(acc_ref)
    sub = GQA_K_SUB

    def keys(j):
        return pl.ds(pl.multiple_of(j * sub, sub), sub)

    def score(j, slot):
        s = _dot(k_ref[keys(j), :], qt2)
        s_ref[slot] = s
        return jnp.max(s.reshape(sub // SUBLANES, SUBLANES, 2 * tq), axis=0)

    def softmax(slot, m_old, part_max):
        m_new = jnp.maximum(m_old, jnp.max(part_max, axis=0, keepdims=True))
        s = s_ref[slot].reshape(sub // SUBLANES, SUBLANES, 2 * tq)
        p_ref[slot] = jnp.exp2(s - m_new[None]).reshape(sub, 2 * tq).astype(BF16)
        return m_new, jnp.exp2(m_old - m_new)

    def accumulate(j, slot, alpha):
        acc = acc_ref[...].reshape(LANES // SUBLANES, SUBLANES, 2 * tq) * alpha[None]
        acc_ref[...] = acc.reshape(LANES, 2 * tq) + _dot(vt_ref[:, keys(j)], p_ref[slot])

    m_init = jnp.full((SUBLANES, 2 * tq), -jnp.inf, F32)

    if n_sub < 3:
        m = m_init
        for j in range(n_sub):
            m, alpha = softmax(0, m, score(j, 0))
            accumulate(j, 0, alpha)
    else:
        part0 = score(0, 0)
        part1 = score(1, 1)
        m, alpha = softmax(0, m_init, part0)

        def step(t, slot, carry):
            m, alpha, part = carry
            part_next = score(t, slot)
            accumulate(t - 2, slot, alpha)
            return softmax(1 - slot, m, part) + (part_next,)

        def trip(n, carry):
            for u in range(GQA_STEPS_PER_TRIP):
                carry = step(2 + n * GQA_STEPS_PER_TRIP + u, u % 2, carry)
            return carry

        n_trips = (n_sub - 2) // GQA_STEPS_PER_TRIP
        carry = lax.fori_loop(0, n_trips, trip, (m, alpha, part1))
        for t in range(2 + n_trips * GQA_STEPS_PER_TRIP, n_sub):
            carry = step(t, t % 2, carry)
        m, alpha, part = carry
        accumulate(n_sub - 2, (n_sub - 2) % 2, alpha)
        m, alpha = softmax((n_sub - 1) % 2, m, part)
        accumulate(n_sub - 1, (n_sub - 1) % 2, alpha)

    acc = acc_ref[...]
    o_t = acc[:HEAD_DIM] / acc[HEAD_DIM:]
    o_ref[...] = jnp.concatenate([o_t[:, :tq], o_t[:, tq:]], axis=0).T.astype(o_ref.dtype)


def _gqa_call(gqt, gk, gvt, n_keys, tq, name):
    nq = gqt.shape[1]
    assert n_keys % GQA_K_SUB == 0 and nq % tq == 0
    return pl.pallas_call(
        functools.partial(_gqa_kernel, n_sub=n_keys // GQA_K_SUB),
        grid=(GQA_KV_HEADS, nq // tq),
        in_specs=[
            pl.BlockSpec((LANES, tq), lambda g, i: (g, i)),
            pl.BlockSpec((n_keys, LANES), lambda g, i: (0, g)),
            pl.BlockSpec((LANES, n_keys), lambda g, i: (g, 0)),
        ],
        out_specs=pl.BlockSpec((tq, LANES), lambda g, i: (i, g)),
        out_shape=jax.ShapeDtypeStruct((nq, GQA_WIDTH), BF16),
        scratch_shapes=[
            pltpu.VMEM((LANES, 2 * tq), F32),
            pltpu.VMEM((2, GQA_K_SUB, 2 * tq), F32),
            pltpu.VMEM((2, GQA_K_SUB, 2 * tq), BF16),
        ],
        compiler_params=_params("arbitrary", "arbitrary"),
        name=name,
    )(gqt, gk, gvt)


def _gqa_attention(gqt, gk, gvt, n_ctx):
    y_ctx = _gqa_call(gqt[:, :n_ctx], gk, gvt, n_ctx, n_ctx, "gqa_attention_ctx")
    y_lat = _gqa_call(gqt[:, n_ctx:], gk, gvt, gk.shape[0], GQA_Q_TILE, "gqa_attention")
    return jnp.concatenate([y_ctx, y_lat], axis=0)


def _out_ffn_kernel(ya_ref, yf_ref, yb_ref, xs_ref, z_ref, yg_ref, x_ref, mod_ref,
                    dskip_ref, snw_ref, wo_ref, fnw_ref, wg_ref, wu_ref, wd_ref, final_ref,
                    o_ref, *, final):
    y = yf_ref[...] + yb_ref[...] + dskip_ref[...] * xs_ref[...]
    y = y * _silu(z_ref[...])
    ms = jnp.mean(y * y, axis=-1, keepdims=True)
    y = y * lax.rsqrt(ms + EPS) * snw_ref[...]
    mix = jnp.concatenate([ya_ref[...], y.astype(BF16), yg_ref[...]], axis=-1)
    g_m = mod_ref[:, 2 * D_MODEL:3 * D_MODEL]
    sh_f = mod_ref[:, 3 * D_MODEL:4 * D_MODEL]
    sc_f = mod_ref[:, 4 * D_MODEL:5 * D_MODEL]
    g_f = mod_ref[:, 5 * D_MODEL:6 * D_MODEL]
    x1 = x_ref[...] + g_m * _dot(mix, wo_ref[...])
    ms1 = jnp.mean(x1 * x1, axis=-1, keepdims=True)
    hf = (x1 * lax.rsqrt(ms1 + EPS) * fnw_ref[...] * (1.0 + sc_f) + sh_f).astype(BF16)
    act = (_silu(_dot(hf, wg_ref[...])) * _dot(hf, wu_ref[...])).astype(BF16)
    x2 = x1 + g_f * _dot(act, wd_ref[...])
    if final:
        ms2 = jnp.mean(x2 * x2, axis=-1, keepdims=True)
        x2 = x2 * lax.rsqrt(ms2 + EPS) * final_ref[...]
    o_ref[...] = x2


def _out_ffn(ya, yf, yb, xconv, z, yg, x_all, mod4, layer, d_full, ssd_nw, w_out, ffn_nw,
             w_gate, w_up, w_down, final_nw, final):
    t = x_all.shape[0]
    skip = 1 if final else 0
    n = t // ROW_TILE - skip
    row = lambda i: (i + skip, 0)
    const = lambda i: (0, 0)
    return pl.pallas_call(
        functools.partial(_out_ffn_kernel, final=final),
        grid=(n,),
        in_specs=[
            pl.BlockSpec((ROW_TILE, NA_WIDTH), row),
            pl.BlockSpec((ROW_TILE, SSD_WIDTH), row),
            pl.BlockSpec((ROW_TILE, SSD_WIDTH), row),
            pl.BlockSpec((ROW_TILE, SSD_WIDTH), row),
            pl.BlockSpec((ROW_TILE, SSD_WIDTH), row),
            pl.BlockSpec((ROW_TILE, GQA_WIDTH), row),
            pl.BlockSpec((ROW_TILE, D_MODEL), row),
            pl.BlockSpec((None, None, 1, 6 * D_MODEL), lambda i: (layer, jnp.minimum(i + skip, 1), 0, 0)),
            pl.BlockSpec((1, SSD_WIDTH), const),
            pl.BlockSpec((1, SSD_WIDTH), const),
            _resident((D_MODEL, D_MODEL), const),
            pl.BlockSpec((1, D_MODEL), const),
            _resident((D_MODEL, FFN_HIDDEN), const),
            _resident((D_MODEL, FFN_HIDDEN), const),
            _resident((FFN_HIDDEN, D_MODEL), const),
            pl.BlockSpec((1, D_MODEL), const),
        ],
        out_specs=pl.BlockSpec((ROW_TILE, D_MODEL), lambda i: (i, 0)),
        out_shape=jax.ShapeDtypeStruct((n * ROW_TILE, D_MODEL), F32),
        compiler_params=_params("parallel"),
        name="out_ffn",
    )(ya, yf, yb, xconv, z, yg, x_all, mod4, d_full, ssd_nw, w_out, ffn_nw, w_gate, w_up, w_down, final_nw)


def _rearranged_w_in(w):
    na_in = 3 * NA_WIDTH
    o_z = na_in
    o_xbc = o_z + SSD_WIDTH
    o_dt = o_xbc + SSD_CONV_CH
    o_gq = o_dt + 2 * SSD_HEADS
    o_gk = o_gq + GQA_WIDTH
    o_gv = o_gk + GQA_KV_HEADS * HEAD_DIM
    na = jnp.concatenate([w[:, :NA_WIDTH] * ATTN_SCALE, w[:, NA_WIDTH:na_in]], axis=1)
    dt = jnp.pad(w[:, o_dt:o_gq], ((0, 0), (0, LANES - 2 * SSD_HEADS)))

    def spread(seg, second_copy):
        heads = [seg[:, h * HEAD_DIM:(h + 1) * HEAD_DIM] for h in range(GQA_KV_HEADS)]
        return jnp.concatenate([p for h in heads for p in (h, h if second_copy else jnp.zeros_like(h))], axis=1)

    return jnp.concatenate(
        [na, w[:, o_z:o_xbc], w[:, o_xbc:o_dt], dt, w[:, o_gq:o_gk], spread(w[:, o_gk:o_gv], True),
         spread(w[:, o_gv:], False)], axis=1).astype(BF16)


def _rope_tables(n_ctx, n_lat):
    freqs = ROPE_THETA ** (-jnp.arange(ROPE_PAIRS, dtype=F32) / ROPE_PAIRS)
    tok = jnp.arange(n_lat)
    rows = (tok // GRID_W).astype(F32)
    cols = (tok % GRID_W).astype(F32)

    def ang(pos):
        a = pos[:, None] * freqs[None, :]
        return jnp.concatenate([a, a], axis=-1)

    a = jnp.concatenate([ang(rows), ang(cols)], axis=-1)
    sign = jnp.where((jnp.arange(HEAD_DIM) % (2 * ROPE_PAIRS)) < ROPE_PAIRS, -1.0, 1.0).astype(F32)
    cos = jnp.concatenate([jnp.ones((n_ctx, HEAD_DIM), F32), jnp.cos(a)], axis=0)
    sin = jnp.concatenate([jnp.zeros((n_ctx, HEAD_DIM), F32), jnp.sin(a) * sign[None, :]], axis=0)
    return jnp.tile(cos, (1, LANES // HEAD_DIM)), jnp.tile(sin, (1, LANES // HEAD_DIM))


def kernel(x, c, ctx, c_ctx, mod_w, mod_b, norm_attn_w, norm_ffn_w, w_in, na_rpb, ssd_conv_w, ssd_conv_b,
           ssd_dt_bias, ssd_a_log, ssd_d, ssd_norm_w, q_norm_w, k_norm_w, w_out, ffn_w_gate, ffn_w_up,
           ffn_w_down, final_norm_w):
    depth = mod_w.shape[0]
    batch, n_lat, _ = x.shape
    n_ctx = ctx.shape[1]
    assert batch == 1 and n_ctx == ROW_TILE and n_lat % (NA_ROWS_PER_STEP * GRID_W) == 0
    assert n_lat % GQA_Q_TILE == 0 and n_lat // GRID_W >= NA_WIN_ROWS

    x_all = jnp.concatenate([ctx[0], x[0]], axis=0)
    cc = jnp.zeros((SUBLANES, D_MODEL), F32).at[0].set(c_ctx).at[1].set(c[0])
    mod = _modulation(cc, mod_w, mod_b)
    mod4 = mod[:, :2].reshape(depth, 2, 1, 6 * D_MODEL)

    cos_t, sin_t = _rope_tables(n_ctx, n_lat)
    blk = jnp.arange(2 * GQA_WIDTH) // HEAD_DIM
    ones_bd = (blk[:, None] == blk[None, :]).astype(BF16)
    head_cols = jnp.arange(SSD_WIDTH) // HEAD_DIM

    for i in range(depth):
        final = i == depth - 1
        w_cat = _rearranged_w_in(w_in[i])
        dt_bias_pad = jnp.pad(ssd_dt_bias[i].reshape(1, -1), ((0, 0), (0, LANES - 2 * SSD_HEADS)))
        qk_w = jnp.concatenate([jnp.tile(q_norm_w[i] * (ATTN_SCALE * LOG2E), GQA_Q_HEADS),
                                jnp.tile(k_norm_w[i], 2 * GQA_KV_HEADS)]).reshape(1, -1)
        na, z, xbc, dt, gq, gk, gv = _inproj(x_all, mod4, i, norm_attn_w[i].reshape(1, -1), w_cat,
                                             dt_bias_pad, qk_w, cos_t, sin_t, ones_bd)
        conv_w_pad = jnp.pad(ssd_conv_w[i], ((0, SUBLANES - SSD_CONV), (0, 0)))
        xconv = _ssd_conv(xbc, conv_w_pad, ssd_conv_b[i].reshape(1, -1))
        a_row = jnp.pad(-jnp.exp(ssd_a_log[i].astype(F32)).reshape(1, -1), ((0, 0), (0, LANES - 2 * SSD_HEADS)))
        yf, yb = _ssd_scan(xconv, dt, a_row, n_ctx // SSD_CHUNK)
        ya = _neighbourhood_attention(na, _na_bias_table(na_rpb[i]), n_ctx)
        yg = _gqa_attention(gq.T, gk, gv.T, n_ctx)
        d_full = ssd_d[i].astype(F32)[head_cols].reshape(1, -1)
        x_all = _out_ffn(ya, yf, yb, xconv, z, yg, x_all, mod4, i, d_full, ssd_norm_w[i].reshape(1, -1),
                         w_out[i].astype(BF16), norm_ffn_w[i].reshape(1, -1), ffn_w_gate[i].astype(BF16),
                         ffn_w_up[i].astype(BF16), ffn_w_down[i].astype(BF16),
                         final_norm_w.reshape(1, -1), final)
    return x_all[None]
```

```python
import functools
import math

import jax
import jax.numpy as jnp
from jax import lax
from jax.experimental import pallas as pl
from jax.experimental.pallas import tpu as pltpu

F32 = jnp.float32
BF16 = jnp.bfloat16

D_MODEL = 1024
GRID_W = 64
HEAD_DIM = 64
NA_WIDTH = 256
NA_HEADS = 4
NA_WIN_ROWS = 8
NA_WIN_COLS = 16
SSD_WIDTH = 512
SSD_HEADS = 8
SSD_GROUPS = 2
SSD_STATE = 128
SSD_CONV = 5
SSD_CHUNK = 128
SSD_CONV_CH = SSD_WIDTH + 2 * SSD_GROUPS * SSD_STATE
GQA_WIDTH = 256
GQA_Q_HEADS = 4
GQA_KV_HEADS = 2
ROPE_THETA = 10000.0
ROPE_PAIRS = HEAD_DIM // 4
FFN_HIDDEN = 2816
EPS = 1e-6
ATTN_SCALE = HEAD_DIM ** -0.5
LOG2E = math.log2(math.e)

LANES = 128
SUBLANES = 8
VMEM_LIMIT_BYTES = 56 * 1024 * 1024

ROW_TILE = 256
NA_ROWS_PER_STEP = 4
GQA_Q_TILE = 256
GQA_K_SUB = 256
GQA_STEPS_PER_TRIP = 8

C_NA = 0
C_Z = C_NA + 3 * NA_WIDTH
C_XBC = C_Z + SSD_WIDTH
C_DT = C_XBC + SSD_CONV_CH
C_GQ = C_DT + LANES
C_GK = C_GQ + GQA_WIDTH
C_GV = C_GK + 2 * LANES
C_END = C_GV + 2 * LANES


def _silu(v):
    return v * (1.0 / (1.0 + jnp.exp(-v)))


def _softplus(v):
    return jnp.maximum(v, 0.0) + jnp.log(1.0 + jnp.exp(-jnp.abs(v)))


def _split3(v):
    hi = v.astype(BF16)
    r1 = v - hi.astype(F32)
    mid = r1.astype(BF16)
    lo = (r1 - mid.astype(F32)).astype(BF16)
    return hi, mid, lo


def _dot(a, b):
    return jnp.dot(a, b, preferred_element_type=F32)


def _dot_nt(a, b):
    return lax.dot_general(a, b, (((1,), (1,)), ((), ())), preferred_element_type=F32)


def _dot_tn(a, b):
    return lax.dot_general(a, b, (((0,), (0,)), ((), ())), preferred_element_type=F32)


def _exact_dot(v, sel):
    hi, mid, lo = _split3(v)
    return _dot(hi, sel) + _dot(mid, sel) + _dot(lo, sel)


def _spread_dot(v, sel):
    hi = v.astype(BF16)
    lo = (v - hi.astype(F32)).astype(BF16)
    return _dot(hi, sel) + _dot(lo, sel)


def _exact_dot_lhs(sel, v):
    hi, mid, lo = _split3(v)
    return _dot(sel, hi) + _dot(sel, mid) + _dot(sel, lo)


def _params(*sem):
    return pltpu.CompilerParams(dimension_semantics=sem, vmem_limit_bytes=VMEM_LIMIT_BYTES)


def _resident(shape, index_map):
    return pl.BlockSpec(shape, index_map, pipeline_mode=pl.Buffered(1))


def _mod_kernel(cc_ref, w_ref, b_ref, o_ref):
    a = _silu(cc_ref[...])
    o_ref[0] = jnp.dot(a, w_ref[0], preferred_element_type=F32) + b_ref[0]


def _modulation(cc, mod_w, mod_b):
    depth = mod_w.shape[0]
    ncol = mod_w.shape[2] // D_MODEL
    return pl.pallas_call(
        _mod_kernel,
        grid=(depth, ncol),
        in_specs=[
            pl.BlockSpec((SUBLANES, D_MODEL), lambda l, j: (0, 0)),
            pl.BlockSpec((1, D_MODEL, D_MODEL), lambda l, j: (l, 0, j)),
            pl.BlockSpec((1, 1, D_MODEL), lambda l, j: (l, 0, j)),
        ],
        out_specs=pl.BlockSpec((1, SUBLANES, D_MODEL), lambda l, j: (l, 0, j)),
        out_shape=jax.ShapeDtypeStruct((depth, SUBLANES, ncol * D_MODEL), F32),
        compiler_params=_params("arbitrary", "arbitrary"),
        name="modulation",
    )(cc, mod_w, mod_b.reshape(depth, 1, -1))


def _residual_operands(x_parts, skip):
    ctx_spec = pl.BlockSpec((ROW_TILE, D_MODEL), lambda i: (0, 0))
    if isinstance(x_parts, tuple):
        lat_spec = pl.BlockSpec((ROW_TILE, D_MODEL), lambda i: (jnp.maximum(i + skip - 1, 0), 0))
        return list(x_parts), [ctx_spec, lat_spec], skip == 0
    row_spec = pl.BlockSpec((ROW_TILE, D_MODEL), lambda i: (i + skip, 0))
    return [x_parts, x_parts], [ctx_spec, row_spec], False


def _inproj_kernel(xc_ref, xl_ref, mod_ref, nw_ref, w_ref, dtb_ref, qkw_ref, cos_ref, sin_ref, ones_ref,
                   na_ref, z_ref, xbc_ref, dt_ref, gq_ref, gk_ref, gv_ref, *, pick_ctx):
    x = jnp.where(pl.program_id(0) == 0, xc_ref[...], xl_ref[...]) if pick_ctx else xl_ref[...]
    ms = jnp.mean(x * x, axis=-1, keepdims=True)
    xn = x * lax.rsqrt(ms + EPS) * nw_ref[...]
    sh = mod_ref[:, 0:D_MODEL]
    sc = mod_ref[:, D_MODEL:2 * D_MODEL]
    h = (xn * (1.0 + sc) + sh).astype(BF16)
    u = _dot(h, w_ref[...])
    na_ref[...] = u[:, C_NA:C_Z].astype(BF16)
    z_ref[...] = u[:, C_Z:C_XBC]
    xbc_ref[...] = u[:, C_XBC:C_DT]
    dt_ref[...] = _softplus(u[:, C_DT:C_GQ] + dtb_ref[...])
    vlane = lax.broadcasted_iota(jnp.int32, (x.shape[0], C_END - C_GV), 1)
    gv_ref[...] = jnp.where(vlane % LANES < HEAD_DIM, u[:, C_GV:C_END], 1.0).astype(BF16)
    g = u[:, C_GQ:C_GV]
    gsq = g * g
    hi = gsq.astype(BF16)
    lo = (gsq - hi.astype(F32)).astype(BF16)
    ss = _dot(hi, ones_ref[...]) + _dot(lo, ones_ref[...])
    gn = g * lax.rsqrt(ss * (1.0 / HEAD_DIM) + EPS) * qkw_ref[...]
    width = gn.shape[-1]
    lane = lax.broadcasted_iota(jnp.int32, gn.shape, 1)
    first = (lane % (2 * ROPE_PAIRS)) < ROPE_PAIRS
    partner = jnp.where(first, pltpu.roll(gn, width - ROPE_PAIRS, 1), pltpu.roll(gn, ROPE_PAIRS, 1))
    cos = jnp.concatenate([cos_ref[...]] * (width // LANES), axis=-1)
    sin = jnp.concatenate([sin_ref[...]] * (width // LANES), axis=-1)
    gr = gn * cos + partner * sin
    gq_ref[...] = gr[:, :GQA_WIDTH].astype(BF16)
    gk_ref[...] = gr[:, GQA_WIDTH:].astype(BF16)


def _inproj(x_parts, t, mod4, layer, norm_w, w_cat, dt_bias_pad, qk_w, cos_t, sin_t, ones_bd):
    n = t // ROW_TILE
    row = lambda i: (i, 0)
    const = lambda i: (0, 0)
    outs = [
        (3 * NA_WIDTH, BF16), (SSD_WIDTH, F32), (SSD_CONV_CH, F32), (LANES, F32),
        (GQA_WIDTH, BF16), (2 * LANES, BF16), (2 * LANES, BF16),
    ]
    x_arrays, x_specs, pick_ctx = _residual_operands(x_parts, 0)
    return pl.pallas_call(
        functools.partial(_inproj_kernel, pick_ctx=pick_ctx),
        grid=(n,),
        in_specs=x_specs + [
            pl.BlockSpec((None, None, 1, 6 * D_MODEL), lambda i: (layer, jnp.minimum(i, 1), 0, 0)),
            pl.BlockSpec((1, D_MODEL), const),
            _resident((D_MODEL, C_END), const),
            pl.BlockSpec((1, LANES), const),
            pl.BlockSpec((1, 2 * GQA_WIDTH), const),
            pl.BlockSpec((ROW_TILE, LANES), row),
            pl.BlockSpec((ROW_TILE, LANES), row),
            _resident((2 * GQA_WIDTH, 2 * GQA_WIDTH), const),
        ],
        out_specs=[pl.BlockSpec((ROW_TILE, w), row) for w, _ in outs],
        out_shape=[jax.ShapeDtypeStruct((t, w), d) for w, d in outs],
        compiler_params=_params("parallel"),
        name="inproj",
    )(*x_arrays, mod4, norm_w, w_cat, dt_bias_pad, qk_w, cos_t, sin_t, ones_bd)


def _conv_kernel(x_ref, prev_ref, next_ref, w_ref, b_ref, o_ref):
    i = pl.program_id(0)
    n = pl.num_programs(0)
    rows = x_ref.shape[0]
    has_prev = i >= 2
    has_next = jnp.logical_and(i >= 1, i < n - 1)
    x = x_ref[...]
    ext = jnp.concatenate([jnp.where(has_prev, prev_ref[...], 0.0), x,
                           jnp.where(has_next, next_ref[...], 0.0)], axis=0)
    total = rows + 2 * SUBLANES
    half = SSD_CONV // 2
    acc = b_ref[...] + w_ref[half:half + 1, :] * x
    for j in range(SSD_CONV):
        if j != half:
            shifted = pltpu.roll(ext, (half - j) % total, 0)
            acc = acc + w_ref[j:j + 1, :] * shifted[SUBLANES:SUBLANES + rows]
    o_ref[...] = _silu(acc)


def _ssd_conv(xbc, conv_w_pad, conv_b):
    t, ch = xbc.shape
    n = t // ROW_TILE
    per = ROW_TILE // SUBLANES
    last = t // SUBLANES - 1
    return pl.pallas_call(
        _conv_kernel,
        grid=(n,),
        in_specs=[
            pl.BlockSpec((ROW_TILE, ch), lambda i: (i, 0)),
            pl.BlockSpec((SUBLANES, ch), lambda i: (jnp.maximum(i * per - 1, 0), 0)),
            pl.BlockSpec((SUBLANES, ch), lambda i: (jnp.minimum((i + 1) * per, last), 0)),
            pl.BlockSpec((SUBLANES, ch), lambda i: (0, 0)),
            pl.BlockSpec((1, ch), lambda i: (0, 0)),
        ],
        out_specs=pl.BlockSpec((ROW_TILE, ch), lambda i: (i, 0)),
        out_shape=jax.ShapeDtypeStruct((t, ch), F32),
        compiler_params=_params("parallel"),
        name="ssd_conv",
    )(xbc, xbc, xbc, conv_w_pad, conv_b)


def _ssd_direction(xbc, dt, a_row, tri, expand, s_ref, lane0, backward):
    q = SSD_CHUNK
    x = xbc[:, :SSD_WIDTH]
    nb = SSD_GROUPS * SSD_STATE
    bmat = xbc[:, SSD_WIDTH:SSD_WIDTH + nb].astype(BF16)
    cmat = xbc[:, SSD_WIDTH + nb:].astype(BF16)
    cum = _exact_dot_lhs(tri, dt * a_row)
    yield
    last = 0 if backward else q - 1
    dt_full = _spread_dot(dt, expand)
    cum_full = _exact_dot(cum, expand)
    yield
    ea_full = jnp.exp(cum_full)
    te_full = jnp.exp(cum_full[last:last + 1, :] - cum_full)
    xr = x * dt_full
    xr_b = xr.astype(BF16)
    xt_b = (xr * te_full).astype(BF16)
    cum_t = cum.T
    ti = lax.broadcasted_iota(jnp.int32, (q, q), 0)
    si = lax.broadcasted_iota(jnp.int32, (q, q), 1)
    keep = (si >= ti) if backward else (si <= ti)
    lane = lax.broadcasted_iota(jnp.int32, (q, LANES), 1)
    heads_per_group = SSD_HEADS // SSD_GROUPS
    width_g = heads_per_group * HEAD_DIM
    pieces = []
    decay_total = ea_full[last:last + 1, :]
    for g in range(SSD_GROUPS):
        bg = bmat[:, g * SSD_STATE:(g + 1) * SSD_STATE]
        cg = cmat[:, g * SSD_STATE:(g + 1) * SSD_STATE]
        gmat = _dot_nt(cg, bg)
        s_old = s_ref[:, g * width_g:(g + 1) * width_g]
        y_off = _dot(cg, s_old.astype(BF16)) * ea_full[:, g * width_g:(g + 1) * width_g]
        s_new = _dot_tn(bg, xt_b[:, g * width_g:(g + 1) * width_g])
        s_ref[:, g * width_g:(g + 1) * width_g] = decay_total[:, g * width_g:(g + 1) * width_g] * s_old + s_new
        yield
        for pair in range(heads_per_group // 2):
            col = g * width_g + pair * LANES
            xr_pair = xr_b[:, col:col + LANES]
            ys = []
            for r in range(2):
                hl = lane0 + g * heads_per_group + 2 * pair + r
                seg = jnp.broadcast_to(cum[:, hl:hl + 1], (q, q)) - jnp.broadcast_to(cum_t[hl:hl + 1, :], (q, q))
                dec = jnp.exp(jnp.where(keep, seg, -jnp.inf))
                ys.append(_dot((gmat * dec).astype(BF16), xr_pair))
            y_diag = jnp.where(lane < HEAD_DIM, ys[0], ys[1])
            pieces.append(y_diag + y_off[:, pair * LANES:(pair + 1) * LANES])
            yield
    return jnp.concatenate(pieces, axis=-1)


def _run_interleaved(*stage_generators):
    results = [None] * len(stage_generators)
    live = list(range(len(stage_generators)))
    while live:
        for idx in list(live):
            try:
                next(stage_generators[idx])
            except StopIteration as done:
                results[idx] = done.value
                live.remove(idx)
    return results


def _ssd_kernel(xf_ref, dtf_ref, xb_ref, dtb_ref, a_ref, tril_ref, triu_ref, ef_ref, eb_ref,
                yf_ref, yb_ref, sf_ref, sb_ref):
    @pl.when(pl.program_id(0) == 0)
    def _():
        sf_ref[...] = jnp.zeros_like(sf_ref)
        sb_ref[...] = jnp.zeros_like(sb_ref)

    a_row = a_ref[...]
    yf, yb = _run_interleaved(
        _ssd_direction(xf_ref[...], dtf_ref[...], a_row, tril_ref[...], ef_ref[...], sf_ref, 0, False),
        _ssd_direction(xb_ref[...], dtb_ref[...], a_row, triu_ref[...], eb_ref[...], sb_ref, SSD_HEADS, True))
    yf_ref[...] = yf
    yb_ref[...] = yb


def _ssd_scan(xconv, dt, a_row, n_ctx_chunks):
    t = xconv.shape[0]
    n = t // SSD_CHUNK
    q = SSD_CHUNK
    r = jnp.arange(q)
    tril = (r[None, :] <= r[:, None]).astype(BF16)
    triu = (r[None, :] >= r[:, None]).astype(BF16)
    lanes = jnp.arange(LANES)[:, None]
    head_of_col = (jnp.arange(SSD_WIDTH) // HEAD_DIM)[None, :]
    expand_f = (lanes == head_of_col).astype(BF16)
    expand_b = (lanes == head_of_col + SSD_HEADS).astype(BF16)

    def fwd(i):
        return (i, 0)

    def bwd(i):
        return (jnp.where(i < n_ctx_chunks, n_ctx_chunks - 1 - i, n - 1 - (i - n_ctx_chunks)), 0)

    const = lambda i: (0, 0)
    return pl.pallas_call(
        _ssd_kernel,
        grid=(n,),
        in_specs=[
            pl.BlockSpec((q, SSD_CONV_CH), fwd),
            pl.BlockSpec((q, LANES), fwd),
            pl.BlockSpec((q, SSD_CONV_CH), bwd),
            pl.BlockSpec((q, LANES), bwd),
            pl.BlockSpec((1, LANES), const),
            pl.BlockSpec((q, q), const),
            pl.BlockSpec((q, q), const),
            pl.BlockSpec((LANES, SSD_WIDTH), const),
            pl.BlockSpec((LANES, SSD_WIDTH), const),
        ],
        out_specs=[pl.BlockSpec((q, SSD_WIDTH), fwd), pl.BlockSpec((q, SSD_WIDTH), bwd)],
        out_shape=[jax.ShapeDtypeStruct((t, SSD_WIDTH), F32)] * 2,
        scratch_shapes=[pltpu.VMEM((SSD_STATE, SSD_WIDTH), F32)] * 2,
        compiler_params=_params("arbitrary"),
        name="ssd_scan",
    )(xconv, dt, xconv, dt, a_row, tril, triu, expand_f, expand_b)


def _na_kernel(q_ref, k_ref, v_ref, bias_ref, o_ref, *, n_ctx, n_rows):
    i = pl.program_id(0)
    is_ctx = i == 0
    kc = k_ref[0:n_ctx, :]
    vc = v_ref[0:n_ctx, :]
    lane = lax.broadcasted_iota(jnp.int32, (GRID_W, NA_WIDTH), 1)
    mine = [(lane >= h * HEAD_DIM) & (lane < (h + 1) * HEAD_DIM) for h in range(NA_HEADS)]
    win = NA_WIN_ROWS * GRID_W

    def lane_tiles(a):
        return [a[:, c * LANES:(c + 1) * LANES] for c in range(a.shape[-1] // LANES)]

    def grid_row(j):
        r = jnp.maximum((i - 1) * NA_ROWS_PER_STEP + j, 0)
        r_start = jnp.clip(r - NA_WIN_ROWS // 2, 0, n_rows - NA_WIN_ROWS)
        variant = jnp.where(is_ctx, NA_WIN_ROWS, r_start - r + NA_WIN_ROWS - 1)
        start = pl.multiple_of(n_ctx + r_start * GRID_W, GRID_W)
        kw = k_ref[pl.ds(start, win), :]
        vw = v_ref[pl.ds(start, win), :]
        qj = q_ref[j * GRID_W:(j + 1) * GRID_W, :]
        qm = jnp.concatenate([jnp.where(mine[h], qj, jnp.zeros_like(qj)) for h in range(NA_HEADS)], axis=0)
        s_w = _dot_nt(qm, kw) + bias_ref[variant]
        s_c = _dot_nt(qm, kc)
        yield
        m = functools.reduce(jnp.maximum, lane_tiles(s_w) + lane_tiles(s_c))
        m = jnp.broadcast_to(jnp.max(m, axis=-1, keepdims=True), m.shape)
        p_w = jnp.exp(s_w - jnp.concatenate([m] * (s_w.shape[-1] // LANES), axis=-1))
        p_c = jnp.exp(s_c - jnp.concatenate([m] * (s_c.shape[-1] // LANES), axis=-1))
        l = jnp.sum(functools.reduce(jnp.add, lane_tiles(p_w) + lane_tiles(p_c)), axis=-1, keepdims=True)
        y = (_dot(p_w.astype(BF16), vw) + _dot(p_c.astype(BF16), vc)) * (1.0 / l)
        out = y[(NA_HEADS - 1) * GRID_W:]
        for h in range(NA_HEADS - 2, -1, -1):
            out = jnp.where(mine[h], y[h * GRID_W:(h + 1) * GRID_W], out)
        o_ref[j * GRID_W:(j + 1) * GRID_W, :] = out.astype(o_ref.dtype)

    _run_interleaved(*[grid_row(j) for j in range(NA_ROWS_PER_STEP)])


def _na_bias_table(rpb):
    col = jnp.arange(GRID_W)
    c_start = jnp.clip(col - NA_WIN_COLS // 2, 0, GRID_W - NA_WIN_COLS)
    in_win = (col[None, :] >= c_start[:, None]) & (col[None, :] < c_start[:, None] + NA_WIN_COLS)
    dc = jnp.clip(col[None, :] - col[:, None] + NA_WIN_COLS - 1, 0, 2 * NA_WIN_COLS - 2)
    n_dc = 2 * NA_WIN_COLS - 1
    n_dr = 2 * NA_WIN_ROWS - 1
    onehot = (dc[None, :, :] == jnp.arange(n_dc)[:, None, None]).astype(F32).reshape(n_dc, GRID_W * GRID_W)
    t2 = jnp.dot(rpb.reshape(NA_HEADS * n_dr, n_dc).astype(F32), onehot, precision=lax.Precision.HIGHEST)
    t2 = jnp.where(in_win[None, None], t2.reshape(NA_HEADS, n_dr, GRID_W, GRID_W), -jnp.inf)
    tab = jnp.stack([t2[:, v:v + NA_WIN_ROWS] for v in range(NA_WIN_ROWS)])
    tab = tab.transpose(0, 1, 3, 2, 4).reshape(NA_WIN_ROWS, NA_HEADS * GRID_W, NA_WIN_ROWS * GRID_W)
    masked = jnp.full((1,) + tab.shape[1:], -jnp.inf, F32)
    return jnp.concatenate([tab, masked], axis=0)


def _neighbourhood_attention(na, bias_tab, n_ctx):
    t = na.shape[0]
    n_rows = (t - n_ctx) // GRID_W
    step_rows = NA_ROWS_PER_STEP * GRID_W
    n = t // step_rows
    return pl.pallas_call(
        functools.partial(_na_kernel, n_ctx=n_ctx, n_rows=n_rows),
        grid=(n,),
        in_specs=[
            pl.BlockSpec((step_rows, NA_WIDTH), lambda i: (i, 0)),
            _resident((t, NA_WIDTH), lambda i: (0, 1)),
            _resident((t, NA_WIDTH), lambda i: (0, 2)),
            _resident(bias_tab.shape, lambda i: (0, 0, 0)),
        ],
        out_specs=pl.BlockSpec((step_rows, NA_WIDTH), lambda i: (i, 0)),
        out_shape=jax.ShapeDtypeStruct((t, NA_WIDTH), BF16),
        compiler_params=_params("parallel"),
        name="neighbourhood_attention",
    )(na, na, na, bias_tab)


def _gqa_kernel(qt_ref, k_ref, vt_ref, o_ref, acc_ref, s0_ref, s1_ref, p0_ref, p1_ref, *, n_sub):
    tq = qt_ref.shape[1]
    qt = qt_ref[...]
    row = lax.broadcasted_iota(jnp.int32, qt.shape, 0)
    top = row < HEAD_DIM
    zero = jnp.zeros_like(qt)
    qt2 = jnp.concatenate([jnp.where(top, qt, zero), jnp.where(top, zero, qt)], axis=1)
    acc_ref[...] = jnp.zeros_like(acc_ref)
    sub = GQA_K_SUB
    s_ref = (s0_ref, s1_ref)
    p_ref = (p0_ref, p1_ref)

    def keys(j):
        return pl.ds(pl.multiple_of(j * sub, sub), sub)

    def score(j, slot):
        s = _dot(k_ref[keys(j), :], qt2)
        s_ref[slot][...] = s
        return jnp.max(s.reshape(sub // SUBLANES, SUBLANES, 2 * tq), axis=0)

    def softmax(slot, m_old, part_max):
        m_new = jnp.maximum(m_old, jnp.max(part_max, axis=0, keepdims=True))
        s = s_ref[slot][...].reshape(sub // SUBLANES, SUBLANES, 2 * tq)
        p_ref[slot][...] = jnp.exp2(s - m_new[None]).reshape(sub, 2 * tq).astype(BF16)
        return m_new, jnp.exp2(m_old - m_new)

    def accumulate(j, slot, alpha):
        acc = acc_ref[...].reshape(LANES // SUBLANES, SUBLANES, 2 * tq) * alpha[None]
        acc_ref[...] = acc.reshape(LANES, 2 * tq) + _dot(vt_ref[:, keys(j)], p_ref[slot][...])

    m_init = jnp.full((SUBLANES, 2 * tq), -jnp.inf, F32)

    if n_sub < 3:
        m = m_init
        for j in range(n_sub):
            m, alpha = softmax(0, m, score(j, 0))
            accumulate(j, 0, alpha)
    else:
        part0 = score(0, 0)
        part1 = score(1, 1)
        m, alpha = softmax(0, m_init, part0)

        def step(t, slot, carry):
            m, alpha, part = carry
            part_next = score(t, slot)
            accumulate(t - 2, slot, alpha)
            return softmax(1 - slot, m, part) + (part_next,)

        def trip(n, carry):
            for u in range(GQA_STEPS_PER_TRIP):
                carry = step(2 + n * GQA_STEPS_PER_TRIP + u, u % 2, carry)
            return carry

        n_trips = (n_sub - 2) // GQA_STEPS_PER_TRIP
        carry = lax.fori_loop(0, n_trips, trip, (m, alpha, part1))
        for t in range(2 + n_trips * GQA_STEPS_PER_TRIP, n_sub):
            carry = step(t, t % 2, carry)
        m, alpha, part = carry
        accumulate(n_sub - 2, (n_sub - 2) % 2, alpha)
        m, alpha = softmax((n_sub - 1) % 2, m, part)
        accumulate(n_sub - 1, (n_sub - 1) % 2, alpha)

    acc = acc_ref[...]
    o_t = acc[:HEAD_DIM] / acc[HEAD_DIM:]
    o_ref[...] = jnp.concatenate([o_t[:, :tq], o_t[:, tq:]], axis=0).T.astype(o_ref.dtype)


def _gqa_call(gqt, gk, gvt, n_keys, tq, name):
    nq = gqt.shape[1]
    assert n_keys % GQA_K_SUB == 0 and nq % tq == 0
    return pl.pallas_call(
        functools.partial(_gqa_kernel, n_sub=n_keys // GQA_K_SUB),
        grid=(GQA_KV_HEADS, nq // tq),
        in_specs=[
            pl.BlockSpec((LANES, tq), lambda g, i: (g, i)),
            pl.BlockSpec((n_keys, LANES), lambda g, i: (0, g)),
            pl.BlockSpec((LANES, n_keys), lambda g, i: (g, 0)),
        ],
        out_specs=pl.BlockSpec((tq, LANES), lambda g, i: (i, g)),
        out_shape=jax.ShapeDtypeStruct((nq, GQA_WIDTH), BF16),
        scratch_shapes=[
            pltpu.VMEM((LANES, 2 * tq), F32),
            pltpu.VMEM((GQA_K_SUB, 2 * tq), F32),
            pltpu.VMEM((GQA_K_SUB, 2 * tq), F32),
            pltpu.VMEM((GQA_K_SUB, 2 * tq), BF16),
            pltpu.VMEM((GQA_K_SUB, 2 * tq), BF16),
        ],
        compiler_params=_params("arbitrary", "arbitrary"),
        name=name,
    )(gqt, gk, gvt)


def _gqa_attention(gqt, gk, gvt, n_ctx):
    y_ctx = _gqa_call(gqt[:, :n_ctx], gk, gvt, n_ctx, n_ctx, "gqa_attention_ctx")
    y_lat = _gqa_call(gqt[:, n_ctx:], gk, gvt, gk.shape[0], GQA_Q_TILE, "gqa_attention")
    return jnp.concatenate([y_ctx, y_lat], axis=0)


def _out_ffn_kernel(ya_ref, yf_ref, yb_ref, xs_ref, z_ref, yg_ref, xc_ref, xl_ref, mod_ref,
                    dskip_ref, snw_ref, wo_ref, fnw_ref, wg_ref, wu_ref, wd_ref, final_ref,
                    o_ref, *, final, pick_ctx):
    x = jnp.where(pl.program_id(0) == 0, xc_ref[...], xl_ref[...]) if pick_ctx else xl_ref[...]
    y = yf_ref[...] + yb_ref[...] + dskip_ref[...] * xs_ref[...]
    y = y * _silu(z_ref[...])
    ms = jnp.mean(y * y, axis=-1, keepdims=True)
    y = y * lax.rsqrt(ms + EPS) * snw_ref[...]
    mix = jnp.concatenate([ya_ref[...], y.astype(BF16), yg_ref[...]], axis=-1)
    g_m = mod_ref[:, 2 * D_MODEL:3 * D_MODEL]
    sh_f = mod_ref[:, 3 * D_MODEL:4 * D_MODEL]
    sc_f = mod_ref[:, 4 * D_MODEL:5 * D_MODEL]
    g_f = mod_ref[:, 5 * D_MODEL:6 * D_MODEL]
    x1 = x + g_m * _dot(mix, wo_ref[...])
    ms1 = jnp.mean(x1 * x1, axis=-1, keepdims=True)
    hf = (x1 * lax.rsqrt(ms1 + EPS) * fnw_ref[...] * (1.0 + sc_f) + sh_f).astype(BF16)
    act = (_silu(_dot(hf, wg_ref[...])) * _dot(hf, wu_ref[...])).astype(BF16)
    x2 = x1 + g_f * _dot(act, wd_ref[...])
    if final:
        ms2 = jnp.mean(x2 * x2, axis=-1, keepdims=True)
        x2 = x2 * lax.rsqrt(ms2 + EPS) * final_ref[...]
    o_ref[...] = x2


def _out_ffn(ya, yf, yb, xconv, z, yg, x_parts, mod4, layer, d_full, ssd_nw, w_out, ffn_nw,
             w_gate, w_up, w_down, final_nw, final):
    t = ya.shape[0]
    skip = 1 if final else 0
    n = t // ROW_TILE - skip
    row = lambda i: (i + skip, 0)
    const = lambda i: (0, 0)
    x_arrays, x_specs, pick_ctx = _residual_operands(x_parts, skip)
    return pl.pallas_call(
        functools.partial(_out_ffn_kernel, final=final, pick_ctx=pick_ctx),
        grid=(n,),
        in_specs=[
            pl.BlockSpec((ROW_TILE, NA_WIDTH), row),
            pl.BlockSpec((ROW_TILE, SSD_WIDTH), row),
            pl.BlockSpec((ROW_TILE, SSD_WIDTH), row),
            pl.BlockSpec((ROW_TILE, SSD_WIDTH), row),
            pl.BlockSpec((ROW_TILE, SSD_WIDTH), row),
            pl.BlockSpec((ROW_TILE, GQA_WIDTH), row),
        ] + x_specs + [
            pl.BlockSpec((None, None, 1, 6 * D_MODEL), lambda i: (layer, jnp.minimum(i + skip, 1), 0, 0)),
            pl.BlockSpec((1, SSD_WIDTH), const),
            pl.BlockSpec((1, SSD_WIDTH), const),
            _resident((D_MODEL, D_MODEL), const),
            pl.BlockSpec((1, D_MODEL), const),
            _resident((D_MODEL, FFN_HIDDEN), const),
            _resident((D_MODEL, FFN_HIDDEN), const),
            _resident((FFN_HIDDEN, D_MODEL), const),
            pl.BlockSpec((1, D_MODEL), const),
        ],
        out_specs=pl.BlockSpec((ROW_TILE, D_MODEL), lambda i: (i, 0)),
        out_shape=jax.ShapeDtypeStruct((n * ROW_TILE, D_MODEL), F32),
        compiler_params=_params("parallel"),
        name="out_ffn",
    )(ya, yf, yb, xconv, z, yg, *x_arrays, mod4, d_full, ssd_nw, w_out, ffn_nw, w_gate, w_up, w_down, final_nw)


def _rearranged_w_in(w):
    na_in = 3 * NA_WIDTH
    o_z = na_in
    o_xbc = o_z + SSD_WIDTH
    o_dt = o_xbc + SSD_CONV_CH
    o_gq = o_dt + 2 * SSD_HEADS
    o_gk = o_gq + GQA_WIDTH
    o_gv = o_gk + GQA_KV_HEADS * HEAD_DIM
    na = jnp.concatenate([w[:, :NA_WIDTH] * ATTN_SCALE, w[:, NA_WIDTH:na_in]], axis=1)
    dt = jnp.pad(w[:, o_dt:o_gq], ((0, 0), (0, LANES - 2 * SSD_HEADS)))

    def spread(seg, second_copy):
        heads = [seg[:, h * HEAD_DIM:(h + 1) * HEAD_DIM] for h in range(GQA_KV_HEADS)]
        return jnp.concatenate([p for h in heads for p in (h, h if second_copy else jnp.zeros_like(h))], axis=1)

    return jnp.concatenate(
        [na, w[:, o_z:o_xbc], w[:, o_xbc:o_dt], dt, w[:, o_gq:o_gk], spread(w[:, o_gk:o_gv], True),
         spread(w[:, o_gv:], False)], axis=1).astype(BF16)


def _rope_tables(n_ctx, n_lat):
    freqs = ROPE_THETA ** (-jnp.arange(ROPE_PAIRS, dtype=F32) / ROPE_PAIRS)
    n_rows = n_lat // GRID_W
    half = 2 * ROPE_PAIRS
    sign = jnp.where(jnp.arange(half) < ROPE_PAIRS, -1.0, 1.0).astype(F32)

    def tables(n_pos):
        a = jnp.arange(n_pos, dtype=F32)[:, None] * freqs[None, :]
        a = jnp.concatenate([a, a], axis=-1)
        return jnp.cos(a), jnp.sin(a) * sign[None, :]

    def per_token(by_row, by_col):
        lat = jnp.concatenate([jnp.broadcast_to(by_row[:, None, :], (n_rows, GRID_W, half)),
                               jnp.broadcast_to(by_col[None, :, :], (n_rows, GRID_W, half))], axis=-1)
        return lat.reshape(n_lat, HEAD_DIM)

    cos_r, sin_r = tables(n_rows)
    cos_c, sin_c = tables(GRID_W)
    cos = jnp.concatenate([jnp.ones((n_ctx, HEAD_DIM), F32), per_token(cos_r, cos_c)], axis=0)
    sin = jnp.concatenate([jnp.zeros((n_ctx, HEAD_DIM), F32), per_token(sin_r, sin_c)], axis=0)
    return jnp.tile(cos, (1, LANES // HEAD_DIM)), jnp.tile(sin, (1, LANES // HEAD_DIM))


def kernel(x, c, ctx, c_ctx, mod_w, mod_b, norm_attn_w, norm_ffn_w, w_in, na_rpb, ssd_conv_w, ssd_conv_b,
           ssd_dt_bias, ssd_a_log, ssd_d, ssd_norm_w, q_norm_w, k_norm_w, w_out, ffn_w_gate, ffn_w_up,
           ffn_w_down, final_norm_w):
    depth = mod_w.shape[0]
    batch, n_lat, _ = x.shape
    n_ctx = ctx.shape[1]
    assert batch == 1 and n_ctx == ROW_TILE and n_lat % (NA_ROWS_PER_STEP * GRID_W) == 0
    assert n_lat % GQA_Q_TILE == 0 and n_lat // GRID_W >= NA_WIN_ROWS

    x_parts = (ctx[0], x[0])
    cc = jnp.zeros((SUBLANES, D_MODEL), F32).at[0].set(c_ctx).at[1].set(c[0])
    mod = _modulation(cc, mod_w, mod_b)
    mod4 = mod[:, :2].reshape(depth, 2, 1, 6 * D_MODEL)

    cos_t, sin_t = _rope_tables(n_ctx, n_lat)
    blk = jnp.arange(2 * GQA_WIDTH) // HEAD_DIM
    ones_bd = (blk[:, None] == blk[None, :]).astype(BF16)
    head_cols = jnp.arange(SSD_WIDTH) // HEAD_DIM

    for i in range(depth):
        final = i == depth - 1
        w_cat = _rearranged_w_in(w_in[i])
        dt_bias_pad = jnp.pad(ssd_dt_bias[i].reshape(1, -1), ((0, 0), (0, LANES - 2 * SSD_HEADS)))
        qk_w = jnp.concatenate([jnp.tile(q_norm_w[i] * (ATTN_SCALE * LOG2E), GQA_Q_HEADS),
                                jnp.tile(k_norm_w[i], 2 * GQA_KV_HEADS)]).reshape(1, -1)
        na, z, xbc, dt, gq, gk, gv = _inproj(x_parts, n_ctx + n_lat, mod4, i, norm_attn_w[i].reshape(1, -1),
                                             w_cat, dt_bias_pad, qk_w, cos_t, sin_t, ones_bd)
        conv_w_pad = jnp.pad(ssd_conv_w[i], ((0, SUBLANES - SSD_CONV), (0, 0)))
        xconv = _ssd_conv(xbc, conv_w_pad, ssd_conv_b[i].reshape(1, -1))
        a_row = jnp.pad(-jnp.exp(ssd_a_log[i].astype(F32)).reshape(1, -1), ((0, 0), (0, LANES - 2 * SSD_HEADS)))
        yf, yb = _ssd_scan(xconv, dt, a_row, n_ctx // SSD_CHUNK)
        ya = _neighbourhood_attention(na, _na_bias_table(na_rpb[i]), n_ctx)
        yg = _gqa_attention(gq.T, gk, gv.T, n_ctx)
        d_full = ssd_d[i].astype(F32)[head_cols].reshape(1, -1)
        x_parts = _out_ffn(ya, yf, yb, xconv, z, yg, x_parts, mod4, i, d_full, ssd_norm_w[i].reshape(1, -1),
                           w_out[i].astype(BF16), norm_ffn_w[i].reshape(1, -1), ffn_w_gate[i].astype(BF16),
                           ffn_w_up[i].astype(BF16), ffn_w_down[i].astype(BF16),
                           final_norm_w.reshape(1, -1), final)
    return x_parts[None]
```

```python
import functools
import math

import jax
import jax.numpy as jnp
from jax import lax
from jax.experimental import pallas as pl
from jax.experimental.pallas import tpu as pltpu

F32 = jnp.float32
BF16 = jnp.bfloat16

D_MODEL = 1024
GRID_W = 64
HEAD_DIM = 64
NA_WIDTH = 256
NA_HEADS = 4
NA_WIN_ROWS = 8
NA_WIN_COLS = 16
SSD_WIDTH = 512
SSD_HEADS = 8
SSD_GROUPS = 2
SSD_STATE = 128
SSD_CONV = 5
SSD_CHUNK = 128
SSD_CONV_CH = SSD_WIDTH + 2 * SSD_GROUPS * SSD_STATE
GQA_WIDTH = 256
GQA_Q_HEADS = 4
GQA_KV_HEADS = 2
ROPE_THETA = 10000.0
ROPE_PAIRS = HEAD_DIM // 4
FFN_HIDDEN = 2816
EPS = 1e-6
ATTN_SCALE = HEAD_DIM ** -0.5
LOG2E = math.log2(math.e)

LANES = 128
SUBLANES = 8
VMEM_LIMIT_BYTES = 56 * 1024 * 1024

ROW_TILE = 256
NA_ROWS_PER_STEP = 4
GQA_Q_TILE = 256
GQA_V_ROWS = HEAD_DIM + 16
GQA_K_SUB = 256
GQA_STEPS_PER_TRIP = 12

C_NA = 0
C_Z = C_NA + 3 * NA_WIDTH
C_XBC = C_Z + SSD_WIDTH
C_DT = C_XBC + SSD_CONV_CH
C_GQ = C_DT + LANES
C_GK = C_GQ + GQA_WIDTH
C_GV = C_GK + 2 * LANES
C_END = C_GV + LANES


def _silu(v):
    return v * (1.0 / (1.0 + jnp.exp(-v)))


def _softplus(v):
    return jnp.maximum(v, 0.0) + jnp.log(1.0 + jnp.exp(-jnp.abs(v)))


def _split3(v):
    hi = v.astype(BF16)
    r1 = v - hi.astype(F32)
    mid = r1.astype(BF16)
    lo = (r1 - mid.astype(F32)).astype(BF16)
    return hi, mid, lo


def _dot(a, b):
    return jnp.dot(a, b, preferred_element_type=F32)


def _dot_nt(a, b):
    return lax.dot_general(a, b, (((1,), (1,)), ((), ())), preferred_element_type=F32)


def _dot_tn(a, b):
    return lax.dot_general(a, b, (((0,), (0,)), ((), ())), preferred_element_type=F32)


def _exact_dot(v, sel):
    hi, mid, lo = _split3(v)
    return _dot(hi, sel) + _dot(mid, sel) + _dot(lo, sel)


def _spread_dot(v, sel):
    hi = v.astype(BF16)
    lo = (v - hi.astype(F32)).astype(BF16)
    return _dot(hi, sel) + _dot(lo, sel)


def _exact_dot_lhs(sel, v):
    hi, mid, lo = _split3(v)
    return _dot(sel, hi) + _dot(sel, mid) + _dot(sel, lo)


def _params(*sem):
    return pltpu.CompilerParams(dimension_semantics=sem, vmem_limit_bytes=VMEM_LIMIT_BYTES)


def _resident(shape, index_map):
    return pl.BlockSpec(shape, index_map, pipeline_mode=pl.Buffered(1))


def _mod_kernel(cc_ref, w_ref, b_ref, o_ref):
    a = _silu(cc_ref[...])
    o_ref[0] = jnp.dot(a, w_ref[0], preferred_element_type=F32) + b_ref[0]


def _modulation(cc, mod_w, mod_b):
    depth = mod_w.shape[0]
    ncol = mod_w.shape[2] // D_MODEL
    return pl.pallas_call(
        _mod_kernel,
        grid=(depth, ncol),
        in_specs=[
            pl.BlockSpec((SUBLANES, D_MODEL), lambda l, j: (0, 0)),
            pl.BlockSpec((1, D_MODEL, D_MODEL), lambda l, j: (l, 0, j)),
            pl.BlockSpec((1, 1, D_MODEL), lambda l, j: (l, 0, j)),
        ],
        out_specs=pl.BlockSpec((1, SUBLANES, D_MODEL), lambda l, j: (l, 0, j)),
        out_shape=jax.ShapeDtypeStruct((depth, SUBLANES, ncol * D_MODEL), F32),
        compiler_params=_params("arbitrary", "arbitrary"),
        name="modulation",
    )(cc, mod_w, mod_b.reshape(depth, 1, -1))


def _residual_operands(x_parts, skip):
    ctx_spec = pl.BlockSpec((ROW_TILE, D_MODEL), lambda i: (0, 0))
    if isinstance(x_parts, tuple):
        lat_spec = pl.BlockSpec((ROW_TILE, D_MODEL), lambda i: (jnp.maximum(i + skip - 1, 0), 0))
        return list(x_parts), [ctx_spec, lat_spec], skip == 0
    row_spec = pl.BlockSpec((ROW_TILE, D_MODEL), lambda i: (i + skip, 0))
    return [x_parts, x_parts], [ctx_spec, row_spec], False


def _inproj_kernel(xc_ref, xl_ref, mod_ref, nw_ref, w_ref, dtb_ref, qkw_ref, cos_ref, sin_ref, ones_ref,
                   na_ref, z_ref, xbc_ref, dt_ref, gq_ref, gk_ref, gv_ref, *, pick_ctx):
    x = jnp.where(pl.program_id(0) == 0, xc_ref[...], xl_ref[...]) if pick_ctx else xl_ref[...]
    ms = jnp.mean(x * x, axis=-1, keepdims=True)
    xn = x * lax.rsqrt(ms + EPS) * nw_ref[...]
    sh = mod_ref[:, 0:D_MODEL]
    sc = mod_ref[:, D_MODEL:2 * D_MODEL]
    h = (xn * (1.0 + sc) + sh).astype(BF16)
    u = _dot(h, w_ref[...])
    na_ref[...] = u[:, C_NA:C_Z].astype(BF16)
    z_ref[...] = u[:, C_Z:C_XBC]
    xbc_ref[...] = u[:, C_XBC:C_DT]
    dt_ref[...] = _softplus(u[:, C_DT:C_GQ] + dtb_ref[...])
    gv_ref[...] = u[:, C_GV:C_END].astype(BF16)
    g = u[:, C_GQ:C_GV]
    gsq = g * g
    hi = gsq.astype(BF16)
    lo = (gsq - hi.astype(F32)).astype(BF16)
    ss = _dot(hi, ones_ref[...]) + _dot(lo, ones_ref[...])
    gn = g * lax.rsqrt(ss * (1.0 / HEAD_DIM) + EPS) * qkw_ref[...]
    width = gn.shape[-1]
    lane = lax.broadcasted_iota(jnp.int32, gn.shape, 1)
    first = (lane % (2 * ROPE_PAIRS)) < ROPE_PAIRS
    partner = jnp.where(first, pltpu.roll(gn, width - ROPE_PAIRS, 1), pltpu.roll(gn, ROPE_PAIRS, 1))
    cos = jnp.concatenate([cos_ref[...]] * (width // LANES), axis=-1)
    sin = jnp.concatenate([sin_ref[...]] * (width // LANES), axis=-1)
    gr = gn * cos + partner * sin
    gq_ref[...] = gr[:, :GQA_WIDTH].astype(BF16)
    gk_ref[...] = gr[:, GQA_WIDTH:].astype(BF16)


def _inproj(x_parts, t, mod4, layer, norm_w, w_cat, dt_bias_pad, qk_w, cos_t, sin_t, ones_bd):
    n = t // ROW_TILE
    row = lambda i: (i, 0)
    const = lambda i: (0, 0)
    outs = [
        (3 * NA_WIDTH, BF16), (SSD_WIDTH, F32), (SSD_CONV_CH, F32), (LANES, F32),
        (GQA_WIDTH, BF16), (2 * LANES, BF16), (LANES, BF16),
    ]
    x_arrays, x_specs, pick_ctx = _residual_operands(x_parts, 0)
    return pl.pallas_call(
        functools.partial(_inproj_kernel, pick_ctx=pick_ctx),
        grid=(n,),
        in_specs=x_specs + [
            pl.BlockSpec((None, None, 1, 6 * D_MODEL), lambda i: (layer, jnp.minimum(i, 1), 0, 0)),
            pl.BlockSpec((1, D_MODEL), const),
            _resident((D_MODEL, C_END), const),
            pl.BlockSpec((1, LANES), const),
            pl.BlockSpec((1, 2 * GQA_WIDTH), const),
            pl.BlockSpec((ROW_TILE, LANES), row),
            pl.BlockSpec((ROW_TILE, LANES), row),
            _resident((2 * GQA_WIDTH, 2 * GQA_WIDTH), const),
        ],
        out_specs=[pl.BlockSpec((ROW_TILE, w), row) for w, _ in outs],
        out_shape=[jax.ShapeDtypeStruct((t, w), d) for w, d in outs],
        compiler_params=_params("parallel"),
        name="inproj",
    )(*x_arrays, mod4, norm_w, w_cat, dt_bias_pad, qk_w, cos_t, sin_t, ones_bd)


def _conv_kernel(x_ref, prev_ref, next_ref, w_ref, b_ref, o_ref):
    i = pl.program_id(0)
    n = pl.num_programs(0)
    rows = x_ref.shape[0]
    has_prev = i >= 2
    has_next = jnp.logical_and(i >= 1, i < n - 1)
    x = x_ref[...]
    ext = jnp.concatenate([jnp.where(has_prev, prev_ref[...], 0.0), x,
                           jnp.where(has_next, next_ref[...], 0.0)], axis=0)
    total = rows + 2 * SUBLANES
    half = SSD_CONV // 2
    acc = b_ref[...] + w_ref[half:half + 1, :] * x
    for j in range(SSD_CONV):
        if j != half:
            shifted = pltpu.roll(ext, (half - j) % total, 0)
            acc = acc + w_ref[j:j + 1, :] * shifted[SUBLANES:SUBLANES + rows]
    o_ref[...] = _silu(acc)


def _ssd_conv(xbc, conv_w_pad, conv_b):
    t, ch = xbc.shape
    n = t // ROW_TILE
    per = ROW_TILE // SUBLANES
    last = t // SUBLANES - 1
    return pl.pallas_call(
        _conv_kernel,
        grid=(n,),
        in_specs=[
            pl.BlockSpec((ROW_TILE, ch), lambda i: (i, 0)),
            pl.BlockSpec((SUBLANES, ch), lambda i: (jnp.maximum(i * per - 1, 0), 0)),
            pl.BlockSpec((SUBLANES, ch), lambda i: (jnp.minimum((i + 1) * per, last), 0)),
            pl.BlockSpec((SUBLANES, ch), lambda i: (0, 0)),
            pl.BlockSpec((1, ch), lambda i: (0, 0)),
        ],
        out_specs=pl.BlockSpec((ROW_TILE, ch), lambda i: (i, 0)),
        out_shape=jax.ShapeDtypeStruct((t, ch), F32),
        compiler_params=_params("parallel"),
        name="ssd_conv",
    )(xbc, xbc, xbc, conv_w_pad, conv_b)


def _ssd_direction(xbc, dt, a_row, tri, expand, s_ref, lane0, backward):
    q = SSD_CHUNK
    x = xbc[:, :SSD_WIDTH]
    nb = SSD_GROUPS * SSD_STATE
    bmat = xbc[:, SSD_WIDTH:SSD_WIDTH + nb].astype(BF16)
    cmat = xbc[:, SSD_WIDTH + nb:].astype(BF16)
    cum = _exact_dot_lhs(tri, dt * a_row)
    yield
    last = 0 if backward else q - 1
    dt_full = _spread_dot(dt, expand)
    cum_full = _exact_dot(cum, expand)
    yield
    ea_full = jnp.exp(cum_full)
    te_full = jnp.exp(cum_full[last:last + 1, :] - cum_full)
    xr = x * dt_full
    xr_b = xr.astype(BF16)
    xt_b = (xr * te_full).astype(BF16)
    cum_t = cum.T
    ti = lax.broadcasted_iota(jnp.int32, (q, q), 0)
    si = lax.broadcasted_iota(jnp.int32, (q, q), 1)
    keep = (si >= ti) if backward else (si <= ti)
    lane = lax.broadcasted_iota(jnp.int32, (q, LANES), 1)
    heads_per_group = SSD_HEADS // SSD_GROUPS
    width_g = heads_per_group * HEAD_DIM
    pieces = []
    decay_total = ea_full[last:last + 1, :]
    for g in range(SSD_GROUPS):
        bg = bmat[:, g * SSD_STATE:(g + 1) * SSD_STATE]
        cg = cmat[:, g * SSD_STATE:(g + 1) * SSD_STATE]
        gmat = _dot_nt(cg, bg)
        s_old = s_ref[:, g * width_g:(g + 1) * width_g]
        y_off = _dot(cg, s_old.astype(BF16)) * ea_full[:, g * width_g:(g + 1) * width_g]
        s_new = _dot_tn(bg, xt_b[:, g * width_g:(g + 1) * width_g])
        s_ref[:, g * width_g:(g + 1) * width_g] = decay_total[:, g * width_g:(g + 1) * width_g] * s_old + s_new
        yield
        for pair in range(heads_per_group // 2):
            col = g * width_g + pair * LANES
            xr_pair = xr_b[:, col:col + LANES]
            ys = []
            for r in range(2):
                hl = lane0 + g * heads_per_group + 2 * pair + r
                seg = jnp.broadcast_to(cum[:, hl:hl + 1], (q, q)) - jnp.broadcast_to(cum_t[hl:hl + 1, :], (q, q))
                dec = jnp.exp(jnp.where(keep, seg, -jnp.inf))
                ys.append(_dot((gmat * dec).astype(BF16), xr_pair))
            y_diag = jnp.where(lane < HEAD_DIM, ys[0], ys[1])
            pieces.append(y_diag + y_off[:, pair * LANES:(pair + 1) * LANES])
            yield
    return jnp.concatenate(pieces, axis=-1)


def _run_interleaved(*stage_generators):
    results = [None] * len(stage_generators)
    live = list(range(len(stage_generators)))
    while live:
        for idx in list(live):
            try:
                next(stage_generators[idx])
            except StopIteration as done:
                results[idx] = done.value
                live.remove(idx)
    return results


def _ssd_kernel(xf_ref, dtf_ref, xb_ref, dtb_ref, a_ref, tril_ref, triu_ref, ef_ref, eb_ref,
                yf_ref, yb_ref, sf_ref, sb_ref):
    @pl.when(pl.program_id(0) == 0)
    def _():
        sf_ref[...] = jnp.zeros_like(sf_ref)
        sb_ref[...] = jnp.zeros_like(sb_ref)

    a_row = a_ref[...]
    yf, yb = _run_interleaved(
        _ssd_direction(xf_ref[...], dtf_ref[...], a_row, tril_ref[...], ef_ref[...], sf_ref, 0, False),
        _ssd_direction(xb_ref[...], dtb_ref[...], a_row, triu_ref[...], eb_ref[...], sb_ref, SSD_HEADS, True))
    yf_ref[...] = yf
    yb_ref[...] = yb


def _ssd_scan(xconv, dt, a_row, n_ctx_chunks):
    t = xconv.shape[0]
    n = t // SSD_CHUNK
    q = SSD_CHUNK
    r = jnp.arange(q)
    tril = (r[None, :] <= r[:, None]).astype(BF16)
    triu = (r[None, :] >= r[:, None]).astype(BF16)
    lanes = jnp.arange(LANES)[:, None]
    head_of_col = (jnp.arange(SSD_WIDTH) // HEAD_DIM)[None, :]
    expand_f = (lanes == head_of_col).astype(BF16)
    expand_b = (lanes == head_of_col + SSD_HEADS).astype(BF16)

    def fwd(i):
        return (i, 0)

    def bwd(i):
        return (jnp.where(i < n_ctx_chunks, n_ctx_chunks - 1 - i, n - 1 - (i - n_ctx_chunks)), 0)

    const = lambda i: (0, 0)
    return pl.pallas_call(
        _ssd_kernel,
        grid=(n,),
        in_specs=[
            pl.BlockSpec((q, SSD_CONV_CH), fwd),
            pl.BlockSpec((q, LANES), fwd),
            pl.BlockSpec((q, SSD_CONV_CH), bwd),
            pl.BlockSpec((q, LANES), bwd),
            pl.BlockSpec((1, LANES), const),
            pl.BlockSpec((q, q), const),
            pl.BlockSpec((q, q), const),
            pl.BlockSpec((LANES, SSD_WIDTH), const),
            pl.BlockSpec((LANES, SSD_WIDTH), const),
        ],
        out_specs=[pl.BlockSpec((q, SSD_WIDTH), fwd), pl.BlockSpec((q, SSD_WIDTH), bwd)],
        out_shape=[jax.ShapeDtypeStruct((t, SSD_WIDTH), F32)] * 2,
        scratch_shapes=[pltpu.VMEM((SSD_STATE, SSD_WIDTH), F32)] * 2,
        compiler_params=_params("arbitrary"),
        name="ssd_scan",
    )(xconv, dt, xconv, dt, a_row, tril, triu, expand_f, expand_b)


def _na_kernel(q_ref, k_ref, v_ref, bias_ref, o_ref, *, n_ctx, n_rows):
    i = pl.program_id(0)
    is_ctx = i == 0
    kc = k_ref[0:n_ctx, :]
    vc = v_ref[0:n_ctx, :]
    lane = lax.broadcasted_iota(jnp.int32, (GRID_W, NA_WIDTH), 1)
    mine = [(lane >= h * HEAD_DIM) & (lane < (h + 1) * HEAD_DIM) for h in range(NA_HEADS)]
    win = NA_WIN_ROWS * GRID_W

    def lane_tiles(a):
        return [a[:, c * LANES:(c + 1) * LANES] for c in range(a.shape[-1] // LANES)]

    def grid_row(j):
        r = jnp.maximum((i - 1) * NA_ROWS_PER_STEP + j, 0)
        r_start = jnp.clip(r - NA_WIN_ROWS // 2, 0, n_rows - NA_WIN_ROWS)
        variant = jnp.where(is_ctx, NA_WIN_ROWS, r_start - r + NA_WIN_ROWS - 1)
        start = pl.multiple_of(n_ctx + r_start * GRID_W, GRID_W)
        kw = k_ref[pl.ds(start, win), :]
        vw = v_ref[pl.ds(start, win), :]
        qj = q_ref[j * GRID_W:(j + 1) * GRID_W, :]
        qm = jnp.concatenate([jnp.where(mine[h], qj, jnp.zeros_like(qj)) for h in range(NA_HEADS)], axis=0)
        s_w = _dot_nt(qm, kw) + bias_ref[variant]
        s_c = _dot_nt(qm, kc)
        yield
        m = functools.reduce(jnp.maximum, lane_tiles(s_w) + lane_tiles(s_c))
        m = jnp.broadcast_to(jnp.max(m, axis=-1, keepdims=True), m.shape)
        p_w = jnp.exp(s_w - jnp.concatenate([m] * (s_w.shape[-1] // LANES), axis=-1))
        p_c = jnp.exp(s_c - jnp.concatenate([m] * (s_c.shape[-1] // LANES), axis=-1))
        l = jnp.sum(functools.reduce(jnp.add, lane_tiles(p_w) + lane_tiles(p_c)), axis=-1, keepdims=True)
        y = (_dot(p_w.astype(BF16), vw) + _dot(p_c.astype(BF16), vc)) * (1.0 / l)
        out = y[(NA_HEADS - 1) * GRID_W:]
        for h in range(NA_HEADS - 2, -1, -1):
            out = jnp.where(mine[h], y[h * GRID_W:(h + 1) * GRID_W], out)
        o_ref[j * GRID_W:(j + 1) * GRID_W, :] = out.astype(o_ref.dtype)

    _run_interleaved(*[grid_row(j) for j in range(NA_ROWS_PER_STEP)])


def _na_bias_table(rpb):
    col = jnp.arange(GRID_W)
    c_start = jnp.clip(col - NA_WIN_COLS // 2, 0, GRID_W - NA_WIN_COLS)
    in_win = (col[None, :] >= c_start[:, None]) & (col[None, :] < c_start[:, None] + NA_WIN_COLS)
    dc = jnp.clip(col[None, :] - col[:, None] + NA_WIN_COLS - 1, 0, 2 * NA_WIN_COLS - 2)
    n_dc = 2 * NA_WIN_COLS - 1
    n_dr = 2 * NA_WIN_ROWS - 1
    onehot = (dc[None, :, :] == jnp.arange(n_dc)[:, None, None]).astype(F32).reshape(n_dc, GRID_W * GRID_W)
    t2 = jnp.dot(rpb.reshape(NA_HEADS * n_dr, n_dc).astype(F32), onehot, precision=lax.Precision.HIGHEST)
    t2 = jnp.where(in_win[None, None], t2.reshape(NA_HEADS, n_dr, GRID_W, GRID_W), -jnp.inf)
    tab = jnp.stack([t2[:, v:v + NA_WIN_ROWS] for v in range(NA_WIN_ROWS)])
    tab = tab.transpose(0, 1, 3, 2, 4).reshape(NA_WIN_ROWS, NA_HEADS * GRID_W, NA_WIN_ROWS * GRID_W)
    masked = jnp.full((1,) + tab.shape[1:], -jnp.inf, F32)
    return jnp.concatenate([tab, masked], axis=0)


def _neighbourhood_attention(na, bias_tab, n_ctx):
    t = na.shape[0]
    n_rows = (t - n_ctx) // GRID_W
    step_rows = NA_ROWS_PER_STEP * GRID_W
    n = t // step_rows
    return pl.pallas_call(
        functools.partial(_na_kernel, n_ctx=n_ctx, n_rows=n_rows),
        grid=(n,),
        in_specs=[
            pl.BlockSpec((step_rows, NA_WIDTH), lambda i: (i, 0)),
            _resident((t, NA_WIDTH), lambda i: (0, 1)),
            _resident((t, NA_WIDTH), lambda i: (0, 2)),
            _resident(bias_tab.shape, lambda i: (0, 0, 0)),
        ],
        out_specs=pl.BlockSpec((step_rows, NA_WIDTH), lambda i: (i, 0)),
        out_shape=jax.ShapeDtypeStruct((t, NA_WIDTH), BF16),
        compiler_params=_params("parallel"),
        name="neighbourhood_attention",
    )(na, na, na, bias_tab)


def _gqa_kernel(qt_ref, k_ref, vt_ref, o_ref, acc_ref, s0_ref, s1_ref, p0_ref, p1_ref, *, n_head, n_sub):
    tq = qt_ref.shape[1]
    qt = qt_ref[...]
    row = lax.broadcasted_iota(jnp.int32, qt.shape, 0)
    top = row < HEAD_DIM
    zero = jnp.zeros_like(qt)
    qt2 = jnp.concatenate([jnp.where(top, qt, zero), jnp.where(top, zero, qt)], axis=1)
    acc_ref[...] = jnp.zeros_like(acc_ref)
    s_ref = (s0_ref, s1_ref)
    p_ref = (p0_ref, p1_ref)

    def keys(j, size):
        return pl.ds(pl.multiple_of(n_head + j * size, LANES), size)

    def score(key_rows, slot):
        size = key_rows.size
        s = _dot(k_ref[key_rows, :], qt2)
        s_ref[slot][0:size, :] = s
        return jnp.max(s.reshape(size // SUBLANES, SUBLANES, 2 * tq), axis=0)

    def softmax(size, slot, m_old, part_max):
        m_new = jnp.maximum(m_old, jnp.max(part_max, axis=0, keepdims=True))
        s = s_ref[slot][0:size, :].reshape(size // SUBLANES, SUBLANES, 2 * tq)
        p_ref[slot][0:size, :] = jnp.exp2(s - m_new[None]).reshape(size, 2 * tq).astype(BF16)
        return m_new, jnp.exp2(m_old - m_new)

    def accumulate(key_rows, slot, alpha):
        size = key_rows.size
        acc = acc_ref[...].reshape(GQA_V_ROWS // SUBLANES, SUBLANES, 2 * tq) * alpha[None]
        acc_ref[...] = acc.reshape(GQA_V_ROWS, 2 * tq) + _dot(vt_ref[:, key_rows], p_ref[slot][0:size, :])

    m = jnp.full((SUBLANES, 2 * tq), -jnp.inf, F32)
    if n_head:
        head = pl.ds(0, n_head)
        m, alpha = softmax(n_head, 0, m, score(head, 0))
        accumulate(head, 0, alpha)

    sub = GQA_K_SUB
    if n_sub:
        assert n_sub >= 3
        part0 = score(keys(0, sub), 0)
        part1 = score(keys(1, sub), 1)
        m, alpha = softmax(sub, 0, m, part0)

        def step(t, slot, carry):
            m, alpha, part = carry
            part_next = score(keys(t, sub), slot)
            accumulate(keys(t - 2, sub), slot, alpha)
            return softmax(sub, 1 - slot, m, part) + (part_next,)

        def trip(n, carry):
            for u in range(GQA_STEPS_PER_TRIP):
                carry = step(2 + n * GQA_STEPS_PER_TRIP + u, u % 2, carry)
            return carry

        n_trips = (n_sub - 2) // GQA_STEPS_PER_TRIP
        carry = lax.fori_loop(0, n_trips, trip, (m, alpha, part1))
        for t in range(2 + n_trips * GQA_STEPS_PER_TRIP, n_sub):
            carry = step(t, t % 2, carry)
        m, alpha, part = carry
        accumulate(keys(n_sub - 2, sub), (n_sub - 2) % 2, alpha)
        m, alpha = softmax(sub, (n_sub - 1) % 2, m, part)
        accumulate(keys(n_sub - 1, sub), (n_sub - 1) % 2, alpha)

    acc = acc_ref[...]
    denom = acc[HEAD_DIM:HEAD_DIM + SUBLANES]
    o_t = (acc[:HEAD_DIM].reshape(HEAD_DIM // SUBLANES, SUBLANES, 2 * tq) / denom[None]).reshape(HEAD_DIM, 2 * tq)
    o_ref[...] = jnp.concatenate([o_t[:, :tq], o_t[:, tq:]], axis=0).T.astype(o_ref.dtype)


def _gqa_call(gqt, gk, gvt, n_head, n_sub, tq, name):
    nq = gqt.shape[1]
    n_keys = n_head + n_sub * GQA_K_SUB
    buf_rows = max(n_head, GQA_K_SUB if n_sub else 0)
    assert n_head % LANES == 0 and nq % tq == 0
    return pl.pallas_call(
        functools.partial(_gqa_kernel, n_head=n_head, n_sub=n_sub),
        grid=(GQA_KV_HEADS, nq // tq),
        in_specs=[
            pl.BlockSpec((LANES, tq), lambda g, i: (g, i)),
            pl.BlockSpec((n_keys, LANES), lambda g, i: (0, g)),
            pl.BlockSpec((GQA_V_ROWS, n_keys), lambda g, i: (g, 0)),
        ],
        out_specs=pl.BlockSpec((tq, LANES), lambda g, i: (i, g)),
        out_shape=jax.ShapeDtypeStruct((nq, GQA_WIDTH), BF16),
        scratch_shapes=[
            pltpu.VMEM((GQA_V_ROWS, 2 * tq), F32),
            pltpu.VMEM((buf_rows, 2 * tq), F32),
            pltpu.VMEM((buf_rows, 2 * tq), F32),
            pltpu.VMEM((buf_rows, 2 * tq), BF16),
            pltpu.VMEM((buf_rows, 2 * tq), BF16),
        ],
        compiler_params=_params("arbitrary", "arbitrary"),
        name=name,
    )(gqt, gk, gvt)


def _gqa_attention(gqt, gk, gvt, n_ctx):
    n_head = n_ctx % GQA_K_SUB
    assert (gk.shape[0] - n_head) % GQA_K_SUB == 0
    y_ctx = _gqa_call(gqt[:, :n_ctx], gk, gvt, n_ctx, 0, n_ctx, "gqa_attention_ctx")
    y_lat = _gqa_call(gqt[:, n_ctx:], gk, gvt, n_head, (gk.shape[0] - n_head) // GQA_K_SUB, GQA_Q_TILE,
                      "gqa_attention")
    return jnp.concatenate([y_ctx, y_lat], axis=0)


def _out_ffn_kernel(ya_ref, yf_ref, yb_ref, xs_ref, z_ref, yg_ref, xc_ref, xl_ref, mod_ref,
                    dskip_ref, snw_ref, wo_ref, fnw_ref, wg_ref, wu_ref, wd_ref, final_ref,
                    o_ref, *, final, pick_ctx):
    x = jnp.where(pl.program_id(0) == 0, xc_ref[...], xl_ref[...]) if pick_ctx else xl_ref[...]
    y = yf_ref[...] + yb_ref[...] + dskip_ref[...] * xs_ref[...]
    y = y * _silu(z_ref[...])
    ms = jnp.mean(y * y, axis=-1, keepdims=True)
    y = y * lax.rsqrt(ms + EPS) * snw_ref[...]
    mix = jnp.concatenate([ya_ref[...], y.astype(BF16), yg_ref[...]], axis=-1)
    g_m = mod_ref[:, 2 * D_MODEL:3 * D_MODEL]
    sh_f = mod_ref[:, 3 * D_MODEL:4 * D_MODEL]
    sc_f = mod_ref[:, 4 * D_MODEL:5 * D_MODEL]
    g_f = mod_ref[:, 5 * D_MODEL:6 * D_MODEL]
    x1 = x + g_m * _dot(mix, wo_ref[...])
    ms1 = jnp.mean(x1 * x1, axis=-1, keepdims=True)
    hf = (x1 * lax.rsqrt(ms1 + EPS) * fnw_ref[...] * (1.0 + sc_f) + sh_f).astype(BF16)
    act = (_silu(_dot(hf, wg_ref[...])) * _dot(hf, wu_ref[...])).astype(BF16)
    x2 = x1 + g_f * _dot(act, wd_ref[...])
    if final:
        ms2 = jnp.mean(x2 * x2, axis=-1, keepdims=True)
        x2 = x2 * lax.rsqrt(ms2 + EPS) * final_ref[...]
    o_ref[...] = x2


def _out_ffn(ya, yf, yb, xconv, z, yg, x_parts, mod4, layer, d_full, ssd_nw, w_out, ffn_nw,
             w_gate, w_up, w_down, final_nw, final):
    t = ya.shape[0]
    skip = 1 if final else 0
    n = t // ROW_TILE - skip
    row = lambda i: (i + skip, 0)
    const = lambda i: (0, 0)
    x_arrays, x_specs, pick_ctx = _residual_operands(x_parts, skip)
    return pl.pallas_call(
        functools.partial(_out_ffn_kernel, final=final, pick_ctx=pick_ctx),
        grid=(n,),
        in_specs=[
            pl.BlockSpec((ROW_TILE, NA_WIDTH), row),
            pl.BlockSpec((ROW_TILE, SSD_WIDTH), row),
            pl.BlockSpec((ROW_TILE, SSD_WIDTH), row),
            pl.BlockSpec((ROW_TILE, SSD_WIDTH), row),
            pl.BlockSpec((ROW_TILE, SSD_WIDTH), row),
            pl.BlockSpec((ROW_TILE, GQA_WIDTH), row),
        ] + x_specs + [
            pl.BlockSpec((None, None, 1, 6 * D_MODEL), lambda i: (layer, jnp.minimum(i + skip, 1), 0, 0)),
            pl.BlockSpec((1, SSD_WIDTH), const),
            pl.BlockSpec((1, SSD_WIDTH), const),
            _resident((D_MODEL, D_MODEL), const),
            pl.BlockSpec((1, D_MODEL), const),
            _resident((D_MODEL, FFN_HIDDEN), const),
            _resident((D_MODEL, FFN_HIDDEN), const),
            _resident((FFN_HIDDEN, D_MODEL), const),
            pl.BlockSpec((1, D_MODEL), const),
        ],
        out_specs=pl.BlockSpec((ROW_TILE, D_MODEL), lambda i: (i, 0)),
        out_shape=jax.ShapeDtypeStruct((n * ROW_TILE, D_MODEL), F32),
        compiler_params=_params("parallel"),
        name="out_ffn",
    )(ya, yf, yb, xconv, z, yg, *x_arrays, mod4, d_full, ssd_nw, w_out, ffn_nw, w_gate, w_up, w_down, final_nw)


def _rearranged_w_in(w):
    na_in = 3 * NA_WIDTH
    o_z = na_in
    o_xbc = o_z + SSD_WIDTH
    o_dt = o_xbc + SSD_CONV_CH
    o_gq = o_dt + 2 * SSD_HEADS
    o_gk = o_gq + GQA_WIDTH
    o_gv = o_gk + GQA_KV_HEADS * HEAD_DIM
    na = jnp.concatenate([w[:, :NA_WIDTH] * ATTN_SCALE, w[:, NA_WIDTH:na_in]], axis=1)
    dt = jnp.pad(w[:, o_dt:o_gq], ((0, 0), (0, LANES - 2 * SSD_HEADS)))

    k_heads = [w[:, o_gk + h * HEAD_DIM:o_gk + (h + 1) * HEAD_DIM] for h in range(GQA_KV_HEADS)]
    k_twice = jnp.concatenate([p for h in k_heads for p in (h, h)], axis=1)
    return jnp.concatenate(
        [na, w[:, o_z:o_xbc], w[:, o_xbc:o_dt], dt, w[:, o_gq:o_gk], k_twice, w[:, o_gv:]], axis=1).astype(BF16)


def _rope_tables(n_ctx, n_lat):
    freqs = ROPE_THETA ** (-jnp.arange(ROPE_PAIRS, dtype=F32) / ROPE_PAIRS)
    n_rows = n_lat // GRID_W
    half = 2 * ROPE_PAIRS
    sign = jnp.where(jnp.arange(half) < ROPE_PAIRS, -1.0, 1.0).astype(F32)

    def tables(n_pos):
        a = jnp.arange(n_pos, dtype=F32)[:, None] * freqs[None, :]
        a = jnp.concatenate([a, a], axis=-1)
        return jnp.cos(a), jnp.sin(a) * sign[None, :]

    def per_token(by_row, by_col):
        lat = jnp.concatenate([jnp.broadcast_to(by_row[:, None, :], (n_rows, GRID_W, half)),
                               jnp.broadcast_to(by_col[None, :, :], (n_rows, GRID_W, half))], axis=-1)
        return lat.reshape(n_lat, HEAD_DIM)

    cos_r, sin_r = tables(n_rows)
    cos_c, sin_c = tables(GRID_W)
    cos = jnp.concatenate([jnp.ones((n_ctx, HEAD_DIM), F32), per_token(cos_r, cos_c)], axis=0)
    sin = jnp.concatenate([jnp.zeros((n_ctx, HEAD_DIM), F32), per_token(sin_r, sin_c)], axis=0)
    return jnp.tile(cos, (1, LANES // HEAD_DIM)), jnp.tile(sin, (1, LANES // HEAD_DIM))


def kernel(x, c, ctx, c_ctx, mod_w, mod_b, norm_attn_w, norm_ffn_w, w_in, na_rpb, ssd_conv_w, ssd_conv_b,
           ssd_dt_bias, ssd_a_log, ssd_d, ssd_norm_w, q_norm_w, k_norm_w, w_out, ffn_w_gate, ffn_w_up,
           ffn_w_down, final_norm_w):
    depth = mod_w.shape[0]
    batch, n_lat, _ = x.shape
    n_ctx = ctx.shape[1]
    assert batch == 1 and n_ctx == ROW_TILE and n_lat % (NA_ROWS_PER_STEP * GRID_W) == 0
    assert n_lat % GQA_Q_TILE == 0 and n_lat // GRID_W >= NA_WIN_ROWS

    x_parts = (ctx[0], x[0])
    cc = jnp.zeros((SUBLANES, D_MODEL), F32).at[0].set(c_ctx).at[1].set(c[0])
    mod = _modulation(cc, mod_w, mod_b)
    mod4 = mod[:, :2].reshape(depth, 2, 1, 6 * D_MODEL)

    cos_t, sin_t = _rope_tables(n_ctx, n_lat)
    blk = jnp.arange(2 * GQA_WIDTH) // HEAD_DIM
    ones_bd = (blk[:, None] == blk[None, :]).astype(BF16)
    head_cols = jnp.arange(SSD_WIDTH) // HEAD_DIM

    for i in range(depth):
        final = i == depth - 1
        w_cat = _rearranged_w_in(w_in[i])
        dt_bias_pad = jnp.pad(ssd_dt_bias[i].reshape(1, -1), ((0, 0), (0, LANES - 2 * SSD_HEADS)))
        qk_w = jnp.concatenate([jnp.tile(q_norm_w[i] * (ATTN_SCALE * LOG2E), GQA_Q_HEADS),
                                jnp.tile(k_norm_w[i], 2 * GQA_KV_HEADS)]).reshape(1, -1)
        na, z, xbc, dt, gq, gk, gv = _inproj(x_parts, n_ctx + n_lat, mod4, i, norm_attn_w[i].reshape(1, -1),
                                             w_cat, dt_bias_pad, qk_w, cos_t, sin_t, ones_bd)
        conv_w_pad = jnp.pad(ssd_conv_w[i], ((0, SUBLANES - SSD_CONV), (0, 0)))
        xconv = _ssd_conv(xbc, conv_w_pad, ssd_conv_b[i].reshape(1, -1))
        a_row = jnp.pad(-jnp.exp(ssd_a_log[i].astype(F32)).reshape(1, -1), ((0, 0), (0, LANES - 2 * SSD_HEADS)))
        yf, yb = _ssd_scan(xconv, dt, a_row, n_ctx // SSD_CHUNK)
        ya = _neighbourhood_attention(na, _na_bias_table(na_rpb[i]), n_ctx)
        ones_rows = jnp.ones((GQA_V_ROWS - HEAD_DIM, gv.shape[0]), BF16)
        gvt = jnp.concatenate([part for g in range(GQA_KV_HEADS)
                               for part in (gv[:, g * HEAD_DIM:(g + 1) * HEAD_DIM].T, ones_rows)], axis=0)
        yg = _gqa_attention(gq.T, gk, gvt, n_ctx)
        d_full = ssd_d[i].astype(F32)[head_cols].reshape(1, -1)
        x_parts = _out_ffn(ya, yf, yb, xconv, z, yg, x_parts, mod4, i, d_full, ssd_norm_w[i].reshape(1, -1),
                           w_out[i].astype(BF16), norm_ffn_w[i].reshape(1, -1), ffn_w_gate[i].astype(BF16),
                           ffn_w_up[i].astype(BF16), ffn_w_down[i].astype(BF16),
                           final_norm_w.reshape(1, -1), final)
    return x_parts[None]
```

```python
import functools
import math

import jax
import jax.numpy as jnp
from jax import lax
from jax.experimental import pallas as pl
from jax.experimental.pallas import tpu as pltpu

F32 = jnp.float32
BF16 = jnp.bfloat16

D_MODEL = 1024
GRID_W = 64
HEAD_DIM = 64
NA_WIDTH = 256
NA_HEADS = 4
NA_WIN_ROWS = 8
NA_WIN_COLS = 16
SSD_WIDTH = 512
SSD_HEADS = 8
SSD_GROUPS = 2
SSD_STATE = 128
SSD_CONV = 5
SSD_CHUNK = 128
SSD_CONV_CH = SSD_WIDTH + 2 * SSD_GROUPS * SSD_STATE
GQA_WIDTH = 256
GQA_Q_HEADS = 4
GQA_KV_HEADS = 2
ROPE_THETA = 10000.0
ROPE_PAIRS = HEAD_DIM // 4
FFN_HIDDEN = 2816
EPS = 1e-6
ATTN_SCALE = HEAD_DIM ** -0.5
LOG2E = math.log2(math.e)

LANES = 128
SUBLANES = 8
VMEM_LIMIT_BYTES = 56 * 1024 * 1024

ROW_TILE = 256
NA_ROWS_PER_STEP = 4
GQA_Q_TILE = 256
GQA_V_ROWS = HEAD_DIM + 16
GQA_K_SUB = 256
GQA_BOUNDED_LOG2 = 60.0
GQA_PV_LAG = 2
GQA_BOUNDED_STEPS_PER_TRIP = 9
GQA_STEPS_PER_TRIP = 8

C_NA = 0
C_Z = C_NA + 3 * NA_WIDTH
C_XBC = C_Z + SSD_WIDTH
C_DT = C_XBC + SSD_CONV_CH
C_GQ = C_DT + LANES
C_GK = C_GQ + GQA_WIDTH
C_GV = C_GK + 2 * LANES
C_END = C_GV + LANES


def _silu(v):
    return v * (1.0 / (1.0 + jnp.exp(-v)))


def _softplus(v):
    return jnp.maximum(v, 0.0) + jnp.log(1.0 + jnp.exp(-jnp.abs(v)))


def _split3(v):
    hi = v.astype(BF16)
    r1 = v - hi.astype(F32)
    mid = r1.astype(BF16)
    lo = (r1 - mid.astype(F32)).astype(BF16)
    return hi, mid, lo


def _dot(a, b):
    return jnp.dot(a, b, preferred_element_type=F32)


def _dot_nt(a, b):
    return lax.dot_general(a, b, (((1,), (1,)), ((), ())), preferred_element_type=F32)


def _dot_tn(a, b):
    return lax.dot_general(a, b, (((0,), (0,)), ((), ())), preferred_element_type=F32)


def _exact_dot(v, sel):
    hi, mid, lo = _split3(v)
    return _dot(hi, sel) + _dot(mid, sel) + _dot(lo, sel)


def _spread_dot(v, sel):
    hi = v.astype(BF16)
    lo = (v - hi.astype(F32)).astype(BF16)
    return _dot(hi, sel) + _dot(lo, sel)


def _exact_dot_lhs(sel, v):
    hi, mid, lo = _split3(v)
    return _dot(sel, hi) + _dot(sel, mid) + _dot(sel, lo)


def _params(*sem):
    return pltpu.CompilerParams(dimension_semantics=sem, vmem_limit_bytes=VMEM_LIMIT_BYTES)


def _resident(shape, index_map):
    return pl.BlockSpec(shape, index_map, pipeline_mode=pl.Buffered(1))


def _mod_kernel(cc_ref, w_ref, b_ref, o_ref):
    a = _silu(cc_ref[...])
    o_ref[0] = jnp.dot(a, w_ref[0], preferred_element_type=F32) + b_ref[0]


def _modulation(cc, mod_w, mod_b):
    depth = mod_w.shape[0]
    ncol = mod_w.shape[2] // D_MODEL
    return pl.pallas_call(
        _mod_kernel,
        grid=(depth, ncol),
        in_specs=[
            pl.BlockSpec((SUBLANES, D_MODEL), lambda l, j: (0, 0)),
            pl.BlockSpec((1, D_MODEL, D_MODEL), lambda l, j: (l, 0, j)),
            pl.BlockSpec((1, 1, D_MODEL), lambda l, j: (l, 0, j)),
        ],
        out_specs=pl.BlockSpec((1, SUBLANES, D_MODEL), lambda l, j: (l, 0, j)),
        out_shape=jax.ShapeDtypeStruct((depth, SUBLANES, ncol * D_MODEL), F32),
        compiler_params=_params("arbitrary", "arbitrary"),
        name="modulation",
    )(cc, mod_w, mod_b.reshape(depth, 1, -1))


def _residual_operands(x_parts, skip):
    ctx_spec = pl.BlockSpec((ROW_TILE, D_MODEL), lambda i: (0, 0))
    if isinstance(x_parts, tuple):
        lat_spec = pl.BlockSpec((ROW_TILE, D_MODEL), lambda i: (jnp.maximum(i + skip - 1, 0), 0))
        return list(x_parts), [ctx_spec, lat_spec], skip == 0
    row_spec = pl.BlockSpec((ROW_TILE, D_MODEL), lambda i: (i + skip, 0))
    return [x_parts, x_parts], [ctx_spec, row_spec], False


def _inproj_kernel(xc_ref, xl_ref, mod_ref, nw_ref, w_ref, dtb_ref, qkw_ref, cos_ref, sin_ref, ones_ref,
                   na_ref, z_ref, xbc_ref, dt_ref, gq_ref, gk_ref, gv_ref, *, pick_ctx):
    x = jnp.where(pl.program_id(0) == 0, xc_ref[...], xl_ref[...]) if pick_ctx else xl_ref[...]
    ms = jnp.mean(x * x, axis=-1, keepdims=True)
    xn = x * lax.rsqrt(ms + EPS) * nw_ref[...]
    sh = mod_ref[:, 0:D_MODEL]
    sc = mod_ref[:, D_MODEL:2 * D_MODEL]
    h = (xn * (1.0 + sc) + sh).astype(BF16)
    u = _dot(h, w_ref[...])
    na_ref[...] = u[:, C_NA:C_Z].astype(BF16)
    z_ref[...] = u[:, C_Z:C_XBC]
    xbc_ref[...] = u[:, C_XBC:C_DT]
    dt_ref[...] = _softplus(u[:, C_DT:C_GQ] + dtb_ref[...])
    gv_ref[...] = u[:, C_GV:C_END].astype(BF16)
    g = u[:, C_GQ:C_GV]
    gsq = g * g
    hi = gsq.astype(BF16)
    lo = (gsq - hi.astype(F32)).astype(BF16)
    ss = _dot(hi, ones_ref[...]) + _dot(lo, ones_ref[...])
    gn = g * lax.rsqrt(ss * (1.0 / HEAD_DIM) + EPS) * qkw_ref[...]
    width = gn.shape[-1]
    lane = lax.broadcasted_iota(jnp.int32, gn.shape, 1)
    first = (lane % (2 * ROPE_PAIRS)) < ROPE_PAIRS
    partner = jnp.where(first, pltpu.roll(gn, width - ROPE_PAIRS, 1), pltpu.roll(gn, ROPE_PAIRS, 1))
    cos = jnp.concatenate([cos_ref[...]] * (width // LANES), axis=-1)
    sin = jnp.concatenate([sin_ref[...]] * (width // LANES), axis=-1)
    gr = gn * cos + partner * sin
    gq_ref[...] = gr[:, :GQA_WIDTH].astype(BF16)
    gk_ref[...] = gr[:, GQA_WIDTH:].astype(BF16)


def _inproj(x_parts, t, mod4, layer, norm_w, w_cat, dt_bias_pad, qk_w, cos_t, sin_t, ones_bd):
    n = t // ROW_TILE
    row = lambda i: (i, 0)
    const = lambda i: (0, 0)
    outs = [
        (3 * NA_WIDTH, BF16), (SSD_WIDTH, F32), (SSD_CONV_CH, F32), (LANES, F32),
        (GQA_WIDTH, BF16), (2 * LANES, BF16), (LANES, BF16),
    ]
    x_arrays, x_specs, pick_ctx = _residual_operands(x_parts, 0)
    return pl.pallas_call(
        functools.partial(_inproj_kernel, pick_ctx=pick_ctx),
        grid=(n,),
        in_specs=x_specs + [
            pl.BlockSpec((None, None, 1, 6 * D_MODEL), lambda i: (layer, jnp.minimum(i, 1), 0, 0)),
            pl.BlockSpec((1, D_MODEL), const),
            _resident((D_MODEL, C_END), const),
            pl.BlockSpec((1, LANES), const),
            pl.BlockSpec((1, 2 * GQA_WIDTH), const),
            pl.BlockSpec((ROW_TILE, LANES), row),
            pl.BlockSpec((ROW_TILE, LANES), row),
            _resident((2 * GQA_WIDTH, 2 * GQA_WIDTH), const),
        ],
        out_specs=[pl.BlockSpec((ROW_TILE, w), row) for w, _ in outs],
        out_shape=[jax.ShapeDtypeStruct((t, w), d) for w, d in outs],
        compiler_params=_params("parallel"),
        name="inproj",
    )(*x_arrays, mod4, norm_w, w_cat, dt_bias_pad, qk_w, cos_t, sin_t, ones_bd)


def _conv_kernel(x_ref, prev_ref, next_ref, w_ref, b_ref, o_ref):
    i = pl.program_id(0)
    n = pl.num_programs(0)
    rows = x_ref.shape[0]
    has_prev = i >= 2
    has_next = jnp.logical_and(i >= 1, i < n - 1)
    x = x_ref[...]
    ext = jnp.concatenate([jnp.where(has_prev, prev_ref[...], 0.0), x,
                           jnp.where(has_next, next_ref[...], 0.0)], axis=0)
    total = rows + 2 * SUBLANES
    half = SSD_CONV // 2
    acc = b_ref[...] + w_ref[half:half + 1, :] * x
    for j in range(SSD_CONV):
        if j != half:
            shifted = pltpu.roll(ext, (half - j) % total, 0)
            acc = acc + w_ref[j:j + 1, :] * shifted[SUBLANES:SUBLANES + rows]
    o_ref[...] = _silu(acc)


def _ssd_conv(xbc, conv_w_pad, conv_b):
    t, ch = xbc.shape
    n = t // ROW_TILE
    per = ROW_TILE // SUBLANES
    last = t // SUBLANES - 1
    return pl.pallas_call(
        _conv_kernel,
        grid=(n,),
        in_specs=[
            pl.BlockSpec((ROW_TILE, ch), lambda i: (i, 0)),
            pl.BlockSpec((SUBLANES, ch), lambda i: (jnp.maximum(i * per - 1, 0), 0)),
            pl.BlockSpec((SUBLANES, ch), lambda i: (jnp.minimum((i + 1) * per, last), 0)),
            pl.BlockSpec((SUBLANES, ch), lambda i: (0, 0)),
            pl.BlockSpec((1, ch), lambda i: (0, 0)),
        ],
        out_specs=pl.BlockSpec((ROW_TILE, ch), lambda i: (i, 0)),
        out_shape=jax.ShapeDtypeStruct((t, ch), F32),
        compiler_params=_params("parallel"),
        name="ssd_conv",
    )(xbc, xbc, xbc, conv_w_pad, conv_b)


def _ssd_direction(xbc, dt, a_row, tri, expand, s_ref, lane0, backward):
    q = SSD_CHUNK
    x = xbc[:, :SSD_WIDTH]
    nb = SSD_GROUPS * SSD_STATE
    bmat = xbc[:, SSD_WIDTH:SSD_WIDTH + nb].astype(BF16)
    cmat = xbc[:, SSD_WIDTH + nb:].astype(BF16)
    cum = _exact_dot_lhs(tri, dt * a_row)
    yield
    last = 0 if backward else q - 1
    dt_full = _spread_dot(dt, expand)
    cum_full = _exact_dot(cum, expand)
    yield
    ea_full = jnp.exp(cum_full)
    te_full = jnp.exp(cum_full[last:last + 1, :] - cum_full)
    xr = x * dt_full
    xr_b = xr.astype(BF16)
    xt_b = (xr * te_full).astype(BF16)
    cum_t = cum.T
    ti = lax.broadcasted_iota(jnp.int32, (q, q), 0)
    si = lax.broadcasted_iota(jnp.int32, (q, q), 1)
    keep = (si >= ti) if backward else (si <= ti)
    lane = lax.broadcasted_iota(jnp.int32, (q, LANES), 1)
    heads_per_group = SSD_HEADS // SSD_GROUPS
    width_g = heads_per_group * HEAD_DIM
    pieces = []
    decay_total = ea_full[last:last + 1, :]
    for g in range(SSD_GROUPS):
        bg = bmat[:, g * SSD_STATE:(g + 1) * SSD_STATE]
        cg = cmat[:, g * SSD_STATE:(g + 1) * SSD_STATE]
        gmat = _dot_nt(cg, bg)
        s_old = s_ref[:, g * width_g:(g + 1) * width_g]
        y_off = _dot(cg, s_old.astype(BF16)) * ea_full[:, g * width_g:(g + 1) * width_g]
        s_new = _dot_tn(bg, xt_b[:, g * width_g:(g + 1) * width_g])
        s_ref[:, g * width_g:(g + 1) * width_g] = decay_total[:, g * width_g:(g + 1) * width_g] * s_old + s_new
        yield
        for pair in range(heads_per_group // 2):
            col = g * width_g + pair * LANES
            xr_pair = xr_b[:, col:col + LANES]
            ys = []
            for r in range(2):
                hl = lane0 + g * heads_per_group + 2 * pair + r
                seg = jnp.broadcast_to(cum[:, hl:hl + 1], (q, q)) - jnp.broadcast_to(cum_t[hl:hl + 1, :], (q, q))
                dec = jnp.exp(jnp.where(keep, seg, -jnp.inf))
                ys.append(_dot((gmat * dec).astype(BF16), xr_pair))
            y_diag = jnp.where(lane < HEAD_DIM, ys[0], ys[1])
            pieces.append(y_diag + y_off[:, pair * LANES:(pair + 1) * LANES])
            yield
    return jnp.concatenate(pieces, axis=-1)


def _run_interleaved(*stage_generators):
    results = [None] * len(stage_generators)
    live = list(range(len(stage_generators)))
    while live:
        for idx in list(live):
            try:
                next(stage_generators[idx])
            except StopIteration as done:
                results[idx] = done.value
                live.remove(idx)
    return results


def _ssd_kernel(xf_ref, dtf_ref, xb_ref, dtb_ref, a_ref, tril_ref, triu_ref, ef_ref, eb_ref,
                yf_ref, yb_ref, sf_ref, sb_ref):
    @pl.when(pl.program_id(0) == 0)
    def _():
        sf_ref[...] = jnp.zeros_like(sf_ref)
        sb_ref[...] = jnp.zeros_like(sb_ref)

    a_row = a_ref[...]
    yf, yb = _run_interleaved(
        _ssd_direction(xf_ref[...], dtf_ref[...], a_row, tril_ref[...], ef_ref[...], sf_ref, 0, False),
        _ssd_direction(xb_ref[...], dtb_ref[...], a_row, triu_ref[...], eb_ref[...], sb_ref, SSD_HEADS, True))
    yf_ref[...] = yf
    yb_ref[...] = yb


def _ssd_scan(xconv, dt, a_row, n_ctx_chunks):
    t = xconv.shape[0]
    n = t // SSD_CHUNK
    q = SSD_CHUNK
    r = jnp.arange(q)
    tril = (r[None, :] <= r[:, None]).astype(BF16)
    triu = (r[None, :] >= r[:, None]).astype(BF16)
    lanes = jnp.arange(LANES)[:, None]
    head_of_col = (jnp.arange(SSD_WIDTH) // HEAD_DIM)[None, :]
    expand_f = (lanes == head_of_col).astype(BF16)
    expand_b = (lanes == head_of_col + SSD_HEADS).astype(BF16)

    def fwd(i):
        return (i, 0)

    def bwd(i):
        return (jnp.where(i < n_ctx_chunks, n_ctx_chunks - 1 - i, n - 1 - (i - n_ctx_chunks)), 0)

    const = lambda i: (0, 0)
    return pl.pallas_call(
        _ssd_kernel,
        grid=(n,),
        in_specs=[
            pl.BlockSpec((q, SSD_CONV_CH), fwd),
            pl.BlockSpec((q, LANES), fwd),
            pl.BlockSpec((q, SSD_CONV_CH), bwd),
            pl.BlockSpec((q, LANES), bwd),
            pl.BlockSpec((1, LANES), const),
            pl.BlockSpec((q, q), const),
            pl.BlockSpec((q, q), const),
            pl.BlockSpec((LANES, SSD_WIDTH), const),
            pl.BlockSpec((LANES, SSD_WIDTH), const),
        ],
        out_specs=[pl.BlockSpec((q, SSD_WIDTH), fwd), pl.BlockSpec((q, SSD_WIDTH), bwd)],
        out_shape=[jax.ShapeDtypeStruct((t, SSD_WIDTH), F32)] * 2,
        scratch_shapes=[pltpu.VMEM((SSD_STATE, SSD_WIDTH), F32)] * 2,
        compiler_params=_params("arbitrary"),
        name="ssd_scan",
    )(xconv, dt, xconv, dt, a_row, tril, triu, expand_f, expand_b)


def _na_kernel(q_ref, k_ref, v_ref, bias_ref, o_ref, *, n_ctx, n_rows):
    i = pl.program_id(0)
    is_ctx = i == 0
    kc = k_ref[0:n_ctx, :]
    vc = v_ref[0:n_ctx, :]
    lane = lax.broadcasted_iota(jnp.int32, (GRID_W, NA_WIDTH), 1)
    mine = [(lane >= h * HEAD_DIM) & (lane < (h + 1) * HEAD_DIM) for h in range(NA_HEADS)]
    win = NA_WIN_ROWS * GRID_W

    def lane_tiles(a):
        return [a[:, c * LANES:(c + 1) * LANES] for c in range(a.shape[-1] // LANES)]

    def grid_row(j):
        r = jnp.maximum((i - 1) * NA_ROWS_PER_STEP + j, 0)
        r_start = jnp.clip(r - NA_WIN_ROWS // 2, 0, n_rows - NA_WIN_ROWS)
        variant = jnp.where(is_ctx, NA_WIN_ROWS, r_start - r + NA_WIN_ROWS - 1)
        start = pl.multiple_of(n_ctx + r_start * GRID_W, GRID_W)
        kw = k_ref[pl.ds(start, win), :]
        vw = v_ref[pl.ds(start, win), :]
        qj = q_ref[j * GRID_W:(j + 1) * GRID_W, :]
        qm = jnp.concatenate([jnp.where(mine[h], qj, jnp.zeros_like(qj)) for h in range(NA_HEADS)], axis=0)
        s_w = _dot_nt(qm, kw) + bias_ref[variant]
        s_c = _dot_nt(qm, kc)
        yield
        m = functools.reduce(jnp.maximum, lane_tiles(s_w) + lane_tiles(s_c))
        m = jnp.broadcast_to(jnp.max(m, axis=-1, keepdims=True), m.shape)
        p_w = jnp.exp(s_w - jnp.concatenate([m] * (s_w.shape[-1] // LANES), axis=-1))
        p_c = jnp.exp(s_c - jnp.concatenate([m] * (s_c.shape[-1] // LANES), axis=-1))
        l = jnp.sum(functools.reduce(jnp.add, lane_tiles(p_w) + lane_tiles(p_c)), axis=-1, keepdims=True)
        y = (_dot(p_w.astype(BF16), vw) + _dot(p_c.astype(BF16), vc)) * (1.0 / l)
        out = y[(NA_HEADS - 1) * GRID_W:]
        for h in range(NA_HEADS - 2, -1, -1):
            out = jnp.where(mine[h], y[h * GRID_W:(h + 1) * GRID_W], out)
        o_ref[j * GRID_W:(j + 1) * GRID_W, :] = out.astype(o_ref.dtype)

    _run_interleaved(*[grid_row(j) for j in range(NA_ROWS_PER_STEP)])


def _na_bias_table(rpb):
    col = jnp.arange(GRID_W)
    c_start = jnp.clip(col - NA_WIN_COLS // 2, 0, GRID_W - NA_WIN_COLS)
    in_win = (col[None, :] >= c_start[:, None]) & (col[None, :] < c_start[:, None] + NA_WIN_COLS)
    dc = jnp.clip(col[None, :] - col[:, None] + NA_WIN_COLS - 1, 0, 2 * NA_WIN_COLS - 2)
    n_dc = 2 * NA_WIN_COLS - 1
    n_dr = 2 * NA_WIN_ROWS - 1
    onehot = (dc[None, :, :] == jnp.arange(n_dc)[:, None, None]).astype(F32).reshape(n_dc, GRID_W * GRID_W)
    t2 = jnp.dot(rpb.reshape(NA_HEADS * n_dr, n_dc).astype(F32), onehot, precision=lax.Precision.HIGHEST)
    t2 = jnp.where(in_win[None, None], t2.reshape(NA_HEADS, n_dr, GRID_W, GRID_W), -jnp.inf)
    tab = jnp.stack([t2[:, v:v + NA_WIN_ROWS] for v in range(NA_WIN_ROWS)])
    tab = tab.transpose(0, 1, 3, 2, 4).reshape(NA_WIN_ROWS, NA_HEADS * GRID_W, NA_WIN_ROWS * GRID_W)
    masked = jnp.full((1,) + tab.shape[1:], -jnp.inf, F32)
    return jnp.concatenate([tab, masked], axis=0)


def _neighbourhood_attention(na, bias_tab, n_ctx):
    t = na.shape[0]
    n_rows = (t - n_ctx) // GRID_W
    step_rows = NA_ROWS_PER_STEP * GRID_W
    n = t // step_rows
    return pl.pallas_call(
        functools.partial(_na_kernel, n_ctx=n_ctx, n_rows=n_rows),
        grid=(n,),
        in_specs=[
            pl.BlockSpec((step_rows, NA_WIDTH), lambda i: (i, 0)),
            _resident((t, NA_WIDTH), lambda i: (0, 1)),
            _resident((t, NA_WIDTH), lambda i: (0, 2)),
            _resident(bias_tab.shape, lambda i: (0, 0, 0)),
        ],
        out_specs=pl.BlockSpec((step_rows, NA_WIDTH), lambda i: (i, 0)),
        out_shape=jax.ShapeDtypeStruct((t, NA_WIDTH), BF16),
        compiler_params=_params("parallel"),
        name="neighbourhood_attention",
    )(na, na, na, bias_tab)


def _gqa_kernel(qt_ref, k_ref, vt_ref, o_ref, acc_ref, s0_ref, s1_ref, p0_ref, p1_ref, *, n_head, n_sub):
    tq = qt_ref.shape[1]
    qt = qt_ref[...]
    row = lax.broadcasted_iota(jnp.int32, qt.shape, 0)
    top = row < HEAD_DIM
    zero = jnp.zeros_like(qt)
    qt2 = jnp.concatenate([jnp.where(top, qt, zero), jnp.where(top, zero, qt)], axis=1)
    acc_ref[...] = jnp.zeros_like(acc_ref)
    s_ref = (s0_ref, s1_ref)
    p_ref = (p0_ref, p1_ref)

    def keys(j, size):
        return pl.ds(pl.multiple_of(n_head + j * size, LANES), size)

    def score(key_rows, slot):
        size = key_rows.size
        s = _dot(k_ref[key_rows, :], qt2)
        s_ref[slot][0:size, :] = s
        return jnp.max(s.reshape(size // SUBLANES, SUBLANES, 2 * tq), axis=0)

    def softmax(size, slot, m_old, part_max):
        m_new = jnp.maximum(m_old, jnp.max(part_max, axis=0, keepdims=True))
        s = s_ref[slot][0:size, :].reshape(size // SUBLANES, SUBLANES, 2 * tq)
        p_ref[slot][0:size, :] = jnp.exp2(s - m_new[None]).reshape(size, 2 * tq).astype(BF16)
        return m_new, jnp.exp2(m_old - m_new)

    def accumulate(key_rows, slot, alpha):
        size = key_rows.size
        acc = acc_ref[...].reshape(GQA_V_ROWS // SUBLANES, SUBLANES, 2 * tq) * alpha[None]
        acc_ref[...] = acc.reshape(GQA_V_ROWS, 2 * tq) + _dot(vt_ref[:, key_rows], p_ref[slot][0:size, :])

    m = jnp.full((SUBLANES, 2 * tq), -jnp.inf, F32)
    if n_head:
        head = pl.ds(0, n_head)
        m, alpha = softmax(n_head, 0, m, score(head, 0))
        accumulate(head, 0, alpha)

    sub = GQA_K_SUB
    if n_sub:
        assert n_sub >= 3
        part0 = score(keys(0, sub), 0)
        part1 = score(keys(1, sub), 1)
        m, alpha = softmax(sub, 0, m, part0)

        def step(t, slot, carry):
            m, alpha, part = carry
            part_next = score(keys(t, sub), slot)
            accumulate(keys(t - 2, sub), slot, alpha)
            return softmax(sub, 1 - slot, m, part) + (part_next,)

        def trip(n, carry):
            for u in range(GQA_STEPS_PER_TRIP):
                carry = step(2 + n * GQA_STEPS_PER_TRIP + u, u % 2, carry)
            return carry

        n_trips = (n_sub - 2) // GQA_STEPS_PER_TRIP
        carry = lax.fori_loop(0, n_trips, trip, (m, alpha, part1))
        for t in range(2 + n_trips * GQA_STEPS_PER_TRIP, n_sub):
            carry = step(t, t % 2, carry)
        m, alpha, part = carry
        accumulate(keys(n_sub - 2, sub), (n_sub - 2) % 2, alpha)
        m, alpha = softmax(sub, (n_sub - 1) % 2, m, part)
        accumulate(keys(n_sub - 1, sub), (n_sub - 1) % 2, alpha)

    acc = acc_ref[...]
    denom = acc[HEAD_DIM:HEAD_DIM + SUBLANES]
    o_t = (acc[:HEAD_DIM].reshape(HEAD_DIM // SUBLANES, SUBLANES, 2 * tq) / denom[None]).reshape(HEAD_DIM, 2 * tq)
    o_ref[...] = jnp.concatenate([o_t[:, :tq], o_t[:, tq:]], axis=0).T.astype(o_ref.dtype)


def _gqa_bounded_kernel(qt_ref, k_ref, vt_ref, o_ref, acc_ref, *p_ref, n_sub):
    tq = qt_ref.shape[1]
    qt = qt_ref[...]
    row = lax.broadcasted_iota(jnp.int32, qt.shape, 0)
    top = row < HEAD_DIM
    zero = jnp.zeros_like(qt)
    qt2 = jnp.concatenate([jnp.where(top, qt, zero), jnp.where(top, zero, qt)], axis=1)
    acc_ref[...] = jnp.zeros_like(acc_ref)
    sub = GQA_K_SUB
    n_slots = len(p_ref)
    lag = n_slots - 1
    steps = GQA_BOUNDED_STEPS_PER_TRIP
    assert steps % n_slots == 0 and n_sub > lag

    def keys(j):
        return pl.ds(pl.multiple_of(j * sub, sub), sub)

    def probs(j, slot):
        p_ref[slot][...] = jnp.exp2(_dot(k_ref[keys(j), :], qt2)).astype(BF16)

    def accumulate(j, slot):
        acc_ref[...] += _dot(vt_ref[:, keys(j)], p_ref[slot][...])

    for t in range(lag):
        probs(t, t % n_slots)

    def trip(n, carry):
        for u in range(steps):
            t = lag + n * steps + u
            probs(t, (lag + u) % n_slots)
            accumulate(t - lag, u % n_slots)
        return carry

    n_trips = (n_sub - lag) // steps
    lax.fori_loop(0, n_trips, trip, 0)
    for t in range(lag + n_trips * steps, n_sub):
        probs(t, t % n_slots)
        accumulate(t - lag, (t - lag) % n_slots)
    for t in range(n_sub - lag, n_sub):
        accumulate(t, t % n_slots)

    acc = acc_ref[...]
    denom = acc[HEAD_DIM:HEAD_DIM + SUBLANES]
    o_t = (acc[:HEAD_DIM].reshape(HEAD_DIM // SUBLANES, SUBLANES, 2 * tq) / denom[None]).reshape(HEAD_DIM, 2 * tq)
    o_ref[...] = jnp.concatenate([o_t[:, :tq], o_t[:, tq:]], axis=0).T.astype(o_ref.dtype)


def _gqa_call(gqt, gk, gvt, n_head, n_sub, tq, name, bounded=False):
    nq = gqt.shape[1]
    n_keys = n_head + n_sub * GQA_K_SUB
    buf_rows = max(n_head, GQA_K_SUB if n_sub else 0)
    assert n_head % LANES == 0 and nq % tq == 0
    if bounded:
        assert n_head == 0
        body = functools.partial(_gqa_bounded_kernel, n_sub=n_sub)
        buffers = [pltpu.VMEM((buf_rows, 2 * tq), BF16)] * (GQA_PV_LAG + 1)
    else:
        body = functools.partial(_gqa_kernel, n_head=n_head, n_sub=n_sub)
        buffers = [pltpu.VMEM((buf_rows, 2 * tq), F32)] * 2 + [pltpu.VMEM((buf_rows, 2 * tq), BF16)] * 2
    return pl.pallas_call(
        body,
        grid=(GQA_KV_HEADS, nq // tq),
        in_specs=[
            pl.BlockSpec((LANES, tq), lambda g, i: (g, i)),
            pl.BlockSpec((n_keys, LANES), lambda g, i: (0, g)),
            pl.BlockSpec((GQA_V_ROWS, n_keys), lambda g, i: (g, 0)),
        ],
        out_specs=pl.BlockSpec((tq, LANES), lambda g, i: (i, g)),
        out_shape=jax.ShapeDtypeStruct((nq, GQA_WIDTH), BF16),
        scratch_shapes=[pltpu.VMEM((GQA_V_ROWS, 2 * tq), F32)] + buffers,
        compiler_params=_params("arbitrary", "arbitrary"),
        name=name,
    )(gqt, gk, gvt)


def _gqa_attention(gqt, gk, gvt, n_ctx, score_bound):
    n_head = n_ctx % GQA_K_SUB
    n_sub = (gk.shape[0] - n_head) // GQA_K_SUB
    assert (gk.shape[0] - n_head) % GQA_K_SUB == 0
    y_ctx = _gqa_call(gqt[:, :n_ctx], gk, gvt, n_ctx, 0, n_ctx, "gqa_attention_ctx")
    q_lat = gqt[:, n_ctx:]

    def with_running_max(q, k, vt):
        return _gqa_call(q, k, vt, n_head, n_sub, GQA_Q_TILE, "gqa_attention")

    def without_running_max(q, k, vt):
        return _gqa_call(q, k, vt, 0, n_sub, GQA_Q_TILE, "gqa_attention_bounded", bounded=True)

    if n_head == 0:
        y_lat = lax.cond(score_bound <= GQA_BOUNDED_LOG2, without_running_max, with_running_max, q_lat, gk, gvt)
    else:
        y_lat = with_running_max(q_lat, gk, gvt)
    return jnp.concatenate([y_ctx, y_lat], axis=0)


def _out_ffn_kernel(ya_ref, yf_ref, yb_ref, xs_ref, z_ref, yg_ref, xc_ref, xl_ref, mod_ref,
                    dskip_ref, snw_ref, wo_ref, fnw_ref, wg_ref, wu_ref, wd_ref, final_ref,
                    o_ref, *, final, pick_ctx):
    x = jnp.where(pl.program_id(0) == 0, xc_ref[...], xl_ref[...]) if pick_ctx else xl_ref[...]
    y = yf_ref[...] + yb_ref[...] + dskip_ref[...] * xs_ref[...]
    y = y * _silu(z_ref[...])
    ms = jnp.mean(y * y, axis=-1, keepdims=True)
    y = y * lax.rsqrt(ms + EPS) * snw_ref[...]
    mix = jnp.concatenate([ya_ref[...], y.astype(BF16), yg_ref[...]], axis=-1)
    g_m = mod_ref[:, 2 * D_MODEL:3 * D_MODEL]
    sh_f = mod_ref[:, 3 * D_MODEL:4 * D_MODEL]
    sc_f = mod_ref[:, 4 * D_MODEL:5 * D_MODEL]
    g_f = mod_ref[:, 5 * D_MODEL:6 * D_MODEL]
    x1 = x + g_m * _dot(mix, wo_ref[...])
    ms1 = jnp.mean(x1 * x1, axis=-1, keepdims=True)
    hf = (x1 * lax.rsqrt(ms1 + EPS) * fnw_ref[...] * (1.0 + sc_f) + sh_f).astype(BF16)
    act = (_silu(_dot(hf, wg_ref[...])) * _dot(hf, wu_ref[...])).astype(BF16)
    x2 = x1 + g_f * _dot(act, wd_ref[...])
    if final:
        ms2 = jnp.mean(x2 * x2, axis=-1, keepdims=True)
        x2 = x2 * lax.rsqrt(ms2 + EPS) * final_ref[...]
    o_ref[...] = x2


def _out_ffn(ya, yf, yb, xconv, z, yg, x_parts, mod4, layer, d_full, ssd_nw, w_out, ffn_nw,
             w_gate, w_up, w_down, final_nw, final):
    t = ya.shape[0]
    skip = 1 if final else 0
    n = t // ROW_TILE - skip
    row = lambda i: (i + skip, 0)
    const = lambda i: (0, 0)
    x_arrays, x_specs, pick_ctx = _residual_operands(x_parts, skip)
    return pl.pallas_call(
        functools.partial(_out_ffn_kernel, final=final, pick_ctx=pick_ctx),
        grid=(n,),
        in_specs=[
            pl.BlockSpec((ROW_TILE, NA_WIDTH), row),
            pl.BlockSpec((ROW_TILE, SSD_WIDTH), row),
            pl.BlockSpec((ROW_TILE, SSD_WIDTH), row),
            pl.BlockSpec((ROW_TILE, SSD_WIDTH), row),
            pl.BlockSpec((ROW_TILE, SSD_WIDTH), row),
            pl.BlockSpec((ROW_TILE, GQA_WIDTH), row),
        ] + x_specs + [
            pl.BlockSpec((None, None, 1, 6 * D_MODEL), lambda i: (layer, jnp.minimum(i + skip, 1), 0, 0)),
            pl.BlockSpec((1, SSD_WIDTH), const),
            pl.BlockSpec((1, SSD_WIDTH), const),
            _resident((D_MODEL, D_MODEL), const),
            pl.BlockSpec((1, D_MODEL), const),
            _resident((D_MODEL, FFN_HIDDEN), const),
            _resident((D_MODEL, FFN_HIDDEN), const),
            _resident((FFN_HIDDEN, D_MODEL), const),
            pl.BlockSpec((1, D_MODEL), const),
        ],
        out_specs=pl.BlockSpec((ROW_TILE, D_MODEL), lambda i: (i, 0)),
        out_shape=jax.ShapeDtypeStruct((n * ROW_TILE, D_MODEL), F32),
        compiler_params=_params("parallel"),
        name="out_ffn",
    )(ya, yf, yb, xconv, z, yg, *x_arrays, mod4, d_full, ssd_nw, w_out, ffn_nw, w_gate, w_up, w_down, final_nw)


def _rearranged_w_in(w):
    na_in = 3 * NA_WIDTH
    o_z = na_in
    o_xbc = o_z + SSD_WIDTH
    o_dt = o_xbc + SSD_CONV_CH
    o_gq = o_dt + 2 * SSD_HEADS
    o_gk = o_gq + GQA_WIDTH
    o_gv = o_gk + GQA_KV_HEADS * HEAD_DIM
    na = jnp.concatenate([w[:, :NA_WIDTH] * ATTN_SCALE, w[:, NA_WIDTH:na_in]], axis=1)
    dt = jnp.pad(w[:, o_dt:o_gq], ((0, 0), (0, LANES - 2 * SSD_HEADS)))

    k_heads = [w[:, o_gk + h * HEAD_DIM:o_gk + (h + 1) * HEAD_DIM] for h in range(GQA_KV_HEADS)]
    k_twice = jnp.concatenate([p for h in k_heads for p in (h, h)], axis=1)
    return jnp.concatenate(
        [na, w[:, o_z:o_xbc], w[:, o_xbc:o_dt], dt, w[:, o_gq:o_gk], k_twice, w[:, o_gv:]], axis=1).astype(BF16)


def _rope_tables(n_ctx, n_lat):
    freqs = ROPE_THETA ** (-jnp.arange(ROPE_PAIRS, dtype=F32) / ROPE_PAIRS)
    n_rows = n_lat // GRID_W
    half = 2 * ROPE_PAIRS
    sign = jnp.where(jnp.arange(half) < ROPE_PAIRS, -1.0, 1.0).astype(F32)

    def tables(n_pos):
        a = jnp.arange(n_pos, dtype=F32)[:, None] * freqs[None, :]
        a = jnp.concatenate([a, a], axis=-1)
        return jnp.cos(a), jnp.sin(a) * sign[None, :]

    def per_token(by_row, by_col):
        lat = jnp.concatenate([jnp.broadcast_to(by_row[:, None, :], (n_rows, GRID_W, half)),
                               jnp.broadcast_to(by_col[None, :, :], (n_rows, GRID_W, half))], axis=-1)
        return lat.reshape(n_lat, HEAD_DIM)

    cos_r, sin_r = tables(n_rows)
    cos_c, sin_c = tables(GRID_W)
    cos = jnp.concatenate([jnp.ones((n_ctx, HEAD_DIM), F32), per_token(cos_r, cos_c)], axis=0)
    sin = jnp.concatenate([jnp.zeros((n_ctx, HEAD_DIM), F32), per_token(sin_r, sin_c)], axis=0)
    return jnp.tile(cos, (1, LANES // HEAD_DIM)), jnp.tile(sin, (1, LANES // HEAD_DIM))


def kernel(x, c, ctx, c_ctx, mod_w, mod_b, norm_attn_w, norm_ffn_w, w_in, na_rpb, ssd_conv_w, ssd_conv_b,
           ssd_dt_bias, ssd_a_log, ssd_d, ssd_norm_w, q_norm_w, k_norm_w, w_out, ffn_w_gate, ffn_w_up,
           ffn_w_down, final_norm_w):
    depth = mod_w.shape[0]
    batch, n_lat, _ = x.shape
    n_ctx = ctx.shape[1]
    assert batch == 1 and n_ctx == ROW_TILE and n_lat % (NA_ROWS_PER_STEP * GRID_W) == 0
    assert n_lat % GQA_Q_TILE == 0 and n_lat // GRID_W >= NA_WIN_ROWS

    x_parts = (ctx[0], x[0])
    cc = jnp.zeros((SUBLANES, D_MODEL), F32).at[0].set(c_ctx).at[1].set(c[0])
    mod = _modulation(cc, mod_w, mod_b)
    mod4 = mod[:, :2].reshape(depth, 2, 1, 6 * D_MODEL)

    cos_t, sin_t = _rope_tables(n_ctx, n_lat)
    blk = jnp.arange(2 * GQA_WIDTH) // HEAD_DIM
    ones_bd = (blk[:, None] == blk[None, :]).astype(BF16)
    head_cols = jnp.arange(SSD_WIDTH) // HEAD_DIM

    for i in range(depth):
        final = i == depth - 1
        w_cat = _rearranged_w_in(w_in[i])
        dt_bias_pad = jnp.pad(ssd_dt_bias[i].reshape(1, -1), ((0, 0), (0, LANES - 2 * SSD_HEADS)))
        qk_w = jnp.concatenate([jnp.tile(q_norm_w[i] * (ATTN_SCALE * LOG2E), GQA_Q_HEADS),
                                jnp.tile(k_norm_w[i], 2 * GQA_KV_HEADS)]).reshape(1, -1)
        na, z, xbc, dt, gq, gk, gv = _inproj(x_parts, n_ctx + n_lat, mod4, i, norm_attn_w[i].reshape(1, -1),
                                             w_cat, dt_bias_pad, qk_w, cos_t, sin_t, ones_bd)
        conv_w_pad = jnp.pad(ssd_conv_w[i], ((0, SUBLANES - SSD_CONV), (0, 0)))
        xconv = _ssd_conv(xbc, conv_w_pad, ssd_conv_b[i].reshape(1, -1))
        a_row = jnp.pad(-jnp.exp(ssd_a_log[i].astype(F32)).reshape(1, -1), ((0, 0), (0, LANES - 2 * SSD_HEADS)))
        yf, yb = _ssd_scan(xconv, dt, a_row, n_ctx // SSD_CHUNK)
        ya = _neighbourhood_attention(na, _na_bias_table(na_rpb[i]), n_ctx)
        ones_rows = jnp.ones((GQA_V_ROWS - HEAD_DIM, gv.shape[0]), BF16)
        gvt = jnp.concatenate([part for g in range(GQA_KV_HEADS)
                               for part in (gv[:, g * HEAD_DIM:(g + 1) * HEAD_DIM].T, ones_rows)], axis=0)
        score_bound = (1.02 * HEAD_DIM * ATTN_SCALE * LOG2E) * jnp.max(jnp.abs(q_norm_w[i])) * jnp.max(
            jnp.abs(k_norm_w[i]))
        yg = _gqa_attention(gq.T, gk, gvt, n_ctx, score_bound)
        d_full = ssd_d[i].astype(F32)[head_cols].reshape(1, -1)
        x_parts = _out_ffn(ya, yf, yb, xconv, z, yg, x_parts, mod4, i, d_full, ssd_norm_w[i].reshape(1, -1),
                           w_out[i].astype(BF16), norm_ffn_w[i].reshape(1, -1), ffn_w_gate[i].astype(BF16),
                           ffn_w_up[i].astype(BF16), ffn_w_down[i].astype(BF16),
                           final_norm_w.reshape(1, -1), final)
    return x_parts[None]
```

```python
import functools
import math

import jax
import jax.numpy as jnp
from jax import lax
from jax.experimental import pallas as pl
from jax.experimental.pallas import tpu as pltpu

F32 = jnp.float32
BF16 = jnp.bfloat16

D_MODEL = 1024
GRID_W = 64
HEAD_DIM = 64
NA_WIDTH = 256
NA_HEADS = 4
NA_WIN_ROWS = 8
NA_WIN_COLS = 16
SSD_WIDTH = 512
SSD_HEADS = 8
SSD_GROUPS = 2
SSD_STATE = 128
SSD_CONV = 5
SSD_CHUNK = 128
SSD_CONV_CH = SSD_WIDTH + 2 * SSD_GROUPS * SSD_STATE
GQA_WIDTH = 256
GQA_Q_HEADS = 4
GQA_KV_HEADS = 2
ROPE_THETA = 10000.0
ROPE_PAIRS = HEAD_DIM // 4
FFN_HIDDEN = 2816
EPS = 1e-6
ATTN_SCALE = HEAD_DIM ** -0.5
LOG2E = math.log2(math.e)

LANES = 128
SUBLANES = 8
VMEM_LIMIT_BYTES = 56 * 1024 * 1024

ROW_TILE = 256
SSD_CHUNKS_PER_STEP = 2
NA_ROWS_PER_STEP = 4
GQA_Q_TILE = 256
GQA_V_ROWS = HEAD_DIM + 16
GQA_K_SUB = 256
GQA_BOUNDED_LOG2 = 60.0
GQA_PV_LAG = 2
GQA_BOUNDED_STEPS_PER_TRIP = 9
GQA_STEPS_PER_TRIP = 8

C_NA = 0
C_Z = C_NA + 3 * NA_WIDTH
C_XBC = C_Z + SSD_WIDTH
C_DT = C_XBC + SSD_CONV_CH
C_GQ = C_DT + LANES
C_GK = C_GQ + GQA_WIDTH
C_GV = C_GK + 2 * LANES
C_END = C_GV + LANES


def _silu(v):
    return v * (1.0 / (1.0 + jnp.exp(-v)))


def _softplus(v):
    return jnp.maximum(v, 0.0) + jnp.log(1.0 + jnp.exp(-jnp.abs(v)))


def _split3(v):
    hi = v.astype(BF16)
    r1 = v - hi.astype(F32)
    mid = r1.astype(BF16)
    lo = (r1 - mid.astype(F32)).astype(BF16)
    return hi, mid, lo


def _dot(a, b):
    return jnp.dot(a, b, preferred_element_type=F32)


def _dot_nt(a, b):
    return lax.dot_general(a, b, (((1,), (1,)), ((), ())), preferred_element_type=F32)


def _dot_tn(a, b):
    return lax.dot_general(a, b, (((0,), (0,)), ((), ())), preferred_element_type=F32)


def _exact_dot(v, sel):
    hi, mid, lo = _split3(v)
    return _dot(hi, sel) + _dot(mid, sel) + _dot(lo, sel)


def _spread_dot(v, sel):
    hi = v.astype(BF16)
    lo = (v - hi.astype(F32)).astype(BF16)
    return _dot(hi, sel) + _dot(lo, sel)


def _exact_dot_lhs(sel, v):
    hi, mid, lo = _split3(v)
    return _dot(sel, hi) + _dot(sel, mid) + _dot(sel, lo)


def _params(*sem):
    return pltpu.CompilerParams(dimension_semantics=sem, vmem_limit_bytes=VMEM_LIMIT_BYTES)


def _resident(shape, index_map):
    return pl.BlockSpec(shape, index_map, pipeline_mode=pl.Buffered(1))


def _mod_kernel(cc_ref, w_ref, b_ref, o_ref):
    a = _silu(cc_ref[...])
    o_ref[0] = jnp.dot(a, w_ref[0], preferred_element_type=F32) + b_ref[0]


def _modulation(cc, mod_w, mod_b):
    depth = mod_w.shape[0]
    ncol = mod_w.shape[2] // D_MODEL
    return pl.pallas_call(
        _mod_kernel,
        grid=(depth, ncol),
        in_specs=[
            pl.BlockSpec((SUBLANES, D_MODEL), lambda l, j: (0, 0)),
            pl.BlockSpec((1, D_MODEL, D_MODEL), lambda l, j: (l, 0, j)),
            pl.BlockSpec((1, 1, D_MODEL), lambda l, j: (l, 0, j)),
        ],
        out_specs=pl.BlockSpec((1, SUBLANES, D_MODEL), lambda l, j: (l, 0, j)),
        out_shape=jax.ShapeDtypeStruct((depth, SUBLANES, ncol * D_MODEL), F32),
        compiler_params=_params("arbitrary", "arbitrary"),
        name="modulation",
    )(cc, mod_w, mod_b.reshape(depth, 1, -1))


def _residual_operands(x_parts, skip):
    ctx_spec = pl.BlockSpec((ROW_TILE, D_MODEL), lambda i: (0, 0))
    if isinstance(x_parts, tuple):
        lat_spec = pl.BlockSpec((ROW_TILE, D_MODEL), lambda i: (jnp.maximum(i + skip - 1, 0), 0))
        return list(x_parts), [ctx_spec, lat_spec], skip == 0
    row_spec = pl.BlockSpec((ROW_TILE, D_MODEL), lambda i: (i + skip, 0))
    return [x_parts, x_parts], [ctx_spec, row_spec], False


def _inproj_kernel(xc_ref, xl_ref, prev_ref, next_ref, mod_ref, nw_ref, w_ref, dtb_ref, qkw_ref, cos_ref, sin_ref,
                   ones_ref, cw_ref, cb_ref, na_ref, z_ref, xconv_ref, dt_ref, gq_ref, gk_ref, gv_ref, *, pick_ctx):
    i = pl.program_id(0)
    n = pl.num_programs(0)
    rows = xl_ref.shape[0]
    x = jnp.where(i == 0, xc_ref[...], xl_ref[...]) if pick_ctx else xl_ref[...]
    x = jnp.concatenate([prev_ref[...], x, next_ref[...]], axis=0)
    ms = jnp.mean(x * x, axis=-1, keepdims=True)
    xn = x * lax.rsqrt(ms + EPS) * nw_ref[...]
    sh = mod_ref[:, 0:D_MODEL]
    sc = mod_ref[:, D_MODEL:2 * D_MODEL]
    h = (xn * (1.0 + sc) + sh).astype(BF16)
    xbc_ext = _dot(h, w_ref[:, C_XBC:C_DT])
    h_tile = h[SUBLANES:SUBLANES + rows]
    u = jnp.concatenate([_dot(h_tile, w_ref[:, C_NA:C_XBC]), xbc_ext[SUBLANES:SUBLANES + rows],
                         _dot(h_tile, w_ref[:, C_DT:C_END])], axis=-1)
    na_ref[...] = u[:, C_NA:C_Z].astype(BF16)
    z_ref[...] = u[:, C_Z:C_XBC]
    has_prev = i >= 2
    has_next = jnp.logical_and(i >= 1, i < n - 1)
    xbc = u[:, C_XBC:C_DT]
    ext = jnp.concatenate([jnp.where(has_prev, xbc_ext[0:SUBLANES], 0.0), xbc,
                           jnp.where(has_next, xbc_ext[SUBLANES + rows:], 0.0)], axis=0)
    total = rows + 2 * SUBLANES
    half = SSD_CONV // 2
    acc = cb_ref[...] + cw_ref[half:half + 1, :] * xbc
    for j in range(SSD_CONV):
        if j != half:
            shifted = pltpu.roll(ext, (half - j) % total, 0)
            acc = acc + cw_ref[j:j + 1, :] * shifted[SUBLANES:SUBLANES + rows]
    xconv_ref[...] = _silu(acc)
    dt_ref[...] = _softplus(u[:, C_DT:C_GQ] + dtb_ref[...])
    gv_ref[...] = u[:, C_GV:C_END].astype(BF16)
    g = u[:, C_GQ:C_GV]
    gsq = g * g
    hi = gsq.astype(BF16)
    lo = (gsq - hi.astype(F32)).astype(BF16)
    ss = _dot(hi, ones_ref[...]) + _dot(lo, ones_ref[...])
    gn = g * lax.rsqrt(ss * (1.0 / HEAD_DIM) + EPS) * qkw_ref[...]
    width = gn.shape[-1]
    lane = lax.broadcasted_iota(jnp.int32, gn.shape, 1)
    first = (lane % (2 * ROPE_PAIRS)) < ROPE_PAIRS
    partner = jnp.where(first, pltpu.roll(gn, width - ROPE_PAIRS, 1), pltpu.roll(gn, ROPE_PAIRS, 1))
    cos = jnp.concatenate([cos_ref[...]] * (width // LANES), axis=-1)
    sin = jnp.concatenate([sin_ref[...]] * (width // LANES), axis=-1)
    gr = gn * cos + partner * sin
    gq_ref[...] = gr[:, :GQA_WIDTH].astype(BF16)
    gk_ref[...] = gr[:, GQA_WIDTH:].astype(BF16)


def _inproj(x_parts, t, mod4, layer, norm_w, w_cat, dt_bias_pad, qk_w, cos_t, sin_t, ones_bd, conv_w_pad, conv_b):
    n = t // ROW_TILE
    row = lambda i: (i, 0)
    const = lambda i: (0, 0)
    outs = [
        (3 * NA_WIDTH, BF16), (SSD_WIDTH, F32), (SSD_CONV_CH, F32), (LANES, F32),
        (GQA_WIDTH, BF16), (2 * LANES, BF16), (LANES, BF16),
    ]
    x_arrays, x_specs, pick_ctx = _residual_operands(x_parts, 0)
    halo_src = x_arrays[1]
    per = ROW_TILE // SUBLANES
    first = (lambda i: (i - 1) * per) if isinstance(x_parts, tuple) else (lambda i: i * per)
    last_blk = halo_src.shape[0] // SUBLANES - 1
    halo_specs = [
        pl.BlockSpec((SUBLANES, D_MODEL), lambda i: (jnp.clip(first(i) - 1, 0, last_blk), 0)),
        pl.BlockSpec((SUBLANES, D_MODEL), lambda i: (jnp.clip(first(i) + per, 0, last_blk), 0)),
    ]
    return pl.pallas_call(
        functools.partial(_inproj_kernel, pick_ctx=pick_ctx),
        grid=(n,),
        in_specs=x_specs + halo_specs + [
            pl.BlockSpec((None, None, 1, 6 * D_MODEL), lambda i: (layer, jnp.minimum(i, 1), 0, 0)),
            pl.BlockSpec((1, D_MODEL), const),
            _resident((D_MODEL, C_END), const),
            pl.BlockSpec((1, LANES), const),
            pl.BlockSpec((1, 2 * GQA_WIDTH), const),
            pl.BlockSpec((ROW_TILE, LANES), row),
            pl.BlockSpec((ROW_TILE, LANES), row),
            _resident((2 * GQA_WIDTH, 2 * GQA_WIDTH), const),
            pl.BlockSpec((SUBLANES, SSD_CONV_CH), const),
            pl.BlockSpec((1, SSD_CONV_CH), const),
        ],
        out_specs=[pl.BlockSpec((ROW_TILE, w), row) for w, _ in outs],
        out_shape=[jax.ShapeDtypeStruct((t, w), d) for w, d in outs],
        compiler_params=_params("parallel"),
        name="inproj",
    )(*x_arrays, halo_src, halo_src, mod4, norm_w, w_cat, dt_bias_pad, qk_w, cos_t, sin_t, ones_bd,
      conv_w_pad, conv_b)


def _ssd_direction(xbc, dt, a_row, tri, expand, s_ref, lane0, backward):
    q = SSD_CHUNK
    x = xbc[:, :SSD_WIDTH]
    nb = SSD_GROUPS * SSD_STATE
    bmat = xbc[:, SSD_WIDTH:SSD_WIDTH + nb].astype(BF16)
    cmat = xbc[:, SSD_WIDTH + nb:].astype(BF16)
    cum = _exact_dot_lhs(tri, dt * a_row)
    yield
    last = 0 if backward else q - 1
    cum_t = cum.T

    def spread(mat_t):
        rows = [jnp.broadcast_to(mat_t[lane0 + h:lane0 + h + 1, :], (HEAD_DIM, q)) for h in range(SSD_HEADS)]
        return jnp.concatenate(rows, axis=0).T

    dt_full = _spread_dot(dt, expand)
    cum_full = spread(cum_t)
    yield
    ea_full = jnp.exp(cum_full)
    te_full = jnp.exp(cum_full[last:last + 1, :] - cum_full)
    xr = x * dt_full
    xr_b = xr.astype(BF16)
    xt_b = (xr * te_full).astype(BF16)
    ti =lax.broadcasted_iota(jnp.int32, (q, q), 0)
    si = lax.broadcasted_iota(jnp.int32, (q, q), 1)
    keep = (si >= ti) if backward else (si <= ti)
    lane = lax.broadcasted_iota(jnp.int32, (q, LANES), 1)
    heads_per_group = SSD_HEADS // SSD_GROUPS
    width_g = heads_per_group * HEAD_DIM
    pieces = []
    decay_total = ea_full[last:last + 1, :]
    for g in range(SSD_GROUPS):
        bg = bmat[:, g * SSD_STATE:(g + 1) * SSD_STATE]
        cg = cmat[:, g * SSD_STATE:(g + 1) * SSD_STATE]
        gmat = _dot_nt(cg, bg)
        s_old = s_ref[:, g * width_g:(g + 1) * width_g]
        y_off = _dot(cg, s_old.astype(BF16)) * ea_full[:, g * width_g:(g + 1) * width_g]
        s_new = _dot_tn(bg, xt_b[:, g * width_g:(g + 1) * width_g])
        s_ref[:, g * width_g:(g + 1) * width_g] = decay_total[:, g * width_g:(g + 1) * width_g] * s_old + s_new
        yield
        for pair in range(heads_per_group // 2):
            col = g * width_g + pair * LANES
            xr_pair = xr_b[:, col:col + LANES]
            ys = []
            for r in range(2):
                hl = lane0 + g * heads_per_group + 2 * pair + r
                seg = jnp.broadcast_to(cum[:, hl:hl + 1], (q, q)) - jnp.broadcast_to(cum_t[hl:hl + 1, :], (q, q))
                dec = jnp.exp(jnp.where(keep, seg, -jnp.inf))
                ys.append(_dot((gmat * dec).astype(BF16), xr_pair))
            y_diag = jnp.where(lane < HEAD_DIM, ys[0], ys[1])
            pieces.append(y_diag + y_off[:, pair * LANES:(pair + 1) * LANES])
            yield
    return jnp.concatenate(pieces, axis=-1)


def _run_interleaved(*stage_generators):
    results = [None] * len(stage_generators)
    live = list(range(len(stage_generators)))
    while live:
        for idx in list(live):
            try:
                next(stage_generators[idx])
            except StopIteration as done:
                results[idx] = done.value
                live.remove(idx)
    return results


def _ssd_kernel(xf_ref, dtf_ref, xb_ref, dtb_ref, a_ref, tril_ref, triu_ref, ef_ref, eb_ref,
                yf_ref, yb_ref, sf_ref, sb_ref):
    @pl.when(pl.program_id(0) == 0)
    def _():
        sf_ref[...] = jnp.zeros_like(sf_ref)
        sb_ref[...] = jnp.zeros_like(sb_ref)

    a_row = a_ref[...]
    q = SSD_CHUNK
    order_f = list(range(SSD_CHUNKS_PER_STEP))
    order_b = order_f[::-1]
    stages = []
    for cf, cb in zip(order_f, order_b):
        rf, rb = slice(cf * q, (cf + 1) * q), slice(cb * q, (cb + 1) * q)
        stages.append(_ssd_direction(xf_ref[rf, :], dtf_ref[rf, :], a_row, tril_ref[...], ef_ref[...],
                                     sf_ref, 0, False))
        stages.append(_ssd_direction(xb_ref[rb, :], dtb_ref[rb, :], a_row, triu_ref[...], eb_ref[...],
                                     sb_ref, SSD_HEADS, True))
    ys = _run_interleaved(*stages)
    for k, (cf, cb) in enumerate(zip(order_f, order_b)):
        yf_ref[cf * q:(cf + 1) * q, :] = ys[2 * k]
        yb_ref[cb * q:(cb + 1) * q, :] = ys[2 * k + 1]


def _ssd_scan(xconv, dt, a_row, n_ctx):
    t = xconv.shape[0]
    q = SSD_CHUNK
    rows = SSD_CHUNKS_PER_STEP * q
    assert n_ctx % rows == 0 and t % rows == 0
    n = t // rows
    n_ctx_blocks = n_ctx // rows
    r = jnp.arange(q)
    tril = (r[None, :] <= r[:, None]).astype(BF16)
    triu = (r[None, :] >= r[:, None]).astype(BF16)
    lanes = jnp.arange(LANES)[:, None]
    head_of_col = (jnp.arange(SSD_WIDTH) // HEAD_DIM)[None, :]
    expand_f = (lanes == head_of_col).astype(BF16)
    expand_b = (lanes == head_of_col + SSD_HEADS).astype(BF16)

    def fwd(i):
        return (i, 0)

    def bwd(i):
        return (jnp.where(i < n_ctx_blocks, n_ctx_blocks - 1 - i, n - 1 - (i - n_ctx_blocks)), 0)

    const = lambda i: (0, 0)
    return pl.pallas_call(
        _ssd_kernel,
        grid=(n,),
        in_specs=[
            pl.BlockSpec((rows, SSD_CONV_CH), fwd),
            pl.BlockSpec((rows, LANES), fwd),
            pl.BlockSpec((rows, SSD_CONV_CH), bwd),
            pl.BlockSpec((rows, LANES), bwd),
            pl.BlockSpec((1, LANES), const),
            pl.BlockSpec((q, q), const),
            pl.BlockSpec((q, q), const),
            pl.BlockSpec((LANES, SSD_WIDTH), const),
            pl.BlockSpec((LANES, SSD_WIDTH), const),
        ],
        out_specs=[pl.BlockSpec((rows, SSD_WIDTH), fwd), pl.BlockSpec((rows, SSD_WIDTH), bwd)],
        out_shape=[jax.ShapeDtypeStruct((t, SSD_WIDTH), F32)] * 2,
        scratch_shapes=[pltpu.VMEM((SSD_STATE, SSD_WIDTH), F32)] * 2,
        compiler_params=_params("arbitrary"),
        name="ssd_scan",
    )(xconv, dt, xconv, dt, a_row, tril, triu, expand_f, expand_b)


def _na_kernel(q_ref, k_ref, v_ref, bias_ref, o_ref, *, n_ctx, n_rows):
    i = pl.program_id(0)
    is_ctx = i == 0
    kc = k_ref[0:n_ctx, :]
    vc = v_ref[0:n_ctx, :]
    lane = lax.broadcasted_iota(jnp.int32, (GRID_W, NA_WIDTH), 1)
    mine = [(lane >= h * HEAD_DIM) & (lane < (h + 1) * HEAD_DIM) for h in range(NA_HEADS)]
    win = NA_WIN_ROWS * GRID_W

    def lane_tiles(a):
        return [a[:, c * LANES:(c + 1) * LANES] for c in range(a.shape[-1] // LANES)]

    def grid_row(j):
        r = jnp.maximum((i - 1) * NA_ROWS_PER_STEP + j, 0)
        r_start = jnp.clip(r - NA_WIN_ROWS // 2, 0, n_rows - NA_WIN_ROWS)
        variant = jnp.where(is_ctx, NA_WIN_ROWS, r_start - r + NA_WIN_ROWS - 1)
        start = pl.multiple_of(n_ctx + r_start * GRID_W, GRID_W)
        kw = k_ref[pl.ds(start, win), :]
        vw = v_ref[pl.ds(start, win), :]
        qj = q_ref[j * GRID_W:(j + 1) * GRID_W, :]
        qm = jnp.concatenate([jnp.where(mine[h], qj, jnp.zeros_like(qj)) for h in range(NA_HEADS)], axis=0)
        s_w = _dot_nt(qm, kw) + bias_ref[variant]
        s_c = _dot_nt(qm, kc)
        yield
        m = functools.reduce(jnp.maximum, lane_tiles(s_w) + lane_tiles(s_c))
        m = jnp.broadcast_to(jnp.max(m, axis=-1, keepdims=True), m.shape)
        p_w = jnp.exp(s_w - jnp.concatenate([m] * (s_w.shape[-1] // LANES), axis=-1))
        p_c = jnp.exp(s_c - jnp.concatenate([m] * (s_c.shape[-1] // LANES), axis=-1))
        l = jnp.sum(functools.reduce(jnp.add, lane_tiles(p_w) + lane_tiles(p_c)), axis=-1, keepdims=True)
        y = (_dot(p_w.astype(BF16), vw) + _dot(p_c.astype(BF16), vc)) * (1.0 / l)
        out = y[(NA_HEADS - 1) * GRID_W:]
        for h in range(NA_HEADS - 2, -1, -1):
            out = jnp.where(mine[h], y[h * GRID_W:(h + 1) * GRID_W], out)
        o_ref[j * GRID_W:(j + 1) * GRID_W, :] = out.astype(o_ref.dtype)

    _run_interleaved(*[grid_row(j) for j in range(NA_ROWS_PER_STEP)])


def _na_bias_table(rpb):
    col = jnp.arange(GRID_W)
    c_start = jnp.clip(col - NA_WIN_COLS // 2, 0, GRID_W - NA_WIN_COLS)
    in_win = (col[None, :] >= c_start[:, None]) & (col[None, :] < c_start[:, None] + NA_WIN_COLS)
    dc = jnp.clip(col[None, :] - col[:, None] + NA_WIN_COLS - 1, 0, 2 * NA_WIN_COLS - 2)
    n_dc = 2 * NA_WIN_COLS - 1
    n_dr = 2 * NA_WIN_ROWS - 1
    onehot = (dc[None, :, :] == jnp.arange(n_dc)[:, None, None]).astype(F32).reshape(n_dc, GRID_W * GRID_W)
    t2 = jnp.dot(rpb.reshape(NA_HEADS * n_dr, n_dc).astype(F32), onehot, precision=lax.Precision.HIGHEST)
    t2 = jnp.where(in_win[None, None], t2.reshape(NA_HEADS, n_dr, GRID_W, GRID_W), -jnp.inf)
    tab = jnp.stack([t2[:, v:v + NA_WIN_ROWS] for v in range(NA_WIN_ROWS)])
    tab = tab.transpose(0, 1, 3, 2, 4).reshape(NA_WIN_ROWS, NA_HEADS * GRID_W, NA_WIN_ROWS * GRID_W)
    masked = jnp.full((1,) + tab.shape[1:], -jnp.inf, F32)
    return jnp.concatenate([tab, masked], axis=0)


def _neighbourhood_attention(na, bias_tab, n_ctx):
    t = na.shape[0]
    n_rows = (t - n_ctx) // GRID_W
    step_rows = NA_ROWS_PER_STEP * GRID_W
    n = t // step_rows
    return pl.pallas_call(
        functools.partial(_na_kernel, n_ctx=n_ctx, n_rows=n_rows),
        grid=(n,),
        in_specs=[
            pl.BlockSpec((step_rows, NA_WIDTH), lambda i: (i, 0)),
            _resident((t, NA_WIDTH), lambda i: (0, 1)),
            _resident((t, NA_WIDTH), lambda i: (0, 2)),
            _resident(bias_tab.shape, lambda i: (0, 0, 0)),
        ],
        out_specs=pl.BlockSpec((step_rows, NA_WIDTH), lambda i: (i, 0)),
        out_shape=jax.ShapeDtypeStruct((t, NA_WIDTH), BF16),
        compiler_params=_params("parallel"),
        name="neighbourhood_attention",
    )(na, na, na, bias_tab)


def _gqa_kernel(qt_ref, k_ref, vt_ref, o_ref, acc_ref, s0_ref, s1_ref, p0_ref, p1_ref, *, n_head, n_sub):
    tq = qt_ref.shape[1]
    qt = qt_ref[...]
    row = lax.broadcasted_iota(jnp.int32, qt.shape, 0)
    top = row < HEAD_DIM
    zero = jnp.zeros_like(qt)
    qt2 = jnp.concatenate([jnp.where(top, qt, zero), jnp.where(top, zero, qt)], axis=1)
    acc_ref[...] = jnp.zeros_like(acc_ref)
    s_ref = (s0_ref, s1_ref)
    p_ref = (p0_ref, p1_ref)

    def keys(j, size):
        return pl.ds(pl.multiple_of(n_head + j * size, LANES), size)

    def score(key_rows, slot):
        size = key_rows.size
        s = _dot(k_ref[key_rows, :], qt2)
        s_ref[slot][0:size, :] = s
        return jnp.max(s.reshape(size // SUBLANES, SUBLANES, 2 * tq), axis=0)

    def softmax(size, slot, m_old, part_max):
        m_new = jnp.maximum(m_old, jnp.max(part_max, axis=0, keepdims=True))
        s = s_ref[slot][0:size, :].reshape(size // SUBLANES, SUBLANES, 2 * tq)
        p_ref[slot][0:size, :] = jnp.exp2(s - m_new[None]).reshape(size, 2 * tq).astype(BF16)
        return m_new, jnp.exp2(m_old - m_new)

    def accumulate(key_rows, slot, alpha):
        size = key_rows.size
        acc = acc_ref[...].reshape(GQA_V_ROWS // SUBLANES, SUBLANES, 2 * tq) * alpha[None]
        acc_ref[...] = acc.reshape(GQA_V_ROWS, 2 * tq) + _dot(vt_ref[:, key_rows], p_ref[slot][0:size, :])

    m = jnp.full((SUBLANES, 2 * tq), -jnp.inf, F32)
    if n_head:
        head = pl.ds(0, n_head)
        m, alpha = softmax(n_head, 0, m, score(head, 0))
        accumulate(head, 0, alpha)

    sub = GQA_K_SUB
    if n_sub:
        assert n_sub >= 3
        part0 = score(keys(0, sub), 0)
        part1 = score(keys(1, sub), 1)
        m, alpha = softmax(sub, 0, m, part0)

        def step(t, slot, carry):
            m, alpha, part = carry
            part_next = score(keys(t, sub), slot)
            accumulate(keys(t - 2, sub), slot, alpha)
            return softmax(sub, 1 - slot, m, part) + (part_next,)

        def trip(n, carry):
            for u in range(GQA_STEPS_PER_TRIP):
                carry = step(2 + n * GQA_STEPS_PER_TRIP + u, u % 2, carry)
            return carry

        n_trips = (n_sub - 2) // GQA_STEPS_PER_TRIP
        carry = lax.fori_loop(0, n_trips, trip, (m, alpha, part1))
        for t in range(2 + n_trips * GQA_STEPS_PER_TRIP, n_sub):
            carry = step(t, t % 2, carry)
        m, alpha, part = carry
        accumulate(keys(n_sub - 2, sub), (n_sub - 2) % 2, alpha)
        m, alpha = softmax(sub, (n_sub - 1) % 2, m, part)
        accumulate(keys(n_sub - 1, sub), (n_sub - 1) % 2, alpha)

    acc = acc_ref[...]
    denom = acc[HEAD_DIM:HEAD_DIM + SUBLANES]
    o_t = (acc[:HEAD_DIM].reshape(HEAD_DIM // SUBLANES, SUBLANES, 2 * tq) / denom[None]).reshape(HEAD_DIM, 2 * tq)
    o_ref[...] = jnp.concatenate([o_t[:, :tq], o_t[:, tq:]], axis=0).T.astype(o_ref.dtype)


def _gqa_bounded_kernel(qt_ref, k_ref, vt_ref, o_ref, acc_ref, *p_ref, n_sub):
    tq = qt_ref.shape[1]
    qt = qt_ref[...]
    row = lax.broadcasted_iota(jnp.int32, qt.shape, 0)
    top = row < HEAD_DIM
    zero = jnp.zeros_like(qt)
    qt2 = jnp.concatenate([jnp.where(top, qt, zero), jnp.where(top, zero, qt)], axis=1)
    acc_ref[...] = jnp.zeros_like(acc_ref)
    sub = GQA_K_SUB
    n_slots = len(p_ref)
    lag = n_slots - 1
    steps = GQA_BOUNDED_STEPS_PER_TRIP
    assert steps % n_slots == 0 and n_sub > lag

    def keys(j):
        return pl.ds(pl.multiple_of(j * sub, sub), sub)

    def probs(j, slot):
        p_ref[slot][...] = jnp.exp2(_dot(k_ref[keys(j), :], qt2)).astype(BF16)

    def accumulate(j, slot):
        acc_ref[...] += _dot(vt_ref[:, keys(j)], p_ref[slot][...])

    for t in range(lag):
        probs(t, t % n_slots)

    def trip(n, carry):
        for u in range(steps):
            t = lag + n * steps + u
            probs(t, (lag + u) % n_slots)
            accumulate(t - lag, u % n_slots)
        return carry

    n_trips = (n_sub - lag) // steps
    lax.fori_loop(0, n_trips, trip, 0)
    for t in range(lag + n_trips * steps, n_sub):
        probs(t, t % n_slots)
        accumulate(t - lag, (t - lag) % n_slots)
    for t in range(n_sub - lag, n_sub):
        accumulate(t, t % n_slots)

    acc = acc_ref[...]
    denom = acc[HEAD_DIM:HEAD_DIM + SUBLANES]
    o_t = (acc[:HEAD_DIM].reshape(HEAD_DIM // SUBLANES, SUBLANES, 2 * tq) / denom[None]).reshape(HEAD_DIM, 2 * tq)
    o_ref[...] = jnp.concatenate([o_t[:, :tq], o_t[:, tq:]], axis=0).T.astype(o_ref.dtype)


def _gqa_call(gqt, gk, gvt, n_head, n_sub, tq, name, bounded=False):
    nq = gqt.shape[1]
    n_keys = n_head + n_sub * GQA_K_SUB
    buf_rows = max(n_head, GQA_K_SUB if n_sub else 0)
    assert n_head % LANES == 0 and nq % tq == 0
    if bounded:
        assert n_head == 0
        body = functools.partial(_gqa_bounded_kernel, n_sub=n_sub)
        buffers = [pltpu.VMEM((buf_rows, 2 * tq), BF16)] * (GQA_PV_LAG + 1)
    else:
        body = functools.partial(_gqa_kernel, n_head=n_head, n_sub=n_sub)
        buffers = [pltpu.VMEM((buf_rows, 2 * tq), F32)] * 2 + [pltpu.VMEM((buf_rows, 2 * tq), BF16)] * 2
    return pl.pallas_call(
        body,
        grid=(GQA_KV_HEADS, nq // tq),
        in_specs=[
            pl.BlockSpec((LANES, tq), lambda g, i: (g, i)),
            pl.BlockSpec((n_keys, LANES), lambda g, i: (0, g)),
            pl.BlockSpec((GQA_V_ROWS, n_keys), lambda g, i: (g, 0)),
        ],
        out_specs=pl.BlockSpec((tq, LANES), lambda g, i: (i, g)),
        out_shape=jax.ShapeDtypeStruct((nq, GQA_WIDTH), BF16),
        scratch_shapes=[pltpu.VMEM((GQA_V_ROWS, 2 * tq), F32)] + buffers,
        compiler_params=_params("arbitrary", "arbitrary"),
        name=name,
    )(gqt, gk, gvt)


def _gqa_attention(gqt, gk, gvt, n_ctx, score_bound):
    n_head = n_ctx % GQA_K_SUB
    n_sub = (gk.shape[0] - n_head) // GQA_K_SUB
    assert (gk.shape[0] - n_head) % GQA_K_SUB == 0
    y_ctx = _gqa_call(gqt[:, :n_ctx], gk, gvt, n_ctx, 0, n_ctx, "gqa_attention_ctx")
    q_lat = gqt[:, n_ctx:]

    def with_running_max(q, k, vt):
        return _gqa_call(q, k, vt, n_head, n_sub, GQA_Q_TILE, "gqa_attention")

    def without_running_max(q, k, vt):
        return _gqa_call(q, k, vt, 0, n_sub, GQA_Q_TILE, "gqa_attention_bounded", bounded=True)

    if n_head == 0:
        y_lat = lax.cond(score_bound <= GQA_BOUNDED_LOG2, without_running_max, with_running_max, q_lat, gk, gvt)
    else:
        y_lat = with_running_max(q_lat, gk, gvt)
    return jnp.concatenate([y_ctx, y_lat], axis=0)


def _out_ffn_kernel(ya_ref, yf_ref, yb_ref, xs_ref, z_ref, yg_ref, xc_ref, xl_ref, mod_ref,
                    dskip_ref, snw_ref, wo_ref, fnw_ref, wg_ref, wu_ref, wd_ref, final_ref,
                    o_ref, *, final, pick_ctx):
    x = jnp.where(pl.program_id(0) == 0, xc_ref[...], xl_ref[...]) if pick_ctx else xl_ref[...]
    y = yf_ref[...] + yb_ref[...] + dskip_ref[...] * xs_ref[...]
    y = y * _silu(z_ref[...])
    ms = jnp.mean(y * y, axis=-1, keepdims=True)
    y = y * lax.rsqrt(ms + EPS) * snw_ref[...]
    mix = jnp.concatenate([ya_ref[...], y.astype(BF16), yg_ref[...]], axis=-1)
    g_m = mod_ref[:, 2 * D_MODEL:3 * D_MODEL]
    sh_f = mod_ref[:, 3 * D_MODEL:4 * D_MODEL]
    sc_f = mod_ref[:, 4 * D_MODEL:5 * D_MODEL]
    g_f = mod_ref[:, 5 * D_MODEL:6 * D_MODEL]
    x1 = x + g_m * _dot(mix, wo_ref[...])
    ms1 = jnp.mean(x1 * x1, axis=-1, keepdims=True)
    hf = (x1 * lax.rsqrt(ms1 + EPS) * fnw_ref[...] * (1.0 + sc_f) + sh_f).astype(BF16)
    act = (_silu(_dot(hf, wg_ref[...])) * _dot(hf, wu_ref[...])).astype(BF16)
    x2 = x1 + g_f * _dot(act, wd_ref[...])
    if final:
        ms2 = jnp.mean(x2 * x2, axis=-1, keepdims=True)
        x2 = x2 * lax.rsqrt(ms2 + EPS) * final_ref[...]
    o_ref[...] = x2


def _out_ffn(ya, yf, yb, xconv, z, yg, x_parts, mod4, layer, d_full, ssd_nw, w_out, ffn_nw,
             w_gate, w_up, w_down, final_nw, final):
    t = ya.shape[0]
    skip = 1 if final else 0
    n = t // ROW_TILE - skip
    row = lambda i: (i + skip, 0)
    const = lambda i: (0, 0)
    x_arrays, x_specs, pick_ctx = _residual_operands(x_parts, skip)
    return pl.pallas_call(
        functools.partial(_out_ffn_kernel, final=final, pick_ctx=pick_ctx),
        grid=(n,),
        in_specs=[
            pl.BlockSpec((ROW_TILE, NA_WIDTH), row),
            pl.BlockSpec((ROW_TILE, SSD_WIDTH), row),
            pl.BlockSpec((ROW_TILE, SSD_WIDTH), row),
            pl.BlockSpec((ROW_TILE, SSD_WIDTH), row),
            pl.BlockSpec((ROW_TILE, SSD_WIDTH), row),
            pl.BlockSpec((ROW_TILE, GQA_WIDTH), row),
        ] + x_specs + [
            pl.BlockSpec((None, None, 1, 6 * D_MODEL), lambda i: (layer, jnp.minimum(i + skip, 1), 0, 0)),
            pl.BlockSpec((1, SSD_WIDTH), const),
            pl.BlockSpec((1, SSD_WIDTH), const),
            _resident((D_MODEL, D_MODEL), const),
            pl.BlockSpec((1, D_MODEL), const),
            _resident((D_MODEL, FFN_HIDDEN), const),
            _resident((D_MODEL, FFN_HIDDEN), const),
            _resident((FFN_HIDDEN, D_MODEL), const),
            pl.BlockSpec((1, D_MODEL), const),
        ],
        out_specs=pl.BlockSpec((ROW_TILE, D_MODEL), lambda i: (i, 0)),
        out_shape=jax.ShapeDtypeStruct((n * ROW_TILE, D_MODEL), F32),
        compiler_params=_params("parallel"),
        name="out_ffn",
    )(ya, yf, yb, xconv, z, yg, *x_arrays, mod4, d_full, ssd_nw, w_out, ffn_nw, w_gate, w_up, w_down, final_nw)


def _rearranged_w_in(w):
    na_in = 3 * NA_WIDTH
    o_z = na_in
    o_xbc = o_z + SSD_WIDTH
    o_dt = o_xbc + SSD_CONV_CH
    o_gq = o_dt + 2 * SSD_HEADS
    o_gk = o_gq + GQA_WIDTH
    o_gv = o_gk + GQA_KV_HEADS * HEAD_DIM
    na = jnp.concatenate([w[:, :NA_WIDTH] * ATTN_SCALE, w[:, NA_WIDTH:na_in]], axis=1)
    dt = jnp.pad(w[:, o_dt:o_gq], ((0, 0), (0, LANES - 2 * SSD_HEADS)))

    k_heads = [w[:, o_gk + h * HEAD_DIM:o_gk + (h + 1) * HEAD_DIM] for h in range(GQA_KV_HEADS)]
    k_twice = jnp.concatenate([p for h in k_heads for p in (h, h)], axis=1)
    return jnp.concatenate(
        [na, w[:, o_z:o_xbc], w[:, o_xbc:o_dt], dt, w[:, o_gq:o_gk], k_twice, w[:, o_gv:]], axis=1).astype(BF16)


def _rope_tables(n_ctx, n_lat):
    freqs = ROPE_THETA ** (-jnp.arange(ROPE_PAIRS, dtype=F32) / ROPE_PAIRS)
    n_rows = n_lat // GRID_W
    half = 2 * ROPE_PAIRS
    sign = jnp.where(jnp.arange(half) < ROPE_PAIRS, -1.0, 1.0).astype(F32)

    def tables(n_pos):
        a = jnp.arange(n_pos, dtype=F32)[:, None] * freqs[None, :]
        a = jnp.concatenate([a, a], axis=-1)
        return jnp.cos(a), jnp.sin(a) * sign[None, :]

    def per_token(by_row, by_col):
        lat = jnp.concatenate([jnp.broadcast_to(by_row[:, None, :], (n_rows, GRID_W, half)),
                               jnp.broadcast_to(by_col[None, :, :], (n_rows, GRID_W, half))], axis=-1)
        return lat.reshape(n_lat, HEAD_DIM)

    cos_r, sin_r = tables(n_rows)
    cos_c, sin_c = tables(GRID_W)
    cos = jnp.concatenate([jnp.ones((n_ctx, HEAD_DIM), F32), per_token(cos_r, cos_c)], axis=0)
    sin = jnp.concatenate([jnp.zeros((n_ctx, HEAD_DIM), F32), per_token(sin_r, sin_c)], axis=0)
    return jnp.tile(cos, (1, LANES // HEAD_DIM)), jnp.tile(sin, (1, LANES // HEAD_DIM))


def kernel(x, c, ctx, c_ctx, mod_w, mod_b, norm_attn_w, norm_ffn_w, w_in, na_rpb, ssd_conv_w, ssd_conv_b,
           ssd_dt_bias, ssd_a_log, ssd_d, ssd_norm_w, q_norm_w, k_norm_w, w_out, ffn_w_gate, ffn_w_up,
           ffn_w_down, final_norm_w):
    depth = mod_w.shape[0]
    batch, n_lat, _ = x.shape
    n_ctx = ctx.shape[1]
    assert batch == 1 and n_ctx == ROW_TILE and n_lat % (NA_ROWS_PER_STEP * GRID_W) == 0
    assert n_lat % GQA_Q_TILE == 0 and n_lat // GRID_W >= NA_WIN_ROWS

    x_parts = (ctx[0], x[0])
    cc = jnp.zeros((SUBLANES, D_MODEL), F32).at[0].set(c_ctx).at[1].set(c[0])
    mod = _modulation(cc, mod_w, mod_b)
    mod4 = mod[:, :2].reshape(depth, 2, 1, 6 * D_MODEL)

    cos_t, sin_t = _rope_tables(n_ctx, n_lat)
    blk = jnp.arange(2 * GQA_WIDTH) // HEAD_DIM
    ones_bd = (blk[:, None] == blk[None, :]).astype(BF16)
    head_cols = jnp.arange(SSD_WIDTH) // HEAD_DIM

    for i in range(depth):
        final = i == depth - 1
        w_cat = _rearranged_w_in(w_in[i])
        dt_bias_pad = jnp.pad(ssd_dt_bias[i].reshape(1, -1), ((0, 0), (0, LANES - 2 * SSD_HEADS)))
        qk_w = jnp.concatenate([jnp.tile(q_norm_w[i] * (ATTN_SCALE * LOG2E), GQA_Q_HEADS),
                                jnp.tile(k_norm_w[i], 2 * GQA_KV_HEADS)]).reshape(1, -1)
        conv_w_pad = jnp.pad(ssd_conv_w[i], ((0, SUBLANES - SSD_CONV), (0, 0)))
        na, z, xconv, dt, gq, gk, gv = _inproj(x_parts, n_ctx + n_lat, mod4, i, norm_attn_w[i].reshape(1, -1),
                                               w_cat, dt_bias_pad, qk_w, cos_t, sin_t, ones_bd, conv_w_pad,
                                               ssd_conv_b[i].reshape(1, -1))
        a_row = jnp.pad(-jnp.exp(ssd_a_log[i].astype(F32)).reshape(1, -1), ((0, 0), (0, LANES - 2 * SSD_HEADS)))
        yf, yb = _ssd_scan(xconv, dt, a_row, n_ctx)
        ya = _neighbourhood_attention(na, _na_bias_table(na_rpb[i]), n_ctx)
        ones_rows = jnp.ones((GQA_V_ROWS - HEAD_DIM, gv.shape[0]), BF16)
        gvt = jnp.concatenate([part for g in range(GQA_KV_HEADS)
                               for part in (gv[:, g * HEAD_DIM:(g + 1) * HEAD_DIM].T, ones_rows)], axis=0)
        score_bound = (1.02 * HEAD_DIM * ATTN_SCALE * LOG2E) * jnp.max(jnp.abs(q_norm_w[i])) * jnp.max(
            jnp.abs(k_norm_w[i]))
        yg = _gqa_attention(gq.T, gk, gvt, n_ctx, score_bound)
        d_full = ssd_d[i].astype(F32)[head_cols].reshape(1, -1)
        x_parts = _out_ffn(ya, yf, yb, xconv, z, yg, x_parts, mod4, i, d_full, ssd_norm_w[i].reshape(1, -1),
                           w_out[i].astype(BF16), norm_ffn_w[i].reshape(1, -1), ffn_w_gate[i].astype(BF16),
                           ffn_w_up[i].astype(BF16), ffn_w_down[i].astype(BF16),
                           final_norm_w.reshape(1, -1), final)
    return x_parts[None]
```

```python
import functools
import math

import jax
import jax.numpy as jnp
from jax import lax
from jax.experimental import pallas as pl
from jax.experimental.pallas import tpu as pltpu

F32 = jnp.float32
BF16 = jnp.bfloat16

D_MODEL = 1024
GRID_W = 64
HEAD_DIM = 64
NA_WIDTH = 256
NA_HEADS = 4
NA_WIN_ROWS = 8
NA_WIN_COLS = 16
SSD_WIDTH = 512
SSD_HEADS = 8
SSD_GROUPS = 2
SSD_STATE = 128
SSD_CONV = 5
SSD_CHUNK = 128
SSD_CONV_CH = SSD_WIDTH + 2 * SSD_GROUPS * SSD_STATE
GQA_WIDTH = 256
GQA_Q_HEADS = 4
GQA_KV_HEADS = 2
ROPE_THETA = 10000.0
ROPE_PAIRS = HEAD_DIM // 4
FFN_HIDDEN = 2816
EPS = 1e-6
ATTN_SCALE = HEAD_DIM ** -0.5
LOG2E = math.log2(math.e)

LANES = 128
SUBLANES = 8
VMEM_LIMIT_BYTES = 56 * 1024 * 1024

ROW_TILE = 256
MOD_COL_TILE = 2048
SSD_CHUNKS_PER_STEP = 2
NA_ROWS_PER_STEP = 4
GQA_Q_TILE = 1024
GQA_V_ROWS = HEAD_DIM + 16
GQA_K_SUB = 256
GQA_BOUNDED_LOG2 = 60.0
GQA_PV_LAG = 2
GQA_BOUNDED_STEPS_PER_TRIP = 9
GQA_STEPS_PER_TRIP = 8

C_NA = 0
C_Z = C_NA + 3 * NA_WIDTH
C_XBC = C_Z + SSD_WIDTH
C_DT = C_XBC + SSD_CONV_CH
C_GQ = C_DT + LANES
C_GK = C_GQ + GQA_WIDTH
C_GV = C_GK + 2 * LANES
C_END = C_GV + LANES


def _silu(v):
    return v * (1.0 / (1.0 + jnp.exp(-v)))


def _softplus(v):
    return jnp.maximum(v, 0.0) + jnp.log(1.0 + jnp.exp(-jnp.abs(v)))


def _split3(v):
    hi = v.astype(BF16)
    r1 = v - hi.astype(F32)
    mid = r1.astype(BF16)
    lo = (r1 - mid.astype(F32)).astype(BF16)
    return hi, mid, lo


def _dot(a, b):
    return jnp.dot(a, b, preferred_element_type=F32)


def _dot_nt(a, b):
    return lax.dot_general(a, b, (((1,), (1,)), ((), ())), preferred_element_type=F32)


def _dot_tn(a, b):
    return lax.dot_general(a, b, (((0,), (0,)), ((), ())), preferred_element_type=F32)


def _exact_dot(v, sel):
    hi, mid, lo = _split3(v)
    return _dot(hi, sel) + _dot(mid, sel) + _dot(lo, sel)


def _spread_dot(v, sel):
    hi = v.astype(BF16)
    lo = (v - hi.astype(F32)).astype(BF16)
    return _dot(hi, sel) + _dot(lo, sel)


def _exact_dot_lhs(sel, v):
    hi, mid, lo = _split3(v)
    return _dot(sel, hi) + _dot(sel, mid) + _dot(sel, lo)


def _params(*sem):
    return pltpu.CompilerParams(dimension_semantics=sem, vmem_limit_bytes=VMEM_LIMIT_BYTES)


def _resident(shape, index_map):
    return pl.BlockSpec(shape, index_map, pipeline_mode=pl.Buffered(1))


def _mod_kernel(cc_ref, w_ref, b_ref, o_ref):
    a = _silu(cc_ref[...])
    o_ref[0] = jnp.dot(a, w_ref[0], preferred_element_type=F32) + b_ref[0]


def _modulation(cc, mod_w, mod_b):
    depth = mod_w.shape[0]
    cols = MOD_COL_TILE
    ncol = mod_w.shape[2] // cols
    return pl.pallas_call(
        _mod_kernel,
        grid=(depth, ncol),
        in_specs=[
            pl.BlockSpec((SUBLANES, D_MODEL), lambda l, j: (0, 0)),
            pl.BlockSpec((1, D_MODEL, cols), lambda l, j: (l, 0, j)),
            pl.BlockSpec((1, 1, cols), lambda l, j: (l, 0, j)),
        ],
        out_specs=pl.BlockSpec((1, SUBLANES, cols), lambda l, j: (l, 0, j)),
        out_shape=jax.ShapeDtypeStruct((depth, SUBLANES, ncol * cols), F32),
        compiler_params=_params("arbitrary", "arbitrary"),
        name="modulation",
    )(cc, mod_w, mod_b.reshape(depth, 1, -1))


def _residual_operands(x_parts, skip):
    ctx_spec = pl.BlockSpec((ROW_TILE, D_MODEL), lambda i: (0, 0))
    if isinstance(x_parts, tuple):
        lat_spec = pl.BlockSpec((ROW_TILE, D_MODEL), lambda i: (jnp.maximum(i + skip - 1, 0), 0))
        return list(x_parts), [ctx_spec, lat_spec], skip == 0
    row_spec = pl.BlockSpec((ROW_TILE, D_MODEL), lambda i: (i + skip, 0))
    return [x_parts, x_parts], [ctx_spec, row_spec], False


def _inproj_kernel(xc_ref, xl_ref, prev_ref, next_ref, mod_ref, nw_ref, w_ref, dtb_ref, qkw_ref, cos_ref, sin_ref,
                   ones_ref, cw_ref, cb_ref, na_ref, z_ref, xconv_ref, dt_ref, gq_ref, gk_ref, gv_ref, *, pick_ctx):
    i = pl.program_id(0)
    n = pl.num_programs(0)
    rows = xl_ref.shape[0]
    x = jnp.where(i == 0, xc_ref[...], xl_ref[...]) if pick_ctx else xl_ref[...]
    x = jnp.concatenate([prev_ref[...], x, next_ref[...]], axis=0)
    ms = jnp.mean(x * x, axis=-1, keepdims=True)
    xn = x * lax.rsqrt(ms + EPS) * nw_ref[...]
    sh = mod_ref[:, 0:D_MODEL]
    sc = mod_ref[:, D_MODEL:2 * D_MODEL]
    h = (xn * (1.0 + sc) + sh).astype(BF16)
    xbc_ext = _dot(h, w_ref[:, C_XBC:C_DT])
    h_tile = h[SUBLANES:SUBLANES + rows]
    u = jnp.concatenate([_dot(h_tile, w_ref[:, C_NA:C_XBC]), xbc_ext[SUBLANES:SUBLANES + rows],
                         _dot(h_tile, w_ref[:, C_DT:C_END])], axis=-1)
    na_ref[...] = u[:, C_NA:C_Z].astype(BF16)
    z_ref[...] = u[:, C_Z:C_XBC]
    has_prev = i >= 2
    has_next = jnp.logical_and(i >= 1, i < n - 1)
    xbc = u[:, C_XBC:C_DT]
    ext = jnp.concatenate([jnp.where(has_prev, xbc_ext[0:SUBLANES], 0.0), xbc,
                           jnp.where(has_next, xbc_ext[SUBLANES + rows:], 0.0)], axis=0)
    total = rows + 2 * SUBLANES
    half = SSD_CONV // 2
    acc = cb_ref[...] + cw_ref[half:half + 1, :] * xbc
    for j in range(SSD_CONV):
        if j != half:
            shifted = pltpu.roll(ext, (half - j) % total, 0)
            acc = acc + cw_ref[j:j + 1, :] * shifted[SUBLANES:SUBLANES + rows]
    xconv_ref[...] = _silu(acc)
    dt_ref[...] = _softplus(u[:, C_DT:C_GQ] + dtb_ref[...])
    gv_ref[...] = u[:, C_GV:C_END].astype(BF16)
    g = u[:, C_GQ:C_GV]
    gsq = g * g
    hi = gsq.astype(BF16)
    lo = (gsq - hi.astype(F32)).astype(BF16)
    ss = _dot(hi, ones_ref[...]) + _dot(lo, ones_ref[...])
    gn = g * lax.rsqrt(ss * (1.0 / HEAD_DIM) + EPS) * qkw_ref[...]
    width = gn.shape[-1]
    lane = lax.broadcasted_iota(jnp.int32, gn.shape, 1)
    first = (lane % (2 * ROPE_PAIRS)) < ROPE_PAIRS
    partner = jnp.where(first, pltpu.roll(gn, width - ROPE_PAIRS, 1), pltpu.roll(gn, ROPE_PAIRS, 1))
    cos = jnp.concatenate([cos_ref[...]] * (width // LANES), axis=-1)
    sin = jnp.concatenate([sin_ref[...]] * (width // LANES), axis=-1)
    gr = gn * cos + partner * sin
    gq_ref[...] = gr[:, :GQA_WIDTH].astype(BF16)
    gk_ref[...] = gr[:, GQA_WIDTH:].astype(BF16)


def _inproj(x_parts, t, mod4, layer, norm_w, w_cat, dt_bias_pad, qk_w, cos_t, sin_t, ones_bd, conv_w_pad, conv_b):
    n = t // ROW_TILE
    row = lambda i: (i, 0)
    const = lambda i: (0, 0)
    outs = [
        (3 * NA_WIDTH, BF16), (SSD_WIDTH, F32), (SSD_CONV_CH, F32), (LANES, F32),
        (GQA_WIDTH, BF16), (2 * LANES, BF16), (LANES, BF16),
    ]
    x_arrays, x_specs, pick_ctx = _residual_operands(x_parts, 0)
    halo_src = x_arrays[1]
    per = ROW_TILE // SUBLANES
    first = (lambda i: (i - 1) * per) if isinstance(x_parts, tuple) else (lambda i: i * per)
    last_blk = halo_src.shape[0] // SUBLANES - 1
    halo_specs = [
        pl.BlockSpec((SUBLANES, D_MODEL), lambda i: (jnp.clip(first(i) - 1, 0, last_blk), 0)),
        pl.BlockSpec((SUBLANES, D_MODEL), lambda i: (jnp.clip(first(i) + per, 0, last_blk), 0)),
    ]
    return pl.pallas_call(
        functools.partial(_inproj_kernel, pick_ctx=pick_ctx),
        grid=(n,),
        in_specs=x_specs + halo_specs + [
            pl.BlockSpec((None, None, 1, 6 * D_MODEL), lambda i: (layer, jnp.minimum(i, 1), 0, 0)),
            pl.BlockSpec((1, D_MODEL), const),
            _resident((D_MODEL, C_END), const),
            pl.BlockSpec((1, LANES), const),
            pl.BlockSpec((1, 2 * GQA_WIDTH), const),
            pl.BlockSpec((ROW_TILE, LANES), row),
            pl.BlockSpec((ROW_TILE, LANES), row),
            _resident((2 * GQA_WIDTH, 2 * GQA_WIDTH), const),
            pl.BlockSpec((SUBLANES, SSD_CONV_CH), const),
            pl.BlockSpec((1, SSD_CONV_CH), const),
        ],
        out_specs=[pl.BlockSpec((ROW_TILE, w), row) for w, _ in outs],
        out_shape=[jax.ShapeDtypeStruct((t, w), d) for w, d in outs],
        compiler_params=_params("parallel"),
        name="inproj",
    )(*x_arrays, halo_src, halo_src, mod4, norm_w, w_cat, dt_bias_pad, qk_w, cos_t, sin_t, ones_bd,
      conv_w_pad, conv_b)


def _ssd_direction(xbc, dt, a_row, tri, expand, s_ref, lane0, backward):
    q = SSD_CHUNK
    x = xbc[:, :SSD_WIDTH]
    nb = SSD_GROUPS * SSD_STATE
    bmat = xbc[:, SSD_WIDTH:SSD_WIDTH + nb].astype(BF16)
    cmat = xbc[:, SSD_WIDTH + nb:].astype(BF16)
    cum = _exact_dot_lhs(tri, dt * a_row)
    yield
    last = 0 if backward else q - 1
    cum_t = cum.T

    def spread(mat_t):
        rows = [jnp.broadcast_to(mat_t[lane0 + h:lane0 + h + 1, :], (HEAD_DIM, q)) for h in range(SSD_HEADS)]
        return jnp.concatenate(rows, axis=0).T

    dt_full = _spread_dot(dt, expand)
    cum_full = spread(cum_t)
    yield
    ea_full = jnp.exp(cum_full)
    te_full = jnp.exp(cum_full[last:last + 1, :] - cum_full)
    xr = x * dt_full
    xr_b = xr.astype(BF16)
    xt_b = (xr * te_full).astype(BF16)
    ti =lax.broadcasted_iota(jnp.int32, (q, q), 0)
    si = lax.broadcasted_iota(jnp.int32, (q, q), 1)
    keep = (si >= ti) if backward else (si <= ti)
    lane = lax.broadcasted_iota(jnp.int32, (q, LANES), 1)
    heads_per_group = SSD_HEADS // SSD_GROUPS
    width_g = heads_per_group * HEAD_DIM
    pieces = []
    decay_total = ea_full[last:last + 1, :]
    for g in range(SSD_GROUPS):
        bg = bmat[:, g * SSD_STATE:(g + 1) * SSD_STATE]
        cg = cmat[:, g * SSD_STATE:(g + 1) * SSD_STATE]
        gmat = _dot_nt(cg, bg)
        s_old = s_ref[:, g * width_g:(g + 1) * width_g]
        y_off = _dot(cg, s_old.astype(BF16)) * ea_full[:, g * width_g:(g + 1) * width_g]
        s_new = _dot_tn(bg, xt_b[:, g * width_g:(g + 1) * width_g])
        s_ref[:, g * width_g:(g + 1) * width_g] = decay_total[:, g * width_g:(g + 1) * width_g] * s_old + s_new
        yield
        for pair in range(heads_per_group // 2):
            col = g * width_g + pair * LANES
            xr_pair = xr_b[:, col:col + LANES]
            ys = []
            for r in range(2):
                hl = lane0 + g * heads_per_group + 2 * pair + r
                seg = jnp.broadcast_to(cum[:, hl:hl + 1], (q, q)) - jnp.broadcast_to(cum_t[hl:hl + 1, :], (q, q))
                dec = jnp.exp(jnp.where(keep, seg, -jnp.inf))
                ys.append(_dot((gmat * dec).astype(BF16), xr_pair))
            y_diag = jnp.where(lane < HEAD_DIM, ys[0], ys[1])
            pieces.append(y_diag + y_off[:, pair * LANES:(pair + 1) * LANES])
            yield
    return jnp.concatenate(pieces, axis=-1)


def _run_interleaved(*stage_generators):
    results = [None] * len(stage_generators)
    live = list(range(len(stage_generators)))
    while live:
        for idx in list(live):
            try:
                next(stage_generators[idx])
            except StopIteration as done:
                results[idx] = done.value
                live.remove(idx)
    return results


def _ssd_kernel(xf_ref, dtf_ref, xb_ref, dtb_ref, a_ref, tril_ref, triu_ref, ef_ref, eb_ref,
                yf_ref, yb_ref, sf_ref, sb_ref):
    @pl.when(pl.program_id(0) == 0)
    def _():
        sf_ref[...] = jnp.zeros_like(sf_ref)
        sb_ref[...] = jnp.zeros_like(sb_ref)

    a_row = a_ref[...]
    q = SSD_CHUNK
    order_f = list(range(SSD_CHUNKS_PER_STEP))
    order_b = order_f[::-1]
    stages = []
    for cf, cb in zip(order_f, order_b):
        rf, rb = slice(cf * q, (cf + 1) * q), slice(cb * q, (cb + 1) * q)
        stages.append(_ssd_direction(xf_ref[rf, :], dtf_ref[rf, :], a_row, tril_ref[...], ef_ref[...],
                                     sf_ref, 0, False))
        stages.append(_ssd_direction(xb_ref[rb, :], dtb_ref[rb, :], a_row, triu_ref[...], eb_ref[...],
                                     sb_ref, SSD_HEADS, True))
    ys = _run_interleaved(*stages)
    for k, (cf, cb) in enumerate(zip(order_f, order_b)):
        yf_ref[cf * q:(cf + 1) * q, :] = ys[2 * k]
        yb_ref[cb * q:(cb + 1) * q, :] = ys[2 * k + 1]


def _ssd_scan(xconv, dt, a_row, n_ctx):
    t = xconv.shape[0]
    q = SSD_CHUNK
    rows = SSD_CHUNKS_PER_STEP * q
    assert n_ctx % rows == 0 and t % rows == 0
    n = t // rows
    n_ctx_blocks = n_ctx // rows
    r = jnp.arange(q)
    tril = (r[None, :] <= r[:, None]).astype(BF16)
    triu = (r[None, :] >= r[:, None]).astype(BF16)
    lanes = jnp.arange(LANES)[:, None]
    head_of_col = (jnp.arange(SSD_WIDTH) // HEAD_DIM)[None, :]
    expand_f = (lanes == head_of_col).astype(BF16)
    expand_b = (lanes == head_of_col + SSD_HEADS).astype(BF16)

    def fwd(i):
        return (i, 0)

    def bwd(i):
        return (jnp.where(i < n_ctx_blocks, n_ctx_blocks - 1 - i, n - 1 - (i - n_ctx_blocks)), 0)

    const = lambda i: (0, 0)
    return pl.pallas_call(
        _ssd_kernel,
        grid=(n,),
        in_specs=[
            pl.BlockSpec((rows, SSD_CONV_CH), fwd),
            pl.BlockSpec((rows, LANES), fwd),
            pl.BlockSpec((rows, SSD_CONV_CH), bwd),
            pl.BlockSpec((rows, LANES), bwd),
            pl.BlockSpec((1, LANES), const),
            pl.BlockSpec((q, q), const),
            pl.BlockSpec((q, q), const),
            pl.BlockSpec((LANES, SSD_WIDTH), const),
            pl.BlockSpec((LANES, SSD_WIDTH), const),
        ],
        out_specs=[pl.BlockSpec((rows, SSD_WIDTH), fwd), pl.BlockSpec((rows, SSD_WIDTH), bwd)],
        out_shape=[jax.ShapeDtypeStruct((t, SSD_WIDTH), F32)] * 2,
        scratch_shapes=[pltpu.VMEM((SSD_STATE, SSD_WIDTH), F32)] * 2,
        compiler_params=_params("arbitrary"),
        name="ssd_scan",
    )(xconv, dt, xconv, dt, a_row, tril, triu, expand_f, expand_b)


def _na_kernel(q_ref, k_ref, v_ref, bias_ref, o_ref, *, n_ctx, n_rows):
    i = pl.program_id(0)
    is_ctx = i == 0
    kc = k_ref[0:n_ctx, :]
    vc = v_ref[0:n_ctx, :]
    lane = lax.broadcasted_iota(jnp.int32, (GRID_W, NA_WIDTH), 1)
    mine = [(lane >= h * HEAD_DIM) & (lane < (h + 1) * HEAD_DIM) for h in range(NA_HEADS)]
    win = NA_WIN_ROWS * GRID_W

    def lane_tiles(a):
        return [a[:, c * LANES:(c + 1) * LANES] for c in range(a.shape[-1] // LANES)]

    def grid_row(j):
        r = jnp.maximum((i - 1) * NA_ROWS_PER_STEP + j, 0)
        r_start = jnp.clip(r - NA_WIN_ROWS // 2, 0, n_rows - NA_WIN_ROWS)
        variant = jnp.where(is_ctx, NA_WIN_ROWS, r_start - r + NA_WIN_ROWS - 1)
        start = pl.multiple_of(n_ctx + r_start * GRID_W, GRID_W)
        kw = k_ref[pl.ds(start, win), :]
        vw = v_ref[pl.ds(start, win), :]
        qj = q_ref[j * GRID_W:(j + 1) * GRID_W, :]
        qm = jnp.concatenate([jnp.where(mine[h], qj, jnp.zeros_like(qj)) for h in range(NA_HEADS)], axis=0)
        s_w = _dot_nt(qm, kw) + bias_ref[variant]
        s_c = _dot_nt(qm, kc)
        yield
        m = functools.reduce(jnp.maximum, lane_tiles(s_w) + lane_tiles(s_c))
        m = jnp.broadcast_to(jnp.max(m, axis=-1, keepdims=True), m.shape)
        p_w = jnp.exp(s_w - jnp.concatenate([m] * (s_w.shape[-1] // LANES), axis=-1))
        p_c = jnp.exp(s_c - jnp.concatenate([m] * (s_c.shape[-1] // LANES), axis=-1))
        l = jnp.sum(functools.reduce(jnp.add, lane_tiles(p_w) + lane_tiles(p_c)), axis=-1, keepdims=True)
        y = (_dot(p_w.astype(BF16), vw) + _dot(p_c.astype(BF16), vc)) * (1.0 / l)
        out = y[(NA_HEADS - 1) * GRID_W:]
        for h in range(NA_HEADS - 2, -1, -1):
            out = jnp.where(mine[h], y[h * GRID_W:(h + 1) * GRID_W], out)
        o_ref[j * GRID_W:(j + 1) * GRID_W, :] = out.astype(o_ref.dtype)

    _run_interleaved(*[grid_row(j) for j in range(NA_ROWS_PER_STEP)])


def _na_bias_table(rpb):
    col = jnp.arange(GRID_W)
    c_start = jnp.clip(col - NA_WIN_COLS // 2, 0, GRID_W - NA_WIN_COLS)
    in_win = (col[None, :] >= c_start[:, None]) & (col[None, :] < c_start[:, None] + NA_WIN_COLS)
    dc = jnp.clip(col[None, :] - col[:, None] + NA_WIN_COLS - 1, 0, 2 * NA_WIN_COLS - 2)
    n_dc = 2 * NA_WIN_COLS - 1
    n_dr = 2 * NA_WIN_ROWS - 1
    onehot = (dc[None, :, :] == jnp.arange(n_dc)[:, None, None]).astype(F32).reshape(n_dc, GRID_W * GRID_W)
    t2 = jnp.dot(rpb.reshape(NA_HEADS * n_dr, n_dc).astype(F32), onehot, precision=lax.Precision.HIGHEST)
    t2 = jnp.where(in_win[None, None], t2.reshape(NA_HEADS, n_dr, GRID_W, GRID_W), -jnp.inf)
    tab = jnp.stack([t2[:, v:v + NA_WIN_ROWS] for v in range(NA_WIN_ROWS)])
    tab = tab.transpose(0, 1, 3, 2, 4).reshape(NA_WIN_ROWS, NA_HEADS * GRID_W, NA_WIN_ROWS * GRID_W)
    masked = jnp.full((1,) + tab.shape[1:], -jnp.inf, F32)
    return jnp.concatenate([tab, masked], axis=0)


def _neighbourhood_attention(na, bias_tab, n_ctx):
    t = na.shape[0]
    n_rows = (t - n_ctx) // GRID_W
    step_rows = NA_ROWS_PER_STEP * GRID_W
    n = t // step_rows
    return pl.pallas_call(
        functools.partial(_na_kernel, n_ctx=n_ctx, n_rows=n_rows),
        grid=(n,),
        in_specs=[
            pl.BlockSpec((step_rows, NA_WIDTH), lambda i: (i, 0)),
            _resident((t, NA_WIDTH), lambda i: (0, 1)),
            _resident((t, NA_WIDTH), lambda i: (0, 2)),
            _resident(bias_tab.shape, lambda i: (0, 0, 0)),
        ],
        out_specs=pl.BlockSpec((step_rows, NA_WIDTH), lambda i: (i, 0)),
        out_shape=jax.ShapeDtypeStruct((t, NA_WIDTH), BF16),
        compiler_params=_params("parallel"),
        name="neighbourhood_attention",
    )(na, na, na, bias_tab)


def _gqa_kernel(qt_ref, k_ref, vt_ref, o_ref, acc_ref, s0_ref, s1_ref, p0_ref, p1_ref, *, n_head, n_sub):
    tq = qt_ref.shape[1]
    qt = qt_ref[...]
    row = lax.broadcasted_iota(jnp.int32, qt.shape, 0)
    top = row < HEAD_DIM
    zero = jnp.zeros_like(qt)
    qt2 = jnp.concatenate([jnp.where(top, qt, zero), jnp.where(top, zero, qt)], axis=1)
    acc_ref[...] = jnp.zeros_like(acc_ref)
    s_ref = (s0_ref, s1_ref)
    p_ref = (p0_ref, p1_ref)

    def keys(j, size):
        return pl.ds(pl.multiple_of(n_head + j * size, LANES), size)

    def score(key_rows, slot):
        size = key_rows.size
        s = _dot(k_ref[key_rows, :], qt2)
        s_ref[slot][0:size, :] = s
        return jnp.max(s.reshape(size // SUBLANES, SUBLANES, 2 * tq), axis=0)

    def softmax(size, slot, m_old, part_max):
        m_new = jnp.maximum(m_old, jnp.max(part_max, axis=0, keepdims=True))
        s = s_ref[slot][0:size, :].reshape(size // SUBLANES, SUBLANES, 2 * tq)
        p_ref[slot][0:size, :] = jnp.exp2(s - m_new[None]).reshape(size, 2 * tq).astype(BF16)
        return m_new, jnp.exp2(m_old - m_new)

    def accumulate(key_rows, slot, alpha):
        size = key_rows.size
        acc = acc_ref[...].reshape(GQA_V_ROWS // SUBLANES, SUBLANES, 2 * tq) * alpha[None]
        acc_ref[...] = acc.reshape(GQA_V_ROWS, 2 * tq) + _dot(vt_ref[:, key_rows], p_ref[slot][0:size, :])

    m = jnp.full((SUBLANES, 2 * tq), -jnp.inf, F32)
    if n_head:
        head = pl.ds(0, n_head)
        m, alpha = softmax(n_head, 0, m, score(head, 0))
        accumulate(head, 0, alpha)

    sub = GQA_K_SUB
    if n_sub:
        assert n_sub >= 3
        part0 = score(keys(0, sub), 0)
        part1 = score(keys(1, sub), 1)
        m, alpha = softmax(sub, 0, m, part0)

        def step(t, slot, carry):
            m, alpha, part = carry
            part_next = score(keys(t, sub), slot)
            accumulate(keys(t - 2, sub), slot, alpha)
            return softmax(sub, 1 - slot, m, part) + (part_next,)

        def trip(n, carry):
            for u in range(GQA_STEPS_PER_TRIP):
                carry = step(2 + n * GQA_STEPS_PER_TRIP + u, u % 2, carry)
            return carry

        n_trips = (n_sub - 2) // GQA_STEPS_PER_TRIP
        carry = lax.fori_loop(0, n_trips, trip, (m, alpha, part1))
        for t in range(2 + n_trips * GQA_STEPS_PER_TRIP, n_sub):
            carry = step(t, t % 2, carry)
        m, alpha, part = carry
        accumulate(keys(n_sub - 2, sub), (n_sub - 2) % 2, alpha)
        m, alpha = softmax(sub, (n_sub - 1) % 2, m, part)
        accumulate(keys(n_sub - 1, sub), (n_sub - 1) % 2, alpha)

    acc = acc_ref[...]
    denom = acc[HEAD_DIM:HEAD_DIM + SUBLANES]
    o_t = (acc[:HEAD_DIM].reshape(HEAD_DIM // SUBLANES, SUBLANES, 2 * tq) / denom[None]).reshape(HEAD_DIM, 2 * tq)
    o_ref[...] = jnp.concatenate([o_t[:, :tq], o_t[:, tq:]], axis=0).T.astype(o_ref.dtype)


def _gqa_bounded_kernel(qt_ref, k_ref, vt_ref, o_ref, acc_ref, *p_ref, n_sub):
    tq = qt_ref.shape[1]
    qt = qt_ref[...]
    row = lax.broadcasted_iota(jnp.int32, qt.shape, 0)
    top = row < HEAD_DIM
    zero = jnp.zeros_like(qt)
    qt2 = jnp.concatenate([jnp.where(top, qt, zero), jnp.where(top, zero, qt)], axis=1)
    acc_ref[...] = jnp.zeros_like(acc_ref)
    sub = GQA_K_SUB
    n_slots = len(p_ref)
    lag = n_slots - 1
    steps = GQA_BOUNDED_STEPS_PER_TRIP
    assert steps % n_slots == 0 and n_sub > lag

    def keys(j):
        return pl.ds(pl.multiple_of(j * sub, sub), sub)

    def probs(j, slot):
        p_ref[slot][...] = jnp.exp2(_dot(k_ref[keys(j), :], qt2)).astype(BF16)

    def accumulate(j, slot):
        acc_ref[...] += _dot(vt_ref[:, keys(j)], p_ref[slot][...])

    for t in range(lag):
        probs(t, t % n_slots)

    def trip(n, carry):
        for u in range(steps):
            t = lag + n * steps + u
            probs(t, (lag + u) % n_slots)
            accumulate(t - lag, u % n_slots)
        return carry

    n_trips = (n_sub - lag) // steps
    lax.fori_loop(0, n_trips, trip, 0)
    for t in range(lag + n_trips * steps, n_sub):
        probs(t, t % n_slots)
        accumulate(t - lag, (t - lag) % n_slots)
    for t in range(n_sub - lag, n_sub):
        accumulate(t, t % n_slots)

    acc = acc_ref[...]
    denom = acc[HEAD_DIM:HEAD_DIM + SUBLANES]
    o_t = (acc[:HEAD_DIM].reshape(HEAD_DIM // SUBLANES, SUBLANES, 2 * tq) / denom[None]).reshape(HEAD_DIM, 2 * tq)
    o_ref[...] = jnp.concatenate([o_t[:, :tq], o_t[:, tq:]], axis=0).T.astype(o_ref.dtype)


def _gqa_call(gqt, gk, gvt, n_head, n_sub, tq, name, bounded=False):
    nq = gqt.shape[1]
    n_keys = n_head + n_sub * GQA_K_SUB
    buf_rows = max(n_head, GQA_K_SUB if n_sub else 0)
    assert n_head % LANES == 0 and nq % tq == 0
    if bounded:
        assert n_head == 0
        body = functools.partial(_gqa_bounded_kernel, n_sub=n_sub)
        buffers = [pltpu.VMEM((buf_rows, 2 * tq), BF16)] * (GQA_PV_LAG + 1)
    else:
        body = functools.partial(_gqa_kernel, n_head=n_head, n_sub=n_sub)
        buffers = [pltpu.VMEM((buf_rows, 2 * tq), F32)] * 2 + [pltpu.VMEM((buf_rows, 2 * tq), BF16)] * 2
    return pl.pallas_call(
        body,
        grid=(GQA_KV_HEADS, nq // tq),
        in_specs=[
            pl.BlockSpec((LANES, tq), lambda g, i: (g, i)),
            pl.BlockSpec((n_keys, LANES), lambda g, i: (0, g)),
            pl.BlockSpec((GQA_V_ROWS, n_keys), lambda g, i: (g, 0)),
        ],
        out_specs=pl.BlockSpec((tq, LANES), lambda g, i: (i, g)),
        out_shape=jax.ShapeDtypeStruct((nq, GQA_WIDTH), BF16),
        scratch_shapes=[pltpu.VMEM((GQA_V_ROWS, 2 * tq), F32)] + buffers,
        compiler_params=_params("arbitrary", "arbitrary"),
        name=name,
    )(gqt, gk, gvt)


def _gqa_attention(gqt, gk, gvt, n_ctx, score_bound):
    n_head = n_ctx % GQA_K_SUB
    n_sub = (gk.shape[0] - n_head) // GQA_K_SUB
    assert (gk.shape[0] - n_head) % GQA_K_SUB == 0
    y_ctx = _gqa_call(gqt[:, :n_ctx], gk, gvt, n_ctx, 0, n_ctx, "gqa_attention_ctx")
    q_lat = gqt[:, n_ctx:]

    def with_running_max(q, k, vt):
        return _gqa_call(q, k, vt, n_head, n_sub, GQA_Q_TILE, "gqa_attention")

    def without_running_max(q, k, vt):
        return _gqa_call(q, k, vt, 0, n_sub, GQA_Q_TILE, "gqa_attention_bounded", bounded=True)

    if n_head == 0:
        y_lat = lax.cond(score_bound <= GQA_BOUNDED_LOG2, without_running_max, with_running_max, q_lat, gk, gvt)
    else:
        y_lat = with_running_max(q_lat, gk, gvt)
    return jnp.concatenate([y_ctx, y_lat], axis=0)


def _out_ffn_kernel(ya_ref, yf_ref, yb_ref, xs_ref, z_ref, yg_ref, xc_ref, xl_ref, mod_ref,
                    dskip_ref, snw_ref, wo_ref, fnw_ref, wg_ref, wu_ref, wd_ref, final_ref,
                    o_ref, *, final, pick_ctx):
    x = jnp.where(pl.program_id(0) == 0, xc_ref[...], xl_ref[...]) if pick_ctx else xl_ref[...]
    y = yf_ref[...] + yb_ref[...] + dskip_ref[...] * xs_ref[...]
    y = y * _silu(z_ref[...])
    ms = jnp.mean(y * y, axis=-1, keepdims=True)
    y = y * lax.rsqrt(ms + EPS) * snw_ref[...]
    mix = jnp.concatenate([ya_ref[...], y.astype(BF16), yg_ref[...]], axis=-1)
    g_m = mod_ref[:, 2 * D_MODEL:3 * D_MODEL]
    sh_f = mod_ref[:, 3 * D_MODEL:4 * D_MODEL]
    sc_f = mod_ref[:, 4 * D_MODEL:5 * D_MODEL]
    g_f = mod_ref[:, 5 * D_MODEL:6 * D_MODEL]
    x1 = x + g_m * _dot(mix, wo_ref[...])
    ms1 = jnp.mean(x1 * x1, axis=-1, keepdims=True)
    hf = (x1 * lax.rsqrt(ms1 + EPS) * fnw_ref[...] * (1.0 + sc_f) + sh_f).astype(BF16)
    act = (_silu(_dot(hf, wg_ref[...])) * _dot(hf, wu_ref[...])).astype(BF16)
    x2 = x1 + g_f * _dot(act, wd_ref[...])
    if final:
        ms2 = jnp.mean(x2 * x2, axis=-1, keepdims=True)
        x2 = x2 * lax.rsqrt(ms2 + EPS) * final_ref[...]
    o_ref[...] = x2


def _out_ffn(ya, yf, yb, xconv, z, yg, x_parts, mod4, layer, d_full, ssd_nw, w_out, ffn_nw,
             w_gate, w_up, w_down, final_nw, final):
    t = ya.shape[0]
    skip = 1 if final else 0
    n = t // ROW_TILE - skip
    row = lambda i: (i + skip, 0)
    const = lambda i: (0, 0)
    x_arrays, x_specs, pick_ctx = _residual_operands(x_parts, skip)
    return pl.pallas_call(
        functools.partial(_out_ffn_kernel, final=final, pick_ctx=pick_ctx),
        grid=(n,),
        in_specs=[
            pl.BlockSpec((ROW_TILE, NA_WIDTH), row),
            pl.BlockSpec((ROW_TILE, SSD_WIDTH), row),
            pl.BlockSpec((ROW_TILE, SSD_WIDTH), row),
            pl.BlockSpec((ROW_TILE, SSD_WIDTH), row),
            pl.BlockSpec((ROW_TILE, SSD_WIDTH), row),
            pl.BlockSpec((ROW_TILE, GQA_WIDTH), row),
        ] + x_specs + [
            pl.BlockSpec((None, None, 1, 6 * D_MODEL), lambda i: (layer, jnp.minimum(i + skip, 1), 0, 0)),
            pl.BlockSpec((1, SSD_WIDTH), const),
            pl.BlockSpec((1, SSD_WIDTH), const),
            _resident((D_MODEL, D_MODEL), const),
            pl.BlockSpec((1, D_MODEL), const),
            _resident((D_MODEL, FFN_HIDDEN), const),
            _resident((D_MODEL, FFN_HIDDEN), const),
            _resident((FFN_HIDDEN, D_MODEL), const),
            pl.BlockSpec((1, D_MODEL), const),
        ],
        out_specs=pl.BlockSpec((ROW_TILE, D_MODEL), lambda i: (i, 0)),
        out_shape=jax.ShapeDtypeStruct((n * ROW_TILE, D_MODEL), F32),
        compiler_params=_params("parallel"),
        name="out_ffn",
    )(ya, yf, yb, xconv, z, yg, *x_arrays, mod4, d_full, ssd_nw, w_out, ffn_nw, w_gate, w_up, w_down, final_nw)


def _rearranged_w_in(w):
    na_in = 3 * NA_WIDTH
    o_z = na_in
    o_xbc = o_z + SSD_WIDTH
    o_dt = o_xbc + SSD_CONV_CH
    o_gq = o_dt + 2 * SSD_HEADS
    o_gk = o_gq + GQA_WIDTH
    o_gv = o_gk + GQA_KV_HEADS * HEAD_DIM
    na = jnp.concatenate([w[:, :NA_WIDTH] * ATTN_SCALE, w[:, NA_WIDTH:na_in]], axis=1)
    dt = jnp.pad(w[:, o_dt:o_gq], ((0, 0), (0, LANES - 2 * SSD_HEADS)))

    k_heads = [w[:, o_gk + h * HEAD_DIM:o_gk + (h + 1) * HEAD_DIM] for h in range(GQA_KV_HEADS)]
    k_twice = jnp.concatenate([p for h in k_heads for p in (h, h)], axis=1)
    return jnp.concatenate(
        [na, w[:, o_z:o_xbc], w[:, o_xbc:o_dt], dt, w[:, o_gq:o_gk], k_twice, w[:, o_gv:]], axis=1).astype(BF16)


def _rope_tables(n_ctx, n_lat):
    freqs = ROPE_THETA ** (-jnp.arange(ROPE_PAIRS, dtype=F32) / ROPE_PAIRS)
    n_rows = n_lat // GRID_W
    half = 2 * ROPE_PAIRS
    sign = jnp.where(jnp.arange(half) < ROPE_PAIRS, -1.0, 1.0).astype(F32)

    def tables(n_pos):
        a = jnp.arange(n_pos, dtype=F32)[:, None] * freqs[None, :]
        a = jnp.concatenate([a, a], axis=-1)
        return jnp.cos(a), jnp.sin(a) * sign[None, :]

    def per_token(by_row, by_col):
        lat = jnp.concatenate([jnp.broadcast_to(by_row[:, None, :], (n_rows, GRID_W, half)),
                               jnp.broadcast_to(by_col[None, :, :], (n_rows, GRID_W, half))], axis=-1)
        return lat.reshape(n_lat, HEAD_DIM)

    cos_r, sin_r = tables(n_rows)
    cos_c, sin_c = tables(GRID_W)
    cos = jnp.concatenate([jnp.ones((n_ctx, HEAD_DIM), F32), per_token(cos_r, cos_c)], axis=0)
    sin = jnp.concatenate([jnp.zeros((n_ctx, HEAD_DIM), F32), per_token(sin_r, sin_c)], axis=0)
    return jnp.tile(cos, (1, LANES // HEAD_DIM)), jnp.tile(sin, (1, LANES // HEAD_DIM))


def kernel(x, c, ctx, c_ctx, mod_w, mod_b, norm_attn_w, norm_ffn_w, w_in, na_rpb, ssd_conv_w, ssd_conv_b,
           ssd_dt_bias, ssd_a_log, ssd_d, ssd_norm_w, q_norm_w, k_norm_w, w_out, ffn_w_gate, ffn_w_up,
           ffn_w_down, final_norm_w):
    depth = mod_w.shape[0]
    batch, n_lat, _ = x.shape
    n_ctx = ctx.shape[1]
    assert batch == 1 and n_ctx == ROW_TILE and n_lat % (NA_ROWS_PER_STEP * GRID_W) == 0
    assert n_lat % GQA_Q_TILE == 0 and n_lat // GRID_W >= NA_WIN_ROWS

    x_parts = (ctx[0], x[0])
    cc = jnp.zeros((SUBLANES, D_MODEL), F32).at[0].set(c_ctx).at[1].set(c[0])
    mod = _modulation(cc, mod_w, mod_b)
    mod4 = mod[:, :2].reshape(depth, 2, 1, 6 * D_MODEL)

    cos_t, sin_t = _rope_tables(n_ctx, n_lat)
    blk = jnp.arange(2 * GQA_WIDTH) // HEAD_DIM
    ones_bd = (blk[:, None] == blk[None, :]).astype(BF16)
    head_cols = jnp.arange(SSD_WIDTH) // HEAD_DIM

    for i in range(depth):
        final = i == depth - 1
        w_cat = _rearranged_w_in(w_in[i])
        dt_bias_pad = jnp.pad(ssd_dt_bias[i].reshape(1, -1), ((0, 0), (0, LANES - 2 * SSD_HEADS)))
        qk_w = jnp.concatenate([jnp.tile(q_norm_w[i] * (ATTN_SCALE * LOG2E), GQA_Q_HEADS),
                                jnp.tile(k_norm_w[i], 2 * GQA_KV_HEADS)]).reshape(1, -1)
        conv_w_pad = jnp.pad(ssd_conv_w[i], ((0, SUBLANES - SSD_CONV), (0, 0)))
        na, z, xconv, dt, gq, gk, gv = _inproj(x_parts, n_ctx + n_lat, mod4, i, norm_attn_w[i].reshape(1, -1),
                                               w_cat, dt_bias_pad, qk_w, cos_t, sin_t, ones_bd, conv_w_pad,
                                               ssd_conv_b[i].reshape(1, -1))
        a_row = jnp.pad(-jnp.exp(ssd_a_log[i].astype(F32)).reshape(1, -1), ((0, 0), (0, LANES - 2 * SSD_HEADS)))
        yf, yb = _ssd_scan(xconv, dt, a_row, n_ctx)
        ya = _neighbourhood_attention(na, _na_bias_table(na_rpb[i]), n_ctx)
        ones_rows = jnp.ones((GQA_V_ROWS - HEAD_DIM, gv.shape[0]), BF16)
        gvt = jnp.concatenate([part for g in range(GQA_KV_HEADS)
                               for part in (gv[:, g * HEAD_DIM:(g + 1) * HEAD_DIM].T, ones_rows)], axis=0)
        score_bound = (1.02 * HEAD_DIM * ATTN_SCALE * LOG2E) * jnp.max(jnp.abs(q_norm_w[i])) * jnp.max(
            jnp.abs(k_norm_w[i]))
        yg = _gqa_attention(gq.T, gk, gvt, n_ctx, score_bound)
        d_full = ssd_d[i].astype(F32)[head_cols].reshape(1, -1)
        x_parts = _out_ffn(ya, yf, yb, xconv, z, yg, x_parts, mod4, i, d_full, ssd_norm_w[i].reshape(1, -1),
                           w_out[i].astype(BF16), norm_ffn_w[i].reshape(1, -1), ffn_w_gate[i].astype(BF16),
                           ffn_w_up[i].astype(BF16), ffn_w_down[i].astype(BF16),
                           final_norm_w.reshape(1, -1), final)
    return x_parts[None]
```

```python
import functools
import math

import jax
import jax.numpy as jnp
import numpy as np
from jax import lax
from jax.experimental import pallas as pl
from jax.experimental.pallas import tpu as pltpu

F32 = jnp.float32
BF16 = jnp.bfloat16

D_MODEL = 1024
GRID_W = 64
HEAD_DIM = 64
NA_WIDTH = 256
NA_HEADS = 4
NA_WIN_ROWS = 8
NA_WIN_COLS = 16
SSD_WIDTH = 512
SSD_HEADS = 8
SSD_GROUPS = 2
SSD_STATE = 128
SSD_CONV = 5
SSD_CHUNK = 128
SSD_CONV_CH = SSD_WIDTH + 2 * SSD_GROUPS * SSD_STATE
GQA_WIDTH = 256
GQA_Q_HEADS = 4
GQA_KV_HEADS = 2
ROPE_THETA = 10000.0
ROPE_PAIRS = HEAD_DIM // 4
FFN_HIDDEN = 2816
EPS = 1e-6
ATTN_SCALE = HEAD_DIM ** -0.5
LOG2E = math.log2(math.e)

LANES = 128
SUBLANES = 8
VMEM_LIMIT_BYTES = 56 * 1024 * 1024

ROW_TILE = 256
MOD_COL_TILE = 2048
SSD_CHUNKS_PER_STEP = 2
NA_ROWS_PER_STEP = 4
GQA_Q_TILE = 256
GQA_V_ROWS = HEAD_DIM + 16
GQA_K_SUB = 256
GQA_BOUNDED_LOG2 = 60.0
GQA_PV_LAG = 2
GQA_BOUNDED_STEPS_PER_TRIP = 9
GQA_STEPS_PER_TRIP = 8

C_NA = 0
C_Z = C_NA + 3 * NA_WIDTH
C_XBC = C_Z + SSD_WIDTH
C_DT = C_XBC + SSD_CONV_CH
C_GQ = C_DT + LANES
C_GK = C_GQ + GQA_WIDTH
C_GV = C_GK + 2 * LANES
C_END = C_GV + LANES


def _silu(v):
    return v * (1.0 / (1.0 + jnp.exp(-v)))


def _softplus(v):
    return jnp.maximum(v, 0.0) + jnp.log(1.0 + jnp.exp(-jnp.abs(v)))


def _split3(v):
    hi = v.astype(BF16)
    r1 = v - hi.astype(F32)
    mid = r1.astype(BF16)
    lo = (r1 - mid.astype(F32)).astype(BF16)
    return hi, mid, lo


def _dot(a, b):
    return jnp.dot(a, b, preferred_element_type=F32)


def _dot_nt(a, b):
    return lax.dot_general(a, b, (((1,), (1,)), ((), ())), preferred_element_type=F32)


def _dot_tn(a, b):
    return lax.dot_general(a, b, (((0,), (0,)), ((), ())), preferred_element_type=F32)


def _exact_dot(v, sel):
    hi, mid, lo = _split3(v)
    return _dot(hi, sel) + _dot(mid, sel) + _dot(lo, sel)


def _spread_dot(v, sel):
    hi = v.astype(BF16)
    lo = (v - hi.astype(F32)).astype(BF16)
    return _dot(hi, sel) + _dot(lo, sel)


def _exact_dot_lhs(sel, v):
    hi, mid, lo = _split3(v)
    return _dot(sel, hi) + _dot(sel, mid) + _dot(sel, lo)


def _params(*sem):
    return pltpu.CompilerParams(dimension_semantics=sem, vmem_limit_bytes=VMEM_LIMIT_BYTES)


def _resident(shape, index_map):
    return pl.BlockSpec(shape, index_map, pipeline_mode=pl.Buffered(1))


def _layer_spec(stacked, layer, resident=False):
    shape = stacked.shape[1:]
    index_map = lambda *_: (layer,) + (0,) * len(shape)
    if resident:
        return pl.BlockSpec((None,) + shape, index_map, pipeline_mode=pl.Buffered(1))
    return pl.BlockSpec((None,) + shape, index_map)


def _mod_kernel(cc_ref, w_ref, b_ref, o_ref):
    a = _silu(cc_ref[...])
    o_ref[0] = jnp.dot(a, w_ref[0], preferred_element_type=F32) + b_ref[0]


def _modulation(cc, mod_w, mod_b):
    depth = mod_w.shape[0]
    cols = MOD_COL_TILE
    ncol = mod_w.shape[2] // cols
    return pl.pallas_call(
        _mod_kernel,
        grid=(depth, ncol),
        in_specs=[
            pl.BlockSpec((SUBLANES, D_MODEL), lambda l, j: (0, 0)),
            pl.BlockSpec((1, D_MODEL, cols), lambda l, j: (l, 0, j)),
            pl.BlockSpec((1, 1, cols), lambda l, j: (l, 0, j)),
        ],
        out_specs=pl.BlockSpec((1, SUBLANES, cols), lambda l, j: (l, 0, j)),
        out_shape=jax.ShapeDtypeStruct((depth, SUBLANES, ncol * cols), F32),
        compiler_params=_params("arbitrary", "arbitrary"),
        name="modulation",
    )(cc, mod_w, mod_b.reshape(depth, 1, -1))


def _residual_operands(x_parts, skip):
    ctx_spec = pl.BlockSpec((ROW_TILE, D_MODEL), lambda i: (0, 0))
    if isinstance(x_parts, tuple):
        lat_spec = pl.BlockSpec((ROW_TILE, D_MODEL), lambda i: (jnp.maximum(i + skip - 1, 0), 0))
        return list(x_parts), [ctx_spec, lat_spec], skip == 0
    row_spec = pl.BlockSpec((ROW_TILE, D_MODEL), lambda i: (i + skip, 0))
    return [x_parts, x_parts], [ctx_spec, row_spec], False


def _inproj_kernel(xc_ref, xl_ref, prev_ref, next_ref, mod_ref, nw_ref, w_ref, dtb_ref, qkw_ref, cos_ref, sin_ref,
                   ones_ref, cw_ref, cb_ref, na_ref, z_ref, xconv_ref, dt_ref, gq_ref, gk_ref, gv_ref, *, pick_ctx):
    i = pl.program_id(0)
    n = pl.num_programs(0)
    rows = xl_ref.shape[0]
    x = jnp.where(i == 0, xc_ref[...], xl_ref[...]) if pick_ctx else xl_ref[...]
    x = jnp.concatenate([prev_ref[...], x, next_ref[...]], axis=0)
    ms = jnp.mean(x * x, axis=-1, keepdims=True)
    xn = x * lax.rsqrt(ms + EPS) * nw_ref[...]
    sh = mod_ref[:, 0:D_MODEL]
    sc = mod_ref[:, D_MODEL:2 * D_MODEL]
    h = (xn * (1.0 + sc) + sh).astype(BF16)
    xbc_ext = _dot(h, w_ref[:, C_XBC:C_DT])
    h_tile = h[SUBLANES:SUBLANES + rows]
    u = jnp.concatenate([_dot(h_tile, w_ref[:, C_NA:C_XBC]), xbc_ext[SUBLANES:SUBLANES + rows],
                         _dot(h_tile, w_ref[:, C_DT:C_END])], axis=-1)
    na_ref[...] = u[:, C_NA:C_Z].astype(BF16)
    z_ref[...] = u[:, C_Z:C_XBC]
    has_prev = i >= 2
    has_next = jnp.logical_and(i >= 1, i < n - 1)
    xbc = u[:, C_XBC:C_DT]
    ext = jnp.concatenate([jnp.where(has_prev, xbc_ext[0:SUBLANES], 0.0), xbc,
                           jnp.where(has_next, xbc_ext[SUBLANES + rows:], 0.0)], axis=0)
    total = rows + 2 * SUBLANES
    half = SSD_CONV // 2
    acc = cb_ref[...] + cw_ref[half:half + 1, :] * xbc
    for j in range(SSD_CONV):
        if j != half:
            shifted = pltpu.roll(ext, (half - j) % total, 0)
            acc = acc + cw_ref[j:j + 1, :] * shifted[SUBLANES:SUBLANES + rows]
    xconv_ref[...] = _silu(acc)
    dt_ref[...] = _softplus(u[:, C_DT:C_GQ] + dtb_ref[...])
    gv_ref[...] = u[:, C_GV:C_END].astype(BF16)
    g = u[:, C_GQ:C_GV]
    gsq = g * g
    hi = gsq.astype(BF16)
    lo = (gsq - hi.astype(F32)).astype(BF16)
    ss = _dot(hi, ones_ref[...]) + _dot(lo, ones_ref[...])
    gn = g * lax.rsqrt(ss * (1.0 / HEAD_DIM) + EPS) * qkw_ref[...]
    width = gn.shape[-1]
    lane = lax.broadcasted_iota(jnp.int32, gn.shape, 1)
    first = (lane % (2 * ROPE_PAIRS)) < ROPE_PAIRS
    partner = jnp.where(first, pltpu.roll(gn, width - ROPE_PAIRS, 1), pltpu.roll(gn, ROPE_PAIRS, 1))
    cos = jnp.concatenate([cos_ref[...]] * (width // LANES), axis=-1)
    sin = jnp.concatenate([sin_ref[...]] * (width // LANES), axis=-1)
    gr = gn * cos + partner * sin
    gq_ref[...] = gr[:, :GQA_WIDTH].astype(BF16)
    gk_ref[...] = gr[:, GQA_WIDTH:].astype(BF16)


def _inproj(x_parts, t, mod4, layer, norm_w, w_cat, dt_bias_pad, qk_w, cos_t, sin_t, ones_bd, conv_w_pad, conv_b):
    n = t // ROW_TILE
    row = lambda i: (i, 0)
    const = lambda i: (0, 0)
    outs = [
        (3 * NA_WIDTH, BF16), (SSD_WIDTH, F32), (SSD_CONV_CH, F32), (LANES, F32),
        (GQA_WIDTH, BF16), (2 * LANES, BF16), (LANES, BF16),
    ]
    x_arrays, x_specs, pick_ctx = _residual_operands(x_parts, 0)
    halo_src = x_arrays[1]
    per = ROW_TILE // SUBLANES
    first = (lambda i: (i - 1) * per) if isinstance(x_parts, tuple) else (lambda i: i * per)
    last_blk = halo_src.shape[0] // SUBLANES - 1
    halo_specs = [
        pl.BlockSpec((SUBLANES, D_MODEL), lambda i: (jnp.clip(first(i) - 1, 0, last_blk), 0)),
        pl.BlockSpec((SUBLANES, D_MODEL), lambda i: (jnp.clip(first(i) + per, 0, last_blk), 0)),
    ]
    return pl.pallas_call(
        functools.partial(_inproj_kernel, pick_ctx=pick_ctx),
        grid=(n,),
        in_specs=x_specs + halo_specs + [
            pl.BlockSpec((None, None, 1, 6 * D_MODEL), lambda i: (layer, jnp.minimum(i, 1), 0, 0)),
            _layer_spec(norm_w, layer),
            _layer_spec(w_cat, layer, resident=True),
            _layer_spec(dt_bias_pad, layer),
            _layer_spec(qk_w, layer),
            pl.BlockSpec((ROW_TILE, LANES), row),
            pl.BlockSpec((ROW_TILE, LANES), row),
            _resident((2 * GQA_WIDTH, 2 * GQA_WIDTH), const),
            _layer_spec(conv_w_pad, layer),
            _layer_spec(conv_b, layer),
        ],
        out_specs=[pl.BlockSpec((ROW_TILE, w), row) for w, _ in outs],
        out_shape=[jax.ShapeDtypeStruct((t, w), d) for w, d in outs],
        compiler_params=_params("parallel"),
        name="inproj",
    )(*x_arrays, halo_src, halo_src, mod4, norm_w, w_cat, dt_bias_pad, qk_w, cos_t, sin_t, ones_bd,
      conv_w_pad, conv_b)


def _ssd_direction(xbc, dt, a_row, tri, expand, s_ref, lane0, backward):
    q = SSD_CHUNK
    x = xbc[:, :SSD_WIDTH]
    nb = SSD_GROUPS * SSD_STATE
    bmat = xbc[:, SSD_WIDTH:SSD_WIDTH + nb].astype(BF16)
    cmat = xbc[:, SSD_WIDTH + nb:].astype(BF16)
    cum = _exact_dot_lhs(tri, dt * a_row)
    yield
    last = 0 if backward else q - 1
    cum_t = cum.T

    def spread(mat_t):
        rows = [jnp.broadcast_to(mat_t[lane0 + h:lane0 + h + 1, :], (HEAD_DIM, q)) for h in range(SSD_HEADS)]
        return jnp.concatenate(rows, axis=0).T

    dt_full = _spread_dot(dt, expand)
    cum_full = spread(cum_t)
    yield
    ea_full = jnp.exp(cum_full)
    te_full = jnp.exp(cum_full[last:last + 1, :] - cum_full)
    xr = x * dt_full
    xr_b = xr.astype(BF16)
    xt_b = (xr * te_full).astype(BF16)
    ti =lax.broadcasted_iota(jnp.int32, (q, q), 0)
    si = lax.broadcasted_iota(jnp.int32, (q, q), 1)
    keep = (si >= ti) if backward else (si <= ti)
    lane = lax.broadcasted_iota(jnp.int32, (q, LANES), 1)
    heads_per_group = SSD_HEADS // SSD_GROUPS
    width_g = heads_per_group * HEAD_DIM
    pieces = []
    decay_total = ea_full[last:last + 1, :]
    for g in range(SSD_GROUPS):
        bg = bmat[:, g * SSD_STATE:(g + 1) * SSD_STATE]
        cg = cmat[:, g * SSD_STATE:(g + 1) * SSD_STATE]
        gmat = _dot_nt(cg, bg)
        s_old = s_ref[:, g * width_g:(g + 1) * width_g]
        y_off = _dot(cg, s_old.astype(BF16)) * ea_full[:, g * width_g:(g + 1) * width_g]
        s_new = _dot_tn(bg, xt_b[:, g * width_g:(g + 1) * width_g])
        s_ref[:, g * width_g:(g + 1) * width_g] = decay_total[:, g * width_g:(g + 1) * width_g] * s_old + s_new
        yield
        for pair in range(heads_per_group // 2):
            col = g * width_g + pair * LANES
            xr_pair = xr_b[:, col:col + LANES]
            ys = []
            for r in range(2):
                hl = lane0 + g * heads_per_group + 2 * pair + r
                seg = jnp.broadcast_to(cum[:, hl:hl + 1], (q, q)) - jnp.broadcast_to(cum_t[hl:hl + 1, :], (q, q))
                dec = jnp.exp(jnp.where(keep, seg, -jnp.inf))
                ys.append(_dot((gmat * dec).astype(BF16), xr_pair))
            y_diag = jnp.where(lane < HEAD_DIM, ys[0], ys[1])
            pieces.append(y_diag + y_off[:, pair * LANES:(pair + 1) * LANES])
            yield
    return jnp.concatenate(pieces, axis=-1)


def _run_interleaved(*stage_generators):
    results = [None] * len(stage_generators)
    live = list(range(len(stage_generators)))
    while live:
        for idx in list(live):
            try:
                next(stage_generators[idx])
            except StopIteration as done:
                results[idx] = done.value
                live.remove(idx)
    return results


def _ssd_kernel(xf_ref, dtf_ref, xb_ref, dtb_ref, a_ref, tril_ref, triu_ref, ef_ref, eb_ref,
                yf_ref, yb_ref, sf_ref, sb_ref):
    @pl.when(pl.program_id(0) == 0)
    def _():
        sf_ref[...] = jnp.zeros_like(sf_ref)
        sb_ref[...] = jnp.zeros_like(sb_ref)

    a_row = a_ref[...]
    q = SSD_CHUNK
    order_f = list(range(SSD_CHUNKS_PER_STEP))
    order_b = order_f[::-1]
    stages = []
    for cf, cb in zip(order_f, order_b):
        rf, rb = slice(cf * q, (cf + 1) * q), slice(cb * q, (cb + 1) * q)
        stages.append(_ssd_direction(xf_ref[rf, :], dtf_ref[rf, :], a_row, tril_ref[...], ef_ref[...],
                                     sf_ref, 0, False))
        stages.append(_ssd_direction(xb_ref[rb, :], dtb_ref[rb, :], a_row, triu_ref[...], eb_ref[...],
                                     sb_ref, SSD_HEADS, True))
    ys = _run_interleaved(*stages)
    for k, (cf, cb) in enumerate(zip(order_f, order_b)):
        yf_ref[cf * q:(cf + 1) * q, :] = ys[2 * k]
        yb_ref[cb * q:(cb + 1) * q, :] = ys[2 * k + 1]


def _ssd_scan(xconv, dt, a_row, layer, n_ctx):
    t = xconv.shape[0]
    q = SSD_CHUNK
    rows = SSD_CHUNKS_PER_STEP * q
    assert n_ctx % rows == 0 and t % rows == 0
    n = t // rows
    n_ctx_blocks = n_ctx // rows
    r = jnp.arange(q)
    tril = (r[None, :] <= r[:, None]).astype(BF16)
    triu = (r[None, :] >= r[:, None]).astype(BF16)
    lanes = jnp.arange(LANES)[:, None]
    head_of_col = (jnp.arange(SSD_WIDTH) // HEAD_DIM)[None, :]
    expand_f = (lanes == head_of_col).astype(BF16)
    expand_b = (lanes == head_of_col + SSD_HEADS).astype(BF16)

    def fwd(i):
        return (i, 0)

    def bwd(i):
        return (jnp.where(i < n_ctx_blocks, n_ctx_blocks - 1 - i, n - 1 - (i - n_ctx_blocks)), 0)

    const = lambda i: (0, 0)
    return pl.pallas_call(
        _ssd_kernel,
        grid=(n,),
        in_specs=[
            pl.BlockSpec((rows, SSD_CONV_CH), fwd),
            pl.BlockSpec((rows, LANES), fwd),
            pl.BlockSpec((rows, SSD_CONV_CH), bwd),
            pl.BlockSpec((rows, LANES), bwd),
            _layer_spec(a_row, layer),
            pl.BlockSpec((q, q), const),
            pl.BlockSpec((q, q), const),
            pl.BlockSpec((LANES, SSD_WIDTH), const),
            pl.BlockSpec((LANES, SSD_WIDTH), const),
        ],
        out_specs=[pl.BlockSpec((rows, SSD_WIDTH), fwd), pl.BlockSpec((rows, SSD_WIDTH), bwd)],
        out_shape=[jax.ShapeDtypeStruct((t, SSD_WIDTH), F32)] * 2,
        scratch_shapes=[pltpu.VMEM((SSD_STATE, SSD_WIDTH), F32)] * 2,
        compiler_params=_params("arbitrary"),
        name="ssd_scan",
    )(xconv, dt, xconv, dt, a_row, tril, triu, expand_f, expand_b)


def _na_kernel(q_ref, k_ref, v_ref, bias_ref, o_ref, *, n_ctx, n_rows):
    i = pl.program_id(0)
    is_ctx = i == 0
    kc = k_ref[0:n_ctx, :]
    vc = v_ref[0:n_ctx, :]
    lane = lax.broadcasted_iota(jnp.int32, (GRID_W, NA_WIDTH), 1)
    mine = [(lane >= h * HEAD_DIM) & (lane < (h + 1) * HEAD_DIM) for h in range(NA_HEADS)]
    win = NA_WIN_ROWS * GRID_W

    def lane_tiles(a):
        return [a[:, c * LANES:(c + 1) * LANES] for c in range(a.shape[-1] // LANES)]

    def grid_row(j):
        r = jnp.maximum((i - 1) * NA_ROWS_PER_STEP + j, 0)
        r_start = jnp.clip(r - NA_WIN_ROWS // 2, 0, n_rows - NA_WIN_ROWS)
        variant = jnp.where(is_ctx, NA_WIN_ROWS, r_start - r + NA_WIN_ROWS - 1)
        start = pl.multiple_of(n_ctx + r_start * GRID_W, GRID_W)
        kw = k_ref[pl.ds(start, win), :]
        vw = v_ref[pl.ds(start, win), :]
        qj = q_ref[j * GRID_W:(j + 1) * GRID_W, :]
        qm = jnp.concatenate([jnp.where(mine[h], qj, jnp.zeros_like(qj)) for h in range(NA_HEADS)], axis=0)
        s_w = _dot_nt(qm, kw) + bias_ref[variant]
        s_c = _dot_nt(qm, kc)
        yield
        m = functools.reduce(jnp.maximum, lane_tiles(s_w) + lane_tiles(s_c))
        m = jnp.broadcast_to(jnp.max(m, axis=-1, keepdims=True), m.shape)
        p_w = jnp.exp(s_w - jnp.concatenate([m] * (s_w.shape[-1] // LANES), axis=-1))
        p_c = jnp.exp(s_c - jnp.concatenate([m] * (s_c.shape[-1] // LANES), axis=-1))
        l = jnp.sum(functools.reduce(jnp.add, lane_tiles(p_w) + lane_tiles(p_c)), axis=-1, keepdims=True)
        y = (_dot(p_w.astype(BF16), vw) + _dot(p_c.astype(BF16), vc)) * (1.0 / l)
        out = y[(NA_HEADS - 1) * GRID_W:]
        for h in range(NA_HEADS - 2, -1, -1):
            out = jnp.where(mine[h], y[h * GRID_W:(h + 1) * GRID_W], out)
        o_ref[j * GRID_W:(j + 1) * GRID_W, :] = out.astype(o_ref.dtype)

    _run_interleaved(*[grid_row(j) for j in range(NA_ROWS_PER_STEP)])


def _na_bias_table(rpb):
    depth = rpb.shape[0]
    col = np.arange(GRID_W)
    c_start = np.clip(col - NA_WIN_COLS // 2, 0, GRID_W - NA_WIN_COLS)
    in_win = (col[None, :] >= c_start[:, None]) & (col[None, :] < c_start[:, None] + NA_WIN_COLS)
    dc = np.clip(col[None, :] - col[:, None] + NA_WIN_COLS - 1, 0, 2 * NA_WIN_COLS - 2)
    n_dc = 2 * NA_WIN_COLS - 1
    n_dr = 2 * NA_WIN_ROWS - 1
    onehot = (dc[None, :, :] == np.arange(n_dc)[:, None, None]).astype(np.float32).reshape(n_dc, GRID_W * GRID_W)
    t2 = jnp.dot(rpb.reshape(depth * NA_HEADS * n_dr, n_dc).astype(F32), onehot, precision=lax.Precision.HIGHEST)
    t2 = jnp.where(in_win[None, None, None], t2.reshape(depth, NA_HEADS, n_dr, GRID_W, GRID_W), -jnp.inf)
    tab = jnp.stack([t2[:, :, v:v + NA_WIN_ROWS] for v in range(NA_WIN_ROWS)], axis=1)
    tab = tab.transpose(0, 1, 2, 4, 3, 5).reshape(depth, NA_WIN_ROWS, NA_HEADS * GRID_W, NA_WIN_ROWS * GRID_W)
    masked = jnp.full((depth, 1) + tab.shape[2:], -jnp.inf, F32)
    return jnp.concatenate([tab, masked], axis=1)


def _neighbourhood_attention(na, bias_tab, layer, n_ctx):
    t = na.shape[0]
    n_rows = (t - n_ctx) // GRID_W
    step_rows = NA_ROWS_PER_STEP * GRID_W
    n = t // step_rows
    return pl.pallas_call(
        functools.partial(_na_kernel, n_ctx=n_ctx, n_rows=n_rows),
        grid=(n,),
        in_specs=[
            pl.BlockSpec((step_rows, NA_WIDTH), lambda i: (i, 0)),
            _resident((t, NA_WIDTH), lambda i: (0, 1)),
            _resident((t, NA_WIDTH), lambda i: (0, 2)),
            _layer_spec(bias_tab, layer, resident=True),
        ],
        out_specs=pl.BlockSpec((step_rows, NA_WIDTH), lambda i: (i, 0)),
        out_shape=jax.ShapeDtypeStruct((t, NA_WIDTH), BF16),
        compiler_params=_params("parallel"),
        name="neighbourhood_attention",
    )(na, na, na, bias_tab)


def _gqa_kernel(qt_ref, k_ref, vt_ref, o_ref, acc_ref, s0_ref, s1_ref, p0_ref, p1_ref, *, n_head, n_sub):
    tq = qt_ref.shape[1]
    qt = qt_ref[...]
    row = lax.broadcasted_iota(jnp.int32, qt.shape, 0)
    top = row < HEAD_DIM
    zero = jnp.zeros_like(qt)
    qt2 = jnp.concatenate([jnp.where(top, qt, zero), jnp.where(top, zero, qt)], axis=1)
    acc_ref[...] = jnp.zeros_like(acc_ref)
    s_ref = (s0_ref, s1_ref)
    p_ref = (p0_ref, p1_ref)

    def keys(j, size):
        return pl.ds(pl.multiple_of(n_head + j * size, LANES), size)

    def score(key_rows, slot):
        size = key_rows.size
        s = _dot(k_ref[key_rows, :], qt2)
        s_ref[slot][0:size, :] = s
        return jnp.max(s.reshape(size // SUBLANES, SUBLANES, 2 * tq), axis=0)

    def softmax(size, slot, m_old, part_max):
        m_new = jnp.maximum(m_old, jnp.max(part_max, axis=0, keepdims=True))
        s = s_ref[slot][0:size, :].reshape(size // SUBLANES, SUBLANES, 2 * tq)
        p_ref[slot][0:size, :] = jnp.exp2(s - m_new[None]).reshape(size, 2 * tq).astype(BF16)
        return m_new, jnp.exp2(m_old - m_new)

    def accumulate(key_rows, slot, alpha):
        size = key_rows.size
        acc = acc_ref[...].reshape(GQA_V_ROWS // SUBLANES, SUBLANES, 2 * tq) * alpha[None]
        acc_ref[...] = acc.reshape(GQA_V_ROWS, 2 * tq) + _dot(vt_ref[:, key_rows], p_ref[slot][0:size, :])

    m = jnp.full((SUBLANES, 2 * tq), -jnp.inf, F32)
    if n_head:
        head = pl.ds(0, n_head)
        m, alpha = softmax(n_head, 0, m, score(head, 0))
        accumulate(head, 0, alpha)

    sub = GQA_K_SUB
    if n_sub:
        assert n_sub >= 3
        part0 = score(keys(0, sub), 0)
        part1 = score(keys(1, sub), 1)
        m, alpha = softmax(sub, 0, m, part0)

        def step(t, slot, carry):
            m, alpha, part = carry
            part_next = score(keys(t, sub), slot)
            accumulate(keys(t - 2, sub), slot, alpha)
            return softmax(sub, 1 - slot, m, part) + (part_next,)

        def trip(n, carry):
            for u in range(GQA_STEPS_PER_TRIP):
                carry = step(2 + n * GQA_STEPS_PER_TRIP + u, u % 2, carry)
            return carry

        n_trips = (n_sub - 2) // GQA_STEPS_PER_TRIP
        carry = lax.fori_loop(0, n_trips, trip, (m, alpha, part1))
        for t in range(2 + n_trips * GQA_STEPS_PER_TRIP, n_sub):
            carry = step(t, t % 2, carry)
        m, alpha, part = carry
        accumulate(keys(n_sub - 2, sub), (n_sub - 2) % 2, alpha)
        m, alpha = softmax(sub, (n_sub - 1) % 2, m, part)
        accumulate(keys(n_sub - 1, sub), (n_sub - 1) % 2, alpha)

    acc = acc_ref[...]
    denom = acc[HEAD_DIM:HEAD_DIM + SUBLANES]
    o_t = (acc[:HEAD_DIM].reshape(HEAD_DIM // SUBLANES, SUBLANES, 2 * tq) / denom[None]).reshape(HEAD_DIM, 2 * tq)
    o_ref[...] = jnp.concatenate([o_t[:, :tq], o_t[:, tq:]], axis=0).T.astype(o_ref.dtype)


def _gqa_bounded_kernel(qt_ref, k_ref, vt_ref, o_ref, acc_ref, *p_ref, n_sub):
    tq = qt_ref.shape[1]
    qt = qt_ref[...]
    row = lax.broadcasted_iota(jnp.int32, qt.shape, 0)
    top = row < HEAD_DIM
    zero = jnp.zeros_like(qt)
    qt2 = jnp.concatenate([jnp.where(top, qt, zero), jnp.where(top, zero, qt)], axis=1)
    acc_ref[...] = jnp.zeros_like(acc_ref)
    sub = GQA_K_SUB
    n_slots = len(p_ref)
    lag = n_slots - 1
    steps = GQA_BOUNDED_STEPS_PER_TRIP
    assert steps % n_slots == 0 and n_sub > lag

    def keys(j):
        return pl.ds(pl.multiple_of(j * sub, sub), sub)

    def probs(j, slot):
        p_ref[slot][...] = jnp.exp2(_dot(k_ref[keys(j), :], qt2)).astype(BF16)

    def accumulate(j, slot):
        acc_ref[...] += _dot(vt_ref[:, keys(j)], p_ref[slot][...])

    for t in range(lag):
        probs(t, t % n_slots)

    def trip(n, carry):
        for u in range(steps):
            t = lag + n * steps + u
            probs(t, (lag + u) % n_slots)
            accumulate(t - lag, u % n_slots)
        return carry

    n_trips = (n_sub - lag) // steps
    lax.fori_loop(0, n_trips, trip, 0)
    for t in range(lag + n_trips * steps, n_sub):
        probs(t, t % n_slots)
        accumulate(t - lag, (t - lag) % n_slots)
    for t in range(n_sub - lag, n_sub):
        accumulate(t, t % n_slots)

    acc = acc_ref[...]
    denom = acc[HEAD_DIM:HEAD_DIM + SUBLANES]
    o_t = (acc[:HEAD_DIM].reshape(HEAD_DIM // SUBLANES, SUBLANES, 2 * tq) / denom[None]).reshape(HEAD_DIM, 2 * tq)
    o_ref[...] = jnp.concatenate([o_t[:, :tq], o_t[:, tq:]], axis=0).T.astype(o_ref.dtype)


def _gqa_call(gqt, gk, gvt, n_head, n_sub, tq, name, bounded=False):
    nq = gqt.shape[1]
    n_keys = n_head + n_sub * GQA_K_SUB
    buf_rows = max(n_head, GQA_K_SUB if n_sub else 0)
    assert n_head % LANES == 0 and nq % tq == 0
    if bounded:
        assert n_head == 0
        body = functools.partial(_gqa_bounded_kernel, n_sub=n_sub)
        buffers = [pltpu.VMEM((buf_rows, 2 * tq), BF16)] * (GQA_PV_LAG + 1)
    else:
        body = functools.partial(_gqa_kernel, n_head=n_head, n_sub=n_sub)
        buffers = [pltpu.VMEM((buf_rows, 2 * tq), F32)] * 2 + [pltpu.VMEM((buf_rows, 2 * tq), BF16)] * 2
    return pl.pallas_call(
        body,
        grid=(GQA_KV_HEADS, nq // tq),
        in_specs=[
            pl.BlockSpec((LANES, tq), lambda g, i: (g, i)),
            pl.BlockSpec((n_keys, LANES), lambda g, i: (0, g)),
            pl.BlockSpec((GQA_V_ROWS, n_keys), lambda g, i: (g, 0)),
        ],
        out_specs=pl.BlockSpec((tq, LANES), lambda g, i: (i, g)),
        out_shape=jax.ShapeDtypeStruct((nq, GQA_WIDTH), BF16),
        scratch_shapes=[pltpu.VMEM((GQA_V_ROWS, 2 * tq), F32)] + buffers,
        compiler_params=_params("arbitrary", "arbitrary"),
        name=name,
    )(gqt, gk, gvt)


def _gqa_attention(gqt, gk, gvt, n_ctx, score_bound):
    n_head = n_ctx % GQA_K_SUB
    n_sub = (gk.shape[0] - n_head) // GQA_K_SUB
    assert (gk.shape[0] - n_head) % GQA_K_SUB == 0
    y_ctx = _gqa_call(gqt[:, :n_ctx], gk, gvt, n_ctx, 0, n_ctx, "gqa_attention_ctx")
    q_lat = gqt[:, n_ctx:]

    def with_running_max(q, k, vt):
        return _gqa_call(q, k, vt, n_head, n_sub, GQA_Q_TILE, "gqa_attention")

    def without_running_max(q, k, vt):
        return _gqa_call(q, k, vt, 0, n_sub, GQA_Q_TILE, "gqa_attention_bounded", bounded=True)

    if n_head == 0:
        y_lat = lax.cond(score_bound <= GQA_BOUNDED_LOG2, without_running_max, with_running_max, q_lat, gk, gvt)
    else:
        y_lat = with_running_max(q_lat, gk, gvt)
    return jnp.concatenate([y_ctx, y_lat], axis=0)


def _out_ffn_kernel(ya_ref, yf_ref, yb_ref, xs_ref, z_ref, yg_ref, xc_ref, xl_ref, mod_ref,
                    dskip_ref, snw_ref, wo_ref, fnw_ref, wg_ref, wu_ref, wd_ref, final_ref,
                    o_ref, *, final, pick_ctx):
    x = jnp.where(pl.program_id(0) == 0, xc_ref[...], xl_ref[...]) if pick_ctx else xl_ref[...]
    y = yf_ref[...] + yb_ref[...] + dskip_ref[...] * xs_ref[...]
    y = y * _silu(z_ref[...])
    ms = jnp.mean(y * y, axis=-1, keepdims=True)
    y = y * lax.rsqrt(ms + EPS) * snw_ref[...]
    mix = jnp.concatenate([ya_ref[...], y.astype(BF16), yg_ref[...]], axis=-1)
    g_m = mod_ref[:, 2 * D_MODEL:3 * D_MODEL]
    sh_f = mod_ref[:, 3 * D_MODEL:4 * D_MODEL]
    sc_f = mod_ref[:, 4 * D_MODEL:5 * D_MODEL]
    g_f = mod_ref[:, 5 * D_MODEL:6 * D_MODEL]
    x1 = x + g_m * _dot(mix, wo_ref[...])
    ms1 = jnp.mean(x1 * x1, axis=-1, keepdims=True)
    hf = (x1 * lax.rsqrt(ms1 + EPS) * fnw_ref[...] * (1.0 + sc_f) + sh_f).astype(BF16)
    act = (_silu(_dot(hf, wg_ref[...])) * _dot(hf, wu_ref[...])).astype(BF16)
    x2 = x1 + g_f * _dot(act, wd_ref[...])
    if final:
        ms2 = jnp.mean(x2 * x2, axis=-1, keepdims=True)
        x2 = x2 * lax.rsqrt(ms2 + EPS) * final_ref[...]
    o_ref[...] = x2


def _out_ffn(ya, yf, yb, xconv, z, yg, x_parts, mod4, layer, d_full, ssd_nw, w_out, ffn_nw,
             w_gate, w_up, w_down, final_nw, final):
    t = ya.shape[0]
    skip = 1 if final else 0
    n = t // ROW_TILE - skip
    row = lambda i: (i + skip, 0)
    const = lambda i: (0, 0)
    x_arrays, x_specs, pick_ctx = _residual_operands(x_parts, skip)
    return pl.pallas_call(
        functools.partial(_out_ffn_kernel, final=final, pick_ctx=pick_ctx),
        grid=(n,),
        in_specs=[
            pl.BlockSpec((ROW_TILE, NA_WIDTH), row),
            pl.BlockSpec((ROW_TILE, SSD_WIDTH), row),
            pl.BlockSpec((ROW_TILE, SSD_WIDTH), row),
            pl.BlockSpec((ROW_TILE, SSD_WIDTH), row),
            pl.BlockSpec((ROW_TILE, SSD_WIDTH), row),
            pl.BlockSpec((ROW_TILE, GQA_WIDTH), row),
        ] + x_specs + [
            pl.BlockSpec((None, None, 1, 6 * D_MODEL), lambda i: (layer, jnp.minimum(i + skip, 1), 0, 0)),
            _layer_spec(d_full, layer),
            _layer_spec(ssd_nw, layer),
            _layer_spec(w_out, layer, resident=True),
            _layer_spec(ffn_nw, layer),
            _layer_spec(w_gate, layer, resident=True),
            _layer_spec(w_up, layer, resident=True),
            _layer_spec(w_down, layer, resident=True),
            pl.BlockSpec((1, D_MODEL), const),
        ],
        out_specs=pl.BlockSpec((ROW_TILE, D_MODEL), lambda i: (i, 0)),
        out_shape=jax.ShapeDtypeStruct((n * ROW_TILE, D_MODEL), F32),
        compiler_params=_params("parallel"),
        name="out_ffn",
    )(ya, yf, yb, xconv, z, yg, *x_arrays, mod4, d_full, ssd_nw, w_out, ffn_nw, w_gate, w_up, w_down, final_nw)


def _rearranged_w_in(w):
    na_in = 3 * NA_WIDTH
    o_z = na_in
    o_xbc = o_z + SSD_WIDTH
    o_dt = o_xbc + SSD_CONV_CH
    o_gq = o_dt + 2 * SSD_HEADS
    o_gk = o_gq + GQA_WIDTH
    o_gv = o_gk + GQA_KV_HEADS * HEAD_DIM
    w = w.astype(BF16)
    q_na = w[..., :NA_WIDTH] * ATTN_SCALE
    dt_pad = jnp.zeros(w.shape[:-1] + (LANES - 2 * SSD_HEADS,), BF16)
    k_heads = [w[..., o_gk + h * HEAD_DIM:o_gk + (h + 1) * HEAD_DIM] for h in range(GQA_KV_HEADS)]
    return jnp.concatenate(
        [q_na, w[..., NA_WIDTH:o_dt], w[..., o_dt:o_gq], dt_pad, w[..., o_gq:o_gk]]
        + [p for h in k_heads for p in (h, h)] + [w[..., o_gv:]], axis=-1)


def _rope_tables(n_ctx, n_lat):
    f32 = np.float32
    freqs = f32(ROPE_THETA) ** (-np.arange(ROPE_PAIRS, dtype=f32) / f32(ROPE_PAIRS))
    n_rows = n_lat // GRID_W
    half = 2 * ROPE_PAIRS
    sign = np.where(np.arange(half) < ROPE_PAIRS, -1.0, 1.0).astype(f32)

    def tables(n_pos):
        a = np.arange(n_pos, dtype=f32)[:, None] * freqs[None, :]
        a = np.concatenate([a, a], axis=-1)
        return np.cos(a), np.sin(a) * sign[None, :]

    def per_token(by_row, by_col):
        lat = np.concatenate([np.broadcast_to(by_row[:, None, :], (n_rows, GRID_W, half)),
                              np.broadcast_to(by_col[None, :, :], (n_rows, GRID_W, half))], axis=-1)
        return lat.reshape(n_lat, HEAD_DIM)

    cos_r, sin_r = tables(n_rows)
    cos_c, sin_c = tables(GRID_W)
    cos = np.concatenate([np.ones((n_ctx, HEAD_DIM), f32), per_token(cos_r, cos_c)], axis=0)
    sin = np.concatenate([np.zeros((n_ctx, HEAD_DIM), f32), per_token(sin_r, sin_c)], axis=0)
    reps = (1, LANES // HEAD_DIM)
    return jnp.asarray(np.tile(cos, reps), F32), jnp.asarray(np.tile(sin, reps), F32)


def kernel(x, c, ctx, c_ctx, mod_w, mod_b, norm_attn_w, norm_ffn_w, w_in, na_rpb, ssd_conv_w, ssd_conv_b,
           ssd_dt_bias, ssd_a_log, ssd_d, ssd_norm_w, q_norm_w, k_norm_w, w_out, ffn_w_gate, ffn_w_up,
           ffn_w_down, final_norm_w):
    depth = mod_w.shape[0]
    batch, n_lat, _ = x.shape
    n_ctx = ctx.shape[1]
    assert batch == 1 and n_ctx == ROW_TILE and n_lat % (NA_ROWS_PER_STEP * GRID_W) == 0
    assert n_lat % GQA_Q_TILE == 0 and n_lat // GRID_W >= NA_WIN_ROWS

    x_parts = (ctx[0], x[0])
    cc = jnp.zeros((SUBLANES, D_MODEL), F32).at[0].set(c_ctx).at[1].set(c[0])
    mod = _modulation(cc, mod_w, mod_b)
    mod4 = mod[:, :2].reshape(depth, 2, 1, 6 * D_MODEL)

    cos_t, sin_t = _rope_tables(n_ctx, n_lat)
    blk = np.arange(2 * GQA_WIDTH) // HEAD_DIM
    ones_bd = jnp.asarray(blk[:, None] == blk[None, :], BF16)
    ones_rows = jnp.ones((GQA_V_ROWS - HEAD_DIM, n_ctx + n_lat), BF16)

    row = lambda p: p.astype(F32).reshape(depth, 1, -1)
    pad_lanes = lambda p: jnp.pad(row(p), ((0, 0), (0, 0), (0, LANES - 2 * SSD_HEADS)))
    w_cat = _rearranged_w_in(w_in)
    dt_bias_pad = pad_lanes(ssd_dt_bias)
    a_row = pad_lanes(-jnp.exp(ssd_a_log.astype(F32)))
    qk_w = row(jnp.concatenate([jnp.tile(q_norm_w * (ATTN_SCALE * LOG2E), (1, GQA_Q_HEADS)),
                                jnp.tile(k_norm_w, (1, 2 * GQA_KV_HEADS))], axis=-1))
    conv_w_pad = jnp.pad(ssd_conv_w, ((0, 0), (0, SUBLANES - SSD_CONV), (0, 0)))
    d_full = row(jnp.repeat(ssd_d, HEAD_DIM, axis=-1))
    bias_tab = _na_bias_table(na_rpb)
    w_out_b, w_gate_b, w_up_b, w_down_b = (w.astype(BF16) for w in (w_out, ffn_w_gate, ffn_w_up, ffn_w_down))
    score_bound = (1.02 * HEAD_DIM * ATTN_SCALE * LOG2E) * jnp.max(jnp.abs(q_norm_w), axis=-1) * jnp.max(
        jnp.abs(k_norm_w), axis=-1)

    for i in range(depth):
        final = i == depth - 1
        na, z, xconv, dt, gq, gk, gv = _inproj(x_parts, n_ctx + n_lat, mod4, i, row(norm_attn_w), w_cat,
                                               dt_bias_pad, qk_w, cos_t, sin_t, ones_bd, conv_w_pad,
                                               row(ssd_conv_b))
        yf, yb = _ssd_scan(xconv, dt, a_row, i, n_ctx)
        ya = _neighbourhood_attention(na, bias_tab, i, n_ctx)
        gvt = jnp.concatenate([part for g in range(GQA_KV_HEADS)
                               for part in (gv[:, g * HEAD_DIM:(g + 1) * HEAD_DIM].T, ones_rows)], axis=0)
        yg = _gqa_attention(gq.T, gk, gvt, n_ctx, score_bound[i])
        x_parts = _out_ffn(ya, yf, yb, xconv, z, yg, x_parts, mod4, i, d_full, row(ssd_norm_w), w_out_b,
                           row(norm_ffn_w), w_gate_b, w_up_b, w_down_b, final_norm_w.reshape(1, -1), final)
    return x_parts[None]
```

```python
import functools
import math

import jax
import jax.numpy as jnp
import numpy as np
from jax import lax
from jax.experimental import pallas as pl
from jax.experimental.pallas import tpu as pltpu

F32 = jnp.float32
BF16 = jnp.bfloat16

D_MODEL = 1024
GRID_W = 64
HEAD_DIM = 64
NA_WIDTH = 256
NA_HEADS = 4
NA_WIN_ROWS = 8
NA_WIN_COLS = 16
SSD_WIDTH = 512
SSD_HEADS = 8
SSD_GROUPS = 2
SSD_STATE = 128
SSD_CONV = 5
SSD_CHUNK = 128
SSD_CONV_CH = SSD_WIDTH + 2 * SSD_GROUPS * SSD_STATE
GQA_WIDTH = 256
GQA_Q_HEADS = 4
GQA_KV_HEADS = 2
ROPE_THETA = 10000.0
ROPE_PAIRS = HEAD_DIM // 4
FFN_HIDDEN = 2816
EPS = 1e-6
ATTN_SCALE = HEAD_DIM ** -0.5
LOG2E = math.log2(math.e)

LANES = 128
SUBLANES = 8
VMEM_LIMIT_BYTES = 56 * 1024 * 1024

ROW_TILE = 256
MOD_COL_TILE = 2048
SSD_CHUNKS_PER_STEP = 2
NA_ROWS_PER_STEP = 4
GQA_Q_TILE = 256
GQA_V_ROWS = HEAD_DIM + 16
GQA_K_SUB = 256
GQA_BOUNDED_LOG2 = 60.0
GQA_PV_LAG = 2
GQA_BOUNDED_STEPS_PER_TRIP = 9
GQA_STEPS_PER_TRIP = 8

C_NA = 0
C_Z = C_NA + 3 * NA_WIDTH
C_XBC = C_Z + SSD_WIDTH
C_DT = C_XBC + SSD_CONV_CH
C_GQ = C_DT + LANES
C_GK = C_GQ + GQA_WIDTH
C_GV = C_GK + 2 * LANES
C_END = C_GV + LANES


def _silu(v):
    return v * (1.0 / (1.0 + jnp.exp(-v)))


def _softplus(v):
    return jnp.maximum(v, 0.0) + jnp.log(1.0 + jnp.exp(-jnp.abs(v)))


def _split3(v):
    hi = v.astype(BF16)
    r1 = v - hi.astype(F32)
    mid = r1.astype(BF16)
    lo = (r1 - mid.astype(F32)).astype(BF16)
    return hi, mid, lo


def _dot(a, b):
    return jnp.dot(a, b, preferred_element_type=F32)


def _dot_nt(a, b):
    return lax.dot_general(a, b, (((1,), (1,)), ((), ())), preferred_element_type=F32)


def _dot_tn(a, b):
    return lax.dot_general(a, b, (((0,), (0,)), ((), ())), preferred_element_type=F32)


def _exact_dot(v, sel):
    hi, mid, lo = _split3(v)
    return _dot(hi, sel) + _dot(mid, sel) + _dot(lo, sel)


def _spread_dot(v, sel):
    hi = v.astype(BF16)
    lo = (v - hi.astype(F32)).astype(BF16)
    return _dot(hi, sel) + _dot(lo, sel)


def _exact_dot_lhs(sel, v):
    hi, mid, lo = _split3(v)
    return _dot(sel, hi) + _dot(sel, mid) + _dot(sel, lo)


def _params(*sem):
    return pltpu.CompilerParams(dimension_semantics=sem, vmem_limit_bytes=VMEM_LIMIT_BYTES)


def _resident(shape, index_map):
    return pl.BlockSpec(shape, index_map, pipeline_mode=pl.Buffered(1))


def _layer_spec(stacked, layer, resident=False):
    shape = stacked.shape[1:]
    index_map = lambda *_: (layer,) + (0,) * len(shape)
    if resident:
        return pl.BlockSpec((None,) + shape, index_map, pipeline_mode=pl.Buffered(1))
    return pl.BlockSpec((None,) + shape, index_map)


def _mod_kernel(cc_ref, w_ref, b_ref, o_ref):
    a = _silu(cc_ref[...])
    o_ref[0] = jnp.dot(a, w_ref[0], preferred_element_type=F32) + b_ref[0]


def _modulation(cc, mod_w, mod_b):
    depth = mod_w.shape[0]
    cols = MOD_COL_TILE
    ncol = mod_w.shape[2] // cols
    return pl.pallas_call(
        _mod_kernel,
        grid=(depth, ncol),
        in_specs=[
            pl.BlockSpec((SUBLANES, D_MODEL), lambda l, j: (0, 0)),
            pl.BlockSpec((1, D_MODEL, cols), lambda l, j: (l, 0, j)),
            pl.BlockSpec((1, 1, cols), lambda l, j: (l, 0, j)),
        ],
        out_specs=pl.BlockSpec((1, SUBLANES, cols), lambda l, j: (l, 0, j)),
        out_shape=jax.ShapeDtypeStruct((depth, SUBLANES, ncol * cols), F32),
        compiler_params=_params("arbitrary", "arbitrary"),
        name="modulation",
    )(cc, mod_w, mod_b.reshape(depth, 1, -1))


def _residual_operands(x_parts, skip):
    ctx_spec = pl.BlockSpec((ROW_TILE, D_MODEL), lambda i: (0, 0))
    if isinstance(x_parts, tuple):
        lat_spec = pl.BlockSpec((ROW_TILE, D_MODEL), lambda i: (jnp.maximum(i + skip - 1, 0), 0))
        return list(x_parts), [ctx_spec, lat_spec], skip == 0
    row_spec = pl.BlockSpec((ROW_TILE, D_MODEL), lambda i: (i + skip, 0))
    return [x_parts, x_parts], [ctx_spec, row_spec], False


def _inproj_kernel(xc_ref, xl_ref, prev_ref, next_ref, mod_ref, nw_ref, w_ref, dtb_ref, qkw_ref, cos_ref, sin_ref,
                   ones_ref, cw_ref, cb_ref, na_ref, z_ref, xconv_ref, dt_ref, gq_ref, gk_ref, gv_ref, *, pick_ctx):
    i = pl.program_id(0)
    n = pl.num_programs(0)
    rows = xl_ref.shape[0]
    x = jnp.where(i == 0, xc_ref[...], xl_ref[...]) if pick_ctx else xl_ref[...]
    x = jnp.concatenate([prev_ref[...], x, next_ref[...]], axis=0)
    ms = jnp.mean(x * x, axis=-1, keepdims=True)
    xn = x * lax.rsqrt(ms + EPS) * nw_ref[...]
    sh = mod_ref[:, 0:D_MODEL]
    sc = mod_ref[:, D_MODEL:2 * D_MODEL]
    h = (xn * (1.0 + sc) + sh).astype(BF16)
    xbc_ext = _dot(h, w_ref[:, C_XBC:C_DT])
    h_tile = h[SUBLANES:SUBLANES + rows]
    u = jnp.concatenate([_dot(h_tile, w_ref[:, C_NA:C_XBC]), xbc_ext[SUBLANES:SUBLANES + rows],
                         _dot(h_tile, w_ref[:, C_DT:C_END])], axis=-1)
    na_ref[...] = u[:, C_NA:C_Z].astype(BF16)
    z_ref[...] = u[:, C_Z:C_XBC]
    has_prev = i >= 2
    has_next = jnp.logical_and(i >= 1, i < n - 1)
    xbc = u[:, C_XBC:C_DT]
    ext = jnp.concatenate([jnp.where(has_prev, xbc_ext[0:SUBLANES], 0.0), xbc,
                           jnp.where(has_next, xbc_ext[SUBLANES + rows:], 0.0)], axis=0)
    total = rows + 2 * SUBLANES
    half = SSD_CONV // 2
    acc = cb_ref[...] + cw_ref[half:half + 1, :] * xbc
    for j in range(SSD_CONV):
        if j != half:
            shifted = pltpu.roll(ext, (half - j) % total, 0)
            acc = acc + cw_ref[j:j + 1, :] * shifted[SUBLANES:SUBLANES + rows]
    xconv_ref[...] = _silu(acc)
    dt_ref[...] = _softplus(u[:, C_DT:C_GQ] + dtb_ref[...])
    gv_ref[...] = u[:, C_GV:C_END].astype(BF16)
    g = u[:, C_GQ:C_GV]
    gsq = g * g
    hi = gsq.astype(BF16)
    lo = (gsq - hi.astype(F32)).astype(BF16)
    ss = _dot(hi, ones_ref[...]) + _dot(lo, ones_ref[...])
    gn = g * lax.rsqrt(ss * (1.0 / HEAD_DIM) + EPS) * qkw_ref[...]
    width = gn.shape[-1]
    lane = lax.broadcasted_iota(jnp.int32, gn.shape, 1)
    first = (lane % (2 * ROPE_PAIRS)) < ROPE_PAIRS
    partner = jnp.where(first, pltpu.roll(gn, width - ROPE_PAIRS, 1), pltpu.roll(gn, ROPE_PAIRS, 1))
    cos = jnp.concatenate([cos_ref[...]] * (width // LANES), axis=-1)
    sin = jnp.concatenate([sin_ref[...]] * (width // LANES), axis=-1)
    gr = gn * cos + partner * sin
    gq_ref[...] = gr[:, :GQA_WIDTH].astype(BF16)
    gk_ref[...] = gr[:, GQA_WIDTH:].astype(BF16)


def _inproj(x_parts, t, mod4, layer, norm_w, w_cat, dt_bias_pad, qk_w, cos_t, sin_t, ones_bd, conv_w_pad, conv_b):
    n = t // ROW_TILE
    row = lambda i: (i, 0)
    const = lambda i: (0, 0)
    outs = [
        (3 * NA_WIDTH, BF16), (SSD_WIDTH, F32), (SSD_CONV_CH, F32), (LANES, F32),
        (GQA_WIDTH, BF16), (2 * LANES, BF16), (LANES, BF16),
    ]
    x_arrays, x_specs, pick_ctx = _residual_operands(x_parts, 0)
    halo_src = x_arrays[1]
    per = ROW_TILE // SUBLANES
    first = (lambda i: (i - 1) * per) if isinstance(x_parts, tuple) else (lambda i: i * per)
    last_blk = halo_src.shape[0] // SUBLANES - 1
    halo_specs = [
        pl.BlockSpec((SUBLANES, D_MODEL), lambda i: (jnp.clip(first(i) - 1, 0, last_blk), 0)),
        pl.BlockSpec((SUBLANES, D_MODEL), lambda i: (jnp.clip(first(i) + per, 0, last_blk), 0)),
    ]
    return pl.pallas_call(
        functools.partial(_inproj_kernel, pick_ctx=pick_ctx),
        grid=(n,),
        in_specs=x_specs + halo_specs + [
            pl.BlockSpec((None, None, 1, 6 * D_MODEL), lambda i: (layer, jnp.minimum(i, 1), 0, 0)),
            _layer_spec(norm_w, layer),
            _layer_spec(w_cat, layer, resident=True),
            _layer_spec(dt_bias_pad, layer),
            _layer_spec(qk_w, layer),
            pl.BlockSpec((ROW_TILE, LANES), row),
            pl.BlockSpec((ROW_TILE, LANES), row),
            _resident((2 * GQA_WIDTH, 2 * GQA_WIDTH), const),
            _layer_spec(conv_w_pad, layer),
            _layer_spec(conv_b, layer),
        ],
        out_specs=[pl.BlockSpec((ROW_TILE, w), row) for w, _ in outs],
        out_shape=[jax.ShapeDtypeStruct((t, w), d) for w, d in outs],
        compiler_params=_params("parallel"),
        name="inproj",
    )(*x_arrays, halo_src, halo_src, mod4, norm_w, w_cat, dt_bias_pad, qk_w, cos_t, sin_t, ones_bd,
      conv_w_pad, conv_b)


def _ssd_direction(xbc, dt, a_row, tri, expand, s_ref, lane0, backward):
    q = SSD_CHUNK
    x = xbc[:, :SSD_WIDTH]
    nb = SSD_GROUPS * SSD_STATE
    bmat = xbc[:, SSD_WIDTH:SSD_WIDTH + nb].astype(BF16)
    cmat = xbc[:, SSD_WIDTH + nb:].astype(BF16)
    cum = _exact_dot_lhs(tri, dt * a_row)
    yield
    last = 0 if backward else q - 1
    cum_t = cum.T

    def spread(mat_t):
        rows = [jnp.broadcast_to(mat_t[lane0 + h:lane0 + h + 1, :], (HEAD_DIM, q)) for h in range(SSD_HEADS)]
        return jnp.concatenate(rows, axis=0).T

    dt_full = _spread_dot(dt, expand)
    cum_full = spread(cum_t)
    yield
    ea_full = jnp.exp(cum_full)
    te_full = jnp.exp(cum_full[last:last + 1, :] - cum_full)
    xr = x * dt_full
    xr_b = xr.astype(BF16)
    xt_b = (xr * te_full).astype(BF16)
    ti =lax.broadcasted_iota(jnp.int32, (q, q), 0)
    si = lax.broadcasted_iota(jnp.int32, (q, q), 1)
    keep = (si >= ti) if backward else (si <= ti)
    lane = lax.broadcasted_iota(jnp.int32, (q, LANES), 1)
    heads_per_group = SSD_HEADS // SSD_GROUPS
    width_g = heads_per_group * HEAD_DIM
    pieces = []
    decay_total = ea_full[last:last + 1, :]
    for g in range(SSD_GROUPS):
        bg = bmat[:, g * SSD_STATE:(g + 1) * SSD_STATE]
        cg = cmat[:, g * SSD_STATE:(g + 1) * SSD_STATE]
        gmat = _dot_nt(cg, bg)
        s_old = s_ref[:, g * width_g:(g + 1) * width_g]
        y_off = _dot(cg, s_old.astype(BF16)) * ea_full[:, g * width_g:(g + 1) * width_g]
        s_new = _dot_tn(bg, xt_b[:, g * width_g:(g + 1) * width_g])
        s_ref[:, g * width_g:(g + 1) * width_g] = decay_total[:, g * width_g:(g + 1) * width_g] * s_old + s_new
        yield
        for pair in range(heads_per_group // 2):
            col = g * width_g + pair * LANES
            xr_pair = xr_b[:, col:col + LANES]
            ys = []
            for r in range(2):
                hl = lane0 + g * heads_per_group + 2 * pair + r
                seg = jnp.broadcast_to(cum[:, hl:hl + 1], (q, q)) - jnp.broadcast_to(cum_t[hl:hl + 1, :], (q, q))
                dec = jnp.exp(jnp.where(keep, seg, -jnp.inf))
                ys.append(_dot((gmat * dec).astype(BF16), xr_pair))
            y_diag = jnp.where(lane < HEAD_DIM, ys[0], ys[1])
            pieces.append(y_diag + y_off[:, pair * LANES:(pair + 1) * LANES])
            yield
    return jnp.concatenate(pieces, axis=-1)


def _run_interleaved(*stage_generators):
    results = [None] * len(stage_generators)
    live = list(range(len(stage_generators)))
    while live:
        for idx in list(live):
            try:
                next(stage_generators[idx])
            except StopIteration as done:
                results[idx] = done.value
                live.remove(idx)
    return results


def _ssd_kernel(xf_ref, dtf_ref, xb_ref, dtb_ref, a_ref, tril_ref, triu_ref, ef_ref, eb_ref,
                yf_ref, yb_ref, sf_ref, sb_ref):
    @pl.when(pl.program_id(0) == 0)
    def _():
        sf_ref[...] = jnp.zeros_like(sf_ref)
        sb_ref[...] = jnp.zeros_like(sb_ref)

    a_row = a_ref[...]
    q = SSD_CHUNK
    order_f = list(range(SSD_CHUNKS_PER_STEP))
    order_b = order_f[::-1]
    stages = []
    for cf, cb in zip(order_f, order_b):
        rf, rb = slice(cf * q, (cf + 1) * q), slice(cb * q, (cb + 1) * q)
        stages.append(_ssd_direction(xf_ref[rf, :], dtf_ref[rf, :], a_row, tril_ref[...], ef_ref[...],
                                     sf_ref, 0, False))
        stages.append(_ssd_direction(xb_ref[rb, :], dtb_ref[rb, :], a_row, triu_ref[...], eb_ref[...],
                                     sb_ref, SSD_HEADS, True))
    ys = _run_interleaved(*stages)
    for k, (cf, cb) in enumerate(zip(order_f, order_b)):
        yf_ref[cf * q:(cf + 1) * q, :] = ys[2 * k]
        yb_ref[cb * q:(cb + 1) * q, :] = ys[2 * k + 1]


def _ssd_scan(xconv, dt, a_row, layer, n_ctx):
    t = xconv.shape[0]
    q = SSD_CHUNK
    rows = SSD_CHUNKS_PER_STEP * q
    assert n_ctx % rows == 0 and t % rows == 0
    n = t // rows
    n_ctx_blocks = n_ctx // rows
    r = jnp.arange(q)
    tril = (r[None, :] <= r[:, None]).astype(BF16)
    triu = (r[None, :] >= r[:, None]).astype(BF16)
    lanes = jnp.arange(LANES)[:, None]
    head_of_col = (jnp.arange(SSD_WIDTH) // HEAD_DIM)[None, :]
    expand_f = (lanes == head_of_col).astype(BF16)
    expand_b = (lanes == head_of_col + SSD_HEADS).astype(BF16)

    def fwd(i):
        return (i, 0)

    def bwd(i):
        return (jnp.where(i < n_ctx_blocks, n_ctx_blocks - 1 - i, n - 1 - (i - n_ctx_blocks)), 0)

    const = lambda i: (0, 0)
    return pl.pallas_call(
        _ssd_kernel,
        grid=(n,),
        in_specs=[
            pl.BlockSpec((rows, SSD_CONV_CH), fwd),
            pl.BlockSpec((rows, LANES), fwd),
            pl.BlockSpec((rows, SSD_CONV_CH), bwd),
            pl.BlockSpec((rows, LANES), bwd),
            _layer_spec(a_row, layer),
            pl.BlockSpec((q, q), const),
            pl.BlockSpec((q, q), const),
            pl.BlockSpec((LANES, SSD_WIDTH), const),
            pl.BlockSpec((LANES, SSD_WIDTH), const),
        ],
        out_specs=[pl.BlockSpec((rows, SSD_WIDTH), fwd), pl.BlockSpec((rows, SSD_WIDTH), bwd)],
        out_shape=[jax.ShapeDtypeStruct((t, SSD_WIDTH), F32)] * 2,
        scratch_shapes=[pltpu.VMEM((SSD_STATE, SSD_WIDTH), F32)] * 2,
        compiler_params=_params("arbitrary"),
        name="ssd_scan",
    )(xconv, dt, xconv, dt, a_row, tril, triu, expand_f, expand_b)


def _na_kernel(q_ref, k_ref, v_ref, bias_ref, o_ref, *, n_ctx, n_rows):
    i = pl.program_id(0)
    is_ctx = i == 0
    kc = k_ref[0:n_ctx, :]
    vc = v_ref[0:n_ctx, :]
    lane = lax.broadcasted_iota(jnp.int32, (GRID_W, NA_WIDTH), 1)
    mine = [(lane >= h * HEAD_DIM) & (lane < (h + 1) * HEAD_DIM) for h in range(NA_HEADS)]
    win = NA_WIN_ROWS * GRID_W

    def lane_tiles(a):
        return [a[:, c * LANES:(c + 1) * LANES] for c in range(a.shape[-1] // LANES)]

    def grid_row(j):
        r = jnp.maximum((i - 1) * NA_ROWS_PER_STEP + j, 0)
        r_start = jnp.clip(r - NA_WIN_ROWS // 2, 0, n_rows - NA_WIN_ROWS)
        variant = jnp.where(is_ctx, NA_WIN_ROWS, r_start - r + NA_WIN_ROWS - 1)
        start = pl.multiple_of(n_ctx + r_start * GRID_W, GRID_W)
        kw = k_ref[pl.ds(start, win), :]
        vw = v_ref[pl.ds(start, win), :]
        qj = q_ref[j * GRID_W:(j + 1) * GRID_W, :]
        qm = jnp.concatenate([jnp.where(mine[h], qj, jnp.zeros_like(qj)) for h in range(NA_HEADS)], axis=0)
        s_w = _dot_nt(qm, kw) + bias_ref[variant]
        s_c = _dot_nt(qm, kc)
        yield
        m = functools.reduce(jnp.maximum, lane_tiles(s_w) + lane_tiles(s_c))
        m = jnp.broadcast_to(jnp.max(m, axis=-1, keepdims=True), m.shape)
        p_w = jnp.exp(s_w - jnp.concatenate([m] * (s_w.shape[-1] // LANES), axis=-1))
        p_c = jnp.exp(s_c - jnp.concatenate([m] * (s_c.shape[-1] // LANES), axis=-1))
        l = jnp.sum(functools.reduce(jnp.add, lane_tiles(p_w) + lane_tiles(p_c)), axis=-1, keepdims=True)
        y = (_dot(p_w.astype(BF16), vw) + _dot(p_c.astype(BF16), vc)) * (1.0 / l)
        out = y[(NA_HEADS - 1) * GRID_W:]
        for h in range(NA_HEADS - 2, -1, -1):
            out = jnp.where(mine[h], y[h * GRID_W:(h + 1) * GRID_W], out)
        o_ref[j * GRID_W:(j + 1) * GRID_W, :] = out.astype(o_ref.dtype)

    _run_interleaved(*[grid_row(j) for j in range(NA_ROWS_PER_STEP)])


def _na_bias_table(rpb):
    depth = rpb.shape[0]
    col = np.arange(GRID_W)
    c_start = np.clip(col - NA_WIN_COLS // 2, 0, GRID_W - NA_WIN_COLS)
    in_win = (col[None, :] >= c_start[:, None]) & (col[None, :] < c_start[:, None] + NA_WIN_COLS)
    dc = np.clip(col[None, :] - col[:, None] + NA_WIN_COLS - 1, 0, 2 * NA_WIN_COLS - 2)
    n_dc = 2 * NA_WIN_COLS - 1
    n_dr = 2 * NA_WIN_ROWS - 1
    onehot = (dc[None, :, :] == np.arange(n_dc)[:, None, None]).astype(np.float32).reshape(n_dc, GRID_W * GRID_W)
    t2 = jnp.dot(rpb.reshape(depth * NA_HEADS * n_dr, n_dc).astype(F32), onehot, precision=lax.Precision.HIGHEST)
    t2 = jnp.where(in_win[None, None, None], t2.reshape(depth, NA_HEADS, n_dr, GRID_W, GRID_W), -jnp.inf)
    tab = jnp.stack([t2[:, :, v:v + NA_WIN_ROWS] for v in range(NA_WIN_ROWS)], axis=1)
    tab = tab.transpose(0, 1, 2, 4, 3, 5).reshape(depth, NA_WIN_ROWS, NA_HEADS * GRID_W, NA_WIN_ROWS * GRID_W)
    masked = jnp.full((depth, 1) + tab.shape[2:], -jnp.inf, F32)
    return jnp.concatenate([tab, masked], axis=1)


def _neighbourhood_attention(na, bias_tab, layer, n_ctx):
    t = na.shape[0]
    n_rows = (t - n_ctx) // GRID_W
    step_rows = NA_ROWS_PER_STEP * GRID_W
    n = t // step_rows
    return pl.pallas_call(
        functools.partial(_na_kernel, n_ctx=n_ctx, n_rows=n_rows),
        grid=(n,),
        in_specs=[
            pl.BlockSpec((step_rows, NA_WIDTH), lambda i: (i, 0)),
            _resident((t, NA_WIDTH), lambda i: (0, 1)),
            _resident((t, NA_WIDTH), lambda i: (0, 2)),
            _layer_spec(bias_tab, layer, resident=True),
        ],
        out_specs=pl.BlockSpec((step_rows, NA_WIDTH), lambda i: (i, 0)),
        out_shape=jax.ShapeDtypeStruct((t, NA_WIDTH), BF16),
        compiler_params=_params("parallel"),
        name="neighbourhood_attention",
    )(na, na, na, bias_tab)


def _gqa_kernel(qt_ref, k_ref, vt_ref, o_ref, acc_ref, s0_ref, s1_ref, p0_ref, p1_ref, *, n_ctx_sub, n_sub):
    tq = qt_ref.shape[1]
    qt = qt_ref[...]
    row = lax.broadcasted_iota(jnp.int32, qt.shape, 0)
    top = row < HEAD_DIM
    zero = jnp.zeros_like(qt)
    qt2 = jnp.concatenate([jnp.where(top, qt, zero), jnp.where(top, zero, qt)], axis=1)
    acc_ref[...] = jnp.zeros_like(acc_ref)
    s_ref = (s0_ref, s1_ref)
    p_ref = (p0_ref, p1_ref)

    def keys(j, size):
        return pl.ds(pl.multiple_of(j * size, LANES), size)

    def score(key_rows, slot):
        size = key_rows.size
        s = _dot(k_ref[key_rows, :], qt2)
        s_ref[slot][0:size, :] = s
        return jnp.max(s.reshape(size // SUBLANES, SUBLANES, 2 * tq), axis=0)

    def softmax(size, slot, m_old, part_max):
        m_new = jnp.maximum(m_old, jnp.max(part_max, axis=0, keepdims=True))
        s = s_ref[slot][0:size, :].reshape(size // SUBLANES, SUBLANES, 2 * tq)
        p_ref[slot][0:size, :] = jnp.exp2(s - m_new[None]).reshape(size, 2 * tq).astype(BF16)
        return m_new, jnp.exp2(m_old - m_new)

    def accumulate(key_rows, slot, alpha):
        size = key_rows.size
        acc = acc_ref[...].reshape(GQA_V_ROWS // SUBLANES, SUBLANES, 2 * tq) * alpha[None]
        acc_ref[...] = acc.reshape(GQA_V_ROWS, 2 * tq) + _dot(vt_ref[:, key_rows], p_ref[slot][0:size, :])

    m_init = jnp.full((SUBLANES, 2 * tq), -jnp.inf, F32)
    sub = GQA_K_SUB
    assert n_sub >= 3

    @pl.when(pl.program_id(1) == 0)
    def _():
        m = m_init
        for j in range(n_ctx_sub):
            m, alpha = softmax(sub, 0, m, score(keys(j, sub), 0))
            accumulate(keys(j, sub), 0, alpha)

    @pl.when(pl.program_id(1) > 0)
    def _():
        part0 = score(keys(0, sub), 0)
        part1 = score(keys(1, sub), 1)
        m, alpha = softmax(sub, 0, m_init, part0)

        def step(t, slot, carry):
            m, alpha, part = carry
            part_next = score(keys(t, sub), slot)
            accumulate(keys(t - 2, sub), slot, alpha)
            return softmax(sub, 1 - slot, m, part) + (part_next,)

        def trip(n, carry):
            for u in range(GQA_STEPS_PER_TRIP):
                carry = step(2 + n * GQA_STEPS_PER_TRIP + u, u % 2, carry)
            return carry

        n_trips = (n_sub - 2) // GQA_STEPS_PER_TRIP
        carry = lax.fori_loop(0, n_trips, trip, (m, alpha, part1))
        for t in range(2 + n_trips * GQA_STEPS_PER_TRIP, n_sub):
            carry = step(t, t % 2, carry)
        m, alpha, part = carry
        accumulate(keys(n_sub - 2, sub), (n_sub - 2) % 2, alpha)
        m, alpha = softmax(sub, (n_sub - 1) % 2, m, part)
        accumulate(keys(n_sub - 1, sub), (n_sub - 1) % 2, alpha)

    acc = acc_ref[...]
    denom = acc[HEAD_DIM:HEAD_DIM + SUBLANES]
    o_t = (acc[:HEAD_DIM].reshape(HEAD_DIM // SUBLANES, SUBLANES, 2 * tq) / denom[None]).reshape(HEAD_DIM, 2 * tq)
    o_ref[...] = jnp.concatenate([o_t[:, :tq], o_t[:, tq:]], axis=0).T.astype(o_ref.dtype)


def _gqa_bounded_kernel(qt_ref, k_ref, vt_ref, o_ref, acc_ref, *p_ref, n_ctx_sub, n_sub):
    tq = qt_ref.shape[1]
    qt = qt_ref[...]
    row = lax.broadcasted_iota(jnp.int32, qt.shape, 0)
    top = row < HEAD_DIM
    zero = jnp.zeros_like(qt)
    qt2 = jnp.concatenate([jnp.where(top, qt, zero), jnp.where(top, zero, qt)], axis=1)
    acc_ref[...] = jnp.zeros_like(acc_ref)
    sub = GQA_K_SUB
    n_slots = len(p_ref)
    lag = n_slots - 1
    steps = GQA_BOUNDED_STEPS_PER_TRIP
    assert steps % n_slots == 0 and n_sub > lag

    def keys(j):
        return pl.ds(pl.multiple_of(j * sub, sub), sub)

    def probs(j, slot):
        p_ref[slot][...] = jnp.exp2(_dot(k_ref[keys(j), :], qt2)).astype(BF16)

    def accumulate(j, slot):
        acc_ref[...] += _dot(vt_ref[:, keys(j)], p_ref[slot][...])

    @pl.when(pl.program_id(1) == 0)
    def _():
        for j in range(n_ctx_sub):
            probs(j, 0)
            accumulate(j, 0)

    @pl.when(pl.program_id(1) > 0)
    def _():
        for t in range(lag):
            probs(t, t % n_slots)

        def trip(n, carry):
            for u in range(steps):
                t = lag + n * steps + u
                probs(t, (lag + u) % n_slots)
                accumulate(t - lag, u % n_slots)
            return carry

        n_trips = (n_sub - lag) // steps
        lax.fori_loop(0, n_trips, trip, 0)
        for t in range(lag + n_trips * steps, n_sub):
            probs(t, t % n_slots)
            accumulate(t - lag, (t - lag) % n_slots)
        for t in range(n_sub - lag, n_sub):
            accumulate(t, t % n_slots)

    acc = acc_ref[...]
    denom = acc[HEAD_DIM:HEAD_DIM + SUBLANES]
    o_t = (acc[:HEAD_DIM].reshape(HEAD_DIM // SUBLANES, SUBLANES, 2 * tq) / denom[None]).reshape(HEAD_DIM, 2 * tq)
    o_ref[...] = jnp.concatenate([o_t[:, :tq], o_t[:, tq:]], axis=0).T.astype(o_ref.dtype)


def _gqa_call(gqt, gk, gvt, n_ctx, name, bounded):
    nq = gqt.shape[1]
    n_keys = gk.shape[0]
    tq = GQA_Q_TILE
    buf_rows = GQA_K_SUB
    assert n_ctx == tq and n_ctx % GQA_K_SUB == 0 and n_keys % GQA_K_SUB == 0 and nq % tq == 0
    sizes = dict(n_ctx_sub=n_ctx // GQA_K_SUB, n_sub=n_keys // GQA_K_SUB)
    if bounded:
        body = functools.partial(_gqa_bounded_kernel, **sizes)
        buffers = [pltpu.VMEM((buf_rows, 2 * tq), BF16)] * (GQA_PV_LAG + 1)
    else:
        body = functools.partial(_gqa_kernel, **sizes)
        buffers = [pltpu.VMEM((buf_rows, 2 * tq), F32)] * 2 + [pltpu.VMEM((buf_rows, 2 * tq), BF16)] * 2
    return pl.pallas_call(
        body,
        grid=(GQA_KV_HEADS, nq // tq),
        in_specs=[
            pl.BlockSpec((LANES, tq), lambda g, i: (g, i)),
            pl.BlockSpec((n_keys, LANES), lambda g, i: (0, g)),
            pl.BlockSpec((GQA_V_ROWS, n_keys), lambda g, i: (g, 0)),
        ],
        out_specs=pl.BlockSpec((tq, LANES), lambda g, i: (i, g)),
        out_shape=jax.ShapeDtypeStruct((nq, GQA_WIDTH), BF16),
        scratch_shapes=[pltpu.VMEM((GQA_V_ROWS, 2 * tq), F32)] + buffers,
        compiler_params=_params("arbitrary", "arbitrary"),
        name=name,
    )(gqt, gk, gvt)


def _gqa_attention(gqt, gk, gvt, n_ctx, score_bound):
    def with_running_max(q, k, vt):
        return _gqa_call(q, k, vt, n_ctx, "gqa_attention", False)

    def without_running_max(q, k, vt):
        return _gqa_call(q, k, vt, n_ctx, "gqa_attention_bounded", True)

    return lax.cond(score_bound <= GQA_BOUNDED_LOG2, without_running_max, with_running_max, gqt, gk, gvt)


def _out_ffn_kernel(ya_ref, yf_ref, yb_ref, xs_ref, z_ref, yg_ref, xc_ref, xl_ref, mod_ref,
                    dskip_ref, snw_ref, wo_ref, fnw_ref, wg_ref, wu_ref, wd_ref, final_ref,
                    o_ref, *, final, pick_ctx):
    x = jnp.where(pl.program_id(0) == 0, xc_ref[...], xl_ref[...]) if pick_ctx else xl_ref[...]
    y = yf_ref[...] + yb_ref[...] + dskip_ref[...] * xs_ref[...]
    y = y * _silu(z_ref[...])
    ms = jnp.mean(y * y, axis=-1, keepdims=True)
    y = y * lax.rsqrt(ms + EPS) * snw_ref[...]
    mix = jnp.concatenate([ya_ref[...], y.astype(BF16), yg_ref[...]], axis=-1)
    g_m = mod_ref[:, 2 * D_MODEL:3 * D_MODEL]
    sh_f = mod_ref[:, 3 * D_MODEL:4 * D_MODEL]
    sc_f = mod_ref[:, 4 * D_MODEL:5 * D_MODEL]
    g_f = mod_ref[:, 5 * D_MODEL:6 * D_MODEL]
    x1 = x + g_m * _dot(mix, wo_ref[...])
    ms1 = jnp.mean(x1 * x1, axis=-1, keepdims=True)
    hf = (x1 * lax.rsqrt(ms1 + EPS) * fnw_ref[...] * (1.0 + sc_f) + sh_f).astype(BF16)
    act = (_silu(_dot(hf, wg_ref[...])) * _dot(hf, wu_ref[...])).astype(BF16)
    x2 = x1 + g_f * _dot(act, wd_ref[...])
    if final:
        ms2 = jnp.mean(x2 * x2, axis=-1, keepdims=True)
        x2 = x2 * lax.rsqrt(ms2 + EPS) * final_ref[...]
    o_ref[...] = x2


def _out_ffn(ya, yf, yb, xconv, z, yg, x_parts, mod4, layer, d_full, ssd_nw, w_out, ffn_nw,
             w_gate, w_up, w_down, final_nw, final):
    t = ya.shape[0]
    skip = 1 if final else 0
    n = t // ROW_TILE - skip
    row = lambda i: (i + skip, 0)
    const = lambda i: (0, 0)
    x_arrays, x_specs, pick_ctx = _residual_operands(x_parts, skip)
    return pl.pallas_call(
        functools.partial(_out_ffn_kernel, final=final, pick_ctx=pick_ctx),
        grid=(n,),
        in_specs=[
            pl.BlockSpec((ROW_TILE, NA_WIDTH), row),
            pl.BlockSpec((ROW_TILE, SSD_WIDTH), row),
            pl.BlockSpec((ROW_TILE, SSD_WIDTH), row),
            pl.BlockSpec((ROW_TILE, SSD_WIDTH), row),
            pl.BlockSpec((ROW_TILE, SSD_WIDTH), row),
            pl.BlockSpec((ROW_TILE, GQA_WIDTH), row),
        ] + x_specs + [
            pl.BlockSpec((None, None, 1, 6 * D_MODEL), lambda i: (layer, jnp.minimum(i + skip, 1), 0, 0)),
            _layer_spec(d_full, layer),
            _layer_spec(ssd_nw, layer),
            _layer_spec(w_out, layer, resident=True),
            _layer_spec(ffn_nw, layer),
            _layer_spec(w_gate, layer, resident=True),
            _layer_spec(w_up, layer, resident=True),
            _layer_spec(w_down, layer, resident=True),
            pl.BlockSpec((1, D_MODEL), const),
        ],
        out_specs=pl.BlockSpec((ROW_TILE, D_MODEL), lambda i: (i, 0)),
        out_shape=jax.ShapeDtypeStruct((n * ROW_TILE, D_MODEL), F32),
        compiler_params=_params("parallel"),
        name="out_ffn",
    )(ya, yf, yb, xconv, z, yg, *x_arrays, mod4, d_full, ssd_nw, w_out, ffn_nw, w_gate, w_up, w_down, final_nw)


def _rearranged_w_in(w):
    na_in = 3 * NA_WIDTH
    o_z = na_in
    o_xbc = o_z + SSD_WIDTH
    o_dt = o_xbc + SSD_CONV_CH
    o_gq = o_dt + 2 * SSD_HEADS
    o_gk = o_gq + GQA_WIDTH
    o_gv = o_gk + GQA_KV_HEADS * HEAD_DIM
    w = w.astype(BF16)
    q_na = w[..., :NA_WIDTH] * ATTN_SCALE
    dt_pad = jnp.zeros(w.shape[:-1] + (LANES - 2 * SSD_HEADS,), BF16)
    k_heads = [w[..., o_gk + h * HEAD_DIM:o_gk + (h + 1) * HEAD_DIM] for h in range(GQA_KV_HEADS)]
    return jnp.concatenate(
        [q_na, w[..., NA_WIDTH:o_dt], w[..., o_dt:o_gq], dt_pad, w[..., o_gq:o_gk]]
        + [p for h in k_heads for p in (h, h)] + [w[..., o_gv:]], axis=-1)


def _rope_tables(n_ctx, n_lat):
    f32 = np.float32
    freqs = f32(ROPE_THETA) ** (-np.arange(ROPE_PAIRS, dtype=f32) / f32(ROPE_PAIRS))
    n_rows = n_lat // GRID_W
    half = 2 * ROPE_PAIRS
    sign = np.where(np.arange(half) < ROPE_PAIRS, -1.0, 1.0).astype(f32)

    def tables(n_pos):
        a = np.arange(n_pos, dtype=f32)[:, None] * freqs[None, :]
        a = np.concatenate([a, a], axis=-1)
        return np.cos(a), np.sin(a) * sign[None, :]

    def per_token(by_row, by_col):
        lat = np.concatenate([np.broadcast_to(by_row[:, None, :], (n_rows, GRID_W, half)),
                              np.broadcast_to(by_col[None, :, :], (n_rows, GRID_W, half))], axis=-1)
        return lat.reshape(n_lat, HEAD_DIM)

    cos_r, sin_r = tables(n_rows)
    cos_c, sin_c = tables(GRID_W)
    cos = np.concatenate([np.ones((n_ctx, HEAD_DIM), f32), per_token(cos_r, cos_c)], axis=0)
    sin = np.concatenate([np.zeros((n_ctx, HEAD_DIM), f32), per_token(sin_r, sin_c)], axis=0)
    reps = (1, LANES // HEAD_DIM)
    return jnp.asarray(np.tile(cos, reps), F32), jnp.asarray(np.tile(sin, reps), F32)


def kernel(x, c, ctx, c_ctx, mod_w, mod_b, norm_attn_w, norm_ffn_w, w_in, na_rpb, ssd_conv_w, ssd_conv_b,
           ssd_dt_bias, ssd_a_log, ssd_d, ssd_norm_w, q_norm_w, k_norm_w, w_out, ffn_w_gate, ffn_w_up,
           ffn_w_down, final_norm_w):
    depth = mod_w.shape[0]
    batch, n_lat, _ = x.shape
    n_ctx = ctx.shape[1]
    assert batch == 1 and n_ctx == ROW_TILE and n_lat % (NA_ROWS_PER_STEP * GRID_W) == 0
    assert n_lat % GQA_Q_TILE == 0 and n_lat // GRID_W >= NA_WIN_ROWS

    x_parts = (ctx[0], x[0])
    cc = jnp.zeros((SUBLANES, D_MODEL), F32).at[0].set(c_ctx).at[1].set(c[0])
    mod = _modulation(cc, mod_w, mod_b)
    mod4 = mod[:, :2].reshape(depth, 2, 1, 6 * D_MODEL)

    cos_t, sin_t = _rope_tables(n_ctx, n_lat)
    blk = np.arange(2 * GQA_WIDTH) // HEAD_DIM
    ones_bd = jnp.asarray(blk[:, None] == blk[None, :], BF16)
    ones_rows = jnp.ones((GQA_V_ROWS - HEAD_DIM, n_ctx + n_lat), BF16)

    row = lambda p: p.astype(F32).reshape(depth, 1, -1)
    pad_lanes = lambda p: jnp.pad(row(p), ((0, 0), (0, 0), (0, LANES - 2 * SSD_HEADS)))
    w_cat = _rearranged_w_in(w_in)
    dt_bias_pad = pad_lanes(ssd_dt_bias)
    a_row = pad_lanes(-jnp.exp(ssd_a_log.astype(F32)))
    qk_w = row(jnp.concatenate([jnp.tile(q_norm_w * (ATTN_SCALE * LOG2E), (1, GQA_Q_HEADS)),
                                jnp.tile(k_norm_w, (1, 2 * GQA_KV_HEADS))], axis=-1))
    conv_w_pad = jnp.pad(ssd_conv_w, ((0, 0), (0, SUBLANES - SSD_CONV), (0, 0)))
    d_full = row(jnp.repeat(ssd_d, HEAD_DIM, axis=-1))
    bias_tab = _na_bias_table(na_rpb)
    w_out_b, w_gate_b, w_up_b, w_down_b = (w.astype(BF16) for w in (w_out, ffn_w_gate, ffn_w_up, ffn_w_down))
    score_bound = (1.02 * HEAD_DIM * ATTN_SCALE * LOG2E) * jnp.max(jnp.abs(q_norm_w), axis=-1) * jnp.max(
        jnp.abs(k_norm_w), axis=-1)

    for i in range(depth):
        final = i == depth - 1
        na, z, xconv, dt, gq, gk, gv = _inproj(x_parts, n_ctx + n_lat, mod4, i, row(norm_attn_w), w_cat,
                                               dt_bias_pad, qk_w, cos_t, sin_t, ones_bd, conv_w_pad,
                                               row(ssd_conv_b))
        yf, yb = _ssd_scan(xconv, dt, a_row, i, n_ctx)
        ya = _neighbourhood_attention(na, bias_tab, i, n_ctx)
        gvt = jnp.concatenate([part for g in range(GQA_KV_HEADS)
                               for part in (gv[:, g * HEAD_DIM:(g + 1) * HEAD_DIM].T, ones_rows)], axis=0)
        yg = _gqa_attention(gq.T, gk, gvt, n_ctx, score_bound[i])
        x_parts = _out_ffn(ya, yf, yb, xconv, z, yg, x_parts, mod4, i, d_full, row(ssd_norm_w), w_out_b,
                           row(norm_ffn_w), w_gate_b, w_up_b, w_down_b, final_norm_w.reshape(1, -1), final)
    return x_parts[None]
```

```python
import functools
import math

import jax
import jax.numpy as jnp
import numpy as np
from jax import lax
from jax.experimental import pallas as pl
from jax.experimental.pallas import tpu as pltpu

F32 = jnp.float32
BF16 = jnp.bfloat16

D_MODEL = 1024
GRID_W = 64
HEAD_DIM = 64
NA_WIDTH = 256
NA_HEADS = 4
NA_WIN_ROWS = 8
NA_WIN_COLS = 16
SSD_WIDTH = 512
SSD_HEADS = 8
SSD_GROUPS = 2
SSD_STATE = 128
SSD_CONV = 5
SSD_CHUNK = 128
SSD_CONV_CH = SSD_WIDTH + 2 * SSD_GROUPS * SSD_STATE
GQA_WIDTH = 256
GQA_Q_HEADS = 4
GQA_KV_HEADS = 2
ROPE_THETA = 10000.0
ROPE_PAIRS = HEAD_DIM // 4
FFN_HIDDEN = 2816
EPS = 1e-6
ATTN_SCALE = HEAD_DIM ** -0.5
LOG2E = math.log2(math.e)

LANES = 128
SUBLANES = 8
VMEM_LIMIT_BYTES = 56 * 1024 * 1024

ROW_TILE = 256
MOD_COL_TILE = 2048
SSD_CHUNKS_PER_STEP = 2
NA_ROWS_PER_STEP = 4
GQA_Q_TILE = 256
GQA_V_ROWS = HEAD_DIM + 16
GQA_K_SUB = 256
GQA_BOUNDED_LOG2 = 60.0
GQA_PV_LAG = 2
GQA_BOUNDED_STEPS_PER_TRIP = 9
GQA_STEPS_PER_TRIP = 8

C_NA = 0
C_Z = C_NA + 3 * NA_WIDTH
C_XBC = C_Z + SSD_WIDTH
C_DT = C_XBC + SSD_CONV_CH
C_GQ = C_DT + LANES
C_GK = C_GQ + GQA_WIDTH
C_GV = C_GK + 2 * LANES
C_END = C_GV + LANES


def _silu(v):
    return v * (1.0 / (1.0 + jnp.exp(-v)))


def _softplus(v):
    return jnp.maximum(v, 0.0) + jnp.log(1.0 + jnp.exp(-jnp.abs(v)))


def _split3(v):
    hi = v.astype(BF16)
    r1 = v - hi.astype(F32)
    mid = r1.astype(BF16)
    lo = (r1 - mid.astype(F32)).astype(BF16)
    return hi, mid, lo


def _dot(a, b):
    return jnp.dot(a, b, preferred_element_type=F32)


def _dot_nt(a, b):
    return lax.dot_general(a, b, (((1,), (1,)), ((), ())), preferred_element_type=F32)


def _dot_tn(a, b):
    return lax.dot_general(a, b, (((0,), (0,)), ((), ())), preferred_element_type=F32)


def _exact_dot(v, sel):
    hi, mid, lo = _split3(v)
    return _dot(hi, sel) + _dot(mid, sel) + _dot(lo, sel)


def _spread_dot(v, sel):
    hi = v.astype(BF16)
    lo = (v - hi.astype(F32)).astype(BF16)
    return _dot(hi, sel) + _dot(lo, sel)


def _exact_dot_lhs(sel, v):
    hi, mid, lo = _split3(v)
    return _dot(sel, hi) + _dot(sel, mid) + _dot(sel, lo)


def _params(*sem):
    return pltpu.CompilerParams(dimension_semantics=sem, vmem_limit_bytes=VMEM_LIMIT_BYTES)


def _resident(shape, index_map):
    return pl.BlockSpec(shape, index_map, pipeline_mode=pl.Buffered(1))


def _layer_spec(stacked, layer, resident=False):
    shape = stacked.shape[1:]
    index_map = lambda *_: (layer,) + (0,) * len(shape)
    if resident:
        return pl.BlockSpec((None,) + shape, index_map, pipeline_mode=pl.Buffered(1))
    return pl.BlockSpec((None,) + shape, index_map)


def _mod_kernel(cc_ref, w_ref, b_ref, o_ref):
    a = _silu(cc_ref[...])
    o_ref[0] = jnp.dot(a, w_ref[0], preferred_element_type=F32) + b_ref[0]


def _modulation(cc, mod_w, mod_b):
    depth = mod_w.shape[0]
    cols = MOD_COL_TILE
    ncol = mod_w.shape[2] // cols
    return pl.pallas_call(
        _mod_kernel,
        grid=(depth, ncol),
        in_specs=[
            pl.BlockSpec((SUBLANES, D_MODEL), lambda l, j: (0, 0)),
            pl.BlockSpec((1, D_MODEL, cols), lambda l, j: (l, 0, j)),
            pl.BlockSpec((1, 1, cols), lambda l, j: (l, 0, j)),
        ],
        out_specs=pl.BlockSpec((1, SUBLANES, cols), lambda l, j: (l, 0, j)),
        out_shape=jax.ShapeDtypeStruct((depth, SUBLANES, ncol * cols), F32),
        compiler_params=_params("arbitrary", "arbitrary"),
        name="modulation",
    )(cc, mod_w, mod_b.reshape(depth, 1, -1))


def _residual_operands(x_parts, skip):
    ctx_spec = pl.BlockSpec((ROW_TILE, D_MODEL), lambda i: (0, 0))
    if isinstance(x_parts, tuple):
        lat_spec = pl.BlockSpec((ROW_TILE, D_MODEL), lambda i: (jnp.maximum(i + skip - 1, 0), 0))
        return list(x_parts), [ctx_spec, lat_spec], skip == 0
    row_spec = pl.BlockSpec((ROW_TILE, D_MODEL), lambda i: (i + skip, 0))
    return [x_parts, x_parts], [ctx_spec, row_spec], False


def _inproj_kernel(xc_ref, xl_ref, prev_ref, next_ref, mod_ref, nw_ref, w_ref, dtb_ref, qkw_ref, cos_ref, sin_ref,
                   ones_ref, cw_ref, cb_ref, na_ref, z_ref, xconv_ref, dt_ref, gq_ref, gk_ref, gv_ref, *, pick_ctx):
    i = pl.program_id(0)
    n = pl.num_programs(0)
    rows = xl_ref.shape[0]
    x = jnp.where(i == 0, xc_ref[...], xl_ref[...]) if pick_ctx else xl_ref[...]
    x = jnp.concatenate([prev_ref[...], x, next_ref[...]], axis=0)
    ms = jnp.mean(x * x, axis=-1, keepdims=True)
    xn = x * lax.rsqrt(ms + EPS) * nw_ref[...]
    sh = mod_ref[:, 0:D_MODEL]
    sc = mod_ref[:, D_MODEL:2 * D_MODEL]
    h = (xn * (1.0 + sc) + sh).astype(BF16)
    xbc_ext = _dot(h, w_ref[:, C_XBC:C_DT])
    h_tile = h[SUBLANES:SUBLANES + rows]
    u = jnp.concatenate([_dot(h_tile, w_ref[:, C_NA:C_XBC]), xbc_ext[SUBLANES:SUBLANES + rows],
                         _dot(h_tile, w_ref[:, C_DT:C_END])], axis=-1)
    na_ref[...] = u[:, C_NA:C_Z].astype(BF16)
    z_ref[...] = u[:, C_Z:C_XBC]
    has_prev = i >= 2
    has_next = jnp.logical_and(i >= 1, i < n - 1)
    xbc = u[:, C_XBC:C_DT]
    ext = jnp.concatenate([jnp.where(has_prev, xbc_ext[0:SUBLANES], 0.0), xbc,
                           jnp.where(has_next, xbc_ext[SUBLANES + rows:], 0.0)], axis=0)
    total = rows + 2 * SUBLANES
    half = SSD_CONV // 2
    acc = cb_ref[...] + cw_ref[half:half + 1, :] * xbc
    for j in range(SSD_CONV):
        if j != half:
            shifted = pltpu.roll(ext, (half - j) % total, 0)
            acc = acc + cw_ref[j:j + 1, :] * shifted[SUBLANES:SUBLANES + rows]
    xconv_ref[...] = _silu(acc)
    dt_ref[...] = _softplus(u[:, C_DT:C_GQ] + dtb_ref[...])
    gv_ref[...] = u[:, C_GV:C_END].astype(BF16)
    g = u[:, C_GQ:C_GV]
    gsq = g * g
    hi = gsq.astype(BF16)
    lo = (gsq - hi.astype(F32)).astype(BF16)
    ss = _dot(hi, ones_ref[...]) + _dot(lo, ones_ref[...])
    gn = g * lax.rsqrt(ss * (1.0 / HEAD_DIM) + EPS) * qkw_ref[...]
    width = gn.shape[-1]
    lane = lax.broadcasted_iota(jnp.int32, gn.shape, 1)
    first = (lane % (2 * ROPE_PAIRS)) < ROPE_PAIRS
    partner = jnp.where(first, pltpu.roll(gn, width - ROPE_PAIRS, 1), pltpu.roll(gn, ROPE_PAIRS, 1))
    cos = jnp.concatenate([cos_ref[...]] * (width // LANES), axis=-1)
    sin = jnp.concatenate([sin_ref[...]] * (width // LANES), axis=-1)
    gr = gn * cos + partner * sin
    gq_ref[...] = gr[:, :GQA_WIDTH].astype(BF16)
    gk_ref[...] = gr[:, GQA_WIDTH:].astype(BF16)


def _inproj(x_parts, t, mod4, layer, norm_w, w_cat, dt_bias_pad, qk_w, cos_t, sin_t, ones_bd, conv_w_pad, conv_b):
    n = t // ROW_TILE
    row = lambda i: (i, 0)
    const = lambda i: (0, 0)
    outs = [
        (3 * NA_WIDTH, BF16), (SSD_WIDTH, F32), (SSD_CONV_CH, F32), (LANES, F32),
        (GQA_WIDTH, BF16), (2 * LANES, BF16), (LANES, BF16),
    ]
    x_arrays, x_specs, pick_ctx = _residual_operands(x_parts, 0)
    halo_src = x_arrays[1]
    per = ROW_TILE // SUBLANES
    first = (lambda i: (i - 1) * per) if isinstance(x_parts, tuple) else (lambda i: i * per)
    last_blk = halo_src.shape[0] // SUBLANES - 1
    halo_specs = [
        pl.BlockSpec((SUBLANES, D_MODEL), lambda i: (jnp.clip(first(i) - 1, 0, last_blk), 0)),
        pl.BlockSpec((SUBLANES, D_MODEL), lambda i: (jnp.clip(first(i) + per, 0, last_blk), 0)),
    ]
    return pl.pallas_call(
        functools.partial(_inproj_kernel, pick_ctx=pick_ctx),
        grid=(n,),
        in_specs=x_specs + halo_specs + [
            pl.BlockSpec((None, None, 1, 6 * D_MODEL), lambda i: (layer, jnp.minimum(i, 1), 0, 0)),
            _layer_spec(norm_w, layer),
            _layer_spec(w_cat, layer, resident=True),
            _layer_spec(dt_bias_pad, layer),
            _layer_spec(qk_w, layer),
            pl.BlockSpec((ROW_TILE, LANES), row),
            pl.BlockSpec((ROW_TILE, LANES), row),
            _resident((2 * GQA_WIDTH, 2 * GQA_WIDTH), const),
            _layer_spec(conv_w_pad, layer),
            _layer_spec(conv_b, layer),
        ],
        out_specs=[pl.BlockSpec((ROW_TILE, w), row) for w, _ in outs],
        out_shape=[jax.ShapeDtypeStruct((t, w), d) for w, d in outs],
        compiler_params=_params("parallel"),
        name="inproj",
    )(*x_arrays, halo_src, halo_src, mod4, norm_w, w_cat, dt_bias_pad, qk_w, cos_t, sin_t, ones_bd,
      conv_w_pad, conv_b)


def _ssd_direction(xbc, dt, a_row, tri, expand, s_ref, lane0, backward):
    q = SSD_CHUNK
    x = xbc[:, :SSD_WIDTH]
    nb = SSD_GROUPS * SSD_STATE
    bmat = xbc[:, SSD_WIDTH:SSD_WIDTH + nb].astype(BF16)
    cmat = xbc[:, SSD_WIDTH + nb:].astype(BF16)
    cum = _exact_dot_lhs(tri, dt * a_row)
    yield
    last = 0 if backward else q - 1
    cum_t = cum.T

    def spread(mat_t):
        rows = [jnp.broadcast_to(mat_t[lane0 + h:lane0 + h + 1, :], (HEAD_DIM, q)) for h in range(SSD_HEADS)]
        return jnp.concatenate(rows, axis=0).T

    dt_full = _spread_dot(dt, expand)
    cum_full = spread(cum_t)
    yield
    ea_full = jnp.exp(cum_full)
    te_full = jnp.exp(cum_full[last:last + 1, :] - cum_full)
    xr = x * dt_full
    xr_b = xr.astype(BF16)
    xt_b = (xr * te_full).astype(BF16)
    ti =lax.broadcasted_iota(jnp.int32, (q, q), 0)
    si = lax.broadcasted_iota(jnp.int32, (q, q), 1)
    keep = (si >= ti) if backward else (si <= ti)
    lane = lax.broadcasted_iota(jnp.int32, (q, LANES), 1)
    heads_per_group = SSD_HEADS // SSD_GROUPS
    width_g = heads_per_group * HEAD_DIM
    pieces = []
    decay_total = ea_full[last:last + 1, :]
    for g in range(SSD_GROUPS):
        bg = bmat[:, g * SSD_STATE:(g + 1) * SSD_STATE]
        cg = cmat[:, g * SSD_STATE:(g + 1) * SSD_STATE]
        gmat = _dot_nt(cg, bg)
        s_old = s_ref[:, g * width_g:(g + 1) * width_g]
        y_off = _dot(cg, s_old.astype(BF16)) * ea_full[:, g * width_g:(g + 1) * width_g]
        s_new = _dot_tn(bg, xt_b[:, g * width_g:(g + 1) * width_g])
        s_ref[:, g * width_g:(g + 1) * width_g] = decay_total[:, g * width_g:(g + 1) * width_g] * s_old + s_new
        yield
        for pair in range(heads_per_group // 2):
            col = g * width_g + pair * LANES
            xr_pair = xr_b[:, col:col + LANES]
            ys = []
            for r in range(2):
                hl = lane0 + g * heads_per_group + 2 * pair + r
                seg = jnp.broadcast_to(cum[:, hl:hl + 1], (q, q)) - jnp.broadcast_to(cum_t[hl:hl + 1, :], (q, q))
                dec = jnp.exp(jnp.where(keep, seg, -jnp.inf))
                ys.append(_dot((gmat * dec).astype(BF16), xr_pair))
            y_diag = jnp.where(lane < HEAD_DIM, ys[0], ys[1])
            pieces.append(y_diag + y_off[:, pair * LANES:(pair + 1) * LANES])
            yield
    return jnp.concatenate(pieces, axis=-1)


def _run_interleaved(*stage_generators):
    results = [None] * len(stage_generators)
    live = list(range(len(stage_generators)))
    while live:
        for idx in list(live):
            try:
                next(stage_generators[idx])
            except StopIteration as done:
                results[idx] = done.value
                live.remove(idx)
    return results


def _ssd_kernel(xf_ref, dtf_ref, xb_ref, dtb_ref, a_ref, tril_ref, triu_ref, ef_ref, eb_ref,
                yf_ref, yb_ref, sf_ref, sb_ref):
    @pl.when(pl.program_id(0) == 0)
    def _():
        sf_ref[...] = jnp.zeros_like(sf_ref)
        sb_ref[...] = jnp.zeros_like(sb_ref)

    a_row = a_ref[...]
    q = SSD_CHUNK
    order_f = list(range(SSD_CHUNKS_PER_STEP))
    order_b = order_f[::-1]
    stages = []
    for cf, cb in zip(order_f, order_b):
        rf, rb = slice(cf * q, (cf + 1) * q), slice(cb * q, (cb + 1) * q)
        stages.append(_ssd_direction(xf_ref[rf, :], dtf_ref[rf, :], a_row, tril_ref[...], ef_ref[...],
                                     sf_ref, 0, False))
        stages.append(_ssd_direction(xb_ref[rb, :], dtb_ref[rb, :], a_row, triu_ref[...], eb_ref[...],
                                     sb_ref, SSD_HEADS, True))
    ys = _run_interleaved(*stages)
    for k, (cf, cb) in enumerate(zip(order_f, order_b)):
        yf_ref[cf * q:(cf + 1) * q, :] = ys[2 * k]
        yb_ref[cb * q:(cb + 1) * q, :] = ys[2 * k + 1]


def _ssd_scan(xconv, dt, a_row, layer, n_ctx):
    t = xconv.shape[0]
    q = SSD_CHUNK
    rows = SSD_CHUNKS_PER_STEP * q
    assert n_ctx % rows == 0 and t % rows == 0
    n = t // rows
    n_ctx_blocks = n_ctx // rows
    r = jnp.arange(q)
    tril = (r[None, :] <= r[:, None]).astype(BF16)
    triu = (r[None, :] >= r[:, None]).astype(BF16)
    lanes = jnp.arange(LANES)[:, None]
    head_of_col = (jnp.arange(SSD_WIDTH) // HEAD_DIM)[None, :]
    expand_f = (lanes == head_of_col).astype(BF16)
    expand_b = (lanes == head_of_col + SSD_HEADS).astype(BF16)

    def fwd(i):
        return (i, 0)

    def bwd(i):
        return (jnp.where(i < n_ctx_blocks, n_ctx_blocks - 1 - i, n - 1 - (i - n_ctx_blocks)), 0)

    const = lambda i: (0, 0)
    return pl.pallas_call(
        _ssd_kernel,
        grid=(n,),
        in_specs=[
            pl.BlockSpec((rows, SSD_CONV_CH), fwd),
            pl.BlockSpec((rows, LANES), fwd),
            pl.BlockSpec((rows, SSD_CONV_CH), bwd),
            pl.BlockSpec((rows, LANES), bwd),
            _layer_spec(a_row, layer),
            pl.BlockSpec((q, q), const),
            pl.BlockSpec((q, q), const),
            pl.BlockSpec((LANES, SSD_WIDTH), const),
            pl.BlockSpec((LANES, SSD_WIDTH), const),
        ],
        out_specs=[pl.BlockSpec((rows, SSD_WIDTH), fwd), pl.BlockSpec((rows, SSD_WIDTH), bwd)],
        out_shape=[jax.ShapeDtypeStruct((t, SSD_WIDTH), F32)] * 2,
        scratch_shapes=[pltpu.VMEM((SSD_STATE, SSD_WIDTH), F32)] * 2,
        compiler_params=_params("arbitrary"),
        name="ssd_scan",
    )(xconv, dt, xconv, dt, a_row, tril, triu, expand_f, expand_b)


def _na_kernel(q_ref, k_ref, v_ref, bias_ref, o_ref, *, n_ctx, n_rows):
    i = pl.program_id(0)
    is_ctx = i == 0
    kc = k_ref[0:n_ctx, :]
    vc = v_ref[0:n_ctx, :]
    lane = lax.broadcasted_iota(jnp.int32, (GRID_W, NA_WIDTH), 1)
    mine = [(lane >= h * HEAD_DIM) & (lane < (h + 1) * HEAD_DIM) for h in range(NA_HEADS)]
    win = NA_WIN_ROWS * GRID_W

    def lane_tiles(a):
        return [a[:, c * LANES:(c + 1) * LANES] for c in range(a.shape[-1] // LANES)]

    def grid_row(j):
        r = jnp.maximum((i - 1) * NA_ROWS_PER_STEP + j, 0)
        r_start = jnp.clip(r - NA_WIN_ROWS // 2, 0, n_rows - NA_WIN_ROWS)
        variant = jnp.where(is_ctx, NA_WIN_ROWS, r_start - r + NA_WIN_ROWS - 1)
        start = pl.multiple_of(n_ctx + r_start * GRID_W, GRID_W)
        kw = k_ref[pl.ds(start, win), :]
        vw = v_ref[pl.ds(start, win), :]
        qj = q_ref[j * GRID_W:(j + 1) * GRID_W, :]
        qm = jnp.concatenate([jnp.where(mine[h], qj, jnp.zeros_like(qj)) for h in range(NA_HEADS)], axis=0)
        s_w = _dot_nt(qm, kw) + bias_ref[variant]
        s_c = _dot_nt(qm, kc)
        yield
        m = functools.reduce(jnp.maximum, lane_tiles(s_w) + lane_tiles(s_c))
        m = jnp.broadcast_to(jnp.max(m, axis=-1, keepdims=True), m.shape)
        p_w = jnp.exp(s_w - jnp.concatenate([m] * (s_w.shape[-1] // LANES), axis=-1))
        p_c = jnp.exp(s_c - jnp.concatenate([m] * (s_c.shape[-1] // LANES), axis=-1))
        l = jnp.sum(functools.reduce(jnp.add, lane_tiles(p_w) + lane_tiles(p_c)), axis=-1, keepdims=True)
        y = (_dot(p_w.astype(BF16), vw) + _dot(p_c.astype(BF16), vc)) * (1.0 / l)
        out = y[(NA_HEADS - 1) * GRID_W:]
        for h in range(NA_HEADS - 2, -1, -1):
            out = jnp.where(mine[h], y[h * GRID_W:(h + 1) * GRID_W], out)
        o_ref[j * GRID_W:(j + 1) * GRID_W, :] = out.astype(o_ref.dtype)

    _run_interleaved(*[grid_row(j) for j in range(NA_ROWS_PER_STEP)])


def _na_bias_table(rpb):
    depth = rpb.shape[0]
    col = np.arange(GRID_W)
    c_start = np.clip(col - NA_WIN_COLS // 2, 0, GRID_W - NA_WIN_COLS)
    in_win = (col[None, :] >= c_start[:, None]) & (col[None, :] < c_start[:, None] + NA_WIN_COLS)
    dc = np.clip(col[None, :] - col[:, None] + NA_WIN_COLS - 1, 0, 2 * NA_WIN_COLS - 2)
    n_dc = 2 * NA_WIN_COLS - 1
    n_dr = 2 * NA_WIN_ROWS - 1
    onehot = (dc[None, :, :] == np.arange(n_dc)[:, None, None]).astype(np.float32).reshape(n_dc, GRID_W * GRID_W)
    t2 = jnp.dot(rpb.reshape(depth * NA_HEADS * n_dr, n_dc).astype(F32), onehot, precision=lax.Precision.HIGHEST)
    t2 = jnp.where(in_win[None, None, None], t2.reshape(depth, NA_HEADS, n_dr, GRID_W, GRID_W), -jnp.inf)
    tab = jnp.stack([t2[:, :, v:v + NA_WIN_ROWS] for v in range(NA_WIN_ROWS)], axis=1)
    tab = tab.transpose(0, 1, 2, 4, 3, 5).reshape(depth, NA_WIN_ROWS, NA_HEADS * GRID_W, NA_WIN_ROWS * GRID_W)
    masked = jnp.full((depth, 1) + tab.shape[2:], -jnp.inf, F32)
    return jnp.concatenate([tab, masked], axis=1)


def _neighbourhood_attention(na, bias_tab, layer, n_ctx):
    t = na.shape[0]
    n_rows = (t - n_ctx) // GRID_W
    step_rows = NA_ROWS_PER_STEP * GRID_W
    n = t // step_rows
    return pl.pallas_call(
        functools.partial(_na_kernel, n_ctx=n_ctx, n_rows=n_rows),
        grid=(n,),
        in_specs=[
            pl.BlockSpec((step_rows, NA_WIDTH), lambda i: (i, 0)),
            _resident((t, NA_WIDTH), lambda i: (0, 1)),
            _resident((t, NA_WIDTH), lambda i: (0, 2)),
            _layer_spec(bias_tab, layer, resident=True),
        ],
        out_specs=pl.BlockSpec((step_rows, NA_WIDTH), lambda i: (i, 0)),
        out_shape=jax.ShapeDtypeStruct((t, NA_WIDTH), BF16),
        compiler_params=_params("parallel"),
        name="neighbourhood_attention",
    )(na, na, na, bias_tab)


def _gqa_kernel(qt_ref, k_ref, vt_ref, o_ref, acc_ref, s0_ref, s1_ref, p0_ref, p1_ref, *, n_ctx_sub, n_sub):
    tq = qt_ref.shape[1]
    qt = qt_ref[...]
    row = lax.broadcasted_iota(jnp.int32, qt.shape, 0)
    top = row < HEAD_DIM
    zero = jnp.zeros_like(qt)
    qt2 = jnp.concatenate([jnp.where(top, qt, zero), jnp.where(top, zero, qt)], axis=1)
    acc_ref[...] = jnp.zeros_like(acc_ref)
    s_ref = (s0_ref, s1_ref)
    p_ref = (p0_ref, p1_ref)

    def keys(j, size):
        return pl.ds(pl.multiple_of(j * size, LANES), size)

    def score(key_rows, slot):
        size = key_rows.size
        s = _dot(k_ref[key_rows, :], qt2)
        s_ref[slot][0:size, :] = s
        return jnp.max(s.reshape(size // SUBLANES, SUBLANES, 2 * tq), axis=0)

    def softmax(size, slot, m_old, part_max):
        m_new = jnp.maximum(m_old, jnp.max(part_max, axis=0, keepdims=True))
        s = s_ref[slot][0:size, :].reshape(size // SUBLANES, SUBLANES, 2 * tq)
        p_ref[slot][0:size, :] = jnp.exp2(s - m_new[None]).reshape(size, 2 * tq).astype(BF16)
        return m_new, jnp.exp2(m_old - m_new)

    def accumulate(key_rows, slot, alpha):
        size = key_rows.size
        acc = acc_ref[...].reshape(GQA_V_ROWS // SUBLANES, SUBLANES, 2 * tq) * alpha[None]
        acc_ref[...] = acc.reshape(GQA_V_ROWS, 2 * tq) + _dot(vt_ref[:, key_rows], p_ref[slot][0:size, :])

    m_init = jnp.full((SUBLANES, 2 * tq), -jnp.inf, F32)
    sub = GQA_K_SUB
    assert n_sub >= 3

    @pl.when(pl.program_id(1) == 0)
    def _():
        m = m_init
        for j in range(n_ctx_sub):
            m, alpha = softmax(sub, 0, m, score(keys(j, sub), 0))
            accumulate(keys(j, sub), 0, alpha)

    @pl.when(pl.program_id(1) > 0)
    def _():
        part0 = score(keys(0, sub), 0)
        part1 = score(keys(1, sub), 1)
        m, alpha = softmax(sub, 0, m_init, part0)

        def step(t, slot, carry):
            m, alpha, part = carry
            part_next = score(keys(t, sub), slot)
            accumulate(keys(t - 2, sub), slot, alpha)
            return softmax(sub, 1 - slot, m, part) + (part_next,)

        def trip(n, carry):
            for u in range(GQA_STEPS_PER_TRIP):
                carry = step(2 + n * GQA_STEPS_PER_TRIP + u, u % 2, carry)
            return carry

        n_trips = (n_sub - 2) // GQA_STEPS_PER_TRIP
        carry = lax.fori_loop(0, n_trips, trip, (m, alpha, part1))
        for t in range(2 + n_trips * GQA_STEPS_PER_TRIP, n_sub):
            carry = step(t, t % 2, carry)
        m, alpha, part = carry
        accumulate(keys(n_sub - 2, sub), (n_sub - 2) % 2, alpha)
        m, alpha = softmax(sub, (n_sub - 1) % 2, m, part)
        accumulate(keys(n_sub - 1, sub), (n_sub - 1) % 2, alpha)

    acc = acc_ref[...]
    denom = acc[HEAD_DIM:HEAD_DIM + SUBLANES]
    o_t = (acc[:HEAD_DIM].reshape(HEAD_DIM // SUBLANES, SUBLANES, 2 * tq) / denom[None]).reshape(HEAD_DIM, 2 * tq)
    o_ref[...] = jnp.concatenate([o_t[:, :tq], o_t[:, tq:]], axis=0).T.astype(o_ref.dtype)


def _gqa_bounded_kernel(qt_ref, k_ref, vt_ref, o_ref, acc_ref, *p_ref, n_ctx_sub, n_sub):
    tq = qt_ref.shape[1]
    qt = qt_ref[...]
    row = lax.broadcasted_iota(jnp.int32, qt.shape, 0)
    top = row < HEAD_DIM
    zero = jnp.zeros_like(qt)
    qt2 = jnp.concatenate([jnp.where(top, qt, zero), jnp.where(top, zero, qt)], axis=1)
    acc_ref[...] = jnp.zeros_like(acc_ref)
    sub = GQA_K_SUB
    n_slots = len(p_ref)
    lag = n_slots - 1
    steps = GQA_BOUNDED_STEPS_PER_TRIP
    assert steps % n_slots == 0 and n_sub > lag

    def keys(j):
        return pl.ds(pl.multiple_of(j * sub, sub), sub)

    def probs(j, slot):
        p_ref[slot][...] = jnp.exp2(_dot(k_ref[keys(j), :], qt2)).astype(BF16)

    def accumulate(j, slot):
        acc_ref[...] += _dot(vt_ref[:, keys(j)], p_ref[slot][...])

    @pl.when(pl.program_id(1) == 0)
    def _():
        for j in range(n_ctx_sub):
            probs(j, 0)
            accumulate(j, 0)

    @pl.when(pl.program_id(1) > 0)
    def _():
        for t in range(lag):
            probs(t, t % n_slots)

        def trip(n, carry):
            for u in range(steps):
                t = lag + n * steps + u
                probs(t, (lag + u) % n_slots)
                accumulate(t - lag, u % n_slots)
            return carry

        n_trips = (n_sub - lag) // steps
        lax.fori_loop(0, n_trips, trip, 0)
        for t in range(lag + n_trips * steps, n_sub):
            probs(t, t % n_slots)
            accumulate(t - lag, (t - lag) % n_slots)
        for t in range(n_sub - lag, n_sub):
            accumulate(t, t % n_slots)

    acc = acc_ref[...]
    denom = acc[HEAD_DIM:HEAD_DIM + SUBLANES]
    o_t = (acc[:HEAD_DIM].reshape(HEAD_DIM // SUBLANES, SUBLANES, 2 * tq) / denom[None]).reshape(HEAD_DIM, 2 * tq)
    o_ref[...] = jnp.concatenate([o_t[:, :tq], o_t[:, tq:]], axis=0).T.astype(o_ref.dtype)


def _gqa_dispatch_kernel(bounded_ref, qt_ref, k_ref, vt_ref, o_ref, acc_ref, s0_ref, s1_ref, *p_ref,
                         n_ctx_sub, n_sub):
    @pl.when(bounded_ref[0] != 0)
    def _():
        _gqa_bounded_kernel(qt_ref, k_ref, vt_ref, o_ref, acc_ref, *p_ref, n_ctx_sub=n_ctx_sub, n_sub=n_sub)

    @pl.when(bounded_ref[0] == 0)
    def _():
        _gqa_kernel(qt_ref, k_ref, vt_ref, o_ref, acc_ref, s0_ref, s1_ref, p_ref[0], p_ref[1],
                    n_ctx_sub=n_ctx_sub, n_sub=n_sub)


def _gqa_attention(gqt, gk, gvt, n_ctx, score_bound):
    nq = gqt.shape[1]
    n_keys = gk.shape[0]
    tq = GQA_Q_TILE
    assert n_ctx == tq and n_ctx % GQA_K_SUB == 0 and n_keys % GQA_K_SUB == 0 and nq % tq == 0
    bounded = (score_bound <= GQA_BOUNDED_LOG2).astype(jnp.int32).reshape(1)
    return pl.pallas_call(
        functools.partial(_gqa_dispatch_kernel, n_ctx_sub=n_ctx // GQA_K_SUB, n_sub=n_keys // GQA_K_SUB),
        grid=(GQA_KV_HEADS, nq // tq),
        in_specs=[
            pl.BlockSpec(memory_space=pltpu.SMEM),
            pl.BlockSpec((LANES, tq), lambda g, i: (g, i)),
            pl.BlockSpec((n_keys, LANES), lambda g, i: (0, g)),
            pl.BlockSpec((GQA_V_ROWS, n_keys), lambda g, i: (g, 0)),
        ],
        out_specs=pl.BlockSpec((tq, LANES), lambda g, i: (i, g)),
        out_shape=jax.ShapeDtypeStruct((nq, GQA_WIDTH), BF16),
        scratch_shapes=[pltpu.VMEM((GQA_V_ROWS, 2 * tq), F32)] + [pltpu.VMEM((GQA_K_SUB, 2 * tq), F32)] * 2
        + [pltpu.VMEM((GQA_K_SUB, 2 * tq), BF16)] * (GQA_PV_LAG + 1),
        compiler_params=_params("arbitrary", "arbitrary"),
        name="gqa_attention",
    )(bounded, gqt, gk, gvt)


def _out_ffn_kernel(ya_ref, yf_ref, yb_ref, xs_ref, z_ref, yg_ref, xc_ref, xl_ref, mod_ref,
                    dskip_ref, snw_ref, wo_ref, fnw_ref, wg_ref, wu_ref, wd_ref, final_ref,
                    o_ref, *, final, pick_ctx):
    x = jnp.where(pl.program_id(0) == 0, xc_ref[...], xl_ref[...]) if pick_ctx else xl_ref[...]
    y = yf_ref[...] + yb_ref[...] + dskip_ref[...] * xs_ref[...]
    y = y * _silu(z_ref[...])
    ms = jnp.mean(y * y, axis=-1, keepdims=True)
    y = y * lax.rsqrt(ms + EPS) * snw_ref[...]
    mix = jnp.concatenate([ya_ref[...], y.astype(BF16), yg_ref[...]], axis=-1)
    g_m = mod_ref[:, 2 * D_MODEL:3 * D_MODEL]
    sh_f = mod_ref[:, 3 * D_MODEL:4 * D_MODEL]
    sc_f = mod_ref[:, 4 * D_MODEL:5 * D_MODEL]
    g_f = mod_ref[:, 5 * D_MODEL:6 * D_MODEL]
    x1 = x + g_m * _dot(mix, wo_ref[...])
    ms1 = jnp.mean(x1 * x1, axis=-1, keepdims=True)
    hf = (x1 * lax.rsqrt(ms1 + EPS) * fnw_ref[...] * (1.0 + sc_f) + sh_f).astype(BF16)
    act = (_silu(_dot(hf, wg_ref[...])) * _dot(hf, wu_ref[...])).astype(BF16)
    x2 = x1 + g_f * _dot(act, wd_ref[...])
    if final:
        ms2 = jnp.mean(x2 * x2, axis=-1, keepdims=True)
        x2 = x2 * lax.rsqrt(ms2 + EPS) * final_ref[...]
    o_ref[...] = x2


def _out_ffn(ya, yf, yb, xconv, z, yg, x_parts, mod4, layer, d_full, ssd_nw, w_out, ffn_nw,
             w_gate, w_up, w_down, final_nw, final):
    t = ya.shape[0]
    skip = 1 if final else 0
    n = t // ROW_TILE - skip
    row = lambda i: (i + skip, 0)
    const = lambda i: (0, 0)
    x_arrays, x_specs, pick_ctx = _residual_operands(x_parts, skip)
    return pl.pallas_call(
        functools.partial(_out_ffn_kernel, final=final, pick_ctx=pick_ctx),
        grid=(n,),
        in_specs=[
            pl.BlockSpec((ROW_TILE, NA_WIDTH), row),
            pl.BlockSpec((ROW_TILE, SSD_WIDTH), row),
            pl.BlockSpec((ROW_TILE, SSD_WIDTH), row),
            pl.BlockSpec((ROW_TILE, SSD_WIDTH), row),
            pl.BlockSpec((ROW_TILE, SSD_WIDTH), row),
            pl.BlockSpec((ROW_TILE, GQA_WIDTH), row),
        ] + x_specs + [
            pl.BlockSpec((None, None, 1, 6 * D_MODEL), lambda i: (layer, jnp.minimum(i + skip, 1), 0, 0)),
            _layer_spec(d_full, layer),
            _layer_spec(ssd_nw, layer),
            _layer_spec(w_out, layer, resident=True),
            _layer_spec(ffn_nw, layer),
            _layer_spec(w_gate, layer, resident=True),
            _layer_spec(w_up, layer, resident=True),
            _layer_spec(w_down, layer, resident=True),
            pl.BlockSpec((1, D_MODEL), const),
        ],
        out_specs=pl.BlockSpec((ROW_TILE, D_MODEL), lambda i: (i, 0)),
        out_shape=jax.ShapeDtypeStruct((n * ROW_TILE, D_MODEL), F32),
        compiler_params=_params("parallel"),
        name="out_ffn",
    )(ya, yf, yb, xconv, z, yg, *x_arrays, mod4, d_full, ssd_nw, w_out, ffn_nw, w_gate, w_up, w_down, final_nw)


def _rearranged_w_in(w):
    na_in = 3 * NA_WIDTH
    o_z = na_in
    o_xbc = o_z + SSD_WIDTH
    o_dt = o_xbc + SSD_CONV_CH
    o_gq = o_dt + 2 * SSD_HEADS
    o_gk = o_gq + GQA_WIDTH
    o_gv = o_gk + GQA_KV_HEADS * HEAD_DIM
    w = w.astype(BF16)
    q_na = w[..., :NA_WIDTH] * ATTN_SCALE
    dt_pad = jnp.zeros(w.shape[:-1] + (LANES - 2 * SSD_HEADS,), BF16)
    k_heads = [w[..., o_gk + h * HEAD_DIM:o_gk + (h + 1) * HEAD_DIM] for h in range(GQA_KV_HEADS)]
    return jnp.concatenate(
        [q_na, w[..., NA_WIDTH:o_dt], w[..., o_dt:o_gq], dt_pad, w[..., o_gq:o_gk]]
        + [p for h in k_heads for p in (h, h)] + [w[..., o_gv:]], axis=-1)


def _rope_tables(n_ctx, n_lat):
    f32 = np.float32
    freqs = f32(ROPE_THETA) ** (-np.arange(ROPE_PAIRS, dtype=f32) / f32(ROPE_PAIRS))
    n_rows = n_lat // GRID_W
    half = 2 * ROPE_PAIRS
    sign = np.where(np.arange(half) < ROPE_PAIRS, -1.0, 1.0).astype(f32)

    def tables(n_pos):
        a = np.arange(n_pos, dtype=f32)[:, None] * freqs[None, :]
        a = np.concatenate([a, a], axis=-1)
        return np.cos(a), np.sin(a) * sign[None, :]

    def per_token(by_row, by_col):
        lat = np.concatenate([np.broadcast_to(by_row[:, None, :], (n_rows, GRID_W, half)),
                              np.broadcast_to(by_col[None, :, :], (n_rows, GRID_W, half))], axis=-1)
        return lat.reshape(n_lat, HEAD_DIM)

    cos_r, sin_r = tables(n_rows)
    cos_c, sin_c = tables(GRID_W)
    cos = np.concatenate([np.ones((n_ctx, HEAD_DIM), f32), per_token(cos_r, cos_c)], axis=0)
    sin = np.concatenate([np.zeros((n_ctx, HEAD_DIM), f32), per_token(sin_r, sin_c)], axis=0)
    reps = (1, LANES // HEAD_DIM)
    return jnp.asarray(np.tile(cos, reps), F32), jnp.asarray(np.tile(sin, reps), F32)


def kernel(x, c, ctx, c_ctx, mod_w, mod_b, norm_attn_w, norm_ffn_w, w_in, na_rpb, ssd_conv_w, ssd_conv_b,
           ssd_dt_bias, ssd_a_log, ssd_d, ssd_norm_w, q_norm_w, k_norm_w, w_out, ffn_w_gate, ffn_w_up,
           ffn_w_down, final_norm_w):
    depth = mod_w.shape[0]
    batch, n_lat, _ = x.shape
    n_ctx = ctx.shape[1]
    assert batch == 1 and n_ctx == ROW_TILE and n_lat % (NA_ROWS_PER_STEP * GRID_W) == 0
    assert n_lat % GQA_Q_TILE == 0 and n_lat // GRID_W >= NA_WIN_ROWS

    x_parts = (ctx[0], x[0])
    cc = jnp.zeros((SUBLANES, D_MODEL), F32).at[0].set(c_ctx).at[1].set(c[0])
    mod = _modulation(cc, mod_w, mod_b)
    mod4 = mod[:, :2].reshape(depth, 2, 1, 6 * D_MODEL)

    cos_t, sin_t = _rope_tables(n_ctx, n_lat)
    blk = np.arange(2 * GQA_WIDTH) // HEAD_DIM
    ones_bd = jnp.asarray(blk[:, None] == blk[None, :], BF16)
    ones_rows = jnp.ones((GQA_V_ROWS - HEAD_DIM, n_ctx + n_lat), BF16)

    row = lambda p: p.astype(F32).reshape(depth, 1, -1)
    pad_lanes = lambda p: jnp.pad(row(p), ((0, 0), (0, 0), (0, LANES - 2 * SSD_HEADS)))
    w_cat = _rearranged_w_in(w_in)
    dt_bias_pad = pad_lanes(ssd_dt_bias)
    a_row = pad_lanes(-jnp.exp(ssd_a_log.astype(F32)))
    qk_w = row(jnp.concatenate([jnp.tile(q_norm_w * (ATTN_SCALE * LOG2E), (1, GQA_Q_HEADS)),
                                jnp.tile(k_norm_w, (1, 2 * GQA_KV_HEADS))], axis=-1))
    conv_w_pad = jnp.pad(ssd_conv_w, ((0, 0), (0, SUBLANES - SSD_CONV), (0, 0)))
    d_full = row(jnp.repeat(ssd_d, HEAD_DIM, axis=-1))
    bias_tab = _na_bias_table(na_rpb)
    w_out_b, w_gate_b, w_up_b, w_down_b = (w.astype(BF16) for w in (w_out, ffn_w_gate, ffn_w_up, ffn_w_down))
    score_bound = (1.02 * HEAD_DIM * ATTN_SCALE * LOG2E) * jnp.max(jnp.abs(q_norm_w), axis=-1) * jnp.max(
        jnp.abs(k_norm_w), axis=-1)

    for i in range(depth):
        final = i == depth - 1
        na, z, xconv, dt, gq, gk, gv = _inproj(x_parts, n_ctx + n_lat, mod4, i, row(norm_attn_w), w_cat,
                                               dt_bias_pad, qk_w, cos_t, sin_t, ones_bd, conv_w_pad,
                                               row(ssd_conv_b))
        yf, yb = _ssd_scan(xconv, dt, a_row, i, n_ctx)
        ya = _neighbourhood_attention(na, bias_tab, i, n_ctx)
        gvt = jnp.concatenate([part for g in range(GQA_KV_HEADS)
                               for part in (gv[:, g * HEAD_DIM:(g + 1) * HEAD_DIM].T, ones_rows)], axis=0)
        yg = _gqa_attention(gq.T, gk, gvt, n_ctx, score_bound[i])
        x_parts = _out_ffn(ya, yf, yb, xconv, z, yg, x_parts, mod4, i, d_full, row(ssd_norm_w), w_out_b,
                           row(norm_ffn_w), w_gate_b, w_up_b, w_down_b, final_norm_w.reshape(1, -1), final)
    return x_parts[None]
```

```python
import functools
import math

import jax
import jax.numpy as jnp
import numpy as np
from jax import lax
from jax.experimental import pallas as pl
from jax.experimental.pallas import tpu as pltpu

F32 = jnp.float32
BF16 = jnp.bfloat16

D_MODEL = 1024
GRID_W = 64
HEAD_DIM = 64
NA_WIDTH = 256
NA_HEADS = 4
NA_WIN_ROWS = 8
NA_WIN_COLS = 16
SSD_WIDTH = 512
SSD_HEADS = 8
SSD_GROUPS = 2
SSD_STATE = 128
SSD_CONV = 5
SSD_CHUNK = 128
SSD_CONV_CH = SSD_WIDTH + 2 * SSD_GROUPS * SSD_STATE
GQA_WIDTH = 256
GQA_Q_HEADS = 4
GQA_KV_HEADS = 2
ROPE_THETA = 10000.0
ROPE_PAIRS = HEAD_DIM // 4
FFN_HIDDEN = 2816
EPS = 1e-6
ATTN_SCALE = HEAD_DIM ** -0.5
LOG2E = math.log2(math.e)

LANES = 128
SUBLANES = 8
VMEM_LIMIT_BYTES = 56 * 1024 * 1024

ROW_TILE = 256
MOD_COL_TILE = 2048
SSD_CHUNKS_PER_STEP = 2
NA_ROWS_PER_STEP = 4
GQA_Q_TILE = 256
GQA_V_ROWS = HEAD_DIM + 16
GQA_K_SUB = 256
GQA_BOUNDED_LOG2 = 60.0
GQA_BOUNDED_PAIRS_PER_TRIP = 6
GQA_STEPS_PER_TRIP = 8

C_NA = 0
C_Z = C_NA + 3 * NA_WIDTH
C_XBC = C_Z + SSD_WIDTH
C_DT = C_XBC + SSD_CONV_CH
C_GQ = C_DT + LANES
C_GK = C_GQ + GQA_WIDTH
C_GV = C_GK + 2 * LANES
C_END = C_GV + LANES


def _silu(v):
    return v * (1.0 / (1.0 + jnp.exp(-v)))


def _softplus(v):
    return jnp.maximum(v, 0.0) + jnp.log(1.0 + jnp.exp(-jnp.abs(v)))


def _split3(v):
    hi = v.astype(BF16)
    r1 = v - hi.astype(F32)
    mid = r1.astype(BF16)
    lo = (r1 - mid.astype(F32)).astype(BF16)
    return hi, mid, lo


def _dot(a, b):
    return jnp.dot(a, b, preferred_element_type=F32)


def _dot_nt(a, b):
    return lax.dot_general(a, b, (((1,), (1,)), ((), ())), preferred_element_type=F32)


def _dot_tn(a, b):
    return lax.dot_general(a, b, (((0,), (0,)), ((), ())), preferred_element_type=F32)


def _exact_dot(v, sel):
    hi, mid, lo = _split3(v)
    return _dot(hi, sel) + _dot(mid, sel) + _dot(lo, sel)


def _spread_dot(v, sel):
    hi = v.astype(BF16)
    lo = (v - hi.astype(F32)).astype(BF16)
    return _dot(hi, sel) + _dot(lo, sel)


def _exact_dot_lhs(sel, v):
    hi, mid, lo = _split3(v)
    return _dot(sel, hi) + _dot(sel, mid) + _dot(sel, lo)


def _params(*sem):
    return pltpu.CompilerParams(dimension_semantics=sem, vmem_limit_bytes=VMEM_LIMIT_BYTES)


def _resident(shape, index_map):
    return pl.BlockSpec(shape, index_map, pipeline_mode=pl.Buffered(1))


def _layer_spec(stacked, layer, resident=False):
    shape = stacked.shape[1:]
    index_map = lambda *_: (layer,) + (0,) * len(shape)
    if resident:
        return pl.BlockSpec((None,) + shape, index_map, pipeline_mode=pl.Buffered(1))
    return pl.BlockSpec((None,) + shape, index_map)


def _mod_kernel(cc_ref, w_ref, b_ref, o_ref):
    a = _silu(cc_ref[...])
    o_ref[0] = jnp.dot(a, w_ref[0], preferred_element_type=F32) + b_ref[0]


def _modulation(cc, mod_w, mod_b):
    depth = mod_w.shape[0]
    cols = MOD_COL_TILE
    ncol = mod_w.shape[2] // cols
    return pl.pallas_call(
        _mod_kernel,
        grid=(depth, ncol),
        in_specs=[
            pl.BlockSpec((SUBLANES, D_MODEL), lambda l, j: (0, 0)),
            pl.BlockSpec((1, D_MODEL, cols), lambda l, j: (l, 0, j)),
            pl.BlockSpec((1, 1, cols), lambda l, j: (l, 0, j)),
        ],
        out_specs=pl.BlockSpec((1, SUBLANES, cols), lambda l, j: (l, 0, j)),
        out_shape=jax.ShapeDtypeStruct((depth, SUBLANES, ncol * cols), F32),
        compiler_params=_params("arbitrary", "arbitrary"),
        name="modulation",
    )(cc, mod_w, mod_b.reshape(depth, 1, -1))


def _residual_operands(x_parts, skip):
    ctx_spec = pl.BlockSpec((ROW_TILE, D_MODEL), lambda i: (0, 0))
    if isinstance(x_parts, tuple):
        lat_spec = pl.BlockSpec((ROW_TILE, D_MODEL), lambda i: (jnp.maximum(i + skip - 1, 0), 0))
        return list(x_parts), [ctx_spec, lat_spec], skip == 0
    row_spec = pl.BlockSpec((ROW_TILE, D_MODEL), lambda i: (i + skip, 0))
    return [x_parts, x_parts], [ctx_spec, row_spec], False


def _inproj_kernel(xc_ref, xl_ref, prev_ref, next_ref, mod_ref, nw_ref, w_ref, dtb_ref, qkw_ref, cos_ref, sin_ref,
                   ones_ref, cw_ref, cb_ref, na_ref, z_ref, xconv_ref, dt_ref, gq_ref, gk_ref, gv_ref, *, pick_ctx):
    i = pl.program_id(0)
    n = pl.num_programs(0)
    rows = xl_ref.shape[0]
    x = jnp.where(i == 0, xc_ref[...], xl_ref[...]) if pick_ctx else xl_ref[...]
    x = jnp.concatenate([prev_ref[...], x, next_ref[...]], axis=0)
    ms = jnp.mean(x * x, axis=-1, keepdims=True)
    xn = x * lax.rsqrt(ms + EPS) * nw_ref[...]
    sh = mod_ref[:, 0:D_MODEL]
    sc = mod_ref[:, D_MODEL:2 * D_MODEL]
    h = (xn * (1.0 + sc) + sh).astype(BF16)
    xbc_ext = _dot(h, w_ref[:, C_XBC:C_DT])
    h_tile = h[SUBLANES:SUBLANES + rows]
    u = jnp.concatenate([_dot(h_tile, w_ref[:, C_NA:C_XBC]), xbc_ext[SUBLANES:SUBLANES + rows],
                         _dot(h_tile, w_ref[:, C_DT:C_END])], axis=-1)
    na_ref[...] = u[:, C_NA:C_Z].astype(BF16)
    z_ref[...] = u[:, C_Z:C_XBC]
    has_prev = i >= 2
    has_next = jnp.logical_and(i >= 1, i < n - 1)
    xbc = u[:, C_XBC:C_DT]
    ext = jnp.concatenate([jnp.where(has_prev, xbc_ext[0:SUBLANES], 0.0), xbc,
                           jnp.where(has_next, xbc_ext[SUBLANES + rows:], 0.0)], axis=0)
    total = rows + 2 * SUBLANES
    half = SSD_CONV // 2
    acc = cb_ref[...] + cw_ref[half:half + 1, :] * xbc
    for j in range(SSD_CONV):
        if j != half:
            shifted = pltpu.roll(ext, (half - j) % total, 0)
            acc = acc + cw_ref[j:j + 1, :] * shifted[SUBLANES:SUBLANES + rows]
    xconv_ref[...] = _silu(acc)
    dt_ref[...] = _softplus(u[:, C_DT:C_GQ] + dtb_ref[...])
    gv_ref[...] = u[:, C_GV:C_END].astype(BF16)
    g = u[:, C_GQ:C_GV]
    gsq = g * g
    hi = gsq.astype(BF16)
    lo = (gsq - hi.astype(F32)).astype(BF16)
    ss = _dot(hi, ones_ref[...]) + _dot(lo, ones_ref[...])
    gn = g * lax.rsqrt(ss * (1.0 / HEAD_DIM) + EPS) * qkw_ref[...]
    width = gn.shape[-1]
    lane = lax.broadcasted_iota(jnp.int32, gn.shape, 1)
    first = (lane % (2 * ROPE_PAIRS)) < ROPE_PAIRS
    partner = jnp.where(first, pltpu.roll(gn, width - ROPE_PAIRS, 1), pltpu.roll(gn, ROPE_PAIRS, 1))
    cos = jnp.concatenate([cos_ref[...]] * (width // LANES), axis=-1)
    sin = jnp.concatenate([sin_ref[...]] * (width // LANES), axis=-1)
    gr = gn * cos + partner * sin
    gq_ref[...] = gr[:, :GQA_WIDTH].astype(BF16)
    gk_ref[...] = gr[:, GQA_WIDTH:].astype(BF16)


def _inproj(x_parts, t, mod4, layer, norm_w, w_cat, dt_bias_pad, qk_w, cos_t, sin_t, ones_bd, conv_w_pad, conv_b):
    n = t // ROW_TILE
    row = lambda i: (i, 0)
    const = lambda i: (0, 0)
    outs = [
        (3 * NA_WIDTH, BF16), (SSD_WIDTH, F32), (SSD_CONV_CH, F32), (LANES, F32),
        (GQA_WIDTH, BF16), (2 * LANES, BF16), (LANES, BF16),
    ]
    x_arrays, x_specs, pick_ctx = _residual_operands(x_parts, 0)
    halo_src = x_arrays[1]
    per = ROW_TILE // SUBLANES
    first = (lambda i: (i - 1) * per) if isinstance(x_parts, tuple) else (lambda i: i * per)
    last_blk = halo_src.shape[0] // SUBLANES - 1
    halo_specs = [
        pl.BlockSpec((SUBLANES, D_MODEL), lambda i: (jnp.clip(first(i) - 1, 0, last_blk), 0)),
        pl.BlockSpec((SUBLANES, D_MODEL), lambda i: (jnp.clip(first(i) + per, 0, last_blk), 0)),
    ]
    return pl.pallas_call(
        functools.partial(_inproj_kernel, pick_ctx=pick_ctx),
        grid=(n,),
        in_specs=x_specs + halo_specs + [
            pl.BlockSpec((None, None, 1, 6 * D_MODEL), lambda i: (layer, jnp.minimum(i, 1), 0, 0)),
            _layer_spec(norm_w, layer),
            _layer_spec(w_cat, layer, resident=True),
            _layer_spec(dt_bias_pad, layer),
            _layer_spec(qk_w, layer),
            pl.BlockSpec((ROW_TILE, LANES), row),
            pl.BlockSpec((ROW_TILE, LANES), row),
            _resident((2 * GQA_WIDTH, 2 * GQA_WIDTH), const),
            _layer_spec(conv_w_pad, layer),
            _layer_spec(conv_b, layer),
        ],
        out_specs=[pl.BlockSpec((ROW_TILE, w), row) for w, _ in outs],
        out_shape=[jax.ShapeDtypeStruct((t, w), d) for w, d in outs],
        compiler_params=_params("parallel"),
        name="inproj",
    )(*x_arrays, halo_src, halo_src, mod4, norm_w, w_cat, dt_bias_pad, qk_w, cos_t, sin_t, ones_bd,
      conv_w_pad, conv_b)


def _ssd_direction(xbc, dt, a_row, tri, expand, s_ref, lane0, backward):
    q = SSD_CHUNK
    x = xbc[:, :SSD_WIDTH]
    nb = SSD_GROUPS * SSD_STATE
    bmat = xbc[:, SSD_WIDTH:SSD_WIDTH + nb].astype(BF16)
    cmat = xbc[:, SSD_WIDTH + nb:].astype(BF16)
    cum = _exact_dot_lhs(tri, dt * a_row)
    yield
    last = 0 if backward else q - 1
    cum_t = cum.T

    def spread(mat_t):
        rows = [jnp.broadcast_to(mat_t[lane0 + h:lane0 + h + 1, :], (HEAD_DIM, q)) for h in range(SSD_HEADS)]
        return jnp.concatenate(rows, axis=0).T

    dt_full = _spread_dot(dt, expand)
    cum_full = spread(cum_t)
    yield
    ea_full = jnp.exp(cum_full)
    te_full = jnp.exp(cum_full[last:last + 1, :] - cum_full)
    xr = x * dt_full
    xr_b = xr.astype(BF16)
    xt_b = (xr * te_full).astype(BF16)
    ti =lax.broadcasted_iota(jnp.int32, (q, q), 0)
    si = lax.broadcasted_iota(jnp.int32, (q, q), 1)
    keep = (si >= ti) if backward else (si <= ti)
    lane = lax.broadcasted_iota(jnp.int32, (q, LANES), 1)
    heads_per_group = SSD_HEADS // SSD_GROUPS
    width_g = heads_per_group * HEAD_DIM
    pieces = []
    decay_total = ea_full[last:last + 1, :]
    for g in range(SSD_GROUPS):
        bg = bmat[:, g * SSD_STATE:(g + 1) * SSD_STATE]
        cg = cmat[:, g * SSD_STATE:(g + 1) * SSD_STATE]
        gmat = _dot_nt(cg, bg)
        s_old = s_ref[:, g * width_g:(g + 1) * width_g]
        y_off = _dot(cg, s_old.astype(BF16)) * ea_full[:, g * width_g:(g + 1) * width_g]
        s_new = _dot_tn(bg, xt_b[:, g * width_g:(g + 1) * width_g])
        s_ref[:, g * width_g:(g + 1) * width_g] = decay_total[:, g * width_g:(g + 1) * width_g] * s_old + s_new
        yield
        for pair in range(heads_per_group // 2):
            col = g * width_g + pair * LANES
            xr_pair = xr_b[:, col:col + LANES]
            ys = []
            for r in range(2):
                hl = lane0 + g * heads_per_group + 2 * pair + r
                seg = jnp.broadcast_to(cum[:, hl:hl + 1], (q, q)) - jnp.broadcast_to(cum_t[hl:hl + 1, :], (q, q))
                dec = jnp.exp(jnp.where(keep, seg, -jnp.inf))
                ys.append(_dot((gmat * dec).astype(BF16), xr_pair))
            y_diag = jnp.where(lane < HEAD_DIM, ys[0], ys[1])
            pieces.append(y_diag + y_off[:, pair * LANES:(pair + 1) * LANES])
            yield
    return jnp.concatenate(pieces, axis=-1)


def _run_interleaved(*stage_generators):
    results = [None] * len(stage_generators)
    live = list(range(len(stage_generators)))
    while live:
        for idx in list(live):
            try:
                next(stage_generators[idx])
            except StopIteration as done:
                results[idx] = done.value
                live.remove(idx)
    return results


def _ssd_kernel(xf_ref, dtf_ref, xb_ref, dtb_ref, a_ref, tril_ref, triu_ref, ef_ref, eb_ref,
                yf_ref, yb_ref, sf_ref, sb_ref):
    @pl.when(pl.program_id(0) == 0)
    def _():
        sf_ref[...] = jnp.zeros_like(sf_ref)
        sb_ref[...] = jnp.zeros_like(sb_ref)

    a_row = a_ref[...]
    q = SSD_CHUNK
    order_f = list(range(SSD_CHUNKS_PER_STEP))
    order_b = order_f[::-1]
    stages = []
    for cf, cb in zip(order_f, order_b):
        rf, rb = slice(cf * q, (cf + 1) * q), slice(cb * q, (cb + 1) * q)
        stages.append(_ssd_direction(xf_ref[rf, :], dtf_ref[rf, :], a_row, tril_ref[...], ef_ref[...],
                                     sf_ref, 0, False))
        stages.append(_ssd_direction(xb_ref[rb, :], dtb_ref[rb, :], a_row, triu_ref[...], eb_ref[...],
                                     sb_ref, SSD_HEADS, True))
    ys = _run_interleaved(*stages)
    for k, (cf, cb) in enumerate(zip(order_f, order_b)):
        yf_ref[cf * q:(cf + 1) * q, :] = ys[2 * k]
        yb_ref[cb * q:(cb + 1) * q, :] = ys[2 * k + 1]


def _ssd_scan(xconv, dt, a_row, layer, n_ctx):
    t = xconv.shape[0]
    q = SSD_CHUNK
    rows = SSD_CHUNKS_PER_STEP * q
    assert n_ctx % rows == 0 and t % rows == 0
    n = t // rows
    n_ctx_blocks = n_ctx // rows
    r = jnp.arange(q)
    tril = (r[None, :] <= r[:, None]).astype(BF16)
    triu = (r[None, :] >= r[:, None]).astype(BF16)
    lanes = jnp.arange(LANES)[:, None]
    head_of_col = (jnp.arange(SSD_WIDTH) // HEAD_DIM)[None, :]
    expand_f = (lanes == head_of_col).astype(BF16)
    expand_b = (lanes == head_of_col + SSD_HEADS).astype(BF16)

    def fwd(i):
        return (i, 0)

    def bwd(i):
        return (jnp.where(i < n_ctx_blocks, n_ctx_blocks - 1 - i, n - 1 - (i - n_ctx_blocks)), 0)

    const = lambda i: (0, 0)
    return pl.pallas_call(
        _ssd_kernel,
        grid=(n,),
        in_specs=[
            pl.BlockSpec((rows, SSD_CONV_CH), fwd),
            pl.BlockSpec((rows, LANES), fwd),
            pl.BlockSpec((rows, SSD_CONV_CH), bwd),
            pl.BlockSpec((rows, LANES), bwd),
            _layer_spec(a_row, layer),
            pl.BlockSpec((q, q), const),
            pl.BlockSpec((q, q), const),
            pl.BlockSpec((LANES, SSD_WIDTH), const),
            pl.BlockSpec((LANES, SSD_WIDTH), const),
        ],
        out_specs=[pl.BlockSpec((rows, SSD_WIDTH), fwd), pl.BlockSpec((rows, SSD_WIDTH), bwd)],
        out_shape=[jax.ShapeDtypeStruct((t, SSD_WIDTH), F32)] * 2,
        scratch_shapes=[pltpu.VMEM((SSD_STATE, SSD_WIDTH), F32)] * 2,
        compiler_params=_params("arbitrary"),
        name="ssd_scan",
    )(xconv, dt, xconv, dt, a_row, tril, triu, expand_f, expand_b)


def _na_kernel(q_ref, k_ref, v_ref, bias_ref, o_ref, *, n_ctx, n_rows):
    i = pl.program_id(0)
    is_ctx = i == 0
    kc = k_ref[0:n_ctx, :]
    vc = v_ref[0:n_ctx, :]
    lane = lax.broadcasted_iota(jnp.int32, (GRID_W, NA_WIDTH), 1)
    mine = [(lane >= h * HEAD_DIM) & (lane < (h + 1) * HEAD_DIM) for h in range(NA_HEADS)]
    win = NA_WIN_ROWS * GRID_W

    def lane_tiles(a):
        return [a[:, c * LANES:(c + 1) * LANES] for c in range(a.shape[-1] // LANES)]

    def grid_row(j):
        r = jnp.maximum((i - 1) * NA_ROWS_PER_STEP + j, 0)
        r_start = jnp.clip(r - NA_WIN_ROWS // 2, 0, n_rows - NA_WIN_ROWS)
        variant = jnp.where(is_ctx, NA_WIN_ROWS, r_start - r + NA_WIN_ROWS - 1)
        start = pl.multiple_of(n_ctx + r_start * GRID_W, GRID_W)
        kw = k_ref[pl.ds(start, win), :]
        vw = v_ref[pl.ds(start, win), :]
        qj = q_ref[j * GRID_W:(j + 1) * GRID_W, :]
        qm = jnp.concatenate([jnp.where(mine[h], qj, jnp.zeros_like(qj)) for h in range(NA_HEADS)], axis=0)
        s_w = _dot_nt(qm, kw) + bias_ref[variant]
        s_c = _dot_nt(qm, kc)
        yield
        m = functools.reduce(jnp.maximum, lane_tiles(s_w) + lane_tiles(s_c))
        m = jnp.broadcast_to(jnp.max(m, axis=-1, keepdims=True), m.shape)
        p_w = jnp.exp(s_w - jnp.concatenate([m] * (s_w.shape[-1] // LANES), axis=-1))
        p_c = jnp.exp(s_c - jnp.concatenate([m] * (s_c.shape[-1] // LANES), axis=-1))
        l = jnp.sum(functools.reduce(jnp.add, lane_tiles(p_w) + lane_tiles(p_c)), axis=-1, keepdims=True)
        y = (_dot(p_w.astype(BF16), vw) + _dot(p_c.astype(BF16), vc)) * (1.0 / l)
        out = y[(NA_HEADS - 1) * GRID_W:]
        for h in range(NA_HEADS - 2, -1, -1):
            out = jnp.where(mine[h], y[h * GRID_W:(h + 1) * GRID_W], out)
        o_ref[j * GRID_W:(j + 1) * GRID_W, :] = out.astype(o_ref.dtype)

    _run_interleaved(*[grid_row(j) for j in range(NA_ROWS_PER_STEP)])


def _na_bias_table(rpb):
    depth = rpb.shape[0]
    col = np.arange(GRID_W)
    c_start = np.clip(col - NA_WIN_COLS // 2, 0, GRID_W - NA_WIN_COLS)
    in_win = (col[None, :] >= c_start[:, None]) & (col[None, :] < c_start[:, None] + NA_WIN_COLS)
    dc = np.clip(col[None, :] - col[:, None] + NA_WIN_COLS - 1, 0, 2 * NA_WIN_COLS - 2)
    n_dc = 2 * NA_WIN_COLS - 1
    n_dr = 2 * NA_WIN_ROWS - 1
    onehot = (dc[None, :, :] == np.arange(n_dc)[:, None, None]).astype(np.float32).reshape(n_dc, GRID_W * GRID_W)
    t2 = jnp.dot(rpb.reshape(depth * NA_HEADS * n_dr, n_dc).astype(F32), onehot, precision=lax.Precision.HIGHEST)
    t2 = jnp.where(in_win[None, None, None], t2.reshape(depth, NA_HEADS, n_dr, GRID_W, GRID_W), -jnp.inf)
    tab = jnp.stack([t2[:, :, v:v + NA_WIN_ROWS] for v in range(NA_WIN_ROWS)], axis=1)
    tab = tab.transpose(0, 1, 2, 4, 3, 5).reshape(depth, NA_WIN_ROWS, NA_HEADS * GRID_W, NA_WIN_ROWS * GRID_W)
    masked = jnp.full((depth, 1) + tab.shape[2:], -jnp.inf, F32)
    return jnp.concatenate([tab, masked], axis=1)


def _neighbourhood_attention(na, bias_tab, layer, n_ctx):
    t = na.shape[0]
    n_rows = (t - n_ctx) // GRID_W
    step_rows = NA_ROWS_PER_STEP * GRID_W
    n = t // step_rows
    return pl.pallas_call(
        functools.partial(_na_kernel, n_ctx=n_ctx, n_rows=n_rows),
        grid=(n,),
        in_specs=[
            pl.BlockSpec((step_rows, NA_WIDTH), lambda i: (i, 0)),
            _resident((t, NA_WIDTH), lambda i: (0, 1)),
            _resident((t, NA_WIDTH), lambda i: (0, 2)),
            _layer_spec(bias_tab, layer, resident=True),
        ],
        out_specs=pl.BlockSpec((step_rows, NA_WIDTH), lambda i: (i, 0)),
        out_shape=jax.ShapeDtypeStruct((t, NA_WIDTH), BF16),
        compiler_params=_params("parallel"),
        name="neighbourhood_attention",
    )(na, na, na, bias_tab)


def _gqa_kernel(qt_ref, k_ref, vt_ref, o_ref, acc_ref, s0_ref, s1_ref, p0_ref, p1_ref, *, n_ctx_sub, n_sub):
    tq = qt_ref.shape[1]
    qt = qt_ref[...]
    row = lax.broadcasted_iota(jnp.int32, qt.shape, 0)
    top = row < HEAD_DIM
    zero = jnp.zeros_like(qt)
    qt2 = jnp.concatenate([jnp.where(top, qt, zero), jnp.where(top, zero, qt)], axis=1)
    acc_ref[...] = jnp.zeros_like(acc_ref)
    s_ref = (s0_ref, s1_ref)
    p_ref = (p0_ref, p1_ref)

    def keys(j, size):
        return pl.ds(pl.multiple_of(j * size, LANES), size)

    def score(key_rows, slot):
        size = key_rows.size
        s = _dot(k_ref[key_rows, :], qt2)
        s_ref[slot][0:size, :] = s
        return jnp.max(s.reshape(size // SUBLANES, SUBLANES, 2 * tq), axis=0)

    def softmax(size, slot, m_old, part_max):
        m_new = jnp.maximum(m_old, jnp.max(part_max, axis=0, keepdims=True))
        s = s_ref[slot][0:size, :].reshape(size // SUBLANES, SUBLANES, 2 * tq)
        p_ref[slot][0:size, :] = jnp.exp2(s - m_new[None]).reshape(size, 2 * tq).astype(BF16)
        return m_new, jnp.exp2(m_old - m_new)

    def accumulate(key_rows, slot, alpha):
        size = key_rows.size
        acc = acc_ref[...].reshape(GQA_V_ROWS // SUBLANES, SUBLANES, 2 * tq) * alpha[None]
        acc_ref[...] = acc.reshape(GQA_V_ROWS, 2 * tq) + _dot(vt_ref[:, key_rows], p_ref[slot][0:size, :])

    m_init = jnp.full((SUBLANES, 2 * tq), -jnp.inf, F32)
    sub = GQA_K_SUB
    assert n_sub >= 3

    @pl.when(pl.program_id(1) == 0)
    def _():
        m = m_init
        for j in range(n_ctx_sub):
            m, alpha = softmax(sub, 0, m, score(keys(j, sub), 0))
            accumulate(keys(j, sub), 0, alpha)

    @pl.when(pl.program_id(1) > 0)
    def _():
        part0 = score(keys(0, sub), 0)
        part1 = score(keys(1, sub), 1)
        m, alpha = softmax(sub, 0, m_init, part0)

        def step(t, slot, carry):
            m, alpha, part = carry
            part_next = score(keys(t, sub), slot)
            accumulate(keys(t - 2, sub), slot, alpha)
            return softmax(sub, 1 - slot, m, part) + (part_next,)

        def trip(n, carry):
            for u in range(GQA_STEPS_PER_TRIP):
                carry = step(2 + n * GQA_STEPS_PER_TRIP + u, u % 2, carry)
            return carry

        n_trips = (n_sub - 2) // GQA_STEPS_PER_TRIP
        carry = lax.fori_loop(0, n_trips, trip, (m, alpha, part1))
        for t in range(2 + n_trips * GQA_STEPS_PER_TRIP, n_sub):
            carry = step(t, t % 2, carry)
        m, alpha, part = carry
        accumulate(keys(n_sub - 2, sub), (n_sub - 2) % 2, alpha)
        m, alpha = softmax(sub, (n_sub - 1) % 2, m, part)
        accumulate(keys(n_sub - 1, sub), (n_sub - 1) % 2, alpha)

    acc = acc_ref[...]
    denom = acc[HEAD_DIM:HEAD_DIM + SUBLANES]
    o_t = (acc[:HEAD_DIM].reshape(HEAD_DIM // SUBLANES, SUBLANES, 2 * tq) / denom[None]).reshape(HEAD_DIM, 2 * tq)
    o_ref[...] = jnp.concatenate([o_t[:, :tq], o_t[:, tq:]], axis=0).T.astype(o_ref.dtype)


def _gqa_bounded_kernel(qt_ref, k_ref, vt_ref, o_ref, acc_ref, *p_ref, n_ctx_sub, n_sub):
    tq = qt_ref.shape[1]
    qt = qt_ref[...]
    row = lax.broadcasted_iota(jnp.int32, qt.shape, 0)
    top = row < HEAD_DIM
    zero = jnp.zeros_like(qt)
    qt2 = jnp.concatenate([jnp.where(top, qt, zero), jnp.where(top, zero, qt)], axis=1)
    acc_ref[...] = jnp.zeros_like(acc_ref)
    sub = GQA_K_SUB
    per_trip = GQA_BOUNDED_PAIRS_PER_TRIP
    assert len(p_ref) == 2 and per_trip % 2 == 0 and n_sub % 2 == 1 and n_sub >= 3

    def keys(j, count=1):
        return pl.ds(pl.multiple_of(j * sub, sub), count * sub)

    def probs(j, buf, half):
        p_ref[buf][half * sub:(half + 1) * sub, :] = jnp.exp2(_dot(k_ref[keys(j), :], qt2)).astype(BF16)

    def accumulate_one(j, buf):
        acc_ref[...] += _dot(vt_ref[:, keys(j)], p_ref[buf][0:sub, :])

    @pl.when(pl.program_id(1) == 0)
    def _():
        for j in range(n_ctx_sub):
            probs(j, 0, 0)
            accumulate_one(j, 0)

    @pl.when(pl.program_id(1) > 0)
    def _():
        def probs_pair(q, buf):
            probs(1 + 2 * q, buf, 0)
            probs(2 + 2 * q, buf, 1)

        def accumulate_pair(q, buf):
            acc_ref[...] += _dot(vt_ref[:, keys(1 + 2 * q, 2)], p_ref[buf][...])

        n_pairs = (n_sub - 1) // 2
        probs(0, 1, 0)
        probs_pair(0, 0)
        accumulate_one(0, 1)

        def trip(n, carry):
            for u in range(per_trip):
                q = 1 + n * per_trip + u
                probs_pair(q, (u + 1) % 2)
                accumulate_pair(q - 1, u % 2)
            return carry

        n_trips = (n_pairs - 1) // per_trip
        lax.fori_loop(0, n_trips, trip, 0)
        for q in range(1 + n_trips * per_trip, n_pairs):
            probs_pair(q, q % 2)
            accumulate_pair(q - 1, (q - 1) % 2)
        accumulate_pair(n_pairs - 1, (n_pairs - 1) % 2)

    acc = acc_ref[...]
    denom = acc[HEAD_DIM:HEAD_DIM + SUBLANES]
    o_t = (acc[:HEAD_DIM].reshape(HEAD_DIM // SUBLANES, SUBLANES, 2 * tq) / denom[None]).reshape(HEAD_DIM, 2 * tq)
    o_ref[...] = jnp.concatenate([o_t[:, :tq], o_t[:, tq:]], axis=0).T.astype(o_ref.dtype)


def _gqa_dispatch_kernel(bounded_ref, qt_ref, k_ref, vt_ref, o_ref, acc_ref, s0_ref, s1_ref, *p_ref,
                         n_ctx_sub, n_sub):
    @pl.when(bounded_ref[0] != 0)
    def _():
        _gqa_bounded_kernel(qt_ref, k_ref, vt_ref, o_ref, acc_ref, *p_ref, n_ctx_sub=n_ctx_sub, n_sub=n_sub)

    @pl.when(bounded_ref[0] == 0)
    def _():
        _gqa_kernel(qt_ref, k_ref, vt_ref, o_ref, acc_ref, s0_ref, s1_ref, p_ref[0], p_ref[1],
                    n_ctx_sub=n_ctx_sub, n_sub=n_sub)


def _gqa_attention(gqt, gk, gvt, n_ctx, score_bound):
    nq = gqt.shape[1]
    n_keys = gk.shape[0]
    tq = GQA_Q_TILE
    assert n_ctx == tq and n_ctx % GQA_K_SUB == 0 and n_keys % GQA_K_SUB == 0 and nq % tq == 0
    bounded = (score_bound <= GQA_BOUNDED_LOG2).astype(jnp.int32).reshape(1)
    return pl.pallas_call(
        functools.partial(_gqa_dispatch_kernel, n_ctx_sub=n_ctx // GQA_K_SUB, n_sub=n_keys // GQA_K_SUB),
        grid=(GQA_KV_HEADS, nq // tq),
        in_specs=[
            pl.BlockSpec(memory_space=pltpu.SMEM),
            pl.BlockSpec((LANES, tq), lambda g, i: (g, i)),
            pl.BlockSpec((n_keys, LANES), lambda g, i: (0, g)),
            pl.BlockSpec((GQA_V_ROWS, n_keys), lambda g, i: (g, 0)),
        ],
        out_specs=pl.BlockSpec((tq, LANES), lambda g, i: (i, g)),
        out_shape=jax.ShapeDtypeStruct((nq, GQA_WIDTH), BF16),
        scratch_shapes=[pltpu.VMEM((GQA_V_ROWS, 2 * tq), F32)] + [pltpu.VMEM((GQA_K_SUB, 2 * tq), F32)] * 2
        + [pltpu.VMEM((2 * GQA_K_SUB, 2 * tq), BF16)] * 2,
        compiler_params=_params("arbitrary", "arbitrary"),
        name="gqa_attention",
    )(bounded, gqt, gk, gvt)


def _out_ffn_kernel(ya_ref, yf_ref, yb_ref, xs_ref, z_ref, yg_ref, xc_ref, xl_ref, mod_ref,
                    dskip_ref, snw_ref, wo_ref, fnw_ref, wg_ref, wu_ref, wd_ref, final_ref,
                    o_ref, *, final, pick_ctx):
    x = jnp.where(pl.program_id(0) == 0, xc_ref[...], xl_ref[...]) if pick_ctx else xl_ref[...]
    y = yf_ref[...] + yb_ref[...] + dskip_ref[...] * xs_ref[...]
    y = y * _silu(z_ref[...])
    ms = jnp.mean(y * y, axis=-1, keepdims=True)
    y = y * lax.rsqrt(ms + EPS) * snw_ref[...]
    mix = jnp.concatenate([ya_ref[...], y.astype(BF16), yg_ref[...]], axis=-1)
    g_m = mod_ref[:, 2 * D_MODEL:3 * D_MODEL]
    sh_f = mod_ref[:, 3 * D_MODEL:4 * D_MODEL]
    sc_f = mod_ref[:, 4 * D_MODEL:5 * D_MODEL]
    g_f = mod_ref[:, 5 * D_MODEL:6 * D_MODEL]
    x1 = x + g_m * _dot(mix, wo_ref[...])
    ms1 = jnp.mean(x1 * x1, axis=-1, keepdims=True)
    hf = (x1 * lax.rsqrt(ms1 + EPS) * fnw_ref[...] * (1.0 + sc_f) + sh_f).astype(BF16)
    act = (_silu(_dot(hf, wg_ref[...])) * _dot(hf, wu_ref[...])).astype(BF16)
    x2 = x1 + g_f * _dot(act, wd_ref[...])
    if final:
        ms2 = jnp.mean(x2 * x2, axis=-1, keepdims=True)
        x2 = x2 * lax.rsqrt(ms2 + EPS) * final_ref[...]
    o_ref[...] = x2


def _out_ffn(ya, yf, yb, xconv, z, yg, x_parts, mod4, layer, d_full, ssd_nw, w_out, ffn_nw,
             w_gate, w_up, w_down, final_nw, final):
    t = ya.shape[0]
    skip = 1 if final else 0
    n = t // ROW_TILE - skip
    row = lambda i: (i + skip, 0)
    const = lambda i: (0, 0)
    x_arrays, x_specs, pick_ctx = _residual_operands(x_parts, skip)
    return pl.pallas_call(
        functools.partial(_out_ffn_kernel, final=final, pick_ctx=pick_ctx),
        grid=(n,),
        in_specs=[
            pl.BlockSpec((ROW_TILE, NA_WIDTH), row),
            pl.BlockSpec((ROW_TILE, SSD_WIDTH), row),
            pl.BlockSpec((ROW_TILE, SSD_WIDTH), row),
            pl.BlockSpec((ROW_TILE, SSD_WIDTH), row),
            pl.BlockSpec((ROW_TILE, SSD_WIDTH), row),
            pl.BlockSpec((ROW_TILE, GQA_WIDTH), row),
        ] + x_specs + [
            pl.BlockSpec((None, None, 1, 6 * D_MODEL), lambda i: (layer, jnp.minimum(i + skip, 1), 0, 0)),
            _layer_spec(d_full, layer),
            _layer_spec(ssd_nw, layer),
            _layer_spec(w_out, layer, resident=True),
            _layer_spec(ffn_nw, layer),
            _layer_spec(w_gate, layer, resident=True),
            _layer_spec(w_up, layer, resident=True),
            _layer_spec(w_down, layer, resident=True),
            pl.BlockSpec((1, D_MODEL), const),
        ],
        out_specs=pl.BlockSpec((ROW_TILE, D_MODEL), lambda i: (i, 0)),
        out_shape=jax.ShapeDtypeStruct((n * ROW_TILE, D_MODEL), F32),
        compiler_params=_params("parallel"),
        name="out_ffn",
    )(ya, yf, yb, xconv, z, yg, *x_arrays, mod4, d_full, ssd_nw, w_out, ffn_nw, w_gate, w_up, w_down, final_nw)


def _rearranged_w_in(w):
    na_in = 3 * NA_WIDTH
    o_z = na_in
    o_xbc = o_z + SSD_WIDTH
    o_dt = o_xbc + SSD_CONV_CH
    o_gq = o_dt + 2 * SSD_HEADS
    o_gk = o_gq + GQA_WIDTH
    o_gv = o_gk + GQA_KV_HEADS * HEAD_DIM
    w = w.astype(BF16)
    q_na = w[..., :NA_WIDTH] * ATTN_SCALE
    dt_pad = jnp.zeros(w.shape[:-1] + (LANES - 2 * SSD_HEADS,), BF16)
    k_heads = [w[..., o_gk + h * HEAD_DIM:o_gk + (h + 1) * HEAD_DIM] for h in range(GQA_KV_HEADS)]
    return jnp.concatenate(
        [q_na, w[..., NA_WIDTH:o_dt], w[..., o_dt:o_gq], dt_pad, w[..., o_gq:o_gk]]
        + [p for h in k_heads for p in (h, h)] + [w[..., o_gv:]], axis=-1)


def _rope_tables(n_ctx, n_lat):
    f32 = np.float32
    freqs = f32(ROPE_THETA) ** (-np.arange(ROPE_PAIRS, dtype=f32) / f32(ROPE_PAIRS))
    n_rows = n_lat // GRID_W
    half = 2 * ROPE_PAIRS
    sign = np.where(np.arange(half) < ROPE_PAIRS, -1.0, 1.0).astype(f32)

    def tables(n_pos):
        a = np.arange(n_pos, dtype=f32)[:, None] * freqs[None, :]
        a = np.concatenate([a, a], axis=-1)
        return np.cos(a), np.sin(a) * sign[None, :]

    def per_token(by_row, by_col):
        lat = np.concatenate([np.broadcast_to(by_row[:, None, :], (n_rows, GRID_W, half)),
                              np.broadcast_to(by_col[None, :, :], (n_rows, GRID_W, half))], axis=-1)
        return lat.reshape(n_lat, HEAD_DIM)

    cos_r, sin_r = tables(n_rows)
    cos_c, sin_c = tables(GRID_W)
    cos = np.concatenate([np.ones((n_ctx, HEAD_DIM), f32), per_token(cos_r, cos_c)], axis=0)
    sin = np.concatenate([np.zeros((n_ctx, HEAD_DIM), f32), per_token(sin_r, sin_c)], axis=0)
    reps = (1, LANES // HEAD_DIM)
    return jnp.asarray(np.tile(cos, reps), F32), jnp.asarray(np.tile(sin, reps), F32)


def kernel(x, c, ctx, c_ctx, mod_w, mod_b, norm_attn_w, norm_ffn_w, w_in, na_rpb, ssd_conv_w, ssd_conv_b,
           ssd_dt_bias, ssd_a_log, ssd_d, ssd_norm_w, q_norm_w, k_norm_w, w_out, ffn_w_gate, ffn_w_up,
           ffn_w_down, final_norm_w):
    depth = mod_w.shape[0]
    batch, n_lat, _ = x.shape
    n_ctx = ctx.shape[1]
    assert batch == 1 and n_ctx == ROW_TILE and n_lat % (NA_ROWS_PER_STEP * GRID_W) == 0
    assert n_lat % GQA_Q_TILE == 0 and n_lat // GRID_W >= NA_WIN_ROWS

    x_parts = (ctx[0], x[0])
    cc = jnp.zeros((SUBLANES, D_MODEL), F32).at[0].set(c_ctx).at[1].set(c[0])
    mod = _modulation(cc, mod_w, mod_b)
    mod4 = mod[:, :2].reshape(depth, 2, 1, 6 * D_MODEL)

    cos_t, sin_t = _rope_tables(n_ctx, n_lat)
    blk = np.arange(2 * GQA_WIDTH) // HEAD_DIM
    ones_bd = jnp.asarray(blk[:, None] == blk[None, :], BF16)
    ones_rows = jnp.ones((GQA_V_ROWS - HEAD_DIM, n_ctx + n_lat), BF16)

    row = lambda p: p.astype(F32).reshape(depth, 1, -1)
    pad_lanes = lambda p: jnp.pad(row(p), ((0, 0), (0, 0), (0, LANES - 2 * SSD_HEADS)))
    w_cat = _rearranged_w_in(w_in)
    dt_bias_pad = pad_lanes(ssd_dt_bias)
    a_row = pad_lanes(-jnp.exp(ssd_a_log.astype(F32)))
    qk_w = row(jnp.concatenate([jnp.tile(q_norm_w * (ATTN_SCALE * LOG2E), (1, GQA_Q_HEADS)),
                                jnp.tile(k_norm_w, (1, 2 * GQA_KV_HEADS))], axis=-1))
    conv_w_pad = jnp.pad(ssd_conv_w, ((0, 0), (0, SUBLANES - SSD_CONV), (0, 0)))
    d_full = row(jnp.repeat(ssd_d, HEAD_DIM, axis=-1))
    bias_tab = _na_bias_table(na_rpb)
    w_out_b, w_gate_b, w_up_b, w_down_b = (w.astype(BF16) for w in (w_out, ffn_w_gate, ffn_w_up, ffn_w_down))
    score_bound = (1.02 * HEAD_DIM * ATTN_SCALE * LOG2E) * jnp.max(jnp.abs(q_norm_w), axis=-1) * jnp.max(
        jnp.abs(k_norm_w), axis=-1)

    for i in range(depth):
        final = i == depth - 1
        na, z, xconv, dt, gq, gk, gv = _inproj(x_parts, n_ctx + n_lat, mod4, i, row(norm_attn_w), w_cat,
                                               dt_bias_pad, qk_w, cos_t, sin_t, ones_bd, conv_w_pad,
                                               row(ssd_conv_b))
        yf, yb = _ssd_scan(xconv, dt, a_row, i, n_ctx)
        ya = _neighbourhood_attention(na, bias_tab, i, n_ctx)
        gvt = jnp.concatenate([part for g in range(GQA_KV_HEADS)
                               for part in (gv[:, g * HEAD_DIM:(g + 1) * HEAD_DIM].T, ones_rows)], axis=0)
        yg = _gqa_attention(gq.T, gk, gvt, n_ctx, score_bound[i])
        x_parts = _out_ffn(ya, yf, yb, xconv, z, yg, x_parts, mod4, i, d_full, row(ssd_norm_w), w_out_b,
                           row(norm_ffn_w), w_gate_b, w_up_b, w_down_b, final_norm_w.reshape(1, -1), final)
    return x_parts[None]
```

```python
import functools
import math

import jax
import jax.numpy as jnp
import numpy as np
from jax import lax
from jax.experimental import pallas as pl
from jax.experimental.pallas import tpu as pltpu

F32 = jnp.float32
BF16 = jnp.bfloat16

D_MODEL = 1024
GRID_W = 64
HEAD_DIM = 64
NA_WIDTH = 256
NA_HEADS = 4
NA_WIN_ROWS = 8
NA_WIN_COLS = 16
SSD_WIDTH = 512
SSD_HEADS = 8
SSD_GROUPS = 2
SSD_STATE = 128
SSD_CONV = 5
SSD_CHUNK = 128
SSD_CONV_CH = SSD_WIDTH + 2 * SSD_GROUPS * SSD_STATE
GQA_WIDTH = 256
GQA_Q_HEADS = 4
GQA_KV_HEADS = 2
ROPE_THETA = 10000.0
ROPE_PAIRS = HEAD_DIM // 4
FFN_HIDDEN = 2816
EPS = 1e-6
ATTN_SCALE = HEAD_DIM ** -0.5
LOG2E = math.log2(math.e)

LANES = 128
SUBLANES = 8
VMEM_LIMIT_BYTES = 56 * 1024 * 1024

ROW_TILE = 256
MOD_COL_TILE = 2048
SSD_CHUNKS_PER_STEP = 2
NA_ROWS_PER_STEP = 4
GQA_Q_TILE = 256
GQA_V_ROWS = HEAD_DIM + 16
GQA_K_SUB = 256
GQA_BOUNDED_LOG2 = 60.0
GQA_PV_LAG = 3
GQA_BOUNDED_STEPS_PER_TRIP = 8
GQA_STEPS_PER_TRIP = 8

C_NA = 0
C_Z = C_NA + 3 * NA_WIDTH
C_XBC = C_Z + SSD_WIDTH
C_DT = C_XBC + SSD_CONV_CH
C_GQ = C_DT + LANES
C_GK = C_GQ + GQA_WIDTH
C_GV = C_GK + 2 * LANES
C_END = C_GV + LANES


def _silu(v):
    return v * (1.0 / (1.0 + jnp.exp(-v)))


def _softplus(v):
    return jnp.maximum(v, 0.0) + jnp.log(1.0 + jnp.exp(-jnp.abs(v)))


def _split3(v):
    hi = v.astype(BF16)
    r1 = v - hi.astype(F32)
    mid = r1.astype(BF16)
    lo = (r1 - mid.astype(F32)).astype(BF16)
    return hi, mid, lo


def _dot(a, b):
    return jnp.dot(a, b, preferred_element_type=F32)


def _dot_nt(a, b):
    return lax.dot_general(a, b, (((1,), (1,)), ((), ())), preferred_element_type=F32)


def _dot_tn(a, b):
    return lax.dot_general(a, b, (((0,), (0,)), ((), ())), preferred_element_type=F32)


def _exact_dot(v, sel):
    hi, mid, lo = _split3(v)
    return _dot(hi, sel) + _dot(mid, sel) + _dot(lo, sel)


def _spread_dot(v, sel):
    hi = v.astype(BF16)
    lo = (v - hi.astype(F32)).astype(BF16)
    return _dot(hi, sel) + _dot(lo, sel)


def _exact_dot_lhs(sel, v):
    hi, mid, lo = _split3(v)
    return _dot(sel, hi) + _dot(sel, mid) + _dot(sel, lo)


def _params(*sem):
    return pltpu.CompilerParams(dimension_semantics=sem, vmem_limit_bytes=VMEM_LIMIT_BYTES)


def _resident(shape, index_map):
    return pl.BlockSpec(shape, index_map, pipeline_mode=pl.Buffered(1))


def _layer_spec(stacked, layer, resident=False):
    shape = stacked.shape[1:]
    index_map = lambda *_: (layer,) + (0,) * len(shape)
    if resident:
        return pl.BlockSpec((None,) + shape, index_map, pipeline_mode=pl.Buffered(1))
    return pl.BlockSpec((None,) + shape, index_map)


def _mod_kernel(cc_ref, w_ref, b_ref, o_ref):
    a = _silu(cc_ref[...])
    o_ref[0] = jnp.dot(a, w_ref[0], preferred_element_type=F32) + b_ref[0]


def _modulation(cc, mod_w, mod_b):
    depth = mod_w.shape[0]
    cols = MOD_COL_TILE
    ncol = mod_w.shape[2] // cols
    return pl.pallas_call(
        _mod_kernel,
        grid=(depth, ncol),
        in_specs=[
            pl.BlockSpec((SUBLANES, D_MODEL), lambda l, j: (0, 0)),
            pl.BlockSpec((1, D_MODEL, cols), lambda l, j: (l, 0, j)),
            pl.BlockSpec((1, 1, cols), lambda l, j: (l, 0, j)),
        ],
        out_specs=pl.BlockSpec((1, SUBLANES, cols), lambda l, j: (l, 0, j)),
        out_shape=jax.ShapeDtypeStruct((depth, SUBLANES, ncol * cols), F32),
        compiler_params=_params("arbitrary", "arbitrary"),
        name="modulation",
    )(cc, mod_w, mod_b.reshape(depth, 1, -1))


def _residual_operands(x_parts, skip):
    ctx_spec = pl.BlockSpec((ROW_TILE, D_MODEL), lambda i: (0, 0))
    if isinstance(x_parts, tuple):
        lat_spec = pl.BlockSpec((ROW_TILE, D_MODEL), lambda i: (jnp.maximum(i + skip - 1, 0), 0))
        return list(x_parts), [ctx_spec, lat_spec], skip == 0
    row_spec = pl.BlockSpec((ROW_TILE, D_MODEL), lambda i: (i + skip, 0))
    return [x_parts, x_parts], [ctx_spec, row_spec], False


def _inproj_kernel(xc_ref, xl_ref, prev_ref, next_ref, mod_ref, nw_ref, w_ref, dtb_ref, qkw_ref, cos_ref, sin_ref,
                   ones_ref, cw_ref, cb_ref, na_ref, z_ref, xconv_ref, dt_ref, gq_ref, gk_ref, gv_ref, *, pick_ctx):
    i = pl.program_id(0)
    n = pl.num_programs(0)
    rows = xl_ref.shape[0]
    x = jnp.where(i == 0, xc_ref[...], xl_ref[...]) if pick_ctx else xl_ref[...]
    x = jnp.concatenate([prev_ref[...], x, next_ref[...]], axis=0)
    ms = jnp.mean(x * x, axis=-1, keepdims=True)
    xn = x * lax.rsqrt(ms + EPS) * nw_ref[...]
    sh = mod_ref[:, 0:D_MODEL]
    sc = mod_ref[:, D_MODEL:2 * D_MODEL]
    h = (xn * (1.0 + sc) + sh).astype(BF16)
    xbc_ext = _dot(h, w_ref[:, C_XBC:C_DT])
    h_tile = h[SUBLANES:SUBLANES + rows]
    u = jnp.concatenate([_dot(h_tile, w_ref[:, C_NA:C_XBC]), xbc_ext[SUBLANES:SUBLANES + rows],
                         _dot(h_tile, w_ref[:, C_DT:C_END])], axis=-1)
    na_ref[...] = u[:, C_NA:C_Z].astype(BF16)
    z_ref[...] = u[:, C_Z:C_XBC]
    has_prev = i >= 2
    has_next = jnp.logical_and(i >= 1, i < n - 1)
    xbc = u[:, C_XBC:C_DT]
    ext = jnp.concatenate([jnp.where(has_prev, xbc_ext[0:SUBLANES], 0.0), xbc,
                           jnp.where(has_next, xbc_ext[SUBLANES + rows:], 0.0)], axis=0)
    total = rows + 2 * SUBLANES
    half = SSD_CONV // 2
    acc = cb_ref[...] + cw_ref[half:half + 1, :] * xbc
    for j in range(SSD_CONV):
        if j != half:
            shifted = pltpu.roll(ext, (half - j) % total, 0)
            acc = acc + cw_ref[j:j + 1, :] * shifted[SUBLANES:SUBLANES + rows]
    xconv_ref[...] = _silu(acc)
    dt_ref[...] = _softplus(u[:, C_DT:C_GQ] + dtb_ref[...])
    gv_ref[...] = u[:, C_GV:C_END].astype(BF16)
    g = u[:, C_GQ:C_GV]
    gsq = g * g
    hi = gsq.astype(BF16)
    lo = (gsq - hi.astype(F32)).astype(BF16)
    ss = _dot(hi, ones_ref[...]) + _dot(lo, ones_ref[...])
    gn = g * lax.rsqrt(ss * (1.0 / HEAD_DIM) + EPS) * qkw_ref[...]
    width = gn.shape[-1]
    lane = lax.broadcasted_iota(jnp.int32, gn.shape, 1)
    first = (lane % (2 * ROPE_PAIRS)) < ROPE_PAIRS
    partner = jnp.where(first, pltpu.roll(gn, width - ROPE_PAIRS, 1), pltpu.roll(gn, ROPE_PAIRS, 1))
    cos = jnp.concatenate([cos_ref[...]] * (width // LANES), axis=-1)
    sin = jnp.concatenate([sin_ref[...]] * (width // LANES), axis=-1)
    gr = gn * cos + partner * sin
    gq_ref[...] = gr[:, :GQA_WIDTH].astype(BF16)
    gk_ref[...] = gr[:, GQA_WIDTH:].astype(BF16)


def _inproj(x_parts, t, mod4, layer, norm_w, w_cat, dt_bias_pad, qk_w, cos_t, sin_t, ones_bd, conv_w_pad, conv_b):
    n = t // ROW_TILE
    row = lambda i: (i, 0)
    const = lambda i: (0, 0)
    outs = [
        (3 * NA_WIDTH, BF16), (SSD_WIDTH, F32), (SSD_CONV_CH, F32), (LANES, F32),
        (GQA_WIDTH, BF16), (2 * LANES, BF16), (LANES, BF16),
    ]
    x_arrays, x_specs, pick_ctx = _residual_operands(x_parts, 0)
    halo_src = x_arrays[1]
    per = ROW_TILE // SUBLANES
    first = (lambda i: (i - 1) * per) if isinstance(x_parts, tuple) else (lambda i: i * per)
    last_blk = halo_src.shape[0] // SUBLANES - 1
    halo_specs = [
        pl.BlockSpec((SUBLANES, D_MODEL), lambda i: (jnp.clip(first(i) - 1, 0, last_blk), 0)),
        pl.BlockSpec((SUBLANES, D_MODEL), lambda i: (jnp.clip(first(i) + per, 0, last_blk), 0)),
    ]
    return pl.pallas_call(
        functools.partial(_inproj_kernel, pick_ctx=pick_ctx),
        grid=(n,),
        in_specs=x_specs + halo_specs + [
            pl.BlockSpec((None, None, 1, 6 * D_MODEL), lambda i: (layer, jnp.minimum(i, 1), 0, 0)),
            _layer_spec(norm_w, layer),
            _layer_spec(w_cat, layer, resident=True),
            _layer_spec(dt_bias_pad, layer),
            _layer_spec(qk_w, layer),
            pl.BlockSpec((ROW_TILE, LANES), row),
            pl.BlockSpec((ROW_TILE, LANES), row),
            _resident((2 * GQA_WIDTH, 2 * GQA_WIDTH), const),
            _layer_spec(conv_w_pad, layer),
            _layer_spec(conv_b, layer),
        ],
        out_specs=[pl.BlockSpec((ROW_TILE, w), row) for w, _ in outs],
        out_shape=[jax.ShapeDtypeStruct((t, w), d) for w, d in outs],
        compiler_params=_params("parallel"),
        name="inproj",
    )(*x_arrays, halo_src, halo_src, mod4, norm_w, w_cat, dt_bias_pad, qk_w, cos_t, sin_t, ones_bd,
      conv_w_pad, conv_b)


def _ssd_direction(xbc, dt, a_row, tri, expand, s_ref, lane0, backward):
    q = SSD_CHUNK
    x = xbc[:, :SSD_WIDTH]
    nb = SSD_GROUPS * SSD_STATE
    bmat = xbc[:, SSD_WIDTH:SSD_WIDTH + nb].astype(BF16)
    cmat = xbc[:, SSD_WIDTH + nb:].astype(BF16)
    cum = _exact_dot_lhs(tri, dt * a_row)
    yield
    last = 0 if backward else q - 1
    cum_t = cum.T

    def spread(mat_t):
        rows = [jnp.broadcast_to(mat_t[lane0 + h:lane0 + h + 1, :], (HEAD_DIM, q)) for h in range(SSD_HEADS)]
        return jnp.concatenate(rows, axis=0).T

    dt_full = _spread_dot(dt, expand)
    cum_full = spread(cum_t)
    yield
    ea_full = jnp.exp(cum_full)
    te_full = jnp.exp(cum_full[last:last + 1, :] - cum_full)
    xr = x * dt_full
    xr_b = xr.astype(BF16)
    xt_b = (xr * te_full).astype(BF16)
    ti =lax.broadcasted_iota(jnp.int32, (q, q), 0)
    si = lax.broadcasted_iota(jnp.int32, (q, q), 1)
    keep = (si >= ti) if backward else (si <= ti)
    lane = lax.broadcasted_iota(jnp.int32, (q, LANES), 1)
    heads_per_group = SSD_HEADS // SSD_GROUPS
    width_g = heads_per_group * HEAD_DIM
    pieces = []
    decay_total = ea_full[last:last + 1, :]
    for g in range(SSD_GROUPS):
        bg = bmat[:, g * SSD_STATE:(g + 1) * SSD_STATE]
        cg = cmat[:, g * SSD_STATE:(g + 1) * SSD_STATE]
        gmat = _dot_nt(cg, bg)
        s_old = s_ref[:, g * width_g:(g + 1) * width_g]
        y_off = _dot(cg, s_old.astype(BF16)) * ea_full[:, g * width_g:(g + 1) * width_g]
        s_new = _dot_tn(bg, xt_b[:, g * width_g:(g + 1) * width_g])
        s_ref[:, g * width_g:(g + 1) * width_g] = decay_total[:, g * width_g:(g + 1) * width_g] * s_old + s_new
        yield
        for pair in range(heads_per_group // 2):
            col = g * width_g + pair * LANES
            xr_pair = xr_b[:, col:col + LANES]
            ys = []
            for r in range(2):
                hl = lane0 + g * heads_per_group + 2 * pair + r
                seg = jnp.broadcast_to(cum[:, hl:hl + 1], (q, q)) - jnp.broadcast_to(cum_t[hl:hl + 1, :], (q, q))
                dec = jnp.exp(jnp.where(keep, seg, -jnp.inf))
                ys.append(_dot((gmat * dec).astype(BF16), xr_pair))
            y_diag = jnp.where(lane < HEAD_DIM, ys[0], ys[1])
            pieces.append(y_diag + y_off[:, pair * LANES:(pair + 1) * LANES])
            yield
    return jnp.concatenate(pieces, axis=-1)


def _run_interleaved(*stage_generators):
    results = [None] * len(stage_generators)
    live = list(range(len(stage_generators)))
    while live:
        for idx in list(live):
            try:
                next(stage_generators[idx])
            except StopIteration as done:
                results[idx] = done.value
                live.remove(idx)
    return results


def _ssd_kernel(xf_ref, dtf_ref, xb_ref, dtb_ref, a_ref, tril_ref, triu_ref, ef_ref, eb_ref,
                yf_ref, yb_ref, sf_ref, sb_ref):
    @pl.when(pl.program_id(0) == 0)
    def _():
        sf_ref[...] = jnp.zeros_like(sf_ref)
        sb_ref[...] = jnp.zeros_like(sb_ref)

    a_row = a_ref[...]
    q = SSD_CHUNK
    order_f = list(range(SSD_CHUNKS_PER_STEP))
    order_b = order_f[::-1]
    stages = []
    for cf, cb in zip(order_f, order_b):
        rf, rb = slice(cf * q, (cf + 1) * q), slice(cb * q, (cb + 1) * q)
        stages.append(_ssd_direction(xf_ref[rf, :], dtf_ref[rf, :], a_row, tril_ref[...], ef_ref[...],
                                     sf_ref, 0, False))
        stages.append(_ssd_direction(xb_ref[rb, :], dtb_ref[rb, :], a_row, triu_ref[...], eb_ref[...],
                                     sb_ref, SSD_HEADS, True))
    ys = _run_interleaved(*stages)
    for k, (cf, cb) in enumerate(zip(order_f, order_b)):
        yf_ref[cf * q:(cf + 1) * q, :] = ys[2 * k]
        yb_ref[cb * q:(cb + 1) * q, :] = ys[2 * k + 1]


def _ssd_scan(xconv, dt, a_row, layer, n_ctx):
    t = xconv.shape[0]
    q = SSD_CHUNK
    rows = SSD_CHUNKS_PER_STEP * q
    assert n_ctx % rows == 0 and t % rows == 0
    n = t // rows
    n_ctx_blocks = n_ctx // rows
    r = jnp.arange(q)
    tril = (r[None, :] <= r[:, None]).astype(BF16)
    triu = (r[None, :] >= r[:, None]).astype(BF16)
    lanes = jnp.arange(LANES)[:, None]
    head_of_col = (jnp.arange(SSD_WIDTH) // HEAD_DIM)[None, :]
    expand_f = (lanes == head_of_col).astype(BF16)
    expand_b = (lanes == head_of_col + SSD_HEADS).astype(BF16)

    def fwd(i):
        return (i, 0)

    def bwd(i):
        return (jnp.where(i < n_ctx_blocks, n_ctx_blocks - 1 - i, n - 1 - (i - n_ctx_blocks)), 0)

    const = lambda i: (0, 0)
    return pl.pallas_call(
        _ssd_kernel,
        grid=(n,),
        in_specs=[
            pl.BlockSpec((rows, SSD_CONV_CH), fwd),
            pl.BlockSpec((rows, LANES), fwd),
            pl.BlockSpec((rows, SSD_CONV_CH), bwd),
            pl.BlockSpec((rows, LANES), bwd),
            _layer_spec(a_row, layer),
            pl.BlockSpec((q, q), const),
            pl.BlockSpec((q, q), const),
            pl.BlockSpec((LANES, SSD_WIDTH), const),
            pl.BlockSpec((LANES, SSD_WIDTH), const),
        ],
        out_specs=[pl.BlockSpec((rows, SSD_WIDTH), fwd), pl.BlockSpec((rows, SSD_WIDTH), bwd)],
        out_shape=[jax.ShapeDtypeStruct((t, SSD_WIDTH), F32)] * 2,
        scratch_shapes=[pltpu.VMEM((SSD_STATE, SSD_WIDTH), F32)] * 2,
        compiler_params=_params("arbitrary"),
        name="ssd_scan",
    )(xconv, dt, xconv, dt, a_row, tril, triu, expand_f, expand_b)


def _na_kernel(q_ref, k_ref, v_ref, bias_ref, o_ref, *, n_ctx, n_rows):
    i = pl.program_id(0)
    is_ctx = i == 0
    kc = k_ref[0:n_ctx, :]
    vc = v_ref[0:n_ctx, :]
    lane = lax.broadcasted_iota(jnp.int32, (GRID_W, NA_WIDTH), 1)
    mine = [(lane >= h * HEAD_DIM) & (lane < (h + 1) * HEAD_DIM) for h in range(NA_HEADS)]
    win = NA_WIN_ROWS * GRID_W

    def lane_tiles(a):
        return [a[:, c * LANES:(c + 1) * LANES] for c in range(a.shape[-1] // LANES)]

    def grid_row(j):
        r = jnp.maximum((i - 1) * NA_ROWS_PER_STEP + j, 0)
        r_start = jnp.clip(r - NA_WIN_ROWS // 2, 0, n_rows - NA_WIN_ROWS)
        variant = jnp.where(is_ctx, NA_WIN_ROWS, r_start - r + NA_WIN_ROWS - 1)
        start = pl.multiple_of(n_ctx + r_start * GRID_W, GRID_W)
        kw = k_ref[pl.ds(start, win), :]
        vw = v_ref[pl.ds(start, win), :]
        qj = q_ref[j * GRID_W:(j + 1) * GRID_W, :]
        qm = jnp.concatenate([jnp.where(mine[h], qj, jnp.zeros_like(qj)) for h in range(NA_HEADS)], axis=0)
        s_w = _dot_nt(qm, kw) + bias_ref[variant]
        s_c = _dot_nt(qm, kc)
        yield
        m = functools.reduce(jnp.maximum, lane_tiles(s_w) + lane_tiles(s_c))
        m = jnp.broadcast_to(jnp.max(m, axis=-1, keepdims=True), m.shape)
        p_w = jnp.exp(s_w - jnp.concatenate([m] * (s_w.shape[-1] // LANES), axis=-1))
        p_c = jnp.exp(s_c - jnp.concatenate([m] * (s_c.shape[-1] // LANES), axis=-1))
        l = jnp.sum(functools.reduce(jnp.add, lane_tiles(p_w) + lane_tiles(p_c)), axis=-1, keepdims=True)
        y = (_dot(p_w.astype(BF16), vw) + _dot(p_c.astype(BF16), vc)) * (1.0 / l)
        out = y[(NA_HEADS - 1) * GRID_W:]
        for h in range(NA_HEADS - 2, -1, -1):
            out = jnp.where(mine[h], y[h * GRID_W:(h + 1) * GRID_W], out)
        o_ref[j * GRID_W:(j + 1) * GRID_W, :] = out.astype(o_ref.dtype)

    _run_interleaved(*[grid_row(j) for j in range(NA_ROWS_PER_STEP)])


def _na_bias_table(rpb):
    depth = rpb.shape[0]
    col = np.arange(GRID_W)
    c_start = np.clip(col - NA_WIN_COLS // 2, 0, GRID_W - NA_WIN_COLS)
    in_win = (col[None, :] >= c_start[:, None]) & (col[None, :] < c_start[:, None] + NA_WIN_COLS)
    dc = np.clip(col[None, :] - col[:, None] + NA_WIN_COLS - 1, 0, 2 * NA_WIN_COLS - 2)
    n_dc = 2 * NA_WIN_COLS - 1
    n_dr = 2 * NA_WIN_ROWS - 1
    onehot = (dc[None, :, :] == np.arange(n_dc)[:, None, None]).astype(np.float32).reshape(n_dc, GRID_W * GRID_W)
    t2 = jnp.dot(rpb.reshape(depth * NA_HEADS * n_dr, n_dc).astype(F32), onehot, precision=lax.Precision.HIGHEST)
    t2 = jnp.where(in_win[None, None, None], t2.reshape(depth, NA_HEADS, n_dr, GRID_W, GRID_W), -jnp.inf)
    tab = jnp.stack([t2[:, :, v:v + NA_WIN_ROWS] for v in range(NA_WIN_ROWS)], axis=1)
    tab = tab.transpose(0, 1, 2, 4, 3, 5).reshape(depth, NA_WIN_ROWS, NA_HEADS * GRID_W, NA_WIN_ROWS * GRID_W)
    masked = jnp.full((depth, 1) + tab.shape[2:], -jnp.inf, F32)
    return jnp.concatenate([tab, masked], axis=1)


def _neighbourhood_attention(na, bias_tab, layer, n_ctx):
    t = na.shape[0]
    n_rows = (t - n_ctx) // GRID_W
    step_rows = NA_ROWS_PER_STEP * GRID_W
    n = t // step_rows
    return pl.pallas_call(
        functools.partial(_na_kernel, n_ctx=n_ctx, n_rows=n_rows),
        grid=(n,),
        in_specs=[
            pl.BlockSpec((step_rows, NA_WIDTH), lambda i: (i, 0)),
            _resident((t, NA_WIDTH), lambda i: (0, 1)),
            _resident((t, NA_WIDTH), lambda i: (0, 2)),
            _layer_spec(bias_tab, layer, resident=True),
        ],
        out_specs=pl.BlockSpec((step_rows, NA_WIDTH), lambda i: (i, 0)),
        out_shape=jax.ShapeDtypeStruct((t, NA_WIDTH), BF16),
        compiler_params=_params("parallel"),
        name="neighbourhood_attention",
    )(na, na, na, bias_tab)


def _gqa_kernel(qt_ref, k_ref, vt_ref, o_ref, acc_ref, s0_ref, s1_ref, p0_ref, p1_ref, *, n_ctx_sub, n_sub):
    tq = qt_ref.shape[1]
    qt = qt_ref[...]
    row = lax.broadcasted_iota(jnp.int32, qt.shape, 0)
    top = row < HEAD_DIM
    zero = jnp.zeros_like(qt)
    qt2 = jnp.concatenate([jnp.where(top, qt, zero), jnp.where(top, zero, qt)], axis=1)
    acc_ref[...] = jnp.zeros_like(acc_ref)
    s_ref = (s0_ref, s1_ref)
    p_ref = (p0_ref, p1_ref)

    def keys(j, size):
        return pl.ds(pl.multiple_of(j * size, LANES), size)

    def score(key_rows, slot):
        size = key_rows.size
        s = _dot(k_ref[key_rows, :], qt2)
        s_ref[slot][0:size, :] = s
        return jnp.max(s.reshape(size // SUBLANES, SUBLANES, 2 * tq), axis=0)

    def softmax(size, slot, m_old, part_max):
        m_new = jnp.maximum(m_old, jnp.max(part_max, axis=0, keepdims=True))
        s = s_ref[slot][0:size, :].reshape(size // SUBLANES, SUBLANES, 2 * tq)
        p_ref[slot][0:size, :] = jnp.exp2(s - m_new[None]).reshape(size, 2 * tq).astype(BF16)
        return m_new, jnp.exp2(m_old - m_new)

    def accumulate(key_rows, slot, alpha):
        size = key_rows.size
        acc = acc_ref[...].reshape(GQA_V_ROWS // SUBLANES, SUBLANES, 2 * tq) * alpha[None]
        acc_ref[...] = acc.reshape(GQA_V_ROWS, 2 * tq) + _dot(vt_ref[:, key_rows], p_ref[slot][0:size, :])

    m_init = jnp.full((SUBLANES, 2 * tq), -jnp.inf, F32)
    sub = GQA_K_SUB
    assert n_sub >= 3

    @pl.when(pl.program_id(1) == 0)
    def _():
        m = m_init
        for j in range(n_ctx_sub):
            m, alpha = softmax(sub, 0, m, score(keys(j, sub), 0))
            accumulate(keys(j, sub), 0, alpha)

    @pl.when(pl.program_id(1) > 0)
    def _():
        part0 = score(keys(0, sub), 0)
        part1 = score(keys(1, sub), 1)
        m, alpha = softmax(sub, 0, m_init, part0)

        def step(t, slot, carry):
            m, alpha, part = carry
            part_next = score(keys(t, sub), slot)
            accumulate(keys(t - 2, sub), slot, alpha)
            return softmax(sub, 1 - slot, m, part) + (part_next,)

        def trip(n, carry):
            for u in range(GQA_STEPS_PER_TRIP):
                carry = step(2 + n * GQA_STEPS_PER_TRIP + u, u % 2, carry)
            return carry

        n_trips = (n_sub - 2) // GQA_STEPS_PER_TRIP
        carry = lax.fori_loop(0, n_trips, trip, (m, alpha, part1))
        for t in range(2 + n_trips * GQA_STEPS_PER_TRIP, n_sub):
            carry = step(t, t % 2, carry)
        m, alpha, part = carry
        accumulate(keys(n_sub - 2, sub), (n_sub - 2) % 2, alpha)
        m, alpha = softmax(sub, (n_sub - 1) % 2, m, part)
        accumulate(keys(n_sub - 1, sub), (n_sub - 1) % 2, alpha)

    acc = acc_ref[...]
    denom = acc[HEAD_DIM:HEAD_DIM + SUBLANES]
    o_t = (acc[:HEAD_DIM].reshape(HEAD_DIM // SUBLANES, SUBLANES, 2 * tq) / denom[None]).reshape(HEAD_DIM, 2 * tq)
    o_ref[...] = jnp.concatenate([o_t[:, :tq], o_t[:, tq:]], axis=0).T.astype(o_ref.dtype)


def _gqa_bounded_kernel(qt_ref, k_ref, vt_ref, o_ref, acc_ref, *p_ref, n_ctx_sub, n_sub):
    tq = qt_ref.shape[1]
    qt = qt_ref[...]
    row = lax.broadcasted_iota(jnp.int32, qt.shape, 0)
    top = row < HEAD_DIM
    zero = jnp.zeros_like(qt)
    qt2 = jnp.concatenate([jnp.where(top, qt, zero), jnp.where(top, zero, qt)], axis=1)
    acc_ref[...] = jnp.zeros_like(acc_ref)
    sub = GQA_K_SUB
    n_slots = len(p_ref)
    lag = n_slots - 1
    steps = GQA_BOUNDED_STEPS_PER_TRIP
    assert steps % n_slots == 0 and n_sub > lag

    def keys(j):
        return pl.ds(pl.multiple_of(j * sub, sub), sub)

    def probs(j, slot):
        p_ref[slot][...] = jnp.exp2(_dot(k_ref[keys(j), :], qt2)).astype(BF16)

    def accumulate(j, slot):
        acc_ref[...] += _dot(vt_ref[:, keys(j)], p_ref[slot][...])

    @pl.when(pl.program_id(1) == 0)
    def _():
        for j in range(n_ctx_sub):
            probs(j, 0)
            accumulate(j, 0)

    @pl.when(pl.program_id(1) > 0)
    def _():
        for t in range(lag):
            probs(t, t % n_slots)

        def trip(n, carry):
            for u in range(steps):
                t = lag + n * steps + u
                probs(t, (lag + u) % n_slots)
                accumulate(t - lag, u % n_slots)
            return carry

        n_trips = (n_sub - lag) // steps
        lax.fori_loop(0, n_trips, trip, 0)
        for t in range(lag + n_trips * steps, n_sub):
            probs(t, t % n_slots)
            accumulate(t - lag, (t - lag) % n_slots)
        for t in range(n_sub - lag, n_sub):
            accumulate(t, t % n_slots)

    acc = acc_ref[...]
    denom = acc[HEAD_DIM:HEAD_DIM + SUBLANES]
    o_t = (acc[:HEAD_DIM].reshape(HEAD_DIM // SUBLANES, SUBLANES, 2 * tq) / denom[None]).reshape(HEAD_DIM, 2 * tq)
    o_ref[...] = jnp.concatenate([o_t[:, :tq], o_t[:, tq:]], axis=0).T.astype(o_ref.dtype)


def _gqa_dispatch_kernel(bounded_ref, qt_ref, k_ref, vt_ref, o_ref, acc_ref, s0_ref, s1_ref, *p_ref,
                         n_ctx_sub, n_sub):
    @pl.when(bounded_ref[0] != 0)
    def _():
        _gqa_bounded_kernel(qt_ref, k_ref, vt_ref, o_ref, acc_ref, *p_ref, n_ctx_sub=n_ctx_sub, n_sub=n_sub)

    @pl.when(bounded_ref[0] == 0)
    def _():
        _gqa_kernel(qt_ref, k_ref, vt_ref, o_ref, acc_ref, s0_ref, s1_ref, p_ref[0], p_ref[1],
                    n_ctx_sub=n_ctx_sub, n_sub=n_sub)


def _gqa_attention(gqt, gk, gvt, n_ctx, score_bound):
    nq = gqt.shape[1]
    n_keys = gk.shape[0]
    tq = GQA_Q_TILE
    assert n_ctx == tq and n_ctx % GQA_K_SUB == 0 and n_keys % GQA_K_SUB == 0 and nq % tq == 0
    bounded = (score_bound <= GQA_BOUNDED_LOG2).astype(jnp.int32).reshape(1)
    return pl.pallas_call(
        functools.partial(_gqa_dispatch_kernel, n_ctx_sub=n_ctx // GQA_K_SUB, n_sub=n_keys // GQA_K_SUB),
        grid=(GQA_KV_HEADS, nq // tq),
        in_specs=[
            pl.BlockSpec(memory_space=pltpu.SMEM),
            pl.BlockSpec((LANES, tq), lambda g, i: (g, i)),
            pl.BlockSpec((n_keys, LANES), lambda g, i: (0, g)),
            pl.BlockSpec((GQA_V_ROWS, n_keys), lambda g, i: (g, 0)),
        ],
        out_specs=pl.BlockSpec((tq, LANES), lambda g, i: (i, g)),
        out_shape=jax.ShapeDtypeStruct((nq, GQA_WIDTH), BF16),
        scratch_shapes=[pltpu.VMEM((GQA_V_ROWS, 2 * tq), F32)] + [pltpu.VMEM((GQA_K_SUB, 2 * tq), F32)] * 2
        + [pltpu.VMEM((GQA_K_SUB, 2 * tq), BF16)] * (GQA_PV_LAG + 1),
        compiler_params=_params("arbitrary", "arbitrary"),
        name="gqa_attention",
    )(bounded, gqt, gk, gvt)


def _out_ffn_kernel(ya_ref, yf_ref, yb_ref, xs_ref, z_ref, yg_ref, xc_ref, xl_ref, mod_ref,
                    dskip_ref, snw_ref, wo_ref, fnw_ref, wg_ref, wu_ref, wd_ref, final_ref,
                    o_ref, *, final, pick_ctx):
    x = jnp.where(pl.program_id(0) == 0, xc_ref[...], xl_ref[...]) if pick_ctx else xl_ref[...]
    y = yf_ref[...] + yb_ref[...] + dskip_ref[...] * xs_ref[...]
    y = y * _silu(z_ref[...])
    ms = jnp.mean(y * y, axis=-1, keepdims=True)
    y = y * lax.rsqrt(ms + EPS) * snw_ref[...]
    mix = jnp.concatenate([ya_ref[...], y.astype(BF16), yg_ref[...]], axis=-1)
    g_m = mod_ref[:, 2 * D_MODEL:3 * D_MODEL]
    sh_f = mod_ref[:, 3 * D_MODEL:4 * D_MODEL]
    sc_f = mod_ref[:, 4 * D_MODEL:5 * D_MODEL]
    g_f = mod_ref[:, 5 * D_MODEL:6 * D_MODEL]
    x1 = x + g_m * _dot(mix, wo_ref[...])
    ms1 = jnp.mean(x1 * x1, axis=-1, keepdims=True)
    hf = (x1 * lax.rsqrt(ms1 + EPS) * fnw_ref[...] * (1.0 + sc_f) + sh_f).astype(BF16)
    act = (_silu(_dot(hf, wg_ref[...])) * _dot(hf, wu_ref[...])).astype(BF16)
    x2 = x1 + g_f * _dot(act, wd_ref[...])
    if final:
        ms2 = jnp.mean(x2 * x2, axis=-1, keepdims=True)
        x2 = x2 * lax.rsqrt(ms2 + EPS) * final_ref[...]
    o_ref[...] = x2


def _out_ffn(ya, yf, yb, xconv, z, yg, x_parts, mod4, layer, d_full, ssd_nw, w_out, ffn_nw,
             w_gate, w_up, w_down, final_nw, final):
    t = ya.shape[0]
    skip = 1 if final else 0
    n = t // ROW_TILE - skip
    row = lambda i: (i + skip, 0)
    const = lambda i: (0, 0)
    x_arrays, x_specs, pick_ctx = _residual_operands(x_parts, skip)
    return pl.pallas_call(
        functools.partial(_out_ffn_kernel, final=final, pick_ctx=pick_ctx),
        grid=(n,),
        in_specs=[
            pl.BlockSpec((ROW_TILE, NA_WIDTH), row),
            pl.BlockSpec((ROW_TILE, SSD_WIDTH), row),
            pl.BlockSpec((ROW_TILE, SSD_WIDTH), row),
            pl.BlockSpec((ROW_TILE, SSD_WIDTH), row),
            pl.BlockSpec((ROW_TILE, SSD_WIDTH), row),
            pl.BlockSpec((ROW_TILE, GQA_WIDTH), row),
        ] + x_specs + [
            pl.BlockSpec((None, None, 1, 6 * D_MODEL), lambda i: (layer, jnp.minimum(i + skip, 1), 0, 0)),
            _layer_spec(d_full, layer),
            _layer_spec(ssd_nw, layer),
            _layer_spec(w_out, layer, resident=True),
            _layer_spec(ffn_nw, layer),
            _layer_spec(w_gate, layer, resident=True),
            _layer_spec(w_up, layer, resident=True),
            _layer_spec(w_down, layer, resident=True),
            pl.BlockSpec((1, D_MODEL), const),
        ],
        out_specs=pl.BlockSpec((ROW_TILE, D_MODEL), lambda i: (i, 0)),
        out_shape=jax.ShapeDtypeStruct((n * ROW_TILE, D_MODEL), F32),
        compiler_params=_params("parallel"),
        name="out_ffn",
    )(ya, yf, yb, xconv, z, yg, *x_arrays, mod4, d_full, ssd_nw, w_out, ffn_nw, w_gate, w_up, w_down, final_nw)


def _rearranged_w_in(w):
    na_in = 3 * NA_WIDTH
    o_z = na_in
    o_xbc = o_z + SSD_WIDTH
    o_dt = o_xbc + SSD_CONV_CH
    o_gq = o_dt + 2 * SSD_HEADS
    o_gk = o_gq + GQA_WIDTH
    o_gv = o_gk + GQA_KV_HEADS * HEAD_DIM
    w = w.astype(BF16)
    q_na = w[..., :NA_WIDTH] * ATTN_SCALE
    dt_pad = jnp.zeros(w.shape[:-1] + (LANES - 2 * SSD_HEADS,), BF16)
    k_heads = [w[..., o_gk + h * HEAD_DIM:o_gk + (h + 1) * HEAD_DIM] for h in range(GQA_KV_HEADS)]
    return jnp.concatenate(
        [q_na, w[..., NA_WIDTH:o_dt], w[..., o_dt:o_gq], dt_pad, w[..., o_gq:o_gk]]
        + [p for h in k_heads for p in (h, h)] + [w[..., o_gv:]], axis=-1)


def _rope_tables(n_ctx, n_lat):
    f32 = np.float32
    freqs = f32(ROPE_THETA) ** (-np.arange(ROPE_PAIRS, dtype=f32) / f32(ROPE_PAIRS))
    n_rows = n_lat // GRID_W
    half = 2 * ROPE_PAIRS
    sign = np.where(np.arange(half) < ROPE_PAIRS, -1.0, 1.0).astype(f32)

    def tables(n_pos):
        a = np.arange(n_pos, dtype=f32)[:, None] * freqs[None, :]
        a = np.concatenate([a, a], axis=-1)
        return np.cos(a), np.sin(a) * sign[None, :]

    def per_token(by_row, by_col):
        lat = np.concatenate([np.broadcast_to(by_row[:, None, :], (n_rows, GRID_W, half)),
                              np.broadcast_to(by_col[None, :, :], (n_rows, GRID_W, half))], axis=-1)
        return lat.reshape(n_lat, HEAD_DIM)

    cos_r, sin_r = tables(n_rows)
    cos_c, sin_c = tables(GRID_W)
    cos = np.concatenate([np.ones((n_ctx, HEAD_DIM), f32), per_token(cos_r, cos_c)], axis=0)
    sin = np.concatenate([np.zeros((n_ctx, HEAD_DIM), f32), per_token(sin_r, sin_c)], axis=0)
    reps = (1, LANES // HEAD_DIM)
    return jnp.asarray(np.tile(cos, reps), F32), jnp.asarray(np.tile(sin, reps), F32)


def kernel(x, c, ctx, c_ctx, mod_w, mod_b, norm_attn_w, norm_ffn_w, w_in, na_rpb, ssd_conv_w, ssd_conv_b,
           ssd_dt_bias, ssd_a_log, ssd_d, ssd_norm_w, q_norm_w, k_norm_w, w_out, ffn_w_gate, ffn_w_up,
           ffn_w_down, final_norm_w):
    depth = mod_w.shape[0]
    batch, n_lat, _ = x.shape
    n_ctx = ctx.shape[1]
    assert batch == 1 and n_ctx == ROW_TILE and n_lat % (NA_ROWS_PER_STEP * GRID_W) == 0
    assert n_lat % GQA_Q_TILE == 0 and n_lat // GRID_W >= NA_WIN_ROWS

    x_parts = (ctx[0], x[0])
    cc = jnp.zeros((SUBLANES, D_MODEL), F32).at[0].set(c_ctx).at[1].set(c[0])
    mod = _modulation(cc, mod_w, mod_b)
    mod4 = mod[:, :2].reshape(depth, 2, 1, 6 * D_MODEL)

    cos_t, sin_t = _rope_tables(n_ctx, n_lat)
    blk = np.arange(2 * GQA_WIDTH) // HEAD_DIM
    ones_bd = jnp.asarray(blk[:, None] == blk[None, :], BF16)
    ones_rows = jnp.ones((GQA_V_ROWS - HEAD_DIM, n_ctx + n_lat), BF16)

    row = lambda p: p.astype(F32).reshape(depth, 1, -1)
    pad_lanes = lambda p: jnp.pad(row(p), ((0, 0), (0, 0), (0, LANES - 2 * SSD_HEADS)))
    w_cat = _rearranged_w_in(w_in)
    dt_bias_pad = pad_lanes(ssd_dt_bias)
    a_row = pad_lanes(-jnp.exp(ssd_a_log.astype(F32)))
    qk_w = row(jnp.concatenate([jnp.tile(q_norm_w * (ATTN_SCALE * LOG2E), (1, GQA_Q_HEADS)),
                                jnp.tile(k_norm_w, (1, 2 * GQA_KV_HEADS))], axis=-1))
    conv_w_pad = jnp.pad(ssd_conv_w, ((0, 0), (0, SUBLANES - SSD_CONV), (0, 0)))
    d_full = row(jnp.repeat(ssd_d, HEAD_DIM, axis=-1))
    bias_tab = _na_bias_table(na_rpb)
    w_out_b, w_gate_b, w_up_b, w_down_b = (w.astype(BF16) for w in (w_out, ffn_w_gate, ffn_w_up, ffn_w_down))
    score_bound = (1.02 * HEAD_DIM * ATTN_SCALE * LOG2E) * jnp.max(jnp.abs(q_norm_w), axis=-1) * jnp.max(
        jnp.abs(k_norm_w), axis=-1)

    for i in range(depth):
        final = i == depth - 1
        na, z, xconv, dt, gq, gk, gv = _inproj(x_parts, n_ctx + n_lat, mod4, i, row(norm_attn_w), w_cat,
                                               dt_bias_pad, qk_w, cos_t, sin_t, ones_bd, conv_w_pad,
                                               row(ssd_conv_b))
        yf, yb = _ssd_scan(xconv, dt, a_row, i, n_ctx)
        ya = _neighbourhood_attention(na, bias_tab, i, n_ctx)
        gvt = jnp.concatenate([part for g in range(GQA_KV_HEADS)
                               for part in (gv[:, g * HEAD_DIM:(g + 1) * HEAD_DIM].T, ones_rows)], axis=0)
        yg = _gqa_attention(gq.T, gk, gvt, n_ctx, score_bound[i])
        x_parts = _out_ffn(ya, yf, yb, xconv, z, yg, x_parts, mod4, i, d_full, row(ssd_norm_w), w_out_b,
                           row(norm_ffn_w), w_gate_b, w_up_b, w_down_b, final_norm_w.reshape(1, -1), final)
    return x_parts[None]
```

```python
import functools
import math

import jax
import jax.numpy as jnp
import numpy as np
from jax import lax
from jax.experimental import pallas as pl
from jax.experimental.pallas import tpu as pltpu

F32 = jnp.float32
BF16 = jnp.bfloat16

D_MODEL = 1024
GRID_W = 64
HEAD_DIM = 64
NA_WIDTH = 256
NA_HEADS = 4
NA_WIN_ROWS = 8
NA_WIN_COLS = 16
SSD_WIDTH = 512
SSD_HEADS = 8
SSD_GROUPS = 2
SSD_STATE = 128
SSD_CONV = 5
SSD_CHUNK = 128
SSD_CONV_CH = SSD_WIDTH + 2 * SSD_GROUPS * SSD_STATE
GQA_WIDTH = 256
GQA_Q_HEADS = 4
GQA_KV_HEADS = 2
ROPE_THETA = 10000.0
ROPE_PAIRS = HEAD_DIM // 4
FFN_HIDDEN = 2816
EPS = 1e-6
ATTN_SCALE = HEAD_DIM ** -0.5
LOG2E = math.log2(math.e)

LANES = 128
SUBLANES = 8
VMEM_LIMIT_BYTES = 56 * 1024 * 1024

ROW_TILE = 256
MOD_COL_TILE = 2048
SSD_CHUNKS_PER_STEP = 2
NA_ROWS_PER_STEP = 4
GQA_Q_TILE = 256
GQA_V_ROWS = HEAD_DIM + 16
GQA_K_SUB = 256
GQA_BOUNDED_LOG2 = 60.0
GQA_PV_LAG = 4
GQA_BOUNDED_STEPS_PER_TRIP = 10
GQA_STEPS_PER_TRIP = 8

C_NA = 0
C_Z = C_NA + 3 * NA_WIDTH
C_XBC = C_Z + SSD_WIDTH
C_DT = C_XBC + SSD_CONV_CH
C_GQ = C_DT + LANES
C_GK = C_GQ + GQA_WIDTH
C_GV = C_GK + 2 * LANES
C_END = C_GV + LANES


def _silu(v):
    return v * (1.0 / (1.0 + jnp.exp(-v)))


def _softplus(v):
    return jnp.maximum(v, 0.0) + jnp.log(1.0 + jnp.exp(-jnp.abs(v)))


def _split3(v):
    hi = v.astype(BF16)
    r1 = v - hi.astype(F32)
    mid = r1.astype(BF16)
    lo = (r1 - mid.astype(F32)).astype(BF16)
    return hi, mid, lo


def _dot(a, b):
    return jnp.dot(a, b, preferred_element_type=F32)


def _dot_nt(a, b):
    return lax.dot_general(a, b, (((1,), (1,)), ((), ())), preferred_element_type=F32)


def _dot_tn(a, b):
    return lax.dot_general(a, b, (((0,), (0,)), ((), ())), preferred_element_type=F32)


def _exact_dot(v, sel):
    hi, mid, lo = _split3(v)
    return _dot(hi, sel) + _dot(mid, sel) + _dot(lo, sel)


def _spread_dot(v, sel):
    hi = v.astype(BF16)
    lo = (v - hi.astype(F32)).astype(BF16)
    return _dot(hi, sel) + _dot(lo, sel)


def _exact_dot_lhs(sel, v):
    hi, mid, lo = _split3(v)
    return _dot(sel, hi) + _dot(sel, mid) + _dot(sel, lo)


def _params(*sem):
    return pltpu.CompilerParams(dimension_semantics=sem, vmem_limit_bytes=VMEM_LIMIT_BYTES)


def _resident(shape, index_map):
    return pl.BlockSpec(shape, index_map, pipeline_mode=pl.Buffered(1))


def _layer_spec(stacked, layer, resident=False):
    shape = stacked.shape[1:]
    index_map = lambda *_: (layer,) + (0,) * len(shape)
    if resident:
        return pl.BlockSpec((None,) + shape, index_map, pipeline_mode=pl.Buffered(1))
    return pl.BlockSpec((None,) + shape, index_map)


def _mod_kernel(cc_ref, w_ref, b_ref, o_ref):
    a = _silu(cc_ref[...])
    o_ref[0] = jnp.dot(a, w_ref[0], preferred_element_type=F32) + b_ref[0]


def _modulation(cc, mod_w, mod_b):
    depth = mod_w.shape[0]
    cols = MOD_COL_TILE
    ncol = mod_w.shape[2] // cols
    return pl.pallas_call(
        _mod_kernel,
        grid=(depth, ncol),
        in_specs=[
            pl.BlockSpec((SUBLANES, D_MODEL), lambda l, j: (0, 0)),
            pl.BlockSpec((1, D_MODEL, cols), lambda l, j: (l, 0, j)),
            pl.BlockSpec((1, 1, cols), lambda l, j: (l, 0, j)),
        ],
        out_specs=pl.BlockSpec((1, SUBLANES, cols), lambda l, j: (l, 0, j)),
        out_shape=jax.ShapeDtypeStruct((depth, SUBLANES, ncol * cols), F32),
        compiler_params=_params("arbitrary", "arbitrary"),
        name="modulation",
    )(cc, mod_w, mod_b.reshape(depth, 1, -1))


def _residual_operands(x_parts, skip):
    ctx_spec = pl.BlockSpec((ROW_TILE, D_MODEL), lambda i: (0, 0))
    if isinstance(x_parts, tuple):
        lat_spec = pl.BlockSpec((ROW_TILE, D_MODEL), lambda i: (jnp.maximum(i + skip - 1, 0), 0))
        return list(x_parts), [ctx_spec, lat_spec], skip == 0
    row_spec = pl.BlockSpec((ROW_TILE, D_MODEL), lambda i: (i + skip, 0))
    return [x_parts, x_parts], [ctx_spec, row_spec], False


def _inproj_kernel(xc_ref, xl_ref, prev_ref, next_ref, mod_ref, nw_ref, w_ref, dtb_ref, qkw_ref, cos_ref, sin_ref,
                   ones_ref, cw_ref, cb_ref, na_ref, z_ref, xconv_ref, dt_ref, gq_ref, gk_ref, gv_ref, *, pick_ctx):
    i = pl.program_id(0)
    n = pl.num_programs(0)
    rows = xl_ref.shape[0]
    x = jnp.where(i == 0, xc_ref[...], xl_ref[...]) if pick_ctx else xl_ref[...]
    x = jnp.concatenate([prev_ref[...], x, next_ref[...]], axis=0)
    ms = jnp.mean(x * x, axis=-1, keepdims=True)
    xn = x * lax.rsqrt(ms + EPS) * nw_ref[...]
    sh = mod_ref[:, 0:D_MODEL]
    sc = mod_ref[:, D_MODEL:2 * D_MODEL]
    h = (xn * (1.0 + sc) + sh).astype(BF16)
    xbc_ext = _dot(h, w_ref[:, C_XBC:C_DT])
    h_tile = h[SUBLANES:SUBLANES + rows]
    u = jnp.concatenate([_dot(h_tile, w_ref[:, C_NA:C_XBC]), xbc_ext[SUBLANES:SUBLANES + rows],
                         _dot(h_tile, w_ref[:, C_DT:C_END])], axis=-1)
    na_ref[...] = u[:, C_NA:C_Z].astype(BF16)
    z_ref[...] = u[:, C_Z:C_XBC]
    has_prev = i >= 2
    has_next = jnp.logical_and(i >= 1, i < n - 1)
    xbc = u[:, C_XBC:C_DT]
    ext = jnp.concatenate([jnp.where(has_prev, xbc_ext[0:SUBLANES], 0.0), xbc,
                           jnp.where(has_next, xbc_ext[SUBLANES + rows:], 0.0)], axis=0)
    total = rows + 2 * SUBLANES
    half = SSD_CONV // 2
    acc = cb_ref[...] + cw_ref[half:half + 1, :] * xbc
    for j in range(SSD_CONV):
        if j != half:
            shifted = pltpu.roll(ext, (half - j) % total, 0)
            acc = acc + cw_ref[j:j + 1, :] * shifted[SUBLANES:SUBLANES + rows]
    xconv_ref[...] = _silu(acc)
    dt_ref[...] = _softplus(u[:, C_DT:C_GQ] + dtb_ref[...])
    gv_ref[...] = u[:, C_GV:C_END].astype(BF16)
    g = u[:, C_GQ:C_GV]
    gsq = g * g
    hi = gsq.astype(BF16)
    lo = (gsq - hi.astype(F32)).astype(BF16)
    ss = _dot(hi, ones_ref[...]) + _dot(lo, ones_ref[...])
    gn = g * lax.rsqrt(ss * (1.0 / HEAD_DIM) + EPS) * qkw_ref[...]
    width = gn.shape[-1]
    lane = lax.broadcasted_iota(jnp.int32, gn.shape, 1)
    first = (lane % (2 * ROPE_PAIRS)) < ROPE_PAIRS
    partner = jnp.where(first, pltpu.roll(gn, width - ROPE_PAIRS, 1), pltpu.roll(gn, ROPE_PAIRS, 1))
    cos = jnp.concatenate([cos_ref[...]] * (width // LANES), axis=-1)
    sin = jnp.concatenate([sin_ref[...]] * (width // LANES), axis=-1)
    gr = gn * cos + partner * sin
    gq_ref[...] = gr[:, :GQA_WIDTH].astype(BF16)
    gk_ref[...] = gr[:, GQA_WIDTH:].astype(BF16)


def _inproj(x_parts, t, mod4, layer, norm_w, w_cat, dt_bias_pad, qk_w, cos_t, sin_t, ones_bd, conv_w_pad, conv_b):
    n = t // ROW_TILE
    row = lambda i: (i, 0)
    const = lambda i: (0, 0)
    outs = [
        (3 * NA_WIDTH, BF16), (SSD_WIDTH, F32), (SSD_CONV_CH, F32), (LANES, F32),
        (GQA_WIDTH, BF16), (2 * LANES, BF16), (LANES, BF16),
    ]
    x_arrays, x_specs, pick_ctx = _residual_operands(x_parts, 0)
    halo_src = x_arrays[1]
    per = ROW_TILE // SUBLANES
    first = (lambda i: (i - 1) * per) if isinstance(x_parts, tuple) else (lambda i: i * per)
    last_blk = halo_src.shape[0] // SUBLANES - 1
    halo_specs = [
        pl.BlockSpec((SUBLANES, D_MODEL), lambda i: (jnp.clip(first(i) - 1, 0, last_blk), 0)),
        pl.BlockSpec((SUBLANES, D_MODEL), lambda i: (jnp.clip(first(i) + per, 0, last_blk), 0)),
    ]
    return pl.pallas_call(
        functools.partial(_inproj_kernel, pick_ctx=pick_ctx),
        grid=(n,),
        in_specs=x_specs + halo_specs + [
            pl.BlockSpec((None, None, 1, 6 * D_MODEL), lambda i: (layer, jnp.minimum(i, 1), 0, 0)),
            _layer_spec(norm_w, layer),
            _layer_spec(w_cat, layer, resident=True),
            _layer_spec(dt_bias_pad, layer),
            _layer_spec(qk_w, layer),
            pl.BlockSpec((ROW_TILE, LANES), row),
            pl.BlockSpec((ROW_TILE, LANES), row),
            _resident((2 * GQA_WIDTH, 2 * GQA_WIDTH), const),
            _layer_spec(conv_w_pad, layer),
            _layer_spec(conv_b, layer),
        ],
        out_specs=[pl.BlockSpec((ROW_TILE, w), row) for w, _ in outs],
        out_shape=[jax.ShapeDtypeStruct((t, w), d) for w, d in outs],
        compiler_params=_params("parallel"),
        name="inproj",
    )(*x_arrays, halo_src, halo_src, mod4, norm_w, w_cat, dt_bias_pad, qk_w, cos_t, sin_t, ones_bd,
      conv_w_pad, conv_b)


def _ssd_direction(xbc, dt, a_row, tri, expand, s_ref, lane0, backward):
    q = SSD_CHUNK
    x = xbc[:, :SSD_WIDTH]
    nb = SSD_GROUPS * SSD_STATE
    bmat = xbc[:, SSD_WIDTH:SSD_WIDTH + nb].astype(BF16)
    cmat = xbc[:, SSD_WIDTH + nb:].astype(BF16)
    cum = _exact_dot_lhs(tri, dt * a_row)
    yield
    last = 0 if backward else q - 1
    cum_t = cum.T

    def spread(mat_t):
        rows = [jnp.broadcast_to(mat_t[lane0 + h:lane0 + h + 1, :], (HEAD_DIM, q)) for h in range(SSD_HEADS)]
        return jnp.concatenate(rows, axis=0).T

    dt_full = _spread_dot(dt, expand)
    cum_full = spread(cum_t)
    yield
    ea_full = jnp.exp(cum_full)
    te_full = jnp.exp(cum_full[last:last + 1, :] - cum_full)
    xr = x * dt_full
    xr_b = xr.astype(BF16)
    xt_b = (xr * te_full).astype(BF16)
    ti =lax.broadcasted_iota(jnp.int32, (q, q), 0)
    si = lax.broadcasted_iota(jnp.int32, (q, q), 1)
    keep = (si >= ti) if backward else (si <= ti)
    lane = lax.broadcasted_iota(jnp.int32, (q, LANES), 1)
    heads_per_group = SSD_HEADS // SSD_GROUPS
    width_g = heads_per_group * HEAD_DIM
    pieces = []
    decay_total = ea_full[last:last + 1, :]
    for g in range(SSD_GROUPS):
        bg = bmat[:, g * SSD_STATE:(g + 1) * SSD_STATE]
        cg = cmat[:, g * SSD_STATE:(g + 1) * SSD_STATE]
        gmat = _dot_nt(cg, bg)
        s_old = s_ref[:, g * width_g:(g + 1) * width_g]
        y_off = _dot(cg, s_old.astype(BF16)) * ea_full[:, g * width_g:(g + 1) * width_g]
        s_new = _dot_tn(bg, xt_b[:, g * width_g:(g + 1) * width_g])
        s_ref[:, g * width_g:(g + 1) * width_g] = decay_total[:, g * width_g:(g + 1) * width_g] * s_old + s_new
        yield
        for pair in range(heads_per_group // 2):
            col = g * width_g + pair * LANES
            xr_pair = xr_b[:, col:col + LANES]
            ys = []
            for r in range(2):
                hl = lane0 + g * heads_per_group + 2 * pair + r
                seg = jnp.broadcast_to(cum[:, hl:hl + 1], (q, q)) - jnp.broadcast_to(cum_t[hl:hl + 1, :], (q, q))
                dec = jnp.exp(jnp.where(keep, seg, -jnp.inf))
                ys.append(_dot((gmat * dec).astype(BF16), xr_pair))
            y_diag = jnp.where(lane < HEAD_DIM, ys[0], ys[1])
            pieces.append(y_diag + y_off[:, pair * LANES:(pair + 1) * LANES])
            yield
    return jnp.concatenate(pieces, axis=-1)


def _run_interleaved(*stage_generators):
    results = [None] * len(stage_generators)
    live = list(range(len(stage_generators)))
    while live:
        for idx in list(live):
            try:
                next(stage_generators[idx])
            except StopIteration as done:
                results[idx] = done.value
                live.remove(idx)
    return results


def _ssd_kernel(xf_ref, dtf_ref, xb_ref, dtb_ref, a_ref, tril_ref, triu_ref, ef_ref, eb_ref,
                yf_ref, yb_ref, sf_ref, sb_ref):
    @pl.when(pl.program_id(0) == 0)
    def _():
        sf_ref[...] = jnp.zeros_like(sf_ref)
        sb_ref[...] = jnp.zeros_like(sb_ref)

    a_row = a_ref[...]
    q = SSD_CHUNK
    order_f = list(range(SSD_CHUNKS_PER_STEP))
    order_b = order_f[::-1]
    stages = []
    for cf, cb in zip(order_f, order_b):
        rf, rb = slice(cf * q, (cf + 1) * q), slice(cb * q, (cb + 1) * q)
        stages.append(_ssd_direction(xf_ref[rf, :], dtf_ref[rf, :], a_row, tril_ref[...], ef_ref[...],
                                     sf_ref, 0, False))
        stages.append(_ssd_direction(xb_ref[rb, :], dtb_ref[rb, :], a_row, triu_ref[...], eb_ref[...],
                                     sb_ref, SSD_HEADS, True))
    ys = _run_interleaved(*stages)
    for k, (cf, cb) in enumerate(zip(order_f, order_b)):
        yf_ref[cf * q:(cf + 1) * q, :] = ys[2 * k]
        yb_ref[cb * q:(cb + 1) * q, :] = ys[2 * k + 1]


def _ssd_scan(xconv, dt, a_row, layer, n_ctx):
    t = xconv.shape[0]
    q = SSD_CHUNK
    rows = SSD_CHUNKS_PER_STEP * q
    assert n_ctx % rows == 0 and t % rows == 0
    n = t // rows
    n_ctx_blocks = n_ctx // rows
    r = jnp.arange(q)
    tril = (r[None, :] <= r[:, None]).astype(BF16)
    triu = (r[None, :] >= r[:, None]).astype(BF16)
    lanes = jnp.arange(LANES)[:, None]
    head_of_col = (jnp.arange(SSD_WIDTH) // HEAD_DIM)[None, :]
    expand_f = (lanes == head_of_col).astype(BF16)
    expand_b = (lanes == head_of_col + SSD_HEADS).astype(BF16)

    def fwd(i):
        return (i, 0)

    def bwd(i):
        return (jnp.where(i < n_ctx_blocks, n_ctx_blocks - 1 - i, n - 1 - (i - n_ctx_blocks)), 0)

    const = lambda i: (0, 0)
    return pl.pallas_call(
        _ssd_kernel,
        grid=(n,),
        in_specs=[
            pl.BlockSpec((rows, SSD_CONV_CH), fwd),
            pl.BlockSpec((rows, LANES), fwd),
            pl.BlockSpec((rows, SSD_CONV_CH), bwd),
            pl.BlockSpec((rows, LANES), bwd),
            _layer_spec(a_row, layer),
            pl.BlockSpec((q, q), const),
            pl.BlockSpec((q, q), const),
            pl.BlockSpec((LANES, SSD_WIDTH), const),
            pl.BlockSpec((LANES, SSD_WIDTH), const),
        ],
        out_specs=[pl.BlockSpec((rows, SSD_WIDTH), fwd), pl.BlockSpec((rows, SSD_WIDTH), bwd)],
        out_shape=[jax.ShapeDtypeStruct((t, SSD_WIDTH), F32)] * 2,
        scratch_shapes=[pltpu.VMEM((SSD_STATE, SSD_WIDTH), F32)] * 2,
        compiler_params=_params("arbitrary"),
        name="ssd_scan",
    )(xconv, dt, xconv, dt, a_row, tril, triu, expand_f, expand_b)


def _na_kernel(q_ref, k_ref, v_ref, bias_ref, o_ref, *, n_ctx, n_rows):
    i = pl.program_id(0)
    is_ctx = i == 0
    kc = k_ref[0:n_ctx, :]
    vc = v_ref[0:n_ctx, :]
    lane = lax.broadcasted_iota(jnp.int32, (GRID_W, NA_WIDTH), 1)
    mine = [(lane >= h * HEAD_DIM) & (lane < (h + 1) * HEAD_DIM) for h in range(NA_HEADS)]
    win = NA_WIN_ROWS * GRID_W

    def lane_tiles(a):
        return [a[:, c * LANES:(c + 1) * LANES] for c in range(a.shape[-1] // LANES)]

    def grid_row(j):
        r = jnp.maximum((i - 1) * NA_ROWS_PER_STEP + j, 0)
        r_start = jnp.clip(r - NA_WIN_ROWS // 2, 0, n_rows - NA_WIN_ROWS)
        variant = jnp.where(is_ctx, NA_WIN_ROWS, r_start - r + NA_WIN_ROWS - 1)
        start = pl.multiple_of(n_ctx + r_start * GRID_W, GRID_W)
        kw = k_ref[pl.ds(start, win), :]
        vw = v_ref[pl.ds(start, win), :]
        qj = q_ref[j * GRID_W:(j + 1) * GRID_W, :]
        qm = jnp.concatenate([jnp.where(mine[h], qj, jnp.zeros_like(qj)) for h in range(NA_HEADS)], axis=0)
        s_w = _dot_nt(qm, kw) + bias_ref[variant]
        s_c = _dot_nt(qm, kc)
        yield
        m = functools.reduce(jnp.maximum, lane_tiles(s_w) + lane_tiles(s_c))
        m = jnp.broadcast_to(jnp.max(m, axis=-1, keepdims=True), m.shape)
        p_w = jnp.exp(s_w - jnp.concatenate([m] * (s_w.shape[-1] // LANES), axis=-1))
        p_c = jnp.exp(s_c - jnp.concatenate([m] * (s_c.shape[-1] // LANES), axis=-1))
        l = jnp.sum(functools.reduce(jnp.add, lane_tiles(p_w) + lane_tiles(p_c)), axis=-1, keepdims=True)
        y = (_dot(p_w.astype(BF16), vw) + _dot(p_c.astype(BF16), vc)) * (1.0 / l)
        out = y[(NA_HEADS - 1) * GRID_W:]
        for h in range(NA_HEADS - 2, -1, -1):
            out = jnp.where(mine[h], y[h * GRID_W:(h + 1) * GRID_W], out)
        o_ref[j * GRID_W:(j + 1) * GRID_W, :] = out.astype(o_ref.dtype)

    _run_interleaved(*[grid_row(j) for j in range(NA_ROWS_PER_STEP)])


def _na_bias_table(rpb):
    depth = rpb.shape[0]
    col = np.arange(GRID_W)
    c_start = np.clip(col - NA_WIN_COLS // 2, 0, GRID_W - NA_WIN_COLS)
    in_win = (col[None, :] >= c_start[:, None]) & (col[None, :] < c_start[:, None] + NA_WIN_COLS)
    dc = np.clip(col[None, :] - col[:, None] + NA_WIN_COLS - 1, 0, 2 * NA_WIN_COLS - 2)
    n_dc = 2 * NA_WIN_COLS - 1
    n_dr = 2 * NA_WIN_ROWS - 1
    onehot = (dc[None, :, :] == np.arange(n_dc)[:, None, None]).astype(np.float32).reshape(n_dc, GRID_W * GRID_W)
    t2 = jnp.dot(rpb.reshape(depth * NA_HEADS * n_dr, n_dc).astype(F32), onehot, precision=lax.Precision.HIGHEST)
    t2 = jnp.where(in_win[None, None, None], t2.reshape(depth, NA_HEADS, n_dr, GRID_W, GRID_W), -jnp.inf)
    tab = jnp.stack([t2[:, :, v:v + NA_WIN_ROWS] for v in range(NA_WIN_ROWS)], axis=1)
    tab = tab.transpose(0, 1, 2, 4, 3, 5).reshape(depth, NA_WIN_ROWS, NA_HEADS * GRID_W, NA_WIN_ROWS * GRID_W)
    masked = jnp.full((depth, 1) + tab.shape[2:], -jnp.inf, F32)
    return jnp.concatenate([tab, masked], axis=1)


def _neighbourhood_attention(na, bias_tab, layer, n_ctx):
    t = na.shape[0]
    n_rows = (t - n_ctx) // GRID_W
    step_rows = NA_ROWS_PER_STEP * GRID_W
    n = t // step_rows
    return pl.pallas_call(
        functools.partial(_na_kernel, n_ctx=n_ctx, n_rows=n_rows),
        grid=(n,),
        in_specs=[
            pl.BlockSpec((step_rows, NA_WIDTH), lambda i: (i, 0)),
            _resident((t, NA_WIDTH), lambda i: (0, 1)),
            _resident((t, NA_WIDTH), lambda i: (0, 2)),
            _layer_spec(bias_tab, layer, resident=True),
        ],
        out_specs=pl.BlockSpec((step_rows, NA_WIDTH), lambda i: (i, 0)),
        out_shape=jax.ShapeDtypeStruct((t, NA_WIDTH), BF16),
        compiler_params=_params("parallel"),
        name="neighbourhood_attention",
    )(na, na, na, bias_tab)


def _gqa_kernel(qt_ref, k_ref, vt_ref, o_ref, acc_ref, s0_ref, s1_ref, p0_ref, p1_ref, *, n_ctx_sub, n_sub):
    tq = qt_ref.shape[1]
    qt = qt_ref[...]
    row = lax.broadcasted_iota(jnp.int32, qt.shape, 0)
    top = row < HEAD_DIM
    zero = jnp.zeros_like(qt)
    qt2 = jnp.concatenate([jnp.where(top, qt, zero), jnp.where(top, zero, qt)], axis=1)
    acc_ref[...] = jnp.zeros_like(acc_ref)
    s_ref = (s0_ref, s1_ref)
    p_ref = (p0_ref, p1_ref)

    def keys(j, size):
        return pl.ds(pl.multiple_of(j * size, LANES), size)

    def score(key_rows, slot):
        size = key_rows.size
        s = _dot(k_ref[key_rows, :], qt2)
        s_ref[slot][0:size, :] = s
        return jnp.max(s.reshape(size // SUBLANES, SUBLANES, 2 * tq), axis=0)

    def softmax(size, slot, m_old, part_max):
        m_new = jnp.maximum(m_old, jnp.max(part_max, axis=0, keepdims=True))
        s = s_ref[slot][0:size, :].reshape(size // SUBLANES, SUBLANES, 2 * tq)
        p_ref[slot][0:size, :] = jnp.exp2(s - m_new[None]).reshape(size, 2 * tq).astype(BF16)
        return m_new, jnp.exp2(m_old - m_new)

    def accumulate(key_rows, slot, alpha):
        size = key_rows.size
        acc = acc_ref[...].reshape(GQA_V_ROWS // SUBLANES, SUBLANES, 2 * tq) * alpha[None]
        acc_ref[...] = acc.reshape(GQA_V_ROWS, 2 * tq) + _dot(vt_ref[:, key_rows], p_ref[slot][0:size, :])

    m_init = jnp.full((SUBLANES, 2 * tq), -jnp.inf, F32)
    sub = GQA_K_SUB
    assert n_sub >= 3

    @pl.when(pl.program_id(1) == 0)
    def _():
        m = m_init
        for j in range(n_ctx_sub):
            m, alpha = softmax(sub, 0, m, score(keys(j, sub), 0))
            accumulate(keys(j, sub), 0, alpha)

    @pl.when(pl.program_id(1) > 0)
    def _():
        part0 = score(keys(0, sub), 0)
        part1 = score(keys(1, sub), 1)
        m, alpha = softmax(sub, 0, m_init, part0)

        def step(t, slot, carry):
            m, alpha, part = carry
            part_next = score(keys(t, sub), slot)
            accumulate(keys(t - 2, sub), slot, alpha)
            return softmax(sub, 1 - slot, m, part) + (part_next,)

        def trip(n, carry):
            for u in range(GQA_STEPS_PER_TRIP):
                carry = step(2 + n * GQA_STEPS_PER_TRIP + u, u % 2, carry)
            return carry

        n_trips = (n_sub - 2) // GQA_STEPS_PER_TRIP
        carry = lax.fori_loop(0, n_trips, trip, (m, alpha, part1))
        for t in range(2 + n_trips * GQA_STEPS_PER_TRIP, n_sub):
            carry = step(t, t % 2, carry)
        m, alpha, part = carry
        accumulate(keys(n_sub - 2, sub), (n_sub - 2) % 2, alpha)
        m, alpha = softmax(sub, (n_sub - 1) % 2, m, part)
        accumulate(keys(n_sub - 1, sub), (n_sub - 1) % 2, alpha)

    acc = acc_ref[...]
    denom = acc[HEAD_DIM:HEAD_DIM + SUBLANES]
    o_t = (acc[:HEAD_DIM].reshape(HEAD_DIM // SUBLANES, SUBLANES, 2 * tq) / denom[None]).reshape(HEAD_DIM, 2 * tq)
    o_ref[...] = jnp.concatenate([o_t[:, :tq], o_t[:, tq:]], axis=0).T.astype(o_ref.dtype)


def _gqa_bounded_kernel(qt_ref, k_ref, vt_ref, o_ref, acc_ref, *p_ref, n_ctx_sub, n_sub):
    tq = qt_ref.shape[1]
    qt = qt_ref[...]
    row = lax.broadcasted_iota(jnp.int32, qt.shape, 0)
    top = row < HEAD_DIM
    zero = jnp.zeros_like(qt)
    qt2 = jnp.concatenate([jnp.where(top, qt, zero), jnp.where(top, zero, qt)], axis=1)
    acc_ref[...] = jnp.zeros_like(acc_ref)
    sub = GQA_K_SUB
    n_slots = len(p_ref)
    lag = n_slots - 1
    steps = GQA_BOUNDED_STEPS_PER_TRIP
    assert steps % n_slots == 0 and n_sub > lag

    def keys(j):
        return pl.ds(pl.multiple_of(j * sub, sub), sub)

    def probs(j, slot):
        p_ref[slot][...] = jnp.exp2(_dot(k_ref[keys(j), :], qt2)).astype(BF16)

    def accumulate(j, slot):
        acc_ref[...] += _dot(vt_ref[:, keys(j)], p_ref[slot][...])

    @pl.when(pl.program_id(1) == 0)
    def _():
        for j in range(n_ctx_sub):
            probs(j, 0)
            accumulate(j, 0)

    @pl.when(pl.program_id(1) > 0)
    def _():
        for t in range(lag):
            probs(t, t % n_slots)

        def trip(n, carry):
            for u in range(steps):
                t = lag + n * steps + u
                probs(t, (lag + u) % n_slots)
                accumulate(t - lag, u % n_slots)
            return carry

        n_trips = (n_sub - lag) // steps
        lax.fori_loop(0, n_trips, trip, 0)
        for t in range(lag + n_trips * steps, n_sub):
            probs(t, t % n_slots)
            accumulate(t - lag, (t - lag) % n_slots)
        for t in range(n_sub - lag, n_sub):
            accumulate(t, t % n_slots)

    acc = acc_ref[...]
    denom = acc[HEAD_DIM:HEAD_DIM + SUBLANES]
    o_t = (acc[:HEAD_DIM].reshape(HEAD_DIM // SUBLANES, SUBLANES, 2 * tq) / denom[None]).reshape(HEAD_DIM, 2 * tq)
    o_ref[...] = jnp.concatenate([o_t[:, :tq], o_t[:, tq:]], axis=0).T.astype(o_ref.dtype)


def _gqa_dispatch_kernel(bounded_ref, qt_ref, k_ref, vt_ref, o_ref, acc_ref, s0_ref, s1_ref, *p_ref,
                         n_ctx_sub, n_sub):
    @pl.when(bounded_ref[0] != 0)
    def _():
        _gqa_bounded_kernel(qt_ref, k_ref, vt_ref, o_ref, acc_ref, *p_ref, n_ctx_sub=n_ctx_sub, n_sub=n_sub)

    @pl.when(bounded_ref[0] == 0)
    def _():
        _gqa_kernel(qt_ref, k_ref, vt_ref, o_ref, acc_ref, s0_ref, s1_ref, p_ref[0], p_ref[1],
                    n_ctx_sub=n_ctx_sub, n_sub=n_sub)


def _gqa_attention(gqt, gk, gvt, n_ctx, score_bound):
    nq = gqt.shape[1]
    n_keys = gk.shape[0]
    tq = GQA_Q_TILE
    assert n_ctx == tq and n_ctx % GQA_K_SUB == 0 and n_keys % GQA_K_SUB == 0 and nq % tq == 0
    bounded = (score_bound <= GQA_BOUNDED_LOG2).astype(jnp.int32).reshape(1)
    return pl.pallas_call(
        functools.partial(_gqa_dispatch_kernel, n_ctx_sub=n_ctx // GQA_K_SUB, n_sub=n_keys // GQA_K_SUB),
        grid=(GQA_KV_HEADS, nq // tq),
        in_specs=[
            pl.BlockSpec(memory_space=pltpu.SMEM),
            pl.BlockSpec((LANES, tq), lambda g, i: (g, i)),
            pl.BlockSpec((n_keys, LANES), lambda g, i: (0, g)),
            pl.BlockSpec((GQA_V_ROWS, n_keys), lambda g, i: (g, 0)),
        ],
        out_specs=pl.BlockSpec((tq, LANES), lambda g, i: (i, g)),
        out_shape=jax.ShapeDtypeStruct((nq, GQA_WIDTH), BF16),
        scratch_shapes=[pltpu.VMEM((GQA_V_ROWS, 2 * tq), F32)] + [pltpu.VMEM((GQA_K_SUB, 2 * tq), F32)] * 2
        + [pltpu.VMEM((GQA_K_SUB, 2 * tq), BF16)] * (GQA_PV_LAG + 1),
        compiler_params=_params("arbitrary", "arbitrary"),
        name="gqa_attention",
    )(bounded, gqt, gk, gvt)


def _out_ffn_kernel(ya_ref, yf_ref, yb_ref, xs_ref, z_ref, yg_ref, xc_ref, xl_ref, mod_ref,
                    dskip_ref, snw_ref, wo_ref, fnw_ref, wg_ref, wu_ref, wd_ref, final_ref,
                    o_ref, *, final, pick_ctx):
    x = jnp.where(pl.program_id(0) == 0, xc_ref[...], xl_ref[...]) if pick_ctx else xl_ref[...]
    y = yf_ref[...] + yb_ref[...] + dskip_ref[...] * xs_ref[...]
    y = y * _silu(z_ref[...])
    ms = jnp.mean(y * y, axis=-1, keepdims=True)
    y = y * lax.rsqrt(ms + EPS) * snw_ref[...]
    mix = jnp.concatenate([ya_ref[...], y.astype(BF16), yg_ref[...]], axis=-1)
    g_m = mod_ref[:, 2 * D_MODEL:3 * D_MODEL]
    sh_f = mod_ref[:, 3 * D_MODEL:4 * D_MODEL]
    sc_f = mod_ref[:, 4 * D_MODEL:5 * D_MODEL]
    g_f = mod_ref[:, 5 * D_MODEL:6 * D_MODEL]
    x1 = x + g_m * _dot(mix, wo_ref[...])
    ms1 = jnp.mean(x1 * x1, axis=-1, keepdims=True)
    hf = (x1 * lax.rsqrt(ms1 + EPS) * fnw_ref[...] * (1.0 + sc_f) + sh_f).astype(BF16)
    act = (_silu(_dot(hf, wg_ref[...])) * _dot(hf, wu_ref[...])).astype(BF16)
    x2 = x1 + g_f * _dot(act, wd_ref[...])
    if final:
        ms2 = jnp.mean(x2 * x2, axis=-1, keepdims=True)
        x2 = x2 * lax.rsqrt(ms2 + EPS) * final_ref[...]
    o_ref[...] = x2


def _out_ffn(ya, yf, yb, xconv, z, yg, x_parts, mod4, layer, d_full, ssd_nw, w_out, ffn_nw,
             w_gate, w_up, w_down, final_nw, final):
    t = ya.shape[0]
    skip = 1 if final else 0
    n = t // ROW_TILE - skip
    row = lambda i: (i + skip, 0)
    const = lambda i: (0, 0)
    x_arrays, x_specs, pick_ctx = _residual_operands(x_parts, skip)
    return pl.pallas_call(
        functools.partial(_out_ffn_kernel, final=final, pick_ctx=pick_ctx),
        grid=(n,),
        in_specs=[
            pl.BlockSpec((ROW_TILE, NA_WIDTH), row),
            pl.BlockSpec((ROW_TILE, SSD_WIDTH), row),
            pl.BlockSpec((ROW_TILE, SSD_WIDTH), row),
            pl.BlockSpec((ROW_TILE, SSD_WIDTH), row),
            pl.BlockSpec((ROW_TILE, SSD_WIDTH), row),
            pl.BlockSpec((ROW_TILE, GQA_WIDTH), row),
        ] + x_specs + [
            pl.BlockSpec((None, None, 1, 6 * D_MODEL), lambda i: (layer, jnp.minimum(i + skip, 1), 0, 0)),
            _layer_spec(d_full, layer),
            _layer_spec(ssd_nw, layer),
            _layer_spec(w_out, layer, resident=True),
            _layer_spec(ffn_nw, layer),
            _layer_spec(w_gate, layer, resident=True),
            _layer_spec(w_up, layer, resident=True),
            _layer_spec(w_down, layer, resident=True),
            pl.BlockSpec((1, D_MODEL), const),
        ],
        out_specs=pl.BlockSpec((ROW_TILE, D_MODEL), lambda i: (i, 0)),
        out_shape=jax.ShapeDtypeStruct((n * ROW_TILE, D_MODEL), F32),
        compiler_params=_params("parallel"),
        name="out_ffn",
    )(ya, yf, yb, xconv, z, yg, *x_arrays, mod4, d_full, ssd_nw, w_out, ffn_nw, w_gate, w_up, w_down, final_nw)


def _rearranged_w_in(w):
    na_in = 3 * NA_WIDTH
    o_z = na_in
    o_xbc = o_z + SSD_WIDTH
    o_dt = o_xbc + SSD_CONV_CH
    o_gq = o_dt + 2 * SSD_HEADS
    o_gk = o_gq + GQA_WIDTH
    o_gv = o_gk + GQA_KV_HEADS * HEAD_DIM
    w = w.astype(BF16)
    q_na = w[..., :NA_WIDTH] * ATTN_SCALE
    dt_pad = jnp.zeros(w.shape[:-1] + (LANES - 2 * SSD_HEADS,), BF16)
    k_heads = [w[..., o_gk + h * HEAD_DIM:o_gk + (h + 1) * HEAD_DIM] for h in range(GQA_KV_HEADS)]
    return jnp.concatenate(
        [q_na, w[..., NA_WIDTH:o_dt], w[..., o_dt:o_gq], dt_pad, w[..., o_gq:o_gk]]
        + [p for h in k_heads for p in (h, h)] + [w[..., o_gv:]], axis=-1)


def _rope_tables(n_ctx, n_lat):
    f32 = np.float32
    freqs = f32(ROPE_THETA) ** (-np.arange(ROPE_PAIRS, dtype=f32) / f32(ROPE_PAIRS))
    n_rows = n_lat // GRID_W
    half = 2 * ROPE_PAIRS
    sign = np.where(np.arange(half) < ROPE_PAIRS, -1.0, 1.0).astype(f32)

    def tables(n_pos):
        a = np.arange(n_pos, dtype=f32)[:, None] * freqs[None, :]
        a = np.concatenate([a, a], axis=-1)
        return np.cos(a), np.sin(a) * sign[None, :]

    def per_token(by_row, by_col):
        lat = np.concatenate([np.broadcast_to(by_row[:, None, :], (n_rows, GRID_W, half)),
                              np.broadcast_to(by_col[None, :, :], (n_rows, GRID_W, half))], axis=-1)
        return lat.reshape(n_lat, HEAD_DIM)

    cos_r, sin_r = tables(n_rows)
    cos_c, sin_c = tables(GRID_W)
    cos = np.concatenate([np.ones((n_ctx, HEAD_DIM), f32), per_token(cos_r, cos_c)], axis=0)
    sin = np.concatenate([np.zeros((n_ctx, HEAD_DIM), f32), per_token(sin_r, sin_c)], axis=0)
    reps = (1, LANES // HEAD_DIM)
    return jnp.asarray(np.tile(cos, reps), F32), jnp.asarray(np.tile(sin, reps), F32)


def kernel(x, c, ctx, c_ctx, mod_w, mod_b, norm_attn_w, norm_ffn_w, w_in, na_rpb, ssd_conv_w, ssd_conv_b,
           ssd_dt_bias, ssd_a_log, ssd_d, ssd_norm_w, q_norm_w, k_norm_w, w_out, ffn_w_gate, ffn_w_up,
           ffn_w_down, final_norm_w):
    depth = mod_w.shape[0]
    batch, n_lat, _ = x.shape
    n_ctx = ctx.shape[1]
    assert batch == 1 and n_ctx == ROW_TILE and n_lat % (NA_ROWS_PER_STEP * GRID_W) == 0
    assert n_lat % GQA_Q_TILE == 0 and n_lat // GRID_W >= NA_WIN_ROWS

    x_parts = (ctx[0], x[0])
    cc = jnp.zeros((SUBLANES, D_MODEL), F32).at[0].set(c_ctx).at[1].set(c[0])
    mod = _modulation(cc, mod_w, mod_b)
    mod4 = mod[:, :2].reshape(depth, 2, 1, 6 * D_MODEL)

    cos_t, sin_t = _rope_tables(n_ctx, n_lat)
    blk = np.arange(2 * GQA_WIDTH) // HEAD_DIM
    ones_bd = jnp.asarray(blk[:, None] == blk[None, :], BF16)
    ones_rows = jnp.ones((GQA_V_ROWS - HEAD_DIM, n_ctx + n_lat), BF16)

    row = lambda p: p.astype(F32).reshape(depth, 1, -1)
    pad_lanes = lambda p: jnp.pad(row(p), ((0, 0), (0, 0), (0, LANES - 2 * SSD_HEADS)))
    w_cat = _rearranged_w_in(w_in)
    dt_bias_pad = pad_lanes(ssd_dt_bias)
    a_row = pad_lanes(-jnp.exp(ssd_a_log.astype(F32)))
    qk_w = row(jnp.concatenate([jnp.tile(q_norm_w * (ATTN_SCALE * LOG2E), (1, GQA_Q_HEADS)),
                                jnp.tile(k_norm_w, (1, 2 * GQA_KV_HEADS))], axis=-1))
    conv_w_pad = jnp.pad(ssd_conv_w, ((0, 0), (0, SUBLANES - SSD_CONV), (0, 0)))
    d_full = row(jnp.repeat(ssd_d, HEAD_DIM, axis=-1))
    bias_tab = _na_bias_table(na_rpb)
    w_out_b, w_gate_b, w_up_b, w_down_b = (w.astype(BF16) for w in (w_out, ffn_w_gate, ffn_w_up, ffn_w_down))
    score_bound = (1.02 * HEAD_DIM * ATTN_SCALE * LOG2E) * jnp.max(jnp.abs(q_norm_w), axis=-1) * jnp.max(
        jnp.abs(k_norm_w), axis=-1)

    for i in range(depth):
        final = i == depth - 1
        na, z, xconv, dt, gq, gk, gv = _inproj(x_parts, n_ctx + n_lat, mod4, i, row(norm_attn_w), w_cat,
                                               dt_bias_pad, qk_w, cos_t, sin_t, ones_bd, conv_w_pad,
                                               row(ssd_conv_b))
        yf, yb = _ssd_scan(xconv, dt, a_row, i, n_ctx)
        ya = _neighbourhood_attention(na, bias_tab, i, n_ctx)
        gvt = jnp.concatenate([part for g in range(GQA_KV_HEADS)
                               for part in (gv[:, g * HEAD_DIM:(g + 1) * HEAD_DIM].T, ones_rows)], axis=0)
        yg = _gqa_attention(gq.T, gk, gvt, n_ctx, score_bound[i])
        x_parts = _out_ffn(ya, yf, yb, xconv, z, yg, x_parts, mod4, i, d_full, row(ssd_norm_w), w_out_b,
                           row(norm_ffn_w), w_gate_b, w_up_b, w_down_b, final_norm_w.reshape(1, -1), final)
    return x_parts[None]
```

```python
import functools
import math

import jax
import jax.numpy as jnp
import numpy as np
from jax import lax
from jax.experimental import pallas as pl
from jax.experimental.pallas import tpu as pltpu

F32 = jnp.float32
BF16 = jnp.bfloat16

D_MODEL = 1024
GRID_W = 64
HEAD_DIM = 64
NA_WIDTH = 256
NA_HEADS = 4
NA_WIN_ROWS = 8
NA_WIN_COLS = 16
SSD_WIDTH = 512
SSD_HEADS = 8
SSD_GROUPS = 2
SSD_STATE = 128
SSD_CONV = 5
SSD_CHUNK = 128
SSD_CONV_CH = SSD_WIDTH + 2 * SSD_GROUPS * SSD_STATE
GQA_WIDTH = 256
GQA_Q_HEADS = 4
GQA_KV_HEADS = 2
ROPE_THETA = 10000.0
ROPE_PAIRS = HEAD_DIM // 4
FFN_HIDDEN = 2816
EPS = 1e-6
ATTN_SCALE = HEAD_DIM ** -0.5
LOG2E = math.log2(math.e)

LANES = 128
SUBLANES = 8
VMEM_LIMIT_BYTES = 56 * 1024 * 1024

ROW_TILE = 256
MOD_COL_TILE = 2048
SSD_CHUNKS_PER_STEP = 2
NA_ROWS_PER_STEP = 4
GQA_Q_TILE = 256
GQA_V_ROWS = HEAD_DIM + 16
GQA_K_SUB = 256
GQA_BOUNDED_LOG2 = 60.0
GQA_PV_LAG = 5
GQA_BOUNDED_STEPS_PER_TRIP = 12
GQA_STEPS_PER_TRIP = 8

C_NA = 0
C_Z = C_NA + 3 * NA_WIDTH
C_XBC = C_Z + SSD_WIDTH
C_DT = C_XBC + SSD_CONV_CH
C_GQ = C_DT + LANES
C_GK = C_GQ + GQA_WIDTH
C_GV = C_GK + 2 * LANES
C_END = C_GV + LANES


def _silu(v):
    return v * (1.0 / (1.0 + jnp.exp(-v)))


def _softplus(v):
    return jnp.maximum(v, 0.0) + jnp.log(1.0 + jnp.exp(-jnp.abs(v)))


def _split3(v):
    hi = v.astype(BF16)
    r1 = v - hi.astype(F32)
    mid = r1.astype(BF16)
    lo = (r1 - mid.astype(F32)).astype(BF16)
    return hi, mid, lo


def _dot(a, b):
    return jnp.dot(a, b, preferred_element_type=F32)


def _dot_nt(a, b):
    return lax.dot_general(a, b, (((1,), (1,)), ((), ())), preferred_element_type=F32)


def _dot_tn(a, b):
    return lax.dot_general(a, b, (((0,), (0,)), ((), ())), preferred_element_type=F32)


def _exact_dot(v, sel):
    hi, mid, lo = _split3(v)
    return _dot(hi, sel) + _dot(mid, sel) + _dot(lo, sel)


def _spread_dot(v, sel):
    hi = v.astype(BF16)
    lo = (v - hi.astype(F32)).astype(BF16)
    return _dot(hi, sel) + _dot(lo, sel)


def _exact_dot_lhs(sel, v):
    hi, mid, lo = _split3(v)
    return _dot(sel, hi) + _dot(sel, mid) + _dot(sel, lo)


def _params(*sem):
    return pltpu.CompilerParams(dimension_semantics=sem, vmem_limit_bytes=VMEM_LIMIT_BYTES)


def _resident(shape, index_map):
    return pl.BlockSpec(shape, index_map, pipeline_mode=pl.Buffered(1))


def _layer_spec(stacked, layer, resident=False):
    shape = stacked.shape[1:]
    index_map = lambda *_: (layer,) + (0,) * len(shape)
    if resident:
        return pl.BlockSpec((None,) + shape, index_map, pipeline_mode=pl.Buffered(1))
    return pl.BlockSpec((None,) + shape, index_map)


def _mod_kernel(cc_ref, w_ref, b_ref, o_ref):
    a = _silu(cc_ref[...])
    o_ref[0] = jnp.dot(a, w_ref[0], preferred_element_type=F32) + b_ref[0]


def _modulation(cc, mod_w, mod_b):
    depth = mod_w.shape[0]
    cols = MOD_COL_TILE
    ncol = mod_w.shape[2] // cols
    return pl.pallas_call(
        _mod_kernel,
        grid=(depth, ncol),
        in_specs=[
            pl.BlockSpec((SUBLANES, D_MODEL), lambda l, j: (0, 0)),
            pl.BlockSpec((1, D_MODEL, cols), lambda l, j: (l, 0, j)),
            pl.BlockSpec((1, 1, cols), lambda l, j: (l, 0, j)),
        ],
        out_specs=pl.BlockSpec((1, SUBLANES, cols), lambda l, j: (l, 0, j)),
        out_shape=jax.ShapeDtypeStruct((depth, SUBLANES, ncol * cols), F32),
        compiler_params=_params("arbitrary", "arbitrary"),
        name="modulation",
    )(cc, mod_w, mod_b.reshape(depth, 1, -1))


def _residual_operands(x_parts, skip):
    ctx_spec = pl.BlockSpec((ROW_TILE, D_MODEL), lambda i: (0, 0))
    if isinstance(x_parts, tuple):
        lat_spec = pl.BlockSpec((ROW_TILE, D_MODEL), lambda i: (jnp.maximum(i + skip - 1, 0), 0))
        return list(x_parts), [ctx_spec, lat_spec], skip == 0
    row_spec = pl.BlockSpec((ROW_TILE, D_MODEL), lambda i: (i + skip, 0))
    return [x_parts, x_parts], [ctx_spec, row_spec], False


def _inproj_kernel(xc_ref, xl_ref, prev_ref, next_ref, mod_ref, nw_ref, w_ref, dtb_ref, qkw_ref, cos_ref, sin_ref,
                   ones_ref, cw_ref, cb_ref, na_ref, z_ref, xconv_ref, dt_ref, gq_ref, gk_ref, gv_ref, *, pick_ctx):
    i = pl.program_id(0)
    n = pl.num_programs(0)
    rows = xl_ref.shape[0]
    x = jnp.where(i == 0, xc_ref[...], xl_ref[...]) if pick_ctx else xl_ref[...]
    x = jnp.concatenate([prev_ref[...], x, next_ref[...]], axis=0)
    ms = jnp.mean(x * x, axis=-1, keepdims=True)
    xn = x * lax.rsqrt(ms + EPS) * nw_ref[...]
    sh = mod_ref[:, 0:D_MODEL]
    sc = mod_ref[:, D_MODEL:2 * D_MODEL]
    h = (xn * (1.0 + sc) + sh).astype(BF16)
    xbc_ext = _dot(h, w_ref[:, C_XBC:C_DT])
    h_tile = h[SUBLANES:SUBLANES + rows]
    u = jnp.concatenate([_dot(h_tile, w_ref[:, C_NA:C_XBC]), xbc_ext[SUBLANES:SUBLANES + rows],
                         _dot(h_tile, w_ref[:, C_DT:C_END])], axis=-1)
    na_ref[...] = u[:, C_NA:C_Z].astype(BF16)
    z_ref[...] = u[:, C_Z:C_XBC]
    has_prev = i >= 2
    has_next = jnp.logical_and(i >= 1, i < n - 1)
    xbc = u[:, C_XBC:C_DT]
    ext = jnp.concatenate([jnp.where(has_prev, xbc_ext[0:SUBLANES], 0.0), xbc,
                           jnp.where(has_next, xbc_ext[SUBLANES + rows:], 0.0)], axis=0)
    total = rows + 2 * SUBLANES
    half = SSD_CONV // 2
    acc = cb_ref[...] + cw_ref[half:half + 1, :] * xbc
    for j in range(SSD_CONV):
        if j != half:
            shifted = pltpu.roll(ext, (half - j) % total, 0)
            acc = acc + cw_ref[j:j + 1, :] * shifted[SUBLANES:SUBLANES + rows]
    xconv_ref[...] = _silu(acc)
    dt_ref[...] = _softplus(u[:, C_DT:C_GQ] + dtb_ref[...])
    gv_ref[...] = u[:, C_GV:C_END].astype(BF16)
    g = u[:, C_GQ:C_GV]
    gsq = g * g
    hi = gsq.astype(BF16)
    lo = (gsq - hi.astype(F32)).astype(BF16)
    ss = _dot(hi, ones_ref[...]) + _dot(lo, ones_ref[...])
    gn = g * lax.rsqrt(ss * (1.0 / HEAD_DIM) + EPS) * qkw_ref[...]
    width = gn.shape[-1]
    lane = lax.broadcasted_iota(jnp.int32, gn.shape, 1)
    first = (lane % (2 * ROPE_PAIRS)) < ROPE_PAIRS
    partner = jnp.where(first, pltpu.roll(gn, width - ROPE_PAIRS, 1), pltpu.roll(gn, ROPE_PAIRS, 1))
    cos = jnp.concatenate([cos_ref[...]] * (width // LANES), axis=-1)
    sin = jnp.concatenate([sin_ref[...]] * (width // LANES), axis=-1)
    gr = gn * cos + partner * sin
    gq_ref[...] = gr[:, :GQA_WIDTH].astype(BF16)
    gk_ref[...] = gr[:, GQA_WIDTH:].astype(BF16)


def _inproj(x_parts, t, mod4, layer, norm_w, w_cat, dt_bias_pad, qk_w, cos_t, sin_t, ones_bd, conv_w_pad, conv_b):
    n = t // ROW_TILE
    row = lambda i: (i, 0)
    const = lambda i: (0, 0)
    outs = [
        (3 * NA_WIDTH, BF16), (SSD_WIDTH, F32), (SSD_CONV_CH, F32), (LANES, F32),
        (GQA_WIDTH, BF16), (2 * LANES, BF16), (LANES, BF16),
    ]
    x_arrays, x_specs, pick_ctx = _residual_operands(x_parts, 0)
    halo_src = x_arrays[1]
    per = ROW_TILE // SUBLANES
    first = (lambda i: (i - 1) * per) if isinstance(x_parts, tuple) else (lambda i: i * per)
    last_blk = halo_src.shape[0] // SUBLANES - 1
    halo_specs = [
        pl.BlockSpec((SUBLANES, D_MODEL), lambda i: (jnp.clip(first(i) - 1, 0, last_blk), 0)),
        pl.BlockSpec((SUBLANES, D_MODEL), lambda i: (jnp.clip(first(i) + per, 0, last_blk), 0)),
    ]
    return pl.pallas_call(
        functools.partial(_inproj_kernel, pick_ctx=pick_ctx),
        grid=(n,),
        in_specs=x_specs + halo_specs + [
            pl.BlockSpec((None, None, 1, 6 * D_MODEL), lambda i: (layer, jnp.minimum(i, 1), 0, 0)),
            _layer_spec(norm_w, layer),
            _layer_spec(w_cat, layer, resident=True),
            _layer_spec(dt_bias_pad, layer),
            _layer_spec(qk_w, layer),
            pl.BlockSpec((ROW_TILE, LANES), row),
            pl.BlockSpec((ROW_TILE, LANES), row),
            _resident((2 * GQA_WIDTH, 2 * GQA_WIDTH), const),
            _layer_spec(conv_w_pad, layer),
            _layer_spec(conv_b, layer),
        ],
        out_specs=[pl.BlockSpec((ROW_TILE, w), row) for w, _ in outs],
        out_shape=[jax.ShapeDtypeStruct((t, w), d) for w, d in outs],
        compiler_params=_params("parallel"),
        name="inproj",
    )(*x_arrays, halo_src, halo_src, mod4, norm_w, w_cat, dt_bias_pad, qk_w, cos_t, sin_t, ones_bd,
      conv_w_pad, conv_b)


def _ssd_direction(xbc, dt, a_row, tri, expand, s_ref, lane0, backward):
    q = SSD_CHUNK
    x = xbc[:, :SSD_WIDTH]
    nb = SSD_GROUPS * SSD_STATE
    bmat = xbc[:, SSD_WIDTH:SSD_WIDTH + nb].astype(BF16)
    cmat = xbc[:, SSD_WIDTH + nb:].astype(BF16)
    cum = _exact_dot_lhs(tri, dt * a_row)
    yield
    last = 0 if backward else q - 1
    cum_t = cum.T

    def spread(mat_t):
        rows = [jnp.broadcast_to(mat_t[lane0 + h:lane0 + h + 1, :], (HEAD_DIM, q)) for h in range(SSD_HEADS)]
        return jnp.concatenate(rows, axis=0).T

    dt_full = _spread_dot(dt, expand)
    cum_full = spread(cum_t)
    yield
    ea_full = jnp.exp(cum_full)
    te_full = jnp.exp(cum_full[last:last + 1, :] - cum_full)
    xr = x * dt_full
    xr_b = xr.astype(BF16)
    xt_b = (xr * te_full).astype(BF16)
    ti =lax.broadcasted_iota(jnp.int32, (q, q), 0)
    si = lax.broadcasted_iota(jnp.int32, (q, q), 1)
    keep = (si >= ti) if backward else (si <= ti)
    lane = lax.broadcasted_iota(jnp.int32, (q, LANES), 1)
    heads_per_group = SSD_HEADS // SSD_GROUPS
    width_g = heads_per_group * HEAD_DIM
    pieces = []
    decay_total = ea_full[last:last + 1, :]
    for g in range(SSD_GROUPS):
        bg = bmat[:, g * SSD_STATE:(g + 1) * SSD_STATE]
        cg = cmat[:, g * SSD_STATE:(g + 1) * SSD_STATE]
        gmat = _dot_nt(cg, bg)
        s_old = s_ref[:, g * width_g:(g + 1) * width_g]
        y_off = _dot(cg, s_old.astype(BF16)) * ea_full[:, g * width_g:(g + 1) * width_g]
        s_new = _dot_tn(bg, xt_b[:, g * width_g:(g + 1) * width_g])
        s_ref[:, g * width_g:(g + 1) * width_g] = decay_total[:, g * width_g:(g + 1) * width_g] * s_old + s_new
        yield
        for pair in range(heads_per_group // 2):
            col = g * width_g + pair * LANES
            xr_pair = xr_b[:, col:col + LANES]
            ys = []
            for r in range(2):
                hl = lane0 + g * heads_per_group + 2 * pair + r
                seg = jnp.broadcast_to(cum[:, hl:hl + 1], (q, q)) - jnp.broadcast_to(cum_t[hl:hl + 1, :], (q, q))
                dec = jnp.exp(jnp.where(keep, seg, -jnp.inf))
                ys.append(_dot((gmat * dec).astype(BF16), xr_pair))
            y_diag = jnp.where(lane < HEAD_DIM, ys[0], ys[1])
            pieces.append(y_diag + y_off[:, pair * LANES:(pair + 1) * LANES])
            yield
    return jnp.concatenate(pieces, axis=-1)


def _run_interleaved(*stage_generators):
    results = [None] * len(stage_generators)
    live = list(range(len(stage_generators)))
    while live:
        for idx in list(live):
            try:
                next(stage_generators[idx])
            except StopIteration as done:
                results[idx] = done.value
                live.remove(idx)
    return results


def _ssd_kernel(xf_ref, dtf_ref, xb_ref, dtb_ref, a_ref, tril_ref, triu_ref, ef_ref, eb_ref,
                yf_ref, yb_ref, sf_ref, sb_ref):
    @pl.when(pl.program_id(0) == 0)
    def _():
        sf_ref[...] = jnp.zeros_like(sf_ref)
        sb_ref[...] = jnp.zeros_like(sb_ref)

    a_row = a_ref[...]
    q = SSD_CHUNK
    order_f = list(range(SSD_CHUNKS_PER_STEP))
    order_b = order_f[::-1]
    stages = []
    for cf, cb in zip(order_f, order_b):
        rf, rb = slice(cf * q, (cf + 1) * q), slice(cb * q, (cb + 1) * q)
        stages.append(_ssd_direction(xf_ref[rf, :], dtf_ref[rf, :], a_row, tril_ref[...], ef_ref[...],
                                     sf_ref, 0, False))
        stages.append(_ssd_direction(xb_ref[rb, :], dtb_ref[rb, :], a_row, triu_ref[...], eb_ref[...],
                                     sb_ref, SSD_HEADS, True))
    ys = _run_interleaved(*stages)
    for k, (cf, cb) in enumerate(zip(order_f, order_b)):
        yf_ref[cf * q:(cf + 1) * q, :] = ys[2 * k]
        yb_ref[cb * q:(cb + 1) * q, :] = ys[2 * k + 1]


def _ssd_scan(xconv, dt, a_row, layer, n_ctx):
    t = xconv.shape[0]
    q = SSD_CHUNK
    rows = SSD_CHUNKS_PER_STEP * q
    assert n_ctx % rows == 0 and t % rows == 0
    n = t // rows
    n_ctx_blocks = n_ctx // rows
    r = jnp.arange(q)
    tril = (r[None, :] <= r[:, None]).astype(BF16)
    triu = (r[None, :] >= r[:, None]).astype(BF16)
    lanes = jnp.arange(LANES)[:, None]
    head_of_col = (jnp.arange(SSD_WIDTH) // HEAD_DIM)[None, :]
    expand_f = (lanes == head_of_col).astype(BF16)
    expand_b = (lanes == head_of_col + SSD_HEADS).astype(BF16)

    def fwd(i):
        return (i, 0)

    def bwd(i):
        return (jnp.where(i < n_ctx_blocks, n_ctx_blocks - 1 - i, n - 1 - (i - n_ctx_blocks)), 0)

    const = lambda i: (0, 0)
    return pl.pallas_call(
        _ssd_kernel,
        grid=(n,),
        in_specs=[
            pl.BlockSpec((rows, SSD_CONV_CH), fwd),
            pl.BlockSpec((rows, LANES), fwd),
            pl.BlockSpec((rows, SSD_CONV_CH), bwd),
            pl.BlockSpec((rows, LANES), bwd),
            _layer_spec(a_row, layer),
            pl.BlockSpec((q, q), const),
            pl.BlockSpec((q, q), const),
            pl.BlockSpec((LANES, SSD_WIDTH), const),
            pl.BlockSpec((LANES, SSD_WIDTH), const),
        ],
        out_specs=[pl.BlockSpec((rows, SSD_WIDTH), fwd), pl.BlockSpec((rows, SSD_WIDTH), bwd)],
        out_shape=[jax.ShapeDtypeStruct((t, SSD_WIDTH), F32)] * 2,
        scratch_shapes=[pltpu.VMEM((SSD_STATE, SSD_WIDTH), F32)] * 2,
        compiler_params=_params("arbitrary"),
        name="ssd_scan",
    )(xconv, dt, xconv, dt, a_row, tril, triu, expand_f, expand_b)


def _na_kernel(q_ref, k_ref, v_ref, bias_ref, o_ref, *, n_ctx, n_rows):
    i = pl.program_id(0)
    is_ctx = i == 0
    kc = k_ref[0:n_ctx, :]
    vc = v_ref[0:n_ctx, :]
    lane = lax.broadcasted_iota(jnp.int32, (GRID_W, NA_WIDTH), 1)
    mine = [(lane >= h * HEAD_DIM) & (lane < (h + 1) * HEAD_DIM) for h in range(NA_HEADS)]
    win = NA_WIN_ROWS * GRID_W

    def lane_tiles(a):
        return [a[:, c * LANES:(c + 1) * LANES] for c in range(a.shape[-1] // LANES)]

    def grid_row(j):
        r = jnp.maximum((i - 1) * NA_ROWS_PER_STEP + j, 0)
        r_start = jnp.clip(r - NA_WIN_ROWS // 2, 0, n_rows - NA_WIN_ROWS)
        variant = jnp.where(is_ctx, NA_WIN_ROWS, r_start - r + NA_WIN_ROWS - 1)
        start = pl.multiple_of(n_ctx + r_start * GRID_W, GRID_W)
        kw = k_ref[pl.ds(start, win), :]
        vw = v_ref[pl.ds(start, win), :]
        qj = q_ref[j * GRID_W:(j + 1) * GRID_W, :]
        qm = jnp.concatenate([jnp.where(mine[h], qj, jnp.zeros_like(qj)) for h in range(NA_HEADS)], axis=0)
        s_w = _dot_nt(qm, kw) + bias_ref[variant]
        s_c = _dot_nt(qm, kc)
        yield
        m = functools.reduce(jnp.maximum, lane_tiles(s_w) + lane_tiles(s_c))
        m = jnp.broadcast_to(jnp.max(m, axis=-1, keepdims=True), m.shape)
        p_w = jnp.exp(s_w - jnp.concatenate([m] * (s_w.shape[-1] // LANES), axis=-1))
        p_c = jnp.exp(s_c - jnp.concatenate([m] * (s_c.shape[-1] // LANES), axis=-1))
        l = jnp.sum(functools.reduce(jnp.add, lane_tiles(p_w) + lane_tiles(p_c)), axis=-1, keepdims=True)
        y = (_dot(p_w.astype(BF16), vw) + _dot(p_c.astype(BF16), vc)) * (1.0 / l)
        out = y[(NA_HEADS - 1) * GRID_W:]
        for h in range(NA_HEADS - 2, -1, -1):
            out = jnp.where(mine[h], y[h * GRID_W:(h + 1) * GRID_W], out)
        o_ref[j * GRID_W:(j + 1) * GRID_W, :] = out.astype(o_ref.dtype)

    _run_interleaved(*[grid_row(j) for j in range(NA_ROWS_PER_STEP)])


def _na_bias_table(rpb):
    depth = rpb.shape[0]
    col = np.arange(GRID_W)
    c_start = np.clip(col - NA_WIN_COLS // 2, 0, GRID_W - NA_WIN_COLS)
    in_win = (col[None, :] >= c_start[:, None]) & (col[None, :] < c_start[:, None] + NA_WIN_COLS)
    dc = np.clip(col[None, :] - col[:, None] + NA_WIN_COLS - 1, 0, 2 * NA_WIN_COLS - 2)
    n_dc = 2 * NA_WIN_COLS - 1
    n_dr = 2 * NA_WIN_ROWS - 1
    onehot = (dc[None, :, :] == np.arange(n_dc)[:, None, None]).astype(np.float32).reshape(n_dc, GRID_W * GRID_W)
    t2 = jnp.dot(rpb.reshape(depth * NA_HEADS * n_dr, n_dc).astype(F32), onehot, precision=lax.Precision.HIGHEST)
    t2 = jnp.where(in_win[None, None, None], t2.reshape(depth, NA_HEADS, n_dr, GRID_W, GRID_W), -jnp.inf)
    tab = jnp.stack([t2[:, :, v:v + NA_WIN_ROWS] for v in range(NA_WIN_ROWS)], axis=1)
    tab = tab.transpose(0, 1, 2, 4, 3, 5).reshape(depth, NA_WIN_ROWS, NA_HEADS * GRID_W, NA_WIN_ROWS * GRID_W)
    masked = jnp.full((depth, 1) + tab.shape[2:], -jnp.inf, F32)
    return jnp.concatenate([tab, masked], axis=1)


def _neighbourhood_attention(na, bias_tab, layer, n_ctx):
    t = na.shape[0]
    n_rows = (t - n_ctx) // GRID_W
    step_rows = NA_ROWS_PER_STEP * GRID_W
    n = t // step_rows
    return pl.pallas_call(
        functools.partial(_na_kernel, n_ctx=n_ctx, n_rows=n_rows),
        grid=(n,),
        in_specs=[
            pl.BlockSpec((step_rows, NA_WIDTH), lambda i: (i, 0)),
            _resident((t, NA_WIDTH), lambda i: (0, 1)),
            _resident((t, NA_WIDTH), lambda i: (0, 2)),
            _layer_spec(bias_tab, layer, resident=True),
        ],
        out_specs=pl.BlockSpec((step_rows, NA_WIDTH), lambda i: (i, 0)),
        out_shape=jax.ShapeDtypeStruct((t, NA_WIDTH), BF16),
        compiler_params=_params("parallel"),
        name="neighbourhood_attention",
    )(na, na, na, bias_tab)


def _gqa_kernel(qt_ref, k_ref, vt_ref, o_ref, acc_ref, s0_ref, s1_ref, p0_ref, p1_ref, *, n_ctx_sub, n_sub):
    tq = qt_ref.shape[1]
    qt = qt_ref[...]
    row = lax.broadcasted_iota(jnp.int32, qt.shape, 0)
    top = row < HEAD_DIM
    zero = jnp.zeros_like(qt)
    qt2 = jnp.concatenate([jnp.where(top, qt, zero), jnp.where(top, zero, qt)], axis=1)
    acc_ref[...] = jnp.zeros_like(acc_ref)
    s_ref = (s0_ref, s1_ref)
    p_ref = (p0_ref, p1_ref)

    def keys(j, size):
        return pl.ds(pl.multiple_of(j * size, LANES), size)

    def score(key_rows, slot):
        size = key_rows.size
        s = _dot(k_ref[key_rows, :], qt2)
        s_ref[slot][0:size, :] = s
        return jnp.max(s.reshape(size // SUBLANES, SUBLANES, 2 * tq), axis=0)

    def softmax(size, slot, m_old, part_max):
        m_new = jnp.maximum(m_old, jnp.max(part_max, axis=0, keepdims=True))
        s = s_ref[slot][0:size, :].reshape(size // SUBLANES, SUBLANES, 2 * tq)
        p_ref[slot][0:size, :] = jnp.exp2(s - m_new[None]).reshape(size, 2 * tq).astype(BF16)
        return m_new, jnp.exp2(m_old - m_new)

    def accumulate(key_rows, slot, alpha):
        size = key_rows.size
        acc = acc_ref[...].reshape(GQA_V_ROWS // SUBLANES, SUBLANES, 2 * tq) * alpha[None]
        acc_ref[...] = acc.reshape(GQA_V_ROWS, 2 * tq) + _dot(vt_ref[:, key_rows], p_ref[slot][0:size, :])

    m_init = jnp.full((SUBLANES, 2 * tq), -jnp.inf, F32)
    sub = GQA_K_SUB
    assert n_sub >= 3

    @pl.when(pl.program_id(1) == 0)
    def _():
        m = m_init
        for j in range(n_ctx_sub):
            m, alpha = softmax(sub, 0, m, score(keys(j, sub), 0))
            accumulate(keys(j, sub), 0, alpha)

    @pl.when(pl.program_id(1) > 0)
    def _():
        part0 = score(keys(0, sub), 0)
        part1 = score(keys(1, sub), 1)
        m, alpha = softmax(sub, 0, m_init, part0)

        def step(t, slot, carry):
            m, alpha, part = carry
            part_next = score(keys(t, sub), slot)
            accumulate(keys(t - 2, sub), slot, alpha)
            return softmax(sub, 1 - slot, m, part) + (part_next,)

        def trip(n, carry):
            for u in range(GQA_STEPS_PER_TRIP):
                carry = step(2 + n * GQA_STEPS_PER_TRIP + u, u % 2, carry)
            return carry

        n_trips = (n_sub - 2) // GQA_STEPS_PER_TRIP
        carry = lax.fori_loop(0, n_trips, trip, (m, alpha, part1))
        for t in range(2 + n_trips * GQA_STEPS_PER_TRIP, n_sub):
            carry = step(t, t % 2, carry)
        m, alpha, part = carry
        accumulate(keys(n_sub - 2, sub), (n_sub - 2) % 2, alpha)
        m, alpha = softmax(sub, (n_sub - 1) % 2, m, part)
        accumulate(keys(n_sub - 1, sub), (n_sub - 1) % 2, alpha)

    acc = acc_ref[...]
    denom = acc[HEAD_DIM:HEAD_DIM + SUBLANES]
    o_t = (acc[:HEAD_DIM].reshape(HEAD_DIM // SUBLANES, SUBLANES, 2 * tq) / denom[None]).reshape(HEAD_DIM, 2 * tq)
    o_ref[...] = jnp.concatenate([o_t[:, :tq], o_t[:, tq:]], axis=0).T.astype(o_ref.dtype)


def _gqa_bounded_kernel(qt_ref, k_ref, vt_ref, o_ref, acc_ref, *p_ref, n_ctx_sub, n_sub):
    tq = qt_ref.shape[1]
    qt = qt_ref[...]
    row = lax.broadcasted_iota(jnp.int32, qt.shape, 0)
    top = row < HEAD_DIM
    zero = jnp.zeros_like(qt)
    qt2 = jnp.concatenate([jnp.where(top, qt, zero), jnp.where(top, zero, qt)], axis=1)
    acc_ref[...] = jnp.zeros_like(acc_ref)
    sub = GQA_K_SUB
    n_slots = len(p_ref)
    lag = n_slots - 1
    steps = GQA_BOUNDED_STEPS_PER_TRIP
    assert steps % n_slots == 0 and n_sub > lag

    def keys(j):
        return pl.ds(pl.multiple_of(j * sub, sub), sub)

    def probs(j, slot):
        p_ref[slot][...] = jnp.exp2(_dot(k_ref[keys(j), :], qt2)).astype(BF16)

    def accumulate(j, slot):
        acc_ref[...] += _dot(vt_ref[:, keys(j)], p_ref[slot][...])

    @pl.when(pl.program_id(1) == 0)
    def _():
        for j in range(n_ctx_sub):
            probs(j, 0)
            accumulate(j, 0)

    @pl.when(pl.program_id(1) > 0)
    def _():
        for t in range(lag):
            probs(t, t % n_slots)

        def trip(n, carry):
            for u in range(steps):
                t = lag + n * steps + u
                probs(t, (lag + u) % n_slots)
                accumulate(t - lag, u % n_slots)
            return carry

        n_trips = (n_sub - lag) // steps
        lax.fori_loop(0, n_trips, trip, 0)
        for t in range(lag + n_trips * steps, n_sub):
            probs(t, t % n_slots)
            accumulate(t - lag, (t - lag) % n_slots)
        for t in range(n_sub - lag, n_sub):
            accumulate(t, t % n_slots)

    acc = acc_ref[...]
    denom = acc[HEAD_DIM:HEAD_DIM + SUBLANES]
    o_t = (acc[:HEAD_DIM].reshape(HEAD_DIM // SUBLANES, SUBLANES, 2 * tq) / denom[None]).reshape(HEAD_DIM, 2 * tq)
    o_ref[...] = jnp.concatenate([o_t[:, :tq], o_t[:, tq:]], axis=0).T.astype(o_ref.dtype)


def _gqa_dispatch_kernel(bounded_ref, qt_ref, k_ref, vt_ref, o_ref, acc_ref, s0_ref, s1_ref, *p_ref,
                         n_ctx_sub, n_sub):
    @pl.when(bounded_ref[0] != 0)
    def _():
        _gqa_bounded_kernel(qt_ref, k_ref, vt_ref, o_ref, acc_ref, *p_ref, n_ctx_sub=n_ctx_sub, n_sub=n_sub)

    @pl.when(bounded_ref[0] == 0)
    def _():
        _gqa_kernel(qt_ref, k_ref, vt_ref, o_ref, acc_ref, s0_ref, s1_ref, p_ref[0], p_ref[1],
                    n_ctx_sub=n_ctx_sub, n_sub=n_sub)


def _gqa_attention(gqt, gk, gvt, n_ctx, score_bound):
    nq = gqt.shape[1]
    n_keys = gk.shape[0]
    tq = GQA_Q_TILE
    assert n_ctx == tq and n_ctx % GQA_K_SUB == 0 and n_keys % GQA_K_SUB == 0 and nq % tq == 0
    bounded = (score_bound <= GQA_BOUNDED_LOG2).astype(jnp.int32).reshape(1)
    return pl.pallas_call(
        functools.partial(_gqa_dispatch_kernel, n_ctx_sub=n_ctx // GQA_K_SUB, n_sub=n_keys // GQA_K_SUB),
        grid=(GQA_KV_HEADS, nq // tq),
        in_specs=[
            pl.BlockSpec(memory_space=pltpu.SMEM),
            pl.BlockSpec((LANES, tq), lambda g, i: (g, i)),
            pl.BlockSpec((n_keys, LANES), lambda g, i: (0, g)),
            pl.BlockSpec((GQA_V_ROWS, n_keys), lambda g, i: (g, 0)),
        ],
        out_specs=pl.BlockSpec((tq, LANES), lambda g, i: (i, g)),
        out_shape=jax.ShapeDtypeStruct((nq, GQA_WIDTH), BF16),
        scratch_shapes=[pltpu.VMEM((GQA_V_ROWS, 2 * tq), F32)] + [pltpu.VMEM((GQA_K_SUB, 2 * tq), F32)] * 2
        + [pltpu.VMEM((GQA_K_SUB, 2 * tq), BF16)] * (GQA_PV_LAG + 1),
        compiler_params=_params("arbitrary", "arbitrary"),
        name="gqa_attention",
    )(bounded, gqt, gk, gvt)


def _out_ffn_kernel(ya_ref, yf_ref, yb_ref, xs_ref, z_ref, yg_ref, xc_ref, xl_ref, mod_ref,
                    dskip_ref, snw_ref, wo_ref, fnw_ref, wg_ref, wu_ref, wd_ref, final_ref,
                    o_ref, *, final, pick_ctx):
    x = jnp.where(pl.program_id(0) == 0, xc_ref[...], xl_ref[...]) if pick_ctx else xl_ref[...]
    y = yf_ref[...] + yb_ref[...] + dskip_ref[...] * xs_ref[...]
    y = y * _silu(z_ref[...])
    ms = jnp.mean(y * y, axis=-1, keepdims=True)
    y = y * lax.rsqrt(ms + EPS) * snw_ref[...]
    mix = jnp.concatenate([ya_ref[...], y.astype(BF16), yg_ref[...]], axis=-1)
    g_m = mod_ref[:, 2 * D_MODEL:3 * D_MODEL]
    sh_f = mod_ref[:, 3 * D_MODEL:4 * D_MODEL]
    sc_f = mod_ref[:, 4 * D_MODEL:5 * D_MODEL]
    g_f = mod_ref[:, 5 * D_MODEL:6 * D_MODEL]
    x1 = x + g_m * _dot(mix, wo_ref[...])
    ms1 = jnp.mean(x1 * x1, axis=-1, keepdims=True)
    hf = (x1 * lax.rsqrt(ms1 + EPS) * fnw_ref[...] * (1.0 + sc_f) + sh_f).astype(BF16)
    act = (_silu(_dot(hf, wg_ref[...])) * _dot(hf, wu_ref[...])).astype(BF16)
    x2 = x1 + g_f * _dot(act, wd_ref[...])
    if final:
        ms2 = jnp.mean(x2 * x2, axis=-1, keepdims=True)
        x2 = x2 * lax.rsqrt(ms2 + EPS) * final_ref[...]
    o_ref[...] = x2


def _out_ffn(ya, yf, yb, xconv, z, yg, x_parts, mod4, layer, d_full, ssd_nw, w_out, ffn_nw,
             w_gate, w_up, w_down, final_nw, final):
    t = ya.shape[0]
    skip = 1 if final else 0
    n = t // ROW_TILE - skip
    row = lambda i: (i + skip, 0)
    const = lambda i: (0, 0)
    x_arrays, x_specs, pick_ctx = _residual_operands(x_parts, skip)
    return pl.pallas_call(
        functools.partial(_out_ffn_kernel, final=final, pick_ctx=pick_ctx),
        grid=(n,),
        in_specs=[
            pl.BlockSpec((ROW_TILE, NA_WIDTH), row),
            pl.BlockSpec((ROW_TILE, SSD_WIDTH), row),
            pl.BlockSpec((ROW_TILE, SSD_WIDTH), row),
            pl.BlockSpec((ROW_TILE, SSD_WIDTH), row),
            pl.BlockSpec((ROW_TILE, SSD_WIDTH), row),
            pl.BlockSpec((ROW_TILE, GQA_WIDTH), row),
        ] + x_specs + [
            pl.BlockSpec((None, None, 1, 6 * D_MODEL), lambda i: (layer, jnp.minimum(i + skip, 1), 0, 0)),
            _layer_spec(d_full, layer),
            _layer_spec(ssd_nw, layer),
            _layer_spec(w_out, layer, resident=True),
            _layer_spec(ffn_nw, layer),
            _layer_spec(w_gate, layer, resident=True),
            _layer_spec(w_up, layer, resident=True),
            _layer_spec(w_down, layer, resident=True),
            pl.BlockSpec((1, D_MODEL), const),
        ],
        out_specs=pl.BlockSpec((ROW_TILE, D_MODEL), lambda i: (i, 0)),
        out_shape=jax.ShapeDtypeStruct((n * ROW_TILE, D_MODEL), F32),
        compiler_params=_params("parallel"),
        name="out_ffn",
    )(ya, yf, yb, xconv, z, yg, *x_arrays, mod4, d_full, ssd_nw, w_out, ffn_nw, w_gate, w_up, w_down, final_nw)


def _rearranged_w_in(w):
    na_in = 3 * NA_WIDTH
    o_z = na_in
    o_xbc = o_z + SSD_WIDTH
    o_dt = o_xbc + SSD_CONV_CH
    o_gq = o_dt + 2 * SSD_HEADS
    o_gk = o_gq + GQA_WIDTH
    o_gv = o_gk + GQA_KV_HEADS * HEAD_DIM
    w = w.astype(BF16)
    q_na = w[..., :NA_WIDTH] * ATTN_SCALE
    dt_pad = jnp.zeros(w.shape[:-1] + (LANES - 2 * SSD_HEADS,), BF16)
    k_heads = [w[..., o_gk + h * HEAD_DIM:o_gk + (h + 1) * HEAD_DIM] for h in range(GQA_KV_HEADS)]
    return jnp.concatenate(
        [q_na, w[..., NA_WIDTH:o_dt], w[..., o_dt:o_gq], dt_pad, w[..., o_gq:o_gk]]
        + [p for h in k_heads for p in (h, h)] + [w[..., o_gv:]], axis=-1)


def _rope_tables(n_ctx, n_lat):
    f32 = np.float32
    freqs = f32(ROPE_THETA) ** (-np.arange(ROPE_PAIRS, dtype=f32) / f32(ROPE_PAIRS))
    n_rows = n_lat // GRID_W
    half = 2 * ROPE_PAIRS
    sign = np.where(np.arange(half) < ROPE_PAIRS, -1.0, 1.0).astype(f32)

    def tables(n_pos):
        a = np.arange(n_pos, dtype=f32)[:, None] * freqs[None, :]
        a = np.concatenate([a, a], axis=-1)
        return np.cos(a), np.sin(a) * sign[None, :]

    def per_token(by_row, by_col):
        lat = np.concatenate([np.broadcast_to(by_row[:, None, :], (n_rows, GRID_W, half)),
                              np.broadcast_to(by_col[None, :, :], (n_rows, GRID_W, half))], axis=-1)
        return lat.reshape(n_lat, HEAD_DIM)

    cos_r, sin_r = tables(n_rows)
    cos_c, sin_c = tables(GRID_W)
    cos = np.concatenate([np.ones((n_ctx, HEAD_DIM), f32), per_token(cos_r, cos_c)], axis=0)
    sin = np.concatenate([np.zeros((n_ctx, HEAD_DIM), f32), per_token(sin_r, sin_c)], axis=0)
    reps = (1, LANES // HEAD_DIM)
    return jnp.asarray(np.tile(cos, reps), F32), jnp.asarray(np.tile(sin, reps), F32)


def kernel(x, c, ctx, c_ctx, mod_w, mod_b, norm_attn_w, norm_ffn_w, w_in, na_rpb, ssd_conv_w, ssd_conv_b,
           ssd_dt_bias, ssd_a_log, ssd_d, ssd_norm_w, q_norm_w, k_norm_w, w_out, ffn_w_gate, ffn_w_up,
           ffn_w_down, final_norm_w):
    depth = mod_w.shape[0]
    batch, n_lat, _ = x.shape
    n_ctx = ctx.shape[1]
    assert batch == 1 and n_ctx == ROW_TILE and n_lat % (NA_ROWS_PER_STEP * GRID_W) == 0
    assert n_lat % GQA_Q_TILE == 0 and n_lat // GRID_W >= NA_WIN_ROWS

    x_parts = (ctx[0], x[0])
    cc = jnp.zeros((SUBLANES, D_MODEL), F32).at[0].set(c_ctx).at[1].set(c[0])
    mod = _modulation(cc, mod_w, mod_b)
    mod4 = mod[:, :2].reshape(depth, 2, 1, 6 * D_MODEL)

    cos_t, sin_t = _rope_tables(n_ctx, n_lat)
    blk = np.arange(2 * GQA_WIDTH) // HEAD_DIM
    ones_bd = jnp.asarray(blk[:, None] == blk[None, :], BF16)
    ones_rows = jnp.ones((GQA_V_ROWS - HEAD_DIM, n_ctx + n_lat), BF16)

    row = lambda p: p.astype(F32).reshape(depth, 1, -1)
    pad_lanes = lambda p: jnp.pad(row(p), ((0, 0), (0, 0), (0, LANES - 2 * SSD_HEADS)))
    w_cat = _rearranged_w_in(w_in)
    dt_bias_pad = pad_lanes(ssd_dt_bias)
    a_row = pad_lanes(-jnp.exp(ssd_a_log.astype(F32)))
    qk_w = row(jnp.concatenate([jnp.tile(q_norm_w * (ATTN_SCALE * LOG2E), (1, GQA_Q_HEADS)),
                                jnp.tile(k_norm_w, (1, 2 * GQA_KV_HEADS))], axis=-1))
    conv_w_pad = jnp.pad(ssd_conv_w, ((0, 0), (0, SUBLANES - SSD_CONV), (0, 0)))
    d_full = row(jnp.repeat(ssd_d, HEAD_DIM, axis=-1))
    bias_tab = _na_bias_table(na_rpb)
    w_out_b, w_gate_b, w_up_b, w_down_b = (w.astype(BF16) for w in (w_out, ffn_w_gate, ffn_w_up, ffn_w_down))
    score_bound = (1.02 * HEAD_DIM * ATTN_SCALE * LOG2E) * jnp.max(jnp.abs(q_norm_w), axis=-1) * jnp.max(
        jnp.abs(k_norm_w), axis=-1)

    for i in range(depth):
        final = i == depth - 1
        na, z, xconv, dt, gq, gk, gv = _inproj(x_parts, n_ctx + n_lat, mod4, i, row(norm_attn_w), w_cat,
                                               dt_bias_pad, qk_w, cos_t, sin_t, ones_bd, conv_w_pad,
                                               row(ssd_conv_b))
        yf, yb = _ssd_scan(xconv, dt, a_row, i, n_ctx)
        ya = _neighbourhood_attention(na, bias_tab, i, n_ctx)
        gvt = jnp.concatenate([part for g in range(GQA_KV_HEADS)
                               for part in (gv[:, g * HEAD_DIM:(g + 1) * HEAD_DIM].T, ones_rows)], axis=0)
        yg = _gqa_attention(gq.T, gk, gvt, n_ctx, score_bound[i])
        x_parts = _out_ffn(ya, yf, yb, xconv, z, yg, x_parts, mod4, i, d_full, row(ssd_norm_w), w_out_b,
                           row(norm_ffn_w), w_gate_b, w_up_b, w_down_b, final_norm_w.reshape(1, -1), final)
    return x_parts[None]
```

```python
import functools
import math

import jax
import jax.numpy as jnp
import numpy as np
from jax import lax
from jax.experimental import pallas as pl
from jax.experimental.pallas import tpu as pltpu

F32 = jnp.float32
BF16 = jnp.bfloat16

D_MODEL = 1024
GRID_W = 64
HEAD_DIM = 64
NA_WIDTH = 256
NA_HEADS = 4
NA_WIN_ROWS = 8
NA_WIN_COLS = 16
SSD_WIDTH = 512
SSD_HEADS = 8
SSD_GROUPS = 2
SSD_STATE = 128
SSD_CONV = 5
SSD_CHUNK = 128
SSD_CONV_CH = SSD_WIDTH + 2 * SSD_GROUPS * SSD_STATE
GQA_WIDTH = 256
GQA_Q_HEADS = 4
GQA_KV_HEADS = 2
ROPE_THETA = 10000.0
ROPE_PAIRS = HEAD_DIM // 4
FFN_HIDDEN = 2816
EPS = 1e-6
ATTN_SCALE = HEAD_DIM ** -0.5
LOG2E = math.log2(math.e)

LANES = 128
SUBLANES = 8
VMEM_LIMIT_BYTES = 56 * 1024 * 1024

ROW_TILE = 256
MOD_COL_TILE = 2048
SSD_CHUNKS_PER_STEP = 2
NA_ROWS_PER_STEP = 4
GQA_Q_TILE = 256
GQA_V_ROWS = HEAD_DIM + 16
GQA_K_SUB = 256
GQA_BOUNDED_LOG2 = 60.0
GQA_PV_LAG = 7
GQA_BOUNDED_STEPS_PER_TRIP = 8
GQA_STEPS_PER_TRIP = 8

C_NA = 0
C_Z = C_NA + 3 * NA_WIDTH
C_XBC = C_Z + SSD_WIDTH
C_DT = C_XBC + SSD_CONV_CH
C_GQ = C_DT + LANES
C_GK = C_GQ + GQA_WIDTH
C_GV = C_GK + 2 * LANES
C_END = C_GV + LANES


def _silu(v):
    return v * (1.0 / (1.0 + jnp.exp(-v)))


def _softplus(v):
    return jnp.maximum(v, 0.0) + jnp.log(1.0 + jnp.exp(-jnp.abs(v)))


def _split3(v):
    hi = v.astype(BF16)
    r1 = v - hi.astype(F32)
    mid = r1.astype(BF16)
    lo = (r1 - mid.astype(F32)).astype(BF16)
    return hi, mid, lo


def _dot(a, b):
    return jnp.dot(a, b, preferred_element_type=F32)


def _dot_nt(a, b):
    return lax.dot_general(a, b, (((1,), (1,)), ((), ())), preferred_element_type=F32)


def _dot_tn(a, b):
    return lax.dot_general(a, b, (((0,), (0,)), ((), ())), preferred_element_type=F32)


def _exact_dot(v, sel):
    hi, mid, lo = _split3(v)
    return _dot(hi, sel) + _dot(mid, sel) + _dot(lo, sel)


def _spread_dot(v, sel):
    hi = v.astype(BF16)
    lo = (v - hi.astype(F32)).astype(BF16)
    return _dot(hi, sel) + _dot(lo, sel)


def _exact_dot_lhs(sel, v):
    hi, mid, lo = _split3(v)
    return _dot(sel, hi) + _dot(sel, mid) + _dot(sel, lo)


def _params(*sem):
    return pltpu.CompilerParams(dimension_semantics=sem, vmem_limit_bytes=VMEM_LIMIT_BYTES)


def _resident(shape, index_map):
    return pl.BlockSpec(shape, index_map, pipeline_mode=pl.Buffered(1))


def _layer_spec(stacked, layer, resident=False):
    shape = stacked.shape[1:]
    index_map = lambda *_: (layer,) + (0,) * len(shape)
    if resident:
        return pl.BlockSpec((None,) + shape, index_map, pipeline_mode=pl.Buffered(1))
    return pl.BlockSpec((None,) + shape, index_map)


def _mod_kernel(cc_ref, w_ref, b_ref, o_ref):
    a = _silu(cc_ref[...])
    o_ref[0] = jnp.dot(a, w_ref[0], preferred_element_type=F32) + b_ref[0]


def _modulation(cc, mod_w, mod_b):
    depth = mod_w.shape[0]
    cols = MOD_COL_TILE
    ncol = mod_w.shape[2] // cols
    return pl.pallas_call(
        _mod_kernel,
        grid=(depth, ncol),
        in_specs=[
            pl.BlockSpec((SUBLANES, D_MODEL), lambda l, j: (0, 0)),
            pl.BlockSpec((1, D_MODEL, cols), lambda l, j: (l, 0, j)),
            pl.BlockSpec((1, 1, cols), lambda l, j: (l, 0, j)),
        ],
        out_specs=pl.BlockSpec((1, SUBLANES, cols), lambda l, j: (l, 0, j)),
        out_shape=jax.ShapeDtypeStruct((depth, SUBLANES, ncol * cols), F32),
        compiler_params=_params("arbitrary", "arbitrary"),
        name="modulation",
    )(cc, mod_w, mod_b.reshape(depth, 1, -1))


def _residual_operands(x_parts, skip):
    ctx_spec = pl.BlockSpec((ROW_TILE, D_MODEL), lambda i: (0, 0))
    if isinstance(x_parts, tuple):
        lat_spec = pl.BlockSpec((ROW_TILE, D_MODEL), lambda i: (jnp.maximum(i + skip - 1, 0), 0))
        return list(x_parts), [ctx_spec, lat_spec], skip == 0
    row_spec = pl.BlockSpec((ROW_TILE, D_MODEL), lambda i: (i + skip, 0))
    return [x_parts, x_parts], [ctx_spec, row_spec], False


def _inproj_kernel(xc_ref, xl_ref, prev_ref, next_ref, mod_ref, nw_ref, w_ref, dtb_ref, qkw_ref, cos_ref, sin_ref,
                   ones_ref, cw_ref, cb_ref, na_ref, z_ref, xconv_ref, dt_ref, gq_ref, gk_ref, gv_ref, *, pick_ctx):
    i = pl.program_id(0)
    n = pl.num_programs(0)
    rows = xl_ref.shape[0]
    x = jnp.where(i == 0, xc_ref[...], xl_ref[...]) if pick_ctx else xl_ref[...]
    x = jnp.concatenate([prev_ref[...], x, next_ref[...]], axis=0)
    ms = jnp.mean(x * x, axis=-1, keepdims=True)
    xn = x * lax.rsqrt(ms + EPS) * nw_ref[...]
    sh = mod_ref[:, 0:D_MODEL]
    sc = mod_ref[:, D_MODEL:2 * D_MODEL]
    h = (xn * (1.0 + sc) + sh).astype(BF16)
    xbc_ext = _dot(h, w_ref[:, C_XBC:C_DT])
    h_tile = h[SUBLANES:SUBLANES + rows]
    u = jnp.concatenate([_dot(h_tile, w_ref[:, C_NA:C_XBC]), xbc_ext[SUBLANES:SUBLANES + rows],
                         _dot(h_tile, w_ref[:, C_DT:C_END])], axis=-1)
    na_ref[...] = u[:, C_NA:C_Z].astype(BF16)
    z_ref[...] = u[:, C_Z:C_XBC]
    has_prev = i >= 2
    has_next = jnp.logical_and(i >= 1, i < n - 1)
    xbc = u[:, C_XBC:C_DT]
    ext = jnp.concatenate([jnp.where(has_prev, xbc_ext[0:SUBLANES], 0.0), xbc,
                           jnp.where(has_next, xbc_ext[SUBLANES + rows:], 0.0)], axis=0)
    total = rows + 2 * SUBLANES
    half = SSD_CONV // 2
    acc = cb_ref[...] + cw_ref[half:half + 1, :] * xbc
    for j in range(SSD_CONV):
        if j != half:
            shifted = pltpu.roll(ext, (half - j) % total, 0)
            acc = acc + cw_ref[j:j + 1, :] * shifted[SUBLANES:SUBLANES + rows]
    xconv_ref[...] = _silu(acc)
    dt_ref[...] = _softplus(u[:, C_DT:C_GQ] + dtb_ref[...])
    gv_ref[...] = u[:, C_GV:C_END].astype(BF16)
    g = u[:, C_GQ:C_GV]
    gsq = g * g
    hi = gsq.astype(BF16)
    lo = (gsq - hi.astype(F32)).astype(BF16)
    ss = _dot(hi, ones_ref[...]) + _dot(lo, ones_ref[...])
    gn = g * lax.rsqrt(ss * (1.0 / HEAD_DIM) + EPS) * qkw_ref[...]
    width = gn.shape[-1]
    lane = lax.broadcasted_iota(jnp.int32, gn.shape, 1)
    first = (lane % (2 * ROPE_PAIRS)) < ROPE_PAIRS
    partner = jnp.where(first, pltpu.roll(gn, width - ROPE_PAIRS, 1), pltpu.roll(gn, ROPE_PAIRS, 1))
    cos = jnp.concatenate([cos_ref[...]] * (width // LANES), axis=-1)
    sin = jnp.concatenate([sin_ref[...]] * (width // LANES), axis=-1)
    gr = gn * cos + partner * sin
    gq_ref[...] = gr[:, :GQA_WIDTH].astype(BF16)
    gk_ref[...] = gr[:, GQA_WIDTH:].astype(BF16)


def _inproj(x_parts, t, mod4, layer, norm_w, w_cat, dt_bias_pad, qk_w, cos_t, sin_t, ones_bd, conv_w_pad, conv_b):
    n = t // ROW_TILE
    row = lambda i: (i, 0)
    const = lambda i: (0, 0)
    outs = [
        (3 * NA_WIDTH, BF16), (SSD_WIDTH, F32), (SSD_CONV_CH, F32), (LANES, F32),
        (GQA_WIDTH, BF16), (2 * LANES, BF16), (LANES, BF16),
    ]
    x_arrays, x_specs, pick_ctx = _residual_operands(x_parts, 0)
    halo_src = x_arrays[1]
    per = ROW_TILE // SUBLANES
    first = (lambda i: (i - 1) * per) if isinstance(x_parts, tuple) else (lambda i: i * per)
    last_blk = halo_src.shape[0] // SUBLANES - 1
    halo_specs = [
        pl.BlockSpec((SUBLANES, D_MODEL), lambda i: (jnp.clip(first(i) - 1, 0, last_blk), 0)),
        pl.BlockSpec((SUBLANES, D_MODEL), lambda i: (jnp.clip(first(i) + per, 0, last_blk), 0)),
    ]
    return pl.pallas_call(
        functools.partial(_inproj_kernel, pick_ctx=pick_ctx),
        grid=(n,),
        in_specs=x_specs + halo_specs + [
            pl.BlockSpec((None, None, 1, 6 * D_MODEL), lambda i: (layer, jnp.minimum(i, 1), 0, 0)),
            _layer_spec(norm_w, layer),
            _layer_spec(w_cat, layer, resident=True),
            _layer_spec(dt_bias_pad, layer),
            _layer_spec(qk_w, layer),
            pl.BlockSpec((ROW_TILE, LANES), row),
            pl.BlockSpec((ROW_TILE, LANES), row),
            _resident((2 * GQA_WIDTH, 2 * GQA_WIDTH), const),
            _layer_spec(conv_w_pad, layer),
            _layer_spec(conv_b, layer),
        ],
        out_specs=[pl.BlockSpec((ROW_TILE, w), row) for w, _ in outs],
        out_shape=[jax.ShapeDtypeStruct((t, w), d) for w, d in outs],
        compiler_params=_params("parallel"),
        name="inproj",
    )(*x_arrays, halo_src, halo_src, mod4, norm_w, w_cat, dt_bias_pad, qk_w, cos_t, sin_t, ones_bd,
      conv_w_pad, conv_b)


def _ssd_direction(xbc, dt, a_row, tri, expand, s_ref, lane0, backward):
    q = SSD_CHUNK
    x = xbc[:, :SSD_WIDTH]
    nb = SSD_GROUPS * SSD_STATE
    bmat = xbc[:, SSD_WIDTH:SSD_WIDTH + nb].astype(BF16)
    cmat = xbc[:, SSD_WIDTH + nb:].astype(BF16)
    cum = _exact_dot_lhs(tri, dt * a_row)
    yield
    last = 0 if backward else q - 1
    cum_t = cum.T

    def spread(mat_t):
        rows = [jnp.broadcast_to(mat_t[lane0 + h:lane0 + h + 1, :], (HEAD_DIM, q)) for h in range(SSD_HEADS)]
        return jnp.concatenate(rows, axis=0).T

    dt_full = _spread_dot(dt, expand)
    cum_full = spread(cum_t)
    yield
    ea_full = jnp.exp(cum_full)
    te_full = jnp.exp(cum_full[last:last + 1, :] - cum_full)
    xr = x * dt_full
    xr_b = xr.astype(BF16)
    xt_b = (xr * te_full).astype(BF16)
    ti =lax.broadcasted_iota(jnp.int32, (q, q), 0)
    si = lax.broadcasted_iota(jnp.int32, (q, q), 1)
    keep = (si >= ti) if backward else (si <= ti)
    lane = lax.broadcasted_iota(jnp.int32, (q, LANES), 1)
    heads_per_group = SSD_HEADS // SSD_GROUPS
    width_g = heads_per_group * HEAD_DIM
    pieces = []
    decay_total = ea_full[last:last + 1, :]
    for g in range(SSD_GROUPS):
        bg = bmat[:, g * SSD_STATE:(g + 1) * SSD_STATE]
        cg = cmat[:, g * SSD_STATE:(g + 1) * SSD_STATE]
        gmat = _dot_nt(cg, bg)
        s_old = s_ref[:, g * width_g:(g + 1) * width_g]
        y_off = _dot(cg, s_old.astype(BF16)) * ea_full[:, g * width_g:(g + 1) * width_g]
        s_new = _dot_tn(bg, xt_b[:, g * width_g:(g + 1) * width_g])
        s_ref[:, g * width_g:(g + 1) * width_g] = decay_total[:, g * width_g:(g + 1) * width_g] * s_old + s_new
        yield
        for pair in range(heads_per_group // 2):
            col = g * width_g + pair * LANES
            xr_pair = xr_b[:, col:col + LANES]
            ys = []
            for r in range(2):
                hl = lane0 + g * heads_per_group + 2 * pair + r
                seg = jnp.broadcast_to(cum[:, hl:hl + 1], (q, q)) - jnp.broadcast_to(cum_t[hl:hl + 1, :], (q, q))
                dec = jnp.exp(jnp.where(keep, seg, -jnp.inf))
                ys.append(_dot((gmat * dec).astype(BF16), xr_pair))
            y_diag = jnp.where(lane < HEAD_DIM, ys[0], ys[1])
            pieces.append(y_diag + y_off[:, pair * LANES:(pair + 1) * LANES])
            yield
    return jnp.concatenate(pieces, axis=-1)


def _run_interleaved(*stage_generators):
    results = [None] * len(stage_generators)
    live = list(range(len(stage_generators)))
    while live:
        for idx in list(live):
            try:
                next(stage_generators[idx])
            except StopIteration as done:
                results[idx] = done.value
                live.remove(idx)
    return results


def _ssd_kernel(xf_ref, dtf_ref, xb_ref, dtb_ref, a_ref, tril_ref, triu_ref, ef_ref, eb_ref,
                yf_ref, yb_ref, sf_ref, sb_ref):
    @pl.when(pl.program_id(0) == 0)
    def _():
        sf_ref[...] = jnp.zeros_like(sf_ref)
        sb_ref[...] = jnp.zeros_like(sb_ref)

    a_row = a_ref[...]
    q = SSD_CHUNK
    order_f = list(range(SSD_CHUNKS_PER_STEP))
    order_b = order_f[::-1]
    stages = []
    for cf, cb in zip(order_f, order_b):
        rf, rb = slice(cf * q, (cf + 1) * q), slice(cb * q, (cb + 1) * q)
        stages.append(_ssd_direction(xf_ref[rf, :], dtf_ref[rf, :], a_row, tril_ref[...], ef_ref[...],
                                     sf_ref, 0, False))
        stages.append(_ssd_direction(xb_ref[rb, :], dtb_ref[rb, :], a_row, triu_ref[...], eb_ref[...],
                                     sb_ref, SSD_HEADS, True))
    ys = _run_interleaved(*stages)
    for k, (cf, cb) in enumerate(zip(order_f, order_b)):
        yf_ref[cf * q:(cf + 1) * q, :] = ys[2 * k]
        yb_ref[cb * q:(cb + 1) * q, :] = ys[2 * k + 1]


def _ssd_scan(xconv, dt, a_row, layer, n_ctx):
    t = xconv.shape[0]
    q = SSD_CHUNK
    rows = SSD_CHUNKS_PER_STEP * q
    assert n_ctx % rows == 0 and t % rows == 0
    n = t // rows
    n_ctx_blocks = n_ctx // rows
    r = jnp.arange(q)
    tril = (r[None, :] <= r[:, None]).astype(BF16)
    triu = (r[None, :] >= r[:, None]).astype(BF16)
    lanes = jnp.arange(LANES)[:, None]
    head_of_col = (jnp.arange(SSD_WIDTH) // HEAD_DIM)[None, :]
    expand_f = (lanes == head_of_col).astype(BF16)
    expand_b = (lanes == head_of_col + SSD_HEADS).astype(BF16)

    def fwd(i):
        return (i, 0)

    def bwd(i):
        return (jnp.where(i < n_ctx_blocks, n_ctx_blocks - 1 - i, n - 1 - (i - n_ctx_blocks)), 0)

    const = lambda i: (0, 0)
    return pl.pallas_call(
        _ssd_kernel,
        grid=(n,),
        in_specs=[
            pl.BlockSpec((rows, SSD_CONV_CH), fwd),
            pl.BlockSpec((rows, LANES), fwd),
            pl.BlockSpec((rows, SSD_CONV_CH), bwd),
            pl.BlockSpec((rows, LANES), bwd),
            _layer_spec(a_row, layer),
            pl.BlockSpec((q, q), const),
            pl.BlockSpec((q, q), const),
            pl.BlockSpec((LANES, SSD_WIDTH), const),
            pl.BlockSpec((LANES, SSD_WIDTH), const),
        ],
        out_specs=[pl.BlockSpec((rows, SSD_WIDTH), fwd), pl.BlockSpec((rows, SSD_WIDTH), bwd)],
        out_shape=[jax.ShapeDtypeStruct((t, SSD_WIDTH), F32)] * 2,
        scratch_shapes=[pltpu.VMEM((SSD_STATE, SSD_WIDTH), F32)] * 2,
        compiler_params=_params("arbitrary"),
        name="ssd_scan",
    )(xconv, dt, xconv, dt, a_row, tril, triu, expand_f, expand_b)


def _na_kernel(q_ref, k_ref, v_ref, bias_ref, o_ref, *, n_ctx, n_rows):
    i = pl.program_id(0)
    is_ctx = i == 0
    kc = k_ref[0:n_ctx, :]
    vc = v_ref[0:n_ctx, :]
    lane = lax.broadcasted_iota(jnp.int32, (GRID_W, NA_WIDTH), 1)
    mine = [(lane >= h * HEAD_DIM) & (lane < (h + 1) * HEAD_DIM) for h in range(NA_HEADS)]
    win = NA_WIN_ROWS * GRID_W

    def lane_tiles(a):
        return [a[:, c * LANES:(c + 1) * LANES] for c in range(a.shape[-1] // LANES)]

    def grid_row(j):
        r = jnp.maximum((i - 1) * NA_ROWS_PER_STEP + j, 0)
        r_start = jnp.clip(r - NA_WIN_ROWS // 2, 0, n_rows - NA_WIN_ROWS)
        variant = jnp.where(is_ctx, NA_WIN_ROWS, r_start - r + NA_WIN_ROWS - 1)
        start = pl.multiple_of(n_ctx + r_start * GRID_W, GRID_W)
        kw = k_ref[pl.ds(start, win), :]
        vw = v_ref[pl.ds(start, win), :]
        qj = q_ref[j * GRID_W:(j + 1) * GRID_W, :]
        qm = jnp.concatenate([jnp.where(mine[h], qj, jnp.zeros_like(qj)) for h in range(NA_HEADS)], axis=0)
        s_w = _dot_nt(qm, kw) + bias_ref[variant]
        s_c = _dot_nt(qm, kc)
        yield
        m = functools.reduce(jnp.maximum, lane_tiles(s_w) + lane_tiles(s_c))
        m = jnp.broadcast_to(jnp.max(m, axis=-1, keepdims=True), m.shape)
        p_w = jnp.exp(s_w - jnp.concatenate([m] * (s_w.shape[-1] // LANES), axis=-1))
        p_c = jnp.exp(s_c - jnp.concatenate([m] * (s_c.shape[-1] // LANES), axis=-1))
        l = jnp.sum(functools.reduce(jnp.add, lane_tiles(p_w) + lane_tiles(p_c)), axis=-1, keepdims=True)
        y = (_dot(p_w.astype(BF16), vw) + _dot(p_c.astype(BF16), vc)) * (1.0 / l)
        out = y[(NA_HEADS - 1) * GRID_W:]
        for h in range(NA_HEADS - 2, -1, -1):
            out = jnp.where(mine[h], y[h * GRID_W:(h + 1) * GRID_W], out)
        o_ref[j * GRID_W:(j + 1) * GRID_W, :] = out.astype(o_ref.dtype)

    _run_interleaved(*[grid_row(j) for j in range(NA_ROWS_PER_STEP)])


def _na_bias_table(rpb):
    depth = rpb.shape[0]
    col = np.arange(GRID_W)
    c_start = np.clip(col - NA_WIN_COLS // 2, 0, GRID_W - NA_WIN_COLS)
    in_win = (col[None, :] >= c_start[:, None]) & (col[None, :] < c_start[:, None] + NA_WIN_COLS)
    dc = np.clip(col[None, :] - col[:, None] + NA_WIN_COLS - 1, 0, 2 * NA_WIN_COLS - 2)
    n_dc = 2 * NA_WIN_COLS - 1
    n_dr = 2 * NA_WIN_ROWS - 1
    onehot = (dc[None, :, :] == np.arange(n_dc)[:, None, None]).astype(np.float32).reshape(n_dc, GRID_W * GRID_W)
    t2 = jnp.dot(rpb.reshape(depth * NA_HEADS * n_dr, n_dc).astype(F32), onehot, precision=lax.Precision.HIGHEST)
    t2 = jnp.where(in_win[None, None, None], t2.reshape(depth, NA_HEADS, n_dr, GRID_W, GRID_W), -jnp.inf)
    tab = jnp.stack([t2[:, :, v:v + NA_WIN_ROWS] for v in range(NA_WIN_ROWS)], axis=1)
    tab = tab.transpose(0, 1, 2, 4, 3, 5).reshape(depth, NA_WIN_ROWS, NA_HEADS * GRID_W, NA_WIN_ROWS * GRID_W)
    masked = jnp.full((depth, 1) + tab.shape[2:], -jnp.inf, F32)
    return jnp.concatenate([tab, masked], axis=1)


def _neighbourhood_attention(na, bias_tab, layer, n_ctx):
    t = na.shape[0]
    n_rows = (t - n_ctx) // GRID_W
    step_rows = NA_ROWS_PER_STEP * GRID_W
    n = t // step_rows
    return pl.pallas_call(
        functools.partial(_na_kernel, n_ctx=n_ctx, n_rows=n_rows),
        grid=(n,),
        in_specs=[
            pl.BlockSpec((step_rows, NA_WIDTH), lambda i: (i, 0)),
            _resident((t, NA_WIDTH), lambda i: (0, 1)),
            _resident((t, NA_WIDTH), lambda i: (0, 2)),
            _layer_spec(bias_tab, layer, resident=True),
        ],
        out_specs=pl.BlockSpec((step_rows, NA_WIDTH), lambda i: (i, 0)),
        out_shape=jax.ShapeDtypeStruct((t, NA_WIDTH), BF16),
        compiler_params=_params("parallel"),
        name="neighbourhood_attention",
    )(na, na, na, bias_tab)


def _gqa_kernel(qt_ref, k_ref, vt_ref, o_ref, acc_ref, s0_ref, s1_ref, p0_ref, p1_ref, *, n_ctx_sub, n_sub):
    tq = qt_ref.shape[1]
    qt = qt_ref[...]
    row = lax.broadcasted_iota(jnp.int32, qt.shape, 0)
    top = row < HEAD_DIM
    zero = jnp.zeros_like(qt)
    qt2 = jnp.concatenate([jnp.where(top, qt, zero), jnp.where(top, zero, qt)], axis=1)
    acc_ref[...] = jnp.zeros_like(acc_ref)
    s_ref = (s0_ref, s1_ref)
    p_ref = (p0_ref, p1_ref)

    def keys(j, size):
        return pl.ds(pl.multiple_of(j * size, LANES), size)

    def score(key_rows, slot):
        size = key_rows.size
        s = _dot(k_ref[key_rows, :], qt2)
        s_ref[slot][0:size, :] = s
        return jnp.max(s.reshape(size // SUBLANES, SUBLANES, 2 * tq), axis=0)

    def softmax(size, slot, m_old, part_max):
        m_new = jnp.maximum(m_old, jnp.max(part_max, axis=0, keepdims=True))
        s = s_ref[slot][0:size, :].reshape(size // SUBLANES, SUBLANES, 2 * tq)
        p_ref[slot][0:size, :] = jnp.exp2(s - m_new[None]).reshape(size, 2 * tq).astype(BF16)
        return m_new, jnp.exp2(m_old - m_new)

    def accumulate(key_rows, slot, alpha):
        size = key_rows.size
        acc = acc_ref[...].reshape(GQA_V_ROWS // SUBLANES, SUBLANES, 2 * tq) * alpha[None]
        acc_ref[...] = acc.reshape(GQA_V_ROWS, 2 * tq) + _dot(vt_ref[:, key_rows], p_ref[slot][0:size, :])

    m_init = jnp.full((SUBLANES, 2 * tq), -jnp.inf, F32)
    sub = GQA_K_SUB
    assert n_sub >= 3

    @pl.when(pl.program_id(1) == 0)
    def _():
        m = m_init
        for j in range(n_ctx_sub):
            m, alpha = softmax(sub, 0, m, score(keys(j, sub), 0))
            accumulate(keys(j, sub), 0, alpha)

    @pl.when(pl.program_id(1) > 0)
    def _():
        part0 = score(keys(0, sub), 0)
        part1 = score(keys(1, sub), 1)
        m, alpha = softmax(sub, 0, m_init, part0)

        def step(t, slot, carry):
            m, alpha, part = carry
            part_next = score(keys(t, sub), slot)
            accumulate(keys(t - 2, sub), slot, alpha)
            return softmax(sub, 1 - slot, m, part) + (part_next,)

        def trip(n, carry):
            for u in range(GQA_STEPS_PER_TRIP):
                carry = step(2 + n * GQA_STEPS_PER_TRIP + u, u % 2, carry)
            return carry

        n_trips = (n_sub - 2) // GQA_STEPS_PER_TRIP
        carry = lax.fori_loop(0, n_trips, trip, (m, alpha, part1))
        for t in range(2 + n_trips * GQA_STEPS_PER_TRIP, n_sub):
            carry = step(t, t % 2, carry)
        m, alpha, part = carry
        accumulate(keys(n_sub - 2, sub), (n_sub - 2) % 2, alpha)
        m, alpha = softmax(sub, (n_sub - 1) % 2, m, part)
        accumulate(keys(n_sub - 1, sub), (n_sub - 1) % 2, alpha)

    acc = acc_ref[...]
    denom = acc[HEAD_DIM:HEAD_DIM + SUBLANES]
    o_t = (acc[:HEAD_DIM].reshape(HEAD_DIM // SUBLANES, SUBLANES, 2 * tq) / denom[None]).reshape(HEAD_DIM, 2 * tq)
    o_ref[...] = jnp.concatenate([o_t[:, :tq], o_t[:, tq:]], axis=0).T.astype(o_ref.dtype)


def _gqa_bounded_kernel(qt_ref, k_ref, vt_ref, o_ref, acc_ref, *p_ref, n_ctx_sub, n_sub):
    tq = qt_ref.shape[1]
    qt = qt_ref[...]
    row = lax.broadcasted_iota(jnp.int32, qt.shape, 0)
    top = row < HEAD_DIM
    zero = jnp.zeros_like(qt)
    qt2 = jnp.concatenate([jnp.where(top, qt, zero), jnp.where(top, zero, qt)], axis=1)
    acc_ref[...] = jnp.zeros_like(acc_ref)
    sub = GQA_K_SUB
    n_slots = len(p_ref)
    lag = n_slots - 1
    steps = GQA_BOUNDED_STEPS_PER_TRIP
    assert steps % n_slots == 0 and n_sub > lag

    def keys(j):
        return pl.ds(pl.multiple_of(j * sub, sub), sub)

    def probs(j, slot):
        p_ref[slot][...] = jnp.exp2(_dot(k_ref[keys(j), :], qt2)).astype(BF16)

    def accumulate(j, slot):
        acc_ref[...] += _dot(vt_ref[:, keys(j)], p_ref[slot][...])

    @pl.when(pl.program_id(1) == 0)
    def _():
        for j in range(n_ctx_sub):
            probs(j, 0)
            accumulate(j, 0)

    @pl.when(pl.program_id(1) > 0)
    def _():
        for t in range(lag):
            probs(t, t % n_slots)

        def trip(n, carry):
            for u in range(steps):
                t = lag + n * steps + u
                probs(t, (lag + u) % n_slots)
                accumulate(t - lag, u % n_slots)
            return carry

        n_trips = (n_sub - lag) // steps
        lax.fori_loop(0, n_trips, trip, 0)
        for t in range(lag + n_trips * steps, n_sub):
            probs(t, t % n_slots)
            accumulate(t - lag, (t - lag) % n_slots)
        for t in range(n_sub - lag, n_sub):
            accumulate(t, t % n_slots)

    acc = acc_ref[...]
    denom = acc[HEAD_DIM:HEAD_DIM + SUBLANES]
    o_t = (acc[:HEAD_DIM].reshape(HEAD_DIM // SUBLANES, SUBLANES, 2 * tq) / denom[None]).reshape(HEAD_DIM, 2 * tq)
    o_ref[...] = jnp.concatenate([o_t[:, :tq], o_t[:, tq:]], axis=0).T.astype(o_ref.dtype)


def _gqa_dispatch_kernel(bounded_ref, qt_ref, k_ref, vt_ref, o_ref, acc_ref, s0_ref, s1_ref, *p_ref,
                         n_ctx_sub, n_sub):
    @pl.when(bounded_ref[0] != 0)
    def _():
        _gqa_bounded_kernel(qt_ref, k_ref, vt_ref, o_ref, acc_ref, *p_ref, n_ctx_sub=n_ctx_sub, n_sub=n_sub)

    @pl.when(bounded_ref[0] == 0)
    def _():
        _gqa_kernel(qt_ref, k_ref, vt_ref, o_ref, acc_ref, s0_ref, s1_ref, p_ref[0], p_ref[1],
                    n_ctx_sub=n_ctx_sub, n_sub=n_sub)


def _gqa_attention(gqt, gk, gvt, n_ctx, score_bound):
    nq = gqt.shape[1]
    n_keys = gk.shape[0]
    tq = GQA_Q_TILE
    assert n_ctx == tq and n_ctx % GQA_K_SUB == 0 and n_keys % GQA_K_SUB == 0 and nq % tq == 0
    bounded = (score_bound <= GQA_BOUNDED_LOG2).astype(jnp.int32).reshape(1)
    return pl.pallas_call(
        functools.partial(_gqa_dispatch_kernel, n_ctx_sub=n_ctx // GQA_K_SUB, n_sub=n_keys // GQA_K_SUB),
        grid=(GQA_KV_HEADS, nq // tq),
        in_specs=[
            pl.BlockSpec(memory_space=pltpu.SMEM),
            pl.BlockSpec((LANES, tq), lambda g, i: (g, i)),
            pl.BlockSpec((n_keys, LANES), lambda g, i: (0, g)),
            pl.BlockSpec((GQA_V_ROWS, n_keys), lambda g, i: (g, 0)),
        ],
        out_specs=pl.BlockSpec((tq, LANES), lambda g, i: (i, g)),
        out_shape=jax.ShapeDtypeStruct((nq, GQA_WIDTH), BF16),
        scratch_shapes=[pltpu.VMEM((GQA_V_ROWS, 2 * tq), F32)] + [pltpu.VMEM((GQA_K_SUB, 2 * tq), F32)] * 2
        + [pltpu.VMEM((GQA_K_SUB, 2 * tq), BF16)] * (GQA_PV_LAG + 1),
        compiler_params=_params("arbitrary", "arbitrary"),
        name="gqa_attention",
    )(bounded, gqt, gk, gvt)


def _out_ffn_kernel(ya_ref, yf_ref, yb_ref, xs_ref, z_ref, yg_ref, xc_ref, xl_ref, mod_ref,
                    dskip_ref, snw_ref, wo_ref, fnw_ref, wg_ref, wu_ref, wd_ref, final_ref,
                    o_ref, *, final, pick_ctx):
    x = jnp.where(pl.program_id(0) == 0, xc_ref[...], xl_ref[...]) if pick_ctx else xl_ref[...]
    y = yf_ref[...] + yb_ref[...] + dskip_ref[...] * xs_ref[...]
    y = y * _silu(z_ref[...])
    ms = jnp.mean(y * y, axis=-1, keepdims=True)
    y = y * lax.rsqrt(ms + EPS) * snw_ref[...]
    mix = jnp.concatenate([ya_ref[...], y.astype(BF16), yg_ref[...]], axis=-1)
    g_m = mod_ref[:, 2 * D_MODEL:3 * D_MODEL]
    sh_f = mod_ref[:, 3 * D_MODEL:4 * D_MODEL]
    sc_f = mod_ref[:, 4 * D_MODEL:5 * D_MODEL]
    g_f = mod_ref[:, 5 * D_MODEL:6 * D_MODEL]
    x1 = x + g_m * _dot(mix, wo_ref[...])
    ms1 = jnp.mean(x1 * x1, axis=-1, keepdims=True)
    hf = (x1 * lax.rsqrt(ms1 + EPS) * fnw_ref[...] * (1.0 + sc_f) + sh_f).astype(BF16)
    act = (_silu(_dot(hf, wg_ref[...])) * _dot(hf, wu_ref[...])).astype(BF16)
    x2 = x1 + g_f * _dot(act, wd_ref[...])
    if final:
        ms2 = jnp.mean(x2 * x2, axis=-1, keepdims=True)
        x2 = x2 * lax.rsqrt(ms2 + EPS) * final_ref[...]
    o_ref[...] = x2


def _out_ffn(ya, yf, yb, xconv, z, yg, x_parts, mod4, layer, d_full, ssd_nw, w_out, ffn_nw,
             w_gate, w_up, w_down, final_nw, final):
    t = ya.shape[0]
    skip = 1 if final else 0
    n = t // ROW_TILE - skip
    row = lambda i: (i + skip, 0)
    const = lambda i: (0, 0)
    x_arrays, x_specs, pick_ctx = _residual_operands(x_parts, skip)
    return pl.pallas_call(
        functools.partial(_out_ffn_kernel, final=final, pick_ctx=pick_ctx),
        grid=(n,),
        in_specs=[
            pl.BlockSpec((ROW_TILE, NA_WIDTH), row),
            pl.BlockSpec((ROW_TILE, SSD_WIDTH), row),
            pl.BlockSpec((ROW_TILE, SSD_WIDTH), row),
            pl.BlockSpec((ROW_TILE, SSD_WIDTH), row),
            pl.BlockSpec((ROW_TILE, SSD_WIDTH), row),
            pl.BlockSpec((ROW_TILE, GQA_WIDTH), row),
        ] + x_specs + [
            pl.BlockSpec((None, None, 1, 6 * D_MODEL), lambda i: (layer, jnp.minimum(i + skip, 1), 0, 0)),
            _layer_spec(d_full, layer),
            _layer_spec(ssd_nw, layer),
            _layer_spec(w_out, layer, resident=True),
            _layer_spec(ffn_nw, layer),
            _layer_spec(w_gate, layer, resident=True),
            _layer_spec(w_up, layer, resident=True),
            _layer_spec(w_down, layer, resident=True),
            pl.BlockSpec((1, D_MODEL), const),
        ],
        out_specs=pl.BlockSpec((ROW_TILE, D_MODEL), lambda i: (i, 0)),
        out_shape=jax.ShapeDtypeStruct((n * ROW_TILE, D_MODEL), F32),
        compiler_params=_params("parallel"),
        name="out_ffn",
    )(ya, yf, yb, xconv, z, yg, *x_arrays, mod4, d_full, ssd_nw, w_out, ffn_nw, w_gate, w_up, w_down, final_nw)


def _rearranged_w_in(w):
    na_in = 3 * NA_WIDTH
    o_z = na_in
    o_xbc = o_z + SSD_WIDTH
    o_dt = o_xbc + SSD_CONV_CH
    o_gq = o_dt + 2 * SSD_HEADS
    o_gk = o_gq + GQA_WIDTH
    o_gv = o_gk + GQA_KV_HEADS * HEAD_DIM
    w = w.astype(BF16)
    q_na = w[..., :NA_WIDTH] * ATTN_SCALE
    dt_pad = jnp.zeros(w.shape[:-1] + (LANES - 2 * SSD_HEADS,), BF16)
    k_heads = [w[..., o_gk + h * HEAD_DIM:o_gk + (h + 1) * HEAD_DIM] for h in range(GQA_KV_HEADS)]
    return jnp.concatenate(
        [q_na, w[..., NA_WIDTH:o_dt], w[..., o_dt:o_gq], dt_pad, w[..., o_gq:o_gk]]
        + [p for h in k_heads for p in (h, h)] + [w[..., o_gv:]], axis=-1)


def _rope_tables(n_ctx, n_lat):
    f32 = np.float32
    freqs = f32(ROPE_THETA) ** (-np.arange(ROPE_PAIRS, dtype=f32) / f32(ROPE_PAIRS))
    n_rows = n_lat // GRID_W
    half = 2 * ROPE_PAIRS
    sign = np.where(np.arange(half) < ROPE_PAIRS, -1.0, 1.0).astype(f32)

    def tables(n_pos):
        a = np.arange(n_pos, dtype=f32)[:, None] * freqs[None, :]
        a = np.concatenate([a, a], axis=-1)
        return np.cos(a), np.sin(a) * sign[None, :]

    def per_token(by_row, by_col):
        lat = np.concatenate([np.broadcast_to(by_row[:, None, :], (n_rows, GRID_W, half)),
                              np.broadcast_to(by_col[None, :, :], (n_rows, GRID_W, half))], axis=-1)
        return lat.reshape(n_lat, HEAD_DIM)

    cos_r, sin_r = tables(n_rows)
    cos_c, sin_c = tables(GRID_W)
    cos = np.concatenate([np.ones((n_ctx, HEAD_DIM), f32), per_token(cos_r, cos_c)], axis=0)
    sin = np.concatenate([np.zeros((n_ctx, HEAD_DIM), f32), per_token(sin_r, sin_c)], axis=0)
    reps = (1, LANES // HEAD_DIM)
    return jnp.asarray(np.tile(cos, reps), F32), jnp.asarray(np.tile(sin, reps), F32)


def kernel(x, c, ctx, c_ctx, mod_w, mod_b, norm_attn_w, norm_ffn_w, w_in, na_rpb, ssd_conv_w, ssd_conv_b,
           ssd_dt_bias, ssd_a_log, ssd_d, ssd_norm_w, q_norm_w, k_norm_w, w_out, ffn_w_gate, ffn_w_up,
           ffn_w_down, final_norm_w):
    depth = mod_w.shape[0]
    batch, n_lat, _ = x.shape
    n_ctx = ctx.shape[1]
    assert batch == 1 and n_ctx == ROW_TILE and n_lat % (NA_ROWS_PER_STEP * GRID_W) == 0
    assert n_lat % GQA_Q_TILE == 0 and n_lat // GRID_W >= NA_WIN_ROWS

    x_parts = (ctx[0], x[0])
    cc = jnp.zeros((SUBLANES, D_MODEL), F32).at[0].set(c_ctx).at[1].set(c[0])
    mod = _modulation(cc, mod_w, mod_b)
    mod4 = mod[:, :2].reshape(depth, 2, 1, 6 * D_MODEL)

    cos_t, sin_t = _rope_tables(n_ctx, n_lat)
    blk = np.arange(2 * GQA_WIDTH) // HEAD_DIM
    ones_bd = jnp.asarray(blk[:, None] == blk[None, :], BF16)
    ones_rows = jnp.ones((GQA_V_ROWS - HEAD_DIM, n_ctx + n_lat), BF16)

    row = lambda p: p.astype(F32).reshape(depth, 1, -1)
    pad_lanes = lambda p: jnp.pad(row(p), ((0, 0), (0, 0), (0, LANES - 2 * SSD_HEADS)))
    w_cat = _rearranged_w_in(w_in)
    dt_bias_pad = pad_lanes(ssd_dt_bias)
    a_row = pad_lanes(-jnp.exp(ssd_a_log.astype(F32)))
    qk_w = row(jnp.concatenate([jnp.tile(q_norm_w * (ATTN_SCALE * LOG2E), (1, GQA_Q_HEADS)),
                                jnp.tile(k_norm_w, (1, 2 * GQA_KV_HEADS))], axis=-1))
    conv_w_pad = jnp.pad(ssd_conv_w, ((0, 0), (0, SUBLANES - SSD_CONV), (0, 0)))
    d_full = row(jnp.repeat(ssd_d, HEAD_DIM, axis=-1))
    bias_tab = _na_bias_table(na_rpb)
    w_out_b, w_gate_b, w_up_b, w_down_b = (w.astype(BF16) for w in (w_out, ffn_w_gate, ffn_w_up, ffn_w_down))
    score_bound = (1.02 * HEAD_DIM * ATTN_SCALE * LOG2E) * jnp.max(jnp.abs(q_norm_w), axis=-1) * jnp.max(
        jnp.abs(k_norm_w), axis=-1)

    for i in range(depth):
        final = i == depth - 1
        na, z, xconv, dt, gq, gk, gv = _inproj(x_parts, n_ctx + n_lat, mod4, i, row(norm_attn_w), w_cat,
                                               dt_bias_pad, qk_w, cos_t, sin_t, ones_bd, conv_w_pad,
                                               row(ssd_conv_b))
        yf, yb = _ssd_scan(xconv, dt, a_row, i, n_ctx)
        ya = _neighbourhood_attention(na, bias_tab, i, n_ctx)
        gvt = jnp.concatenate([part for g in range(GQA_KV_HEADS)
                               for part in (gv[:, g * HEAD_DIM:(g + 1) * HEAD_DIM].T, ones_rows)], axis=0)
        yg = _gqa_attention(gq.T, gk, gvt, n_ctx, score_bound[i])
        x_parts = _out_ffn(ya, yf, yb, xconv, z, yg, x_parts, mod4, i, d_full, row(ssd_norm_w), w_out_b,
                           row(norm_ffn_w), w_gate_b, w_up_b, w_down_b, final_norm_w.reshape(1, -1), final)
    return x_parts[None]
```
